```python
import jax, jax.numpy as jnp
from jax import lax
import numpy as np

D_MODEL = 1024
BATCH = 16
SEQ = 256
DEPTH = 2
DEC_BATCH = 4
DEC_SEQ = 1024
PAST_LEN = 256

GRID_W = 64
CHUNK = 64
N_EVEN = (DEPTH + 1) // 2
N_ODD = DEPTH // 2
GLA_HEADS = 4
GLA_DK = 64
GLA_DV = 128
GLA_LOW_RANK = 16
GLA_NORMALIZER = 16.0
RET_HEADS = 4
RET_DK = 128
RET_DV = 128
RET_BASE_EXP = 5.0
ROPE_BASE = 10000.0
HG_HEADS = 8
HG_DK = 128
HG_DV = 128
N_EXPERTS = 32
TOP_K = 4
D_FF = 1024
SWIGLU_ALPHA = 1.702
SWIGLU_LIMIT = 7.0
EPS = 1e-6

GLA_QK = GLA_HEADS * GLA_DK
GLA_V = GLA_HEADS * GLA_DV
RET_QK = RET_HEADS * RET_DK
RET_V = RET_HEADS * RET_DV
EVEN_SPLITS = (GLA_QK, GLA_QK, GLA_V, GLA_LOW_RANK, GLA_LOW_RANK, GLA_V, RET_QK, RET_QK, RET_V, RET_V)
EVEN_IN = 2 * GLA_QK + 2 * GLA_V + 2 * GLA_LOW_RANK + 2 * RET_QK + 2 * RET_V
EVEN_MIX = GLA_V + RET_V
HG_W = HG_HEADS * HG_DK
HG_V = HG_HEADS * HG_DV
ODD_SPLITS = (HG_W, HG_W, HG_W, HG_V, HG_V)
ODD_IN = 3 * HG_W + 2 * HG_V

kernel_name = "bidir_gla_retnet_hgrn2_moe_prefix_context_step"


def rms_norm(x, gain=None):
    xf = x.astype(jnp.float32)
    y = xf * lax.rsqrt(jnp.mean(xf * xf, axis=-1, keepdims=True) + EPS)
    if gain is not None:
        y = y * gain.astype(jnp.float32)
    return y.astype(x.dtype)


def split_cols(x, sizes):
    out, start = [], 0
    for s in sizes:
        out.append(x[..., start:start + s])
        start += s
    return out


def to_heads(x, n_heads):
    b, t, _ = x.shape
    return x.reshape(b, t, n_heads, -1).transpose(0, 2, 1, 3)


def from_heads(x):
    b, h, t, d = x.shape
    return x.transpose(0, 2, 1, 3).reshape(b, t, h * d)


def flip_t(a):
    return jnp.flip(a, axis=2)


def axial_rope(n_tokens):
    rows = n_tokens // GRID_W
    r = jnp.repeat(jnp.arange(rows, dtype=jnp.float32), GRID_W)
    col = jnp.tile(jnp.arange(GRID_W, dtype=jnp.float32), rows)
    nf = RET_DK // 4
    inv = ROPE_BASE ** (-jnp.arange(nf, dtype=jnp.float32) / nf)
    ang = jnp.concatenate([r[:, None] * inv, col[:, None] * inv], axis=-1)
    return jnp.cos(ang), jnp.sin(ang)


def apply_rope(x, cos, sin):
    half = x.shape[-1] // 2
    x1, x2 = x[..., :half], x[..., half:]
    cos, sin = cos.astype(x.dtype), sin.astype(x.dtype)
    return jnp.concatenate([x1 * cos - x2 * sin, x1 * sin + x2 * cos], axis=-1)


def chunk_gated_scan(q, k, v, log_f, s0):
    B, H, T, DK = q.shape
    DV = v.shape[-1]
    n = T // CHUNK
    f32 = jnp.float32
    q, k, g = (a.astype(f32).reshape(B, H, n, CHUNK, DK) for a in (q, k, log_f))
    v = v.astype(f32).reshape(B, H, n, CHUNK, DV)
    b = jnp.cumsum(g, axis=3)
    b_last = b[:, :, :, -1:, :]
    b_mid = b[:, :, :, CHUNK // 2 - 1:CHUNK // 2, :]
    causal = jnp.tril(jnp.ones((CHUNK, CHUNK), dtype=bool))
    scores = jnp.einsum('bhnid,bhnjd->bhnij', q * jnp.exp(b - b_mid), k * jnp.exp(b_mid - b))
    o_intra = jnp.einsum('bhnij,bhnje->bhnie', jnp.where(causal, scores, 0.0), v)
    u = jnp.einsum('bhncd,bhnce->nbhde', k * jnp.exp(b_last - b), v)
    decay = jnp.exp(jnp.moveaxis(b_last[:, :, :, 0, :], 2, 0))

    def step(s, inp):
        d, uu = inp
        return d[..., None] * s + uu, s

    s_fin, s_in = lax.scan(step, s0.astype(f32), (decay, u))
    o_inter = jnp.einsum('bhncd,nbhde->bhnce', q * jnp.exp(b), s_in)
    return (o_intra + o_inter).reshape(B, H, T, DV), s_fin


def chunk_retention(q, k, v, log_gamma, s0):
    B, H, T, DK = q.shape
    DV = v.shape[-1]
    n = T // CHUNK
    f32 = jnp.float32
    q, k = (a.astype(f32).reshape(B, H, n, CHUNK, DK) for a in (q, k))
    v = v.astype(f32).reshape(B, H, n, CHUNK, DV)
    lg = log_gamma.astype(f32)
    pos = jnp.arange(CHUNK, dtype=f32)
    diff = pos[:, None] - pos[None, :]
    dmask = jnp.where(diff >= 0, jnp.exp(jnp.maximum(diff, 0.0)[None] * lg[:, None, None]), 0.0)
    scores = jnp.einsum('bhnid,bhnjd->bhnij', q, k) * dmask[None, :, None]
    o_intra = jnp.einsum('bhnij,bhnje->bhnie', scores, v)
    zeta = jnp.exp((CHUNK - 1.0 - pos)[None, :] * lg[:, None])
    xi = jnp.exp((pos + 1.0)[None, :] * lg[:, None])
    u = jnp.einsum('bhncd,bhnce,hc->nbhde', k, v, zeta)
    d_chunk = jnp.exp(CHUNK * lg)[None, :, None, None]

    def step(s, uu):
        return d_chunk * s + uu, s

    s_fin, s_in = lax.scan(step, s0.astype(f32), u)
    o_inter = jnp.einsum('bhncd,nbhde,hc->bhnce', q, s_in, xi)
    return (o_intra + o_inter).reshape(B, H, T, DV), s_fin


def even_mixer(h, w_in, w_out, gk_w, gk_b, gla_gain, ret_exp, s_gla, s_ret, rope):
    gq, gk, gv, glr_f, glr_b, gg, rq, rk, rv, rg = split_cols(h @ w_in, EVEN_SPLITS)
    q = to_heads(gq, GLA_HEADS) * (GLA_DK ** -0.5)
    k = to_heads(gk, GLA_HEADS)
    v = to_heads(gv, GLA_HEADS)

    def gla_gate(lr, d):
        z = (lr @ gk_w[d] + gk_b[d]).astype(jnp.float32)
        return to_heads(jax.nn.log_sigmoid(z) / GLA_NORMALIZER, GLA_HEADS)

    o_f, sg_f = chunk_gated_scan(q, k, v, gla_gate(glr_f, 0), s_gla[:, 0])
    o_b, sg_b = chunk_gated_scan(flip_t(q), flip_t(k), flip_t(v), flip_t(gla_gate(glr_b, 1)), s_gla[:, 1])
    o_gla = from_heads(rms_norm(o_f + flip_t(o_b), gla_gain)).astype(h.dtype) * jax.nn.silu(gg)
    q = to_heads(rq, RET_HEADS)
    k = to_heads(rk, RET_HEADS) * (RET_DK ** -0.5)
    v = to_heads(rv, RET_HEADS)
    if rope is not None:
        q = apply_rope(q, rope[0], rope[1])
        k = apply_rope(k, rope[0], rope[1])
    log_gamma = jnp.log1p(-jnp.exp2(-ret_exp.astype(jnp.float32)))
    r_f, sr_f = chunk_retention(q, k, v, log_gamma[0], s_ret[:, 0])
    r_b, sr_b = chunk_retention(flip_t(q), flip_t(k), flip_t(v), log_gamma[1], s_ret[:, 1])
    o_ret = from_heads(rms_norm(r_f + flip_t(r_b))).astype(h.dtype) * jax.nn.silu(rg)
    out = jnp.concatenate([o_gla, o_ret], axis=-1) @ w_out
    return out, jnp.stack([sg_f, sg_b], axis=1), jnp.stack([sr_f, sr_b], axis=1)


def odd_mixer(h, w_in, w_out, lb, gain, s_hg):
    q, ff, fb, i, g = split_cols(h @ w_in, ODD_SPLITS)
    q = to_heads(jax.nn.silu(q), HG_HEADS)
    i = to_heads(i, HG_HEADS)

    def gates(fz, d):
        f = lb[d] + (1.0 - lb[d]) * jax.nn.sigmoid(fz.astype(jnp.float32))
        return to_heads(1.0 - f, HG_HEADS), to_heads(jnp.log(f), HG_HEADS)

    k_f, lf_f = gates(ff, 0)
    k_b, lf_b = gates(fb, 1)
    o_f, s_f = chunk_gated_scan(q, k_f, i, lf_f, s_hg[:, 0])
    o_b, s_b = chunk_gated_scan(flip_t(q), flip_t(k_b), flip_t(i), flip_t(lf_b), s_hg[:, 1])
    o = from_heads(rms_norm(o_f + flip_t(o_b), gain)).astype(h.dtype) * jax.nn.silu(g)
    return o @ w_out, jnp.stack([s_f, s_b], axis=1)


def moe_ffn(h, rw, rb, w1, b1, w2, b2):
    B, T, D = h.shape
    t = h.reshape(B * T, D)
    logits = (t @ rw + rb).astype(jnp.float32)
    top_v, top_i = lax.top_k(logits, TOP_K)
    w = jax.nn.softmax(top_v, axis=-1)
    gates = jnp.einsum('nk,nke->ne', w, jax.nn.one_hot(top_i, N_EXPERTS, dtype=jnp.float32))
    hid = jnp.einsum('nd,edf->nef', t, w1) + b1[None]
    x_glu = jnp.minimum(hid[..., ::2], SWIGLU_LIMIT)
    x_lin = jnp.clip(hid[..., 1::2], -SWIGLU_LIMIT, SWIGLU_LIMIT)
    act = x_glu * jax.nn.sigmoid(SWIGLU_ALPHA * x_glu) * (x_lin + 1.0)
    act = act * gates[..., None].astype(act.dtype)
    out = jnp.einsum('nef,efd->nd', act, w2) + gates.astype(t.dtype) @ b2
    return out.reshape(B, T, D)


def setup_inputs(seed: int = 0) -> dict:
    key = jax.random.key(seed)
    ks = jax.random.split(key, 32)
    f32 = jnp.float32
    D = D_MODEL

    def nrm(i, shape, scale):
        return scale * jax.random.normal(ks[i], shape, f32)

    return {
        "x_prompt": nrm(0, (BATCH, SEQ, D), 1.0),
        "x_sample": nrm(1, (DEC_BATCH, DEC_SEQ, D), 1.0),
        "c": nrm(2, (DEC_BATCH, D), 1.0),
        "state_gla": nrm(3, (DEC_BATCH, N_EVEN, 2, GLA_HEADS, GLA_DK, GLA_DV), 1.0),
        "state_ret": nrm(4, (DEC_BATCH, N_EVEN, 2, RET_HEADS, RET_DK, RET_DV), 1.0),
        "state_hgrn": nrm(5, (DEC_BATCH, N_ODD, 2, HG_HEADS, HG_DK, HG_DV), 1.0),
        "c_ctx": nrm(6, (D,), 1.0),
        "norm_mix": 1.0 + nrm(7, (DEPTH, D), 0.02),
        "norm_ffn": 1.0 + nrm(8, (DEPTH, D), 0.02),
        "ada_w": nrm(9, (DEPTH, D, 6 * D), 0.5 * D ** -0.5),
        "ada_b": nrm(10, (DEPTH, 6 * D), 0.02),
        "w_in_even": nrm(11, (N_EVEN, D, EVEN_IN), D ** -0.5),
        "w_out_even": nrm(12, (N_EVEN, EVEN_MIX, D), EVEN_MIX ** -0.5),
        "gla_gk_w": nrm(13, (N_EVEN, 2, GLA_LOW_RANK, GLA_QK), GLA_LOW_RANK ** -0.5),
        "gla_gk_b": nrm(14, (N_EVEN, 2, GLA_QK), 0.1),
        "gla_gain": 1.0 + nrm(15, (N_EVEN, GLA_DV), 0.02),
        "ret_decay_exp": RET_BASE_EXP + jnp.arange(RET_HEADS, dtype=f32) + nrm(16, (N_EVEN, 2, RET_HEADS), 0.1),
        "w_in_odd": nrm(17, (N_ODD, D, ODD_IN), D ** -0.5),
        "w_out_odd": nrm(18, (N_ODD, HG_V, D), HG_V ** -0.5),
        "hgrn_lb_logits": nrm(19, (DEPTH, 2, HG_W), 0.1),
        "hgrn_gain": 1.0 + nrm(20, (N_ODD, HG_DV), 0.02),
        "router_w": nrm(21, (DEPTH, D, N_EXPERTS), D ** -0.5),
        "router_b": nrm(22, (DEPTH, N_EXPERTS), 0.01),
        "moe_w1": nrm(23, (DEPTH, N_EXPERTS, D, 2 * D_FF), D ** -0.5),
        "moe_b1": nrm(24, (DEPTH, N_EXPERTS, 2 * D_FF), 0.01),
        "moe_w2": nrm(25, (DEPTH, N_EXPERTS, D_FF, D), D_FF ** -0.5),
        "moe_b2": nrm(26, (DEPTH, N_EXPERTS, D), 0.01),
        "final_norm": 1.0 + nrm(27, (D,), 0.02),
    }


def reference(x_prompt, x_sample, c, state_gla, state_ret, state_hgrn, c_ctx, norm_mix, norm_ffn, ada_w, ada_b,
              w_in_even, w_out_even, gla_gk_w, gla_gk_b, gla_gain, ret_decay_exp, w_in_odd, w_out_odd,
              hgrn_lb_logits, hgrn_gain, router_w, router_b, moe_w1, moe_b1, moe_w2, moe_b2, final_norm):
    f32 = jnp.float32
    p = jax.nn.softmax(hgrn_lb_logits.astype(f32), axis=0)
    lb_all = jnp.cumsum(p, axis=0) - p[0]

    def trunk(x, cond, init_states, rope):
        finals = []
        for l in range(DEPTH):
            mods = jnp.split(jax.nn.silu(cond) @ ada_w[l] + ada_b[l], 6, axis=-1)
            sh1, sc1, g1, sh2, sc2, g2 = mods
            hn = rms_norm(x, norm_mix[l]) * (1.0 + sc1) + sh1
            e = l // 2
            if l % 2 == 0:
                s_g, s_r = init_states[l]
                mix, fg, fr = even_mixer(hn, w_in_even[e], w_out_even[e], gla_gk_w[e], gla_gk_b[e], gla_gain[e],
                                         ret_decay_exp[e], s_g, s_r, rope)
                finals.append((fg, fr))
            else:
                (s_h,) = init_states[l]
                mix, fh = odd_mixer(hn, w_in_odd[e], w_out_odd[e], lb_all[l], hgrn_gain[e], s_h)
                finals.append((fh,))
            x = x + g1 * mix
            hn = rms_norm(x, norm_ffn[l]) * (1.0 + sc2) + sh2
            x = x + g2 * moe_ffn(hn, router_w[l], router_b[l], moe_w1[l], moe_b1[l], moe_w2[l], moe_b2[l])
        return rms_norm(x, final_norm), finals

    B = x_prompt.shape[0]
    ctx_init = []
    for l in range(DEPTH):
        if l % 2 == 0:
            ctx_init.append((jnp.zeros((B, 2, GLA_HEADS, GLA_DK, GLA_DV), f32),
                             jnp.zeros((B, 2, RET_HEADS, RET_DK, RET_DV), f32)))
        else:
            ctx_init.append((jnp.zeros((B, 2, HG_HEADS, HG_DK, HG_DV), f32),))
    y_prompt, ctx_finals = trunk(x_prompt, c_ctx, ctx_init, None)
    new_state_gla = jnp.stack([ctx_finals[l][0] for l in range(0, DEPTH, 2)], axis=1).astype(x_prompt.dtype)
    new_state_ret = jnp.stack([ctx_finals[l][1] for l in range(0, DEPTH, 2)], axis=1).astype(x_prompt.dtype)
    new_state_hgrn = jnp.stack([ctx_finals[l][0] for l in range(1, DEPTH, 2)], axis=1).astype(x_prompt.dtype)

    lat_init = []
    for l in range(DEPTH):
        if l % 2 == 0:
            lat_init.append((state_gla[:, l // 2], state_ret[:, l // 2]))
        else:
            lat_init.append((state_hgrn[:, l // 2],))
    rope = axial_rope(x_sample.shape[1])
    y_sample, _ = trunk(x_sample, c[:, None, :], lat_init, rope)

    return (y_prompt, y_sample, new_state_gla, new_state_ret, new_state_hgrn)
```

```python
import functools

import jax
import jax.numpy as jnp
from jax import lax
from jax.experimental import pallas as pl
from jax.experimental.pallas import tpu as pltpu

F32 = jnp.float32
BF16 = jnp.bfloat16

D_MODEL = 1024
DEPTH = 2
GRID_W = 64
GLA_HEADS, GLA_DK, GLA_DV, GLA_LOW_RANK = 4, 64, 128, 16
GLA_NORMALIZER = 16.0
RET_HEADS, RET_DK, RET_DV = 4, 128, 128
ROPE_BASE = 10000.0
HG_HEADS, HG_DK, HG_DV = 8, 128, 128
N_EXPERTS, TOP_K, D_FF = 32, 4, 1024
SWIGLU_ALPHA, SWIGLU_LIMIT = 1.702, 7.0
EPS = 1e-6

LANES = 128
SCAN_CHUNK = 128
TOKEN_TILE = 256
MOE_TILE = 256
VMEM_LIMIT = 56 * 1024 * 1024

_GQ, _GK, _GV, _GG, _RQ, _RK, _RV, _RG, _EVEN_MAIN = 0, 256, 512, 1024, 1536, 2048, 2560, 3072, 3584
_HQ, _HFF, _HFB, _HI, _HG, _ODD_MAIN = 0, 1024, 2048, 3072, 4096, 5120


def _dot(a, b):
    return jnp.dot(a, b, preferred_element_type=F32)


def _dot_nt(a, b):
    return lax.dot_general(a, b, (((1,), (1,)), ((), ())), preferred_element_type=F32)


def _rms(x, gain=None):
    y = x * lax.rsqrt(jnp.mean(x * x, axis=-1, keepdims=True) + EPS)
    if gain is not None:
        y = y * gain
    return y


def _sigmoid(x):
    return 1.0 / (1.0 + jnp.exp(-x))


def _silu(x):
    return x * _sigmoid(x)


def _cparams(sem, vmem=VMEM_LIMIT):
    return pltpu.CompilerParams(dimension_semantics=sem, vmem_limit_bytes=vmem)


def _resident(shape):
    nd = len(shape)
    return pl.BlockSpec(shape, lambda *_: (0,) * nd, pipeline_mode=pl.Buffered(1))


def _ada_kernel(c_ref, w_ref, b_ref, o_ref):
    o_ref[0] = _dot(_silu(c_ref[...]), w_ref[0]) + b_ref[0]


def _ada_mods(cond8, ada_w, ada_b):
    tn = 1536
    return pl.pallas_call(
        _ada_kernel,
        grid=(DEPTH, 6 * D_MODEL // tn),
        in_specs=[
            pl.BlockSpec((8, D_MODEL), lambda l, j: (0, 0)),
            pl.BlockSpec((1, D_MODEL, tn), lambda l, j: (l, 0, j)),
            pl.BlockSpec((1, 1, tn), lambda l, j: (l, 0, j)),
        ],
        out_specs=pl.BlockSpec((1, 8, tn), lambda l, j: (l, 0, j)),
        out_shape=jax.ShapeDtypeStruct((DEPTH, 8, 6 * D_MODEL), F32),
        compiler_params=_cparams(("arbitrary", "arbitrary")),
        name="ada_mods",
    )(cond8, ada_w, ada_b.reshape(DEPTH, 1, 6 * D_MODEL))


def _in_proj_kernel(*refs, first, even, nb_ctx):
    it = iter(refs)
    if first:
        xa_ref, xb_ref = next(it), next(it)
    else:
        xp_ref = next(it)
        y_refs = [next(it) for _ in range(TOP_K)]
        modp_ref = next(it)
    gain_ref, mod_ref, w_ref = next(it), next(it), next(it)
    if even:
        wlr_ref, gkw_ref, gkb_ref = next(it), next(it), next(it)
    proj_ref = next(it)
    if even:
        g_ref = next(it)
    xcur_ref = next(it)

    D = D_MODEL
    i = pl.program_id(0)
    if first:
        x = jnp.where(i < nb_ctx, xa_ref[...], xb_ref[...])
    else:
        g2 = modp_ref[0][:, 5 * D:6 * D]
        x = xp_ref[...] + g2 * ((y_refs[0][...] + y_refs[1][...]) + (y_refs[2][...] + y_refs[3][...]))
    xcur_ref[...] = x
    m = mod_ref[0]
    hn = _rms(x, gain_ref[...]) * (1.0 + m[:, D:2 * D]) + m[:, 0:D]
    proj_ref[...] = _dot(hn, w_ref[...])
    if even:
        z = _dot(_dot(hn, wlr_ref[...]), gkw_ref[...]) + gkb_ref[...]
        g_ref[...] = (jnp.minimum(z, 0.0) - jnp.log(1.0 + jnp.exp(-jnp.abs(z)))) * (1.0 / GLA_NORMALIZER)


def _cond_row(i, nb_ctx, tiles_per_lat_seq):
    return jnp.where(i < nb_ctx, 0, 1 + (i - nb_ctx) // tiles_per_lat_seq)


def _in_proj(layer, n_ctx, n_tok, lat_seq, mods, gain, w_main, *, x_parts=None, x_prev=None, y4=None, even_extra=None):
    first = x_parts is not None
    even = even_extra is not None
    tm = TOKEN_TILE
    nb = n_tok // tm
    nb_ctx = n_ctx // tm
    tpl = lat_seq // tm
    np_cols = w_main.shape[1]
    D = D_MODEL

    def mod_map(l):
        return lambda i: (l * 8 + _cond_row(i, nb_ctx, tpl), 0, 0)

    row = lambda i: (i, 0)
    args, specs = [], []
    if first:
        xa, xb = x_parts
        args += [xa, xb]
        specs += [pl.BlockSpec((tm, D), lambda i: (jnp.minimum(i, nb_ctx - 1), 0)),
                  pl.BlockSpec((tm, D), lambda i: (jnp.maximum(i - nb_ctx, 0), 0))]
    else:
        args += [x_prev] + [y4] * TOP_K + [mods]
        specs += [pl.BlockSpec((tm, D), row)]
        specs += [pl.BlockSpec((tm, D), (lambda k: (lambda i: (k * nb + i, 0)))(k)) for k in range(TOP_K)]
        specs += [pl.BlockSpec((1, 1, 6 * D), mod_map(layer - 1))]
    args += [gain.reshape(1, D), mods, w_main]
    specs += [_resident((1, D)), pl.BlockSpec((1, 1, 6 * D), mod_map(layer)), _resident((D, np_cols))]
    out_shapes = [jax.ShapeDtypeStruct((n_tok, np_cols), F32)]
    out_specs = [pl.BlockSpec((tm, np_cols), row)]
    if even:
        w_lr, gkw, gkb = even_extra
        args += [w_lr, gkw, gkb]
        specs += [_resident(w_lr.shape), _resident(gkw.shape), _resident(gkb.shape)]
        out_shapes.append(jax.ShapeDtypeStruct((n_tok, 2 * GLA_HEADS * GLA_DK), F32))
        out_specs.append(pl.BlockSpec((tm, 2 * GLA_HEADS * GLA_DK), row))
    out_shapes.append(jax.ShapeDtypeStruct((n_tok, D), F32))
    out_specs.append(pl.BlockSpec((tm, D), row))
    return pl.pallas_call(
        functools.partial(_in_proj_kernel, first=first, even=even, nb_ctx=nb_ctx),
        grid=(nb,),
        in_specs=specs,
        out_specs=out_specs,
        out_shape=out_shapes,
        compiler_params=_cparams(("arbitrary",)),
        name=f"in_proj_l{layer}",
    )(*args)


def _tri(c, lower):
    r = lax.broadcasted_iota(jnp.int32, (c, c), 0)
    s = lax.broadcasted_iota(jnp.int32, (c, c), 1)
    return (r >= s) if lower else (r <= s)


def _cumsum_mm(tri_bf16, g):
    g_hi = g.astype(BF16)
    g_lo = (g - g_hi.astype(F32)).astype(BF16)
    r = _dot(tri_bf16, jnp.concatenate([g_hi, g_lo], axis=1))
    w = g.shape[1]
    return r[:, :w] + r[:, w:]


def _gated_scan_kernel(*refs, variant, T, has_s0, want_final, hpb):
    C = SCAN_CHUNK
    nc = T // C
    it = iter(refs)
    if variant == "gla":
        q_ref, k_ref, v_ref, og_ref, gf_ref, gb_ref, gain_ref = (next(it) for _ in range(7))
    else:
        q_ref, ff_ref, fb_ref, v_ref, og_ref, lbl_ref, gain_ref = (next(it) for _ in range(7))
    s0_ref = next(it) if has_s0 else None
    y_ref = next(it)
    sfin_ref = next(it) if want_final else None
    qi_scr, o_scr, u_scr, dec_scr, sin_scr = (next(it) for _ in range(5))

    lane = lax.broadcasted_iota(jnp.int32, (1, LANES), 1)
    if hpb == 2:
        masks = [lane < GLA_DK, lane >= GLA_DK]
    else:
        masks = [None]
    tri_l, tri_u = _tri(C, True), _tri(C, False)
    tri_l16, tri_u16 = tri_l.astype(F32).astype(BF16), tri_u.astype(F32).astype(BF16)
    mid_f, mid_b = C // 2 - 1, C // 2

    if variant == "hgrn":
        lgs = [lbl_ref[l] for l in range(DEPTH)]
        mx = functools.reduce(jnp.maximum, lgs)
        es = [jnp.exp(l - mx) for l in lgs]
        tot = functools.reduce(lambda a, b: a + b, es)
        ps = [e / tot for e in es]
        layer = DEPTH - 1
        lb = functools.reduce(lambda a, b: a + b, ps[:layer + 1]) - ps[0]
        lb_f, lb_b = lb[0:1], lb[1:2]

    for n in range(nc):
        rows = pl.ds(n * C, C)
        if variant == "gla":
            q = q_ref[rows, :] * (GLA_DK ** -0.5)
            kf = kb = k_ref[rows, :]
            gf, gb = gf_ref[rows, :], gb_ref[rows, :]
        else:
            q = _silu(q_ref[rows, :])
            f_f = lb_f + (1.0 - lb_f) * _sigmoid(ff_ref[rows, :])
            f_b = lb_b + (1.0 - lb_b) * _sigmoid(fb_ref[rows, :])
            kf, kb = 1.0 - f_f, 1.0 - f_b
            gf, gb = jnp.log(f_f), jnp.log(f_b)
        bf = _cumsum_mm(tri_l16, gf)
        bb = _cumsum_mm(tri_u16, gb)
        bf_mid, bf_end = bf[mid_f:mid_f + 1], bf[C - 1:C]
        bb_mid, bb_end = bb[mid_b:mid_b + 1], bb[0:1]
        qd_f, kd_f = q * jnp.exp(bf - bf_mid), kf * jnp.exp(bf_mid - bf)
        qd_b, kd_b = q * jnp.exp(bb - bb_mid), kb * jnp.exp(bb_mid - bb)
        kend = jnp.concatenate([kf * jnp.exp(bf_end - bf), kb * jnp.exp(bb_end - bb)], axis=1)
        qi_scr[rows, :] = jnp.concatenate([q * jnp.exp(bf), q * jnp.exp(bb)], axis=1)
        dec_scr[n] = jnp.broadcast_to(jnp.concatenate([jnp.exp(bf_end), jnp.exp(bb_end)], axis=1), (8, 2 * LANES))
        for h in range(hpb):
            v = v_ref[rows, h * LANES:(h + 1) * LANES]
            if masks[h] is None:
                qf_h, qb_h, kend_h = qd_f, qd_b, kend
            else:
                qf_h, qb_h = jnp.where(masks[h], qd_f, 0.0), jnp.where(masks[h], qd_b, 0.0)
                kend_h = jnp.where(jnp.concatenate([masks[h], masks[h]], axis=1), kend, 0.0)
            s = jnp.where(tri_l, _dot_nt(qf_h, kd_f), 0.0) + jnp.where(tri_u, _dot_nt(qb_h, kd_b), 0.0)
            o_scr[rows, h * LANES:(h + 1) * LANES] = _dot(s, v)
            u_scr[h, n] = _dot(v.T, kend_h)

    for h in range(hpb):
        if has_s0:
            s_f, s_b = s0_ref[0, 0, 0].T, s0_ref[0, 1, 0].T
            if masks[h] is not None:
                s_f, s_b = jnp.where(masks[h], s_f, 0.0), jnp.where(masks[h], s_b, 0.0)
        else:
            s_f = s_b = jnp.zeros((LANES, LANES), F32)
        for n in range(nc):
            sin_scr[h, n, :, 0:LANES] = s_f
            s_f = s_f * dec_scr[n, 0:1, 0:LANES] + u_scr[h, n, :, 0:LANES]
        for n in reversed(range(nc)):
            sin_scr[h, n, :, LANES:2 * LANES] = s_b
            s_b = s_b * dec_scr[n, 0:1, LANES:2 * LANES] + u_scr[h, n, :, LANES:2 * LANES]
        if want_final:
            if h == 0:
                fin_f, fin_b = s_f, s_b
            else:
                fin_f, fin_b = fin_f + s_f, fin_b + s_b
    if want_final:
        sfin_ref[0, 0, 0] = fin_f.T
        sfin_ref[0, 1, 0] = fin_b.T

    gain = gain_ref[...]
    for n in range(nc):
        rows = pl.ds(n * C, C)
        qi = qi_scr[rows, :]
        for h in range(hpb):
            cols = slice(h * LANES, (h + 1) * LANES)
            o = o_scr[rows, cols] + _dot_nt(qi, sin_scr[h, n])
            y_ref[rows, cols] = (_rms(o, gain) * _silu(og_ref[rows, cols])).astype(y_ref.dtype)


def _scan_scratch(T, hpb):
    nc = T // SCAN_CHUNK
    return [
        pltpu.VMEM((T, 2 * LANES), F32),
        pltpu.VMEM((T, hpb * LANES), F32),
        pltpu.VMEM((hpb, nc, LANES, 2 * LANES), F32),
        pltpu.VMEM((nc, 8, 2 * LANES), F32),
        pltpu.VMEM((hpb, nc, LANES, 2 * LANES), F32),
    ]


def _gla_scan(proj, gdec, gain, s0, y_alias, row0, B, T, want_final):
    rb0 = row0 // T
    n_tok = proj.shape[0]
    has_s0 = s0 is not None

    def col(base, width=LANES):
        return lambda b, j: (rb0 + b, base // width + j)

    args = [proj, proj, proj, proj, gdec, gdec, gain.reshape(1, GLA_DV)]
    specs = [pl.BlockSpec((T, LANES), col(_GQ)), pl.BlockSpec((T, LANES), col(_GK)),
             pl.BlockSpec((T, 2 * LANES), col(_GV, 2 * LANES)), pl.BlockSpec((T, 2 * LANES), col(_GG, 2 * LANES)),
             pl.BlockSpec((T, LANES), col(0)), pl.BlockSpec((T, LANES), col(GLA_HEADS * GLA_DK)),
             pl.BlockSpec((1, GLA_DV), lambda b, j: (0, 0))]
    st_spec = pl.BlockSpec((1, 2, 1, LANES, LANES), lambda b, j: (b, 0, j, 0, 0))
    if has_s0:
        args.append(s0.reshape(B, 2, GLA_HEADS // 2, 2 * GLA_DK, GLA_DV))
        specs.append(st_spec)
    out_shapes = [jax.ShapeDtypeStruct((n_tok, GLA_HEADS * GLA_DV), BF16)]
    out_specs = [pl.BlockSpec((T, 2 * LANES), lambda b, j: (rb0 + b, j))]
    if want_final:
        out_shapes.append(jax.ShapeDtypeStruct((B, 2, GLA_HEADS // 2, 2 * GLA_DK, GLA_DV), F32))
        out_specs.append(st_spec)
    aliases = {}
    if y_alias is not None:
        args.append(y_alias)
        specs.append(pl.BlockSpec(memory_space=pl.ANY))
        aliases = {len(args) - 1: 0}
    kern = functools.partial(_gated_scan_kernel, variant="gla", T=T, has_s0=has_s0, want_final=want_final, hpb=2)
    if y_alias is not None:
        kern = _drop_alias_ref(kern, len(args) - 1)
    return pl.pallas_call(
        kern, grid=(B, GLA_HEADS // 2), in_specs=specs, out_specs=out_specs, out_shape=out_shapes,
        scratch_shapes=_scan_scratch(T, 2), input_output_aliases=aliases,
        compiler_params=_cparams(("arbitrary", "arbitrary")), name=f"gla_scan_T{T}",
    )(*args)


def _drop_alias_ref(kern, pos):
    def wrapped(*refs):
        return kern(*(refs[:pos] + refs[pos + 1:]))
    return wrapped


def _hgrn_scan(proj, lb_logits, gain, s0, y_alias, row0, B, T, want_final):
    rb0 = row0 // T
    n_tok = proj.shape[0]
    has_s0 = s0 is not None

    def col(base):
        return lambda b, j: (rb0 + b, base // LANES + j)

    args = [proj, proj, proj, proj, proj, lb_logits, gain.reshape(1, HG_DV)]
    specs = [pl.BlockSpec((T, LANES), col(_HQ)), pl.BlockSpec((T, LANES), col(_HFF)),
             pl.BlockSpec((T, LANES), col(_HFB)), pl.BlockSpec((T, LANES), col(_HI)),
             pl.BlockSpec((T, LANES), col(_HG)),
             pl.BlockSpec((DEPTH, 2, LANES), lambda b, j: (0, 0, j)),
             pl.BlockSpec((1, HG_DV), lambda b, j: (0, 0))]
    st_spec = pl.BlockSpec((1, 2, 1, LANES, LANES), lambda b, j: (b, 0, j, 0, 0))
    if has_s0:
        args.append(s0.reshape(B, 2, HG_HEADS, HG_DK, HG_DV))
        specs.append(st_spec)
    out_shapes = [jax.ShapeDtypeStruct((n_tok, HG_HEADS * HG_DV), BF16)]
    out_specs = [pl.BlockSpec((T, LANES), lambda b, j: (rb0 + b, j))]
    if want_final:
        out_shapes.append(jax.ShapeDtypeStruct((B, 2, HG_HEADS, HG_DK, HG_DV), F32))
        out_specs.append(st_spec)
    aliases = {}
    if y_alias is not None:
        args.append(y_alias)
        specs.append(pl.BlockSpec(memory_space=pl.ANY))
        aliases = {len(args) - 1: 0}
    kern = functools.partial(_gated_scan_kernel, variant="hgrn", T=T, has_s0=has_s0, want_final=want_final, hpb=1)
    if y_alias is not None:
        kern = _drop_alias_ref(kern, len(args) - 1)
    return pl.pallas_call(
        kern, grid=(B, HG_HEADS), in_specs=specs, out_specs=out_specs, out_shape=out_shapes,
        scratch_shapes=_scan_scratch(T, 1), input_output_aliases=aliases,
        compiler_params=_cparams(("arbitrary", "arbitrary")), name=f"hgrn_scan_T{T}",
    )(*args)


def _ret_scan_kernel(*refs, T, has_s0, want_final, rope):
    C = SCAN_CHUNK
    nc = T // C
    it = iter(refs)
    q_ref, k_ref, v_ref, og_ref, dexp_ref = (next(it) for _ in range(5))
    if rope:
        cos_ref, sin_ref = next(it), next(it)
    s0_ref = next(it) if has_s0 else None
    y_ref = next(it)
    sfin_ref = next(it) if want_final else None
    qi_scr, o_scr, u_scr, sin_scr = (next(it) for _ in range(4))

    lg = jnp.log1p(-jnp.exp2(-dexp_ref[0]))
    lg_f, lg_b = lg[0:1], lg[1:2]
    r = lax.broadcasted_iota(jnp.int32, (C, C), 0)
    s = lax.broadcasted_iota(jnp.int32, (C, C), 1)
    dist = (r - s).astype(F32)
    dmask = (jnp.where(r >= s, jnp.exp(jnp.maximum(dist, 0.0) * lg_f[:, 0:1]), 0.0)
             + jnp.where(r <= s, jnp.exp(jnp.maximum(-dist, 0.0) * lg_b[:, 0:1]), 0.0))
    pos = lax.broadcasted_iota(jnp.int32, (C, LANES), 0).astype(F32)
    xi = jnp.concatenate([jnp.exp((pos + 1.0) * lg_f), jnp.exp((C - pos) * lg_b)], axis=1)
    zeta = jnp.concatenate([jnp.exp((C - 1.0 - pos) * lg_f), jnp.exp(pos * lg_b)], axis=1)
    d_f, d_b = jnp.exp(C * lg_f), jnp.exp(C * lg_b)

    def rot(x, rows):
        if not rope:
            return x
        return x * cos_ref[rows, :] + pltpu.roll(x, RET_DK // 2, axis=1) * sin_ref[rows, :]

    for n in range(nc):
        rows = pl.ds(n * C, C)
        q = rot(q_ref[rows, :], rows)
        k = rot(k_ref[rows, :] * (RET_DK ** -0.5), rows)
        v = v_ref[rows, :]
        o_scr[rows, :] = _dot(_dot_nt(q, k) * dmask, v)
        qi_scr[rows, :] = jnp.concatenate([q, q], axis=1) * xi
        u_scr[n] = _dot(v.T, jnp.concatenate([k, k], axis=1) * zeta)

    if has_s0:
        s_f, s_b = s0_ref[0, 0, 0].T, s0_ref[0, 1, 0].T
    else:
        s_f = s_b = jnp.zeros((LANES, LANES), F32)
    for n in range(nc):
        sin_scr[n, :, 0:LANES] = s_f
        s_f = s_f * d_f + u_scr[n, :, 0:LANES]
    for n in reversed(range(nc)):
        sin_scr[n, :, LANES:2 * LANES] = s_b
        s_b = s_b * d_b + u_scr[n, :, LANES:2 * LANES]
    if want_final:
        sfin_ref[0, 0, 0] = s_f.T
        sfin_ref[0, 1, 0] = s_b.T

    for n in range(nc):
        rows = pl.ds(n * C, C)
        o = o_scr[rows, :] + _dot_nt(qi_scr[rows, :], sin_scr[n])
        y_ref[rows, :] = (_rms(o) * _silu(og_ref[rows, :])).astype(y_ref.dtype)


def _ret_scan(proj, dexp, rope_tabs, s0, y_alias, row0, B, T, want_final):
    rb0 = row0 // T
    n_tok = proj.shape[0]
    nc = T // SCAN_CHUNK
    has_s0 = s0 is not None
    rope = rope_tabs is not None

    def col(base):
        return lambda b, j: (rb0 + b, base // LANES + j)

    args = [proj, proj, proj, proj, dexp]
    specs = [pl.BlockSpec((T, LANES), col(_RQ)), pl.BlockSpec((T, LANES), col(_RK)),
             pl.BlockSpec((T, LANES), col(_RV)), pl.BlockSpec((T, LANES), col(_RG)),
             pl.BlockSpec((1, 2, LANES), lambda b, j: (j, 0, 0))]
    if rope:
        args += list(rope_tabs)
        specs += [pl.BlockSpec((T, LANES), lambda b, j: (0, 0))] * 2
    st_spec = pl.BlockSpec((1, 2, 1, LANES, LANES), lambda b, j: (b, 0, j, 0, 0))
    if has_s0:
        args.append(s0.reshape(B, 2, RET_HEADS, RET_DK, RET_DV))
        specs.append(st_spec)
    out_shapes = [jax.ShapeDtypeStruct((n_tok, RET_HEADS * RET_DV), BF16)]
    out_specs = [pl.BlockSpec((T, LANES), lambda b, j: (rb0 + b, j))]
    if want_final:
        out_shapes.append(jax.ShapeDtypeStruct((B, 2, RET_HEADS, RET_DK, RET_DV), F32))
        out_specs.append(st_spec)
    aliases = {}
    if y_alias is not None:
        args.append(y_alias)
        specs.append(pl.BlockSpec(memory_space=pl.ANY))
        aliases = {len(args) - 1: 0}
    kern = functools.partial(_ret_scan_kernel, T=T, has_s0=has_s0, want_final=want_final, rope=rope)
    if y_alias is not None:
        kern = _drop_alias_ref(kern, len(args) - 1)
    scratch = [pltpu.VMEM((T, 2 * LANES), F32), pltpu.VMEM((T, LANES), F32),
               pltpu.VMEM((nc, LANES, 2 * LANES), F32), pltpu.VMEM((nc, LANES, 2 * LANES), F32)]
    return pl.pallas_call(
        kern, grid=(B, RET_HEADS), in_specs=specs, out_specs=out_specs, out_shape=out_shapes,
        scratch_shapes=scratch, input_output_aliases=aliases,
        compiler_params=_cparams(("arbitrary", "arbitrary")), name=f"ret_scan_T{T}",
    )(*args)


def _out_proj_kernel(*refs, n_mix):
    it = iter(refs)
    y_refs = [next(it) for _ in range(n_mix)]
    wo_ref, x_ref, mod_ref, gain_ref, rw_ref, rb_ref = (next(it) for _ in range(6))
    xnew_ref, hn_ref, idx_ref, wt_ref = (next(it) for _ in range(4))
    D = D_MODEL
    mix = None
    r0 = 0
    for y_ref in y_refs:
        w = y_ref.shape[1]
        part = _dot(y_ref[...], wo_ref[r0:r0 + w, :])
        mix = part if mix is None else mix + part
        r0 += w
    m = mod_ref[0]
    xn = x_ref[...] + m[:, 2 * D:3 * D] * mix
    xnew_ref[...] = xn
    hn = _rms(xn, gain_ref[...]) * (1.0 + m[:, 4 * D:5 * D]) + m[:, 3 * D:4 * D]
    hn_hi = hn.astype(BF16)
    hn_ref[...] = hn_hi
    hn_lo = (hn - hn_hi.astype(F32)).astype(BF16)
    rw = rw_ref[...]
    rw_hi = rw.astype(BF16)
    rw_lo = (rw - rw_hi.astype(F32)).astype(BF16)
    logits = (_dot(hn_hi, rw_hi) + (_dot(hn_lo, rw_hi) + _dot(hn_hi, rw_lo))) + rb_ref[...]
    lane = lax.broadcasted_iota(jnp.int32, logits.shape, 1).astype(F32)
    vals, idxs = [], []
    cur = logits
    for _ in range(TOP_K):
        mx = jnp.max(cur, axis=-1, keepdims=True)
        ik = jnp.min(jnp.where(cur == mx, lane, float(LANES)), axis=-1, keepdims=True)
        vals.append(mx)
        idxs.append(ik)
        cur = jnp.where(lane == ik, -jnp.inf, cur)
    es = [jnp.exp(v - vals[0]) for v in vals]
    tot = (es[0] + es[1]) + (es[2] + es[3])
    idx_out = jnp.zeros(logits.shape, F32)
    wt_out = jnp.zeros(logits.shape, F32)
    for k in range(TOP_K):
        idx_out = jnp.where(lane == float(k), idxs[k], idx_out)
        wt_out = jnp.where(lane == float(k), es[k] / tot, wt_out)
    idx_ref[...] = idx_out.astype(jnp.int32)
    wt_ref[...] = wt_out


def _out_proj(layer, n_ctx, n_tok, lat_seq, ys, w_out, x_cur, mods, gain, rw_pad, rb_pad):
    tm = TOKEN_TILE
    nb, nb_ctx, tpl = n_tok // tm, n_ctx // tm, lat_seq // tm
    D = D_MODEL
    row = lambda i: (i, 0)
    specs = [pl.BlockSpec((tm, y.shape[1]), row) for y in ys]
    specs += [_resident((D, D)), pl.BlockSpec((tm, D), row),
              pl.BlockSpec((1, 1, 6 * D), lambda i: (layer * 8 + _cond_row(i, nb_ctx, tpl), 0, 0)),
              _resident((1, D)), _resident((D, LANES)), _resident((1, LANES))]
    return pl.pallas_call(
        functools.partial(_out_proj_kernel, n_mix=len(ys)),
        grid=(nb,),
        in_specs=specs,
        out_specs=[pl.BlockSpec((tm, D), row), pl.BlockSpec((tm, D), row),
                   pl.BlockSpec((tm, LANES), row), pl.BlockSpec((tm, LANES), row)],
        out_shape=[jax.ShapeDtypeStruct((n_tok, D), F32), jax.ShapeDtypeStruct((n_tok, D), BF16),
                   jax.ShapeDtypeStruct((n_tok, LANES), jnp.int32), jax.ShapeDtypeStruct((n_tok, LANES), F32)],
        compiler_params=_cparams(("arbitrary",)),
        name=f"out_proj_l{layer}",
    )(*ys, w_out, x_cur, mods, gain.reshape(1, D), rw_pad, rb_pad)


def _moe_kernel(te_ref, nv_ref, x_ref, gate_ref, w1_ref, b1_ref, w2_ref, b2_ref, o_ref):
    g = pl.program_id(0)

    @pl.when(g < nv_ref[0])
    def _():
        hid = _dot(x_ref[...], w1_ref[0]) + b1_ref[0]
        nxt = pltpu.roll(hid, 2 * D_FF - 1, axis=1)
        glu = jnp.minimum(hid, SWIGLU_LIMIT)
        lin = jnp.clip(nxt, -SWIGLU_LIMIT, SWIGLU_LIMIT)
        gate = gate_ref[...]
        act = (glu * _sigmoid(SWIGLU_ALPHA * glu) * (lin + 1.0)) * gate
        blk = 4 * LANES
        rr = lax.broadcasted_iota(jnp.int32, (blk, blk // 2), 0)
        cc = lax.broadcasted_iota(jnp.int32, (blk, blk // 2), 1)
        sel = (rr == 2 * cc).astype(BF16)
        parts = [_dot(act[:, c * blk:(c + 1) * blk].astype(BF16), sel) for c in range(2 * D_FF // blk)]
        act_c = jnp.concatenate(parts, axis=1)
        o_ref[...] = _dot(act_c, w2_ref[0]) + gate * b2_ref[0]

    @pl.when(g >= nv_ref[0])
    def _():
        o_ref[...] = jnp.zeros(o_ref.shape, o_ref.dtype)


def _moe_experts(n_tiles, tile_expert, n_valid, xs, gates, w1, b1, w2, b2):
    tm = MOE_TILE
    D, F2 = D_MODEL, 2 * D_FF
    grid_spec = pltpu.PrefetchScalarGridSpec(
        num_scalar_prefetch=2,
        grid=(n_tiles,),
        in_specs=[
            pl.BlockSpec((tm, D), lambda g, te, nv: (g, 0)),
            pl.BlockSpec((tm, 1), lambda g, te, nv: (g, 0)),
            pl.BlockSpec((1, D, F2), lambda g, te, nv: (te[g], 0, 0)),
            pl.BlockSpec((1, 1, F2), lambda g, te, nv: (te[g], 0, 0)),
            pl.BlockSpec((1, D_FF, D), lambda g, te, nv: (te[g], 0, 0)),
            pl.BlockSpec((1, 1, D), lambda g, te, nv: (te[g], 0, 0)),
        ],
        out_specs=pl.BlockSpec((tm, D), lambda g, te, nv: (g, 0)),
    )
    return pl.pallas_call(
        _moe_kernel,
        grid_spec=grid_spec,
        out_shape=jax.ShapeDtypeStruct((n_tiles * tm, D), F32),
        compiler_params=_cparams(("arbitrary",)),
        name="moe_experts",
    )(tile_expert, n_valid, xs, gates, w1, b1.reshape(N_EXPERTS, 1, F2), w2, b2.reshape(N_EXPERTS, 1, D))


def _moe_layer(hn_bf16, idx, wts, w1, b1, w2, b2):
    n_tok = hn_bf16.shape[0]
    tm = MOE_TILE
    n_pairs = n_tok * TOP_K
    n_tiles = n_pairs // tm + N_EXPERTS
    e_flat = idx.reshape(-1)
    order = jnp.argsort(e_flat, stable=True).astype(jnp.int32)
    counts = jnp.sum((e_flat[:, None] == jnp.arange(N_EXPERTS, dtype=jnp.int32)[None, :]).astype(jnp.int32), axis=0)
    tiles_per_e = (counts + tm - 1) // tm
    tile_end = jnp.cumsum(tiles_per_e)
    tile_start = tile_end - tiles_per_e
    cnt_start = jnp.cumsum(counts) - counts
    n_valid = tile_end[-1:]
    tile_ids = jnp.arange(n_tiles, dtype=jnp.int32)
    tile_expert = jnp.minimum(jnp.searchsorted(tile_end, tile_ids, side="right"), N_EXPERTS - 1).astype(jnp.int32)
    last_valid_expert = tile_expert[jnp.maximum(n_valid[0] - 1, 0)]
    tile_expert = jnp.where(tile_ids < n_valid[0], tile_expert, last_valid_expert)
    slot = jnp.arange(n_tiles * tm, dtype=jnp.int32)
    slot_e = jnp.repeat(tile_expert, tm)
    local = slot - jnp.take(tile_start, slot_e) * tm
    valid = (local < jnp.take(counts, slot_e)) & (slot < n_valid[0] * tm)
    sorted_rank = jnp.clip(jnp.take(cnt_start, slot_e) + local, 0, n_pairs - 1)
    slot_pair = jnp.take(order, sorted_rank)
    xs = jnp.take(hn_bf16, slot_pair // TOP_K, axis=0)
    gate = jnp.where(valid, jnp.take(wts.reshape(-1), slot_pair), 0.0).reshape(-1, 1)
    ys = _moe_experts(n_tiles, tile_expert, n_valid.astype(jnp.int32), xs, gate, w1, b1, w2, b2)
    rank_in_sorted = jnp.zeros((n_pairs,), jnp.int32).at[order].set(jnp.arange(n_pairs, dtype=jnp.int32))
    pos = jnp.take(tile_start, e_flat) * tm + (rank_in_sorted - jnp.take(cnt_start, e_flat))
    pos_kmajor = pos.reshape(n_tok, TOP_K).T.reshape(-1)
    return jnp.take(ys, pos_kmajor, axis=0)


def _final_kernel(*refs, nb_ctx):
    x_ref = refs[0]
    y_refs = refs[1:1 + TOP_K]
    mod_ref, gain_ref, oa_ref, ob_ref = refs[1 + TOP_K:]
    D = D_MODEL
    i = pl.program_id(0)
    g2 = mod_ref[0][:, 5 * D:6 * D]
    x = x_ref[...] + g2 * ((y_refs[0][...] + y_refs[1][...]) + (y_refs[2][...] + y_refs[3][...]))
    out = _rms(x, gain_ref[...])

    @pl.when(i < nb_ctx)
    def _():
        oa_ref[...] = out

    @pl.when(i >= nb_ctx)
    def _():
        ob_ref[...] = out


def _final(n_ctx, n_tok, lat_seq, x_new, y4, mods, gain):
    tm = TOKEN_TILE
    nb, nb_ctx, tpl = n_tok // tm, n_ctx // tm, lat_seq // tm
    D = D_MODEL
    specs = [pl.BlockSpec((tm, D), lambda i: (i, 0))]
    specs += [pl.BlockSpec((tm, D), (lambda k: (lambda i: (k * nb + i, 0)))(k)) for k in range(TOP_K)]
    specs += [pl.BlockSpec((1, 1, 6 * D), lambda i: ((DEPTH - 1) * 8 + _cond_row(i, nb_ctx, tpl), 0, 0)),
              _resident((1, D))]
    return pl.pallas_call(
        functools.partial(_final_kernel, nb_ctx=nb_ctx),
        grid=(nb,),
        in_specs=specs,
        out_specs=[pl.BlockSpec((tm, D), lambda i: (jnp.minimum(i, nb_ctx - 1), 0)),
                   pl.BlockSpec((tm, D), lambda i: (jnp.maximum(i - nb_ctx, 0), 0))],
        out_shape=[jax.ShapeDtypeStruct((n_ctx, D), F32), jax.ShapeDtypeStruct((n_tok - n_ctx, D), F32)],
        compiler_params=_cparams(("arbitrary",)),
        name="final_norm",
    )(x_new, *([y4] * TOP_K), mods, gain.reshape(1, D))


def _rope_tables(n_tokens):
    rows = n_tokens // GRID_W
    r = jnp.repeat(jnp.arange(rows, dtype=F32), GRID_W)
    col = jnp.tile(jnp.arange(GRID_W, dtype=F32), rows)
    nf = RET_DK // 4
    inv = ROPE_BASE ** (-jnp.arange(nf, dtype=F32) / nf)
    ang = jnp.concatenate([r[:, None] * inv, col[:, None] * inv], axis=-1)
    cos, sin = jnp.cos(ang), jnp.sin(ang)
    return jnp.concatenate([cos, cos], axis=-1), jnp.concatenate([-sin, sin], axis=-1)


def kernel(x_prompt, x_sample, c, state_gla, state_ret, state_hgrn, c_ctx, norm_mix, norm_ffn, ada_w, ada_b,
           w_in_even, w_out_even, gla_gk_w, gla_gk_b, gla_gain, ret_decay_exp, w_in_odd, w_out_odd,
           hgrn_lb_logits, hgrn_gain, router_w, router_b, moe_w1, moe_b1, moe_w2, moe_b2, final_norm):
    D = D_MODEL
    B_ctx, T_ctx, _ = x_prompt.shape
    B_lat, T_lat, _ = x_sample.shape
    n_ctx, n_lat = B_ctx * T_ctx, B_lat * T_lat
    n_tok = n_ctx + n_lat
    assert n_ctx % TOKEN_TILE == 0 and T_lat % TOKEN_TILE == 0 and B_lat + 1 <= 8
    assert T_ctx % SCAN_CHUNK == 0 and T_lat % SCAN_CHUNK == 0 and n_ctx % T_lat == 0

    cond8 = jnp.concatenate([c_ctx[None, :], c, jnp.zeros((8 - 1 - B_lat, D), F32)], axis=0)
    mods = _ada_mods(cond8, ada_w, ada_b).reshape(DEPTH * 8, 1, 6 * D)

    rw_pad = jnp.pad(router_w, ((0, 0), (0, 0), (0, LANES - N_EXPERTS)))
    rb_pad = jnp.pad(router_b, ((0, 0), (0, LANES - N_EXPERTS)), constant_values=-1e30)

    w_even = w_in_even[0]
    lr0 = GLA_HEADS * (2 * GLA_DK + GLA_DV)
    lr1 = lr0 + 2 * GLA_LOW_RANK
    w_main = jnp.concatenate([w_even[:, :lr0], w_even[:, lr1:]], axis=1)
    w_lr = jnp.pad(w_even[:, lr0:lr1], ((0, 0), (0, LANES - 2 * GLA_LOW_RANK)))
    nqk = GLA_HEADS * GLA_DK
    gkw = jnp.zeros((LANES, 2 * nqk), F32)
    gkw = gkw.at[0:GLA_LOW_RANK, 0:nqk].set(gla_gk_w[0, 0])
    gkw = gkw.at[GLA_LOW_RANK:2 * GLA_LOW_RANK, nqk:].set(gla_gk_w[0, 1])
    gkb = gla_gk_b[0].reshape(1, 2 * nqk)
    proj, gdec, x_cur = _in_proj(0, n_ctx, n_tok, T_lat, mods, norm_mix[0], w_main,
                                 x_parts=(x_prompt.reshape(n_ctx, D), x_sample.reshape(n_lat, D)),
                                 even_extra=(w_lr, gkw, gkb))

    y_gla, fin_gla = _gla_scan(proj, gdec, gla_gain[0], None, None, 0, B_ctx, T_ctx, True)
    (y_gla,) = _gla_scan(proj, gdec, gla_gain[0], state_gla[:, 0], y_gla, n_ctx, B_lat, T_lat, False)
    dexp = jnp.broadcast_to(ret_decay_exp[0].T[:, :, None], (RET_HEADS, 2, LANES))
    y_ret, fin_ret = _ret_scan(proj, dexp, None, None, None, 0, B_ctx, T_ctx, True)
    (y_ret,) = _ret_scan(proj, dexp, _rope_tables(T_lat), state_ret[:, 0], y_ret, n_ctx, B_lat, T_lat, False)

    x_new, hn, idx, wts = _out_proj(0, n_ctx, n_tok, T_lat, [y_gla, y_ret], w_out_even[0], x_cur, mods,
                                    norm_ffn[0], rw_pad[0], rb_pad[0:1])
    y4 = _moe_layer(hn, idx[:, :TOP_K], wts[:, :TOP_K], moe_w1[0], moe_b1[0], moe_w2[0], moe_b2[0])

    (proj, x_cur) = _in_proj(1, n_ctx, n_tok, T_lat, mods, norm_mix[1], w_in_odd[0], x_prev=x_new, y4=y4)
    y_h, fin_h = _hgrn_scan(proj, hgrn_lb_logits, hgrn_gain[0], None, None, 0, B_ctx, T_ctx, True)
    (y_h,) = _hgrn_scan(proj, hgrn_lb_logits, hgrn_gain[0], state_hgrn[:, 0], y_h, n_ctx, B_lat, T_lat, False)
    x_new, hn, idx, wts = _out_proj(1, n_ctx, n_tok, T_lat, [y_h], w_out_odd[0], x_cur, mods,
                                    norm_ffn[1], rw_pad[1], rb_pad[1:2])
    y4 = _moe_layer(hn, idx[:, :TOP_K], wts[:, :TOP_K], moe_w1[1], moe_b1[1], moe_w2[1], moe_b2[1])

    y_ctx, y_lat = _final(n_ctx, n_tok, T_lat, x_new, y4, mods, final_norm)

    new_state_gla = fin_gla.reshape(B_ctx, 1, 2, GLA_HEADS, GLA_DK, GLA_DV)
    new_state_ret = fin_ret.reshape(B_ctx, 1, 2, RET_HEADS, RET_DK, RET_DV)
    new_state_hgrn = fin_h.reshape(B_ctx, 1, 2, HG_HEADS, HG_DK, HG_DV)
    return (y_ctx.reshape(B_ctx, T_ctx, D), y_lat.reshape(B_lat, T_lat, D), new_state_gla, new_state_ret,
            new_state_hgrn)
```

```python
import functools

import jax
import jax.numpy as jnp
from jax import lax
from jax.experimental import pallas as pl
from jax.experimental.pallas import tpu as pltpu

F32 = jnp.float32
BF16 = jnp.bfloat16

D_MODEL = 1024
DEPTH = 2
GRID_W = 64
GLA_HEADS, GLA_DK, GLA_DV, GLA_LOW_RANK = 4, 64, 128, 16
GLA_NORMALIZER = 16.0
RET_HEADS, RET_DK, RET_DV = 4, 128, 128
ROPE_BASE = 10000.0
HG_HEADS, HG_DK, HG_DV = 8, 128, 128
N_EXPERTS, TOP_K, D_FF = 32, 4, 1024
SWIGLU_ALPHA, SWIGLU_LIMIT = 1.702, 7.0
EPS = 1e-6

LANES = 128
SCAN_CHUNK = 128
TOKEN_TILE = 256
MOE_TILE = 256
VMEM_LIMIT = 56 * 1024 * 1024

_GQ, _GK, _GV, _GG, _RQ, _RK, _RV, _RG, _EVEN_MAIN = 0, 256, 512, 1024, 1536, 2048, 2560, 3072, 3584
_HQ, _HFF, _HFB, _HI, _HG, _ODD_MAIN = 0, 1024, 2048, 3072, 4096, 5120


def _dot(a, b):
    return jnp.dot(a, b, preferred_element_type=F32)


def _dot_nt(a, b):
    return lax.dot_general(a, b, (((1,), (1,)), ((), ())), preferred_element_type=F32)


def _rms(x, gain=None):
    y = x * lax.rsqrt(jnp.mean(x * x, axis=-1, keepdims=True) + EPS)
    if gain is not None:
        y = y * gain
    return y


def _sigmoid(x):
    return 1.0 / (1.0 + jnp.exp(-x))


def _silu(x):
    return x * _sigmoid(x)


def _cparams(sem, vmem=VMEM_LIMIT):
    return pltpu.CompilerParams(dimension_semantics=sem, vmem_limit_bytes=vmem)


def _resident(shape):
    nd = len(shape)
    return pl.BlockSpec(shape, lambda *_: (0,) * nd, pipeline_mode=pl.Buffered(1))


def _ada_kernel(c_ref, w_ref, b_ref, o_ref):
    o_ref[0] = _dot(_silu(c_ref[...]), w_ref[0]) + b_ref[0]


def _ada_mods(cond8, ada_w, ada_b):
    tn = 1536
    return pl.pallas_call(
        _ada_kernel,
        grid=(DEPTH, 6 * D_MODEL // tn),
        in_specs=[
            pl.BlockSpec((8, D_MODEL), lambda l, j: (0, 0)),
            pl.BlockSpec((1, D_MODEL, tn), lambda l, j: (l, 0, j)),
            pl.BlockSpec((1, 1, tn), lambda l, j: (l, 0, j)),
        ],
        out_specs=pl.BlockSpec((1, 8, tn), lambda l, j: (l, 0, j)),
        out_shape=jax.ShapeDtypeStruct((DEPTH, 8, 6 * D_MODEL), F32),
        compiler_params=_cparams(("arbitrary", "arbitrary")),
        name="ada_mods",
    )(cond8, ada_w, ada_b.reshape(DEPTH, 1, 6 * D_MODEL))


def _in_proj_kernel(*refs, first, even, nb_ctx):
    it = iter(refs)
    if first:
        xa_ref, xb_ref = next(it), next(it)
    else:
        xp_ref = next(it)
        y_refs = [next(it) for _ in range(TOP_K)]
        modp_ref = next(it)
    gain_ref, mod_ref, w_ref = next(it), next(it), next(it)
    if even:
        wlr_ref, gkw_ref, gkb_ref = next(it), next(it), next(it)
    proj_ref = next(it)
    if even:
        g_ref = next(it)
    xcur_ref = next(it)

    D = D_MODEL
    i = pl.program_id(0)
    if first:
        x = jnp.where(i < nb_ctx, xa_ref[...], xb_ref[...])
    else:
        g2 = modp_ref[0][:, 5 * D:6 * D]
        x = xp_ref[...] + g2 * ((y_refs[0][...] + y_refs[1][...]) + (y_refs[2][...] + y_refs[3][...]))
    xcur_ref[...] = x
    m = mod_ref[0]
    hn = _rms(x, gain_ref[...]) * (1.0 + m[:, D:2 * D]) + m[:, 0:D]
    proj_ref[...] = _dot(hn, w_ref[...])
    if even:
        z = _dot(_dot(hn, wlr_ref[...]), gkw_ref[...]) + gkb_ref[...]
        g_ref[...] = (jnp.minimum(z, 0.0) - jnp.log(1.0 + jnp.exp(-jnp.abs(z)))) * (1.0 / GLA_NORMALIZER)


def _cond_row(i, nb_ctx, tiles_per_lat_seq):
    return jnp.where(i < nb_ctx, 0, 1 + (i - nb_ctx) // tiles_per_lat_seq)


def _in_proj(layer, n_ctx, n_tok, lat_seq, mods, gain, w_main, *, x_parts=None, x_prev=None, y4=None, even_extra=None):
    first = x_parts is not None
    even = even_extra is not None
    tm = TOKEN_TILE
    nb = n_tok // tm
    nb_ctx = n_ctx // tm
    tpl = lat_seq // tm
    np_cols = w_main.shape[1]
    D = D_MODEL

    def mod_map(l):
        return lambda i: (l * 8 + _cond_row(i, nb_ctx, tpl), 0, 0)

    row = lambda i: (i, 0)
    args, specs = [], []
    if first:
        xa, xb = x_parts
        args += [xa, xb]
        specs += [pl.BlockSpec((tm, D), lambda i: (jnp.minimum(i, nb_ctx - 1), 0)),
                  pl.BlockSpec((tm, D), lambda i: (jnp.maximum(i - nb_ctx, 0), 0))]
    else:
        args += [x_prev] + [y4] * TOP_K + [mods]
        specs += [pl.BlockSpec((tm, D), row)]
        specs += [pl.BlockSpec((tm, D), (lambda k: (lambda i: (k * nb + i, 0)))(k)) for k in range(TOP_K)]
        specs += [pl.BlockSpec((1, 1, 6 * D), mod_map(layer - 1))]
    args += [gain.reshape(1, D), mods, w_main]
    specs += [_resident((1, D)), pl.BlockSpec((1, 1, 6 * D), mod_map(layer)), _resident((D, np_cols))]
    out_shapes = [jax.ShapeDtypeStruct((n_tok, np_cols), F32)]
    out_specs = [pl.BlockSpec((tm, np_cols), row)]
    if even:
        w_lr, gkw, gkb = even_extra
        args += [w_lr, gkw, gkb]
        specs += [_resident(w_lr.shape), _resident(gkw.shape), _resident(gkb.shape)]
        out_shapes.append(jax.ShapeDtypeStruct((n_tok, 2 * GLA_HEADS * GLA_DK), F32))
        out_specs.append(pl.BlockSpec((tm, 2 * GLA_HEADS * GLA_DK), row))
    out_shapes.append(jax.ShapeDtypeStruct((n_tok, D), F32))
    out_specs.append(pl.BlockSpec((tm, D), row))
    return pl.pallas_call(
        functools.partial(_in_proj_kernel, first=first, even=even, nb_ctx=nb_ctx),
        grid=(nb,),
        in_specs=specs,
        out_specs=out_specs,
        out_shape=out_shapes,
        compiler_params=_cparams(("arbitrary",)),
        name=f"in_proj_l{layer}",
    )(*args)


def _tri(c, lower):
    r = lax.broadcasted_iota(jnp.int32, (c, c), 0)
    s = lax.broadcasted_iota(jnp.int32, (c, c), 1)
    return (r >= s) if lower else (r <= s)


def _cumsum_mm(tri_bf16, g):
    g_hi = g.astype(BF16)
    g_lo = (g - g_hi.astype(F32)).astype(BF16)
    r = _dot(tri_bf16, jnp.concatenate([g_hi, g_lo], axis=1))
    w = g.shape[1]
    return r[:, :w] + r[:, w:]


def _gated_scan_kernel(*refs, variant, T, has_s0, want_final, hpb):
    C = SCAN_CHUNK
    nc = T // C
    it = iter(refs)
    if variant == "gla":
        q_ref, k_ref, v_ref, og_ref, gf_ref, gb_ref, gain_ref = (next(it) for _ in range(7))
    else:
        q_ref, ff_ref, fb_ref, v_ref, og_ref, lbl_ref, gain_ref = (next(it) for _ in range(7))
    s0_ref = next(it) if has_s0 else None
    y_ref = next(it)
    sfin_ref = next(it) if want_final else None
    qi_scr, o_scr, u_scr, dec_scr, sin_scr = (next(it) for _ in range(5))

    lane = lax.broadcasted_iota(jnp.int32, (1, LANES), 1)
    if hpb == 2:
        masks = [lane < GLA_DK, lane >= GLA_DK]
    else:
        masks = [None]
    tri_l, tri_u = _tri(C, True), _tri(C, False)
    tri_l16, tri_u16 = tri_l.astype(F32).astype(BF16), tri_u.astype(F32).astype(BF16)
    mid_f, mid_b = C // 2 - 1, C // 2

    if variant == "hgrn":
        lgs = [lbl_ref[l] for l in range(DEPTH)]
        mx = functools.reduce(jnp.maximum, lgs)
        es = [jnp.exp(l - mx) for l in lgs]
        tot = functools.reduce(lambda a, b: a + b, es)
        ps = [e / tot for e in es]
        layer = DEPTH - 1
        lb = functools.reduce(lambda a, b: a + b, ps[:layer + 1]) - ps[0]
        lb_f, lb_b = lb[0:1], lb[1:2]

    for n in range(nc):
        rows = pl.ds(n * C, C)
        if variant == "gla":
            q = q_ref[rows, :] * (GLA_DK ** -0.5)
            kf = kb = k_ref[rows, :]
            gf, gb = gf_ref[rows, :], gb_ref[rows, :]
        else:
            q = _silu(q_ref[rows, :])
            f_f = lb_f + (1.0 - lb_f) * _sigmoid(ff_ref[rows, :])
            f_b = lb_b + (1.0 - lb_b) * _sigmoid(fb_ref[rows, :])
            kf, kb = 1.0 - f_f, 1.0 - f_b
            gf, gb = jnp.log(f_f), jnp.log(f_b)
        bf = _cumsum_mm(tri_l16, gf)
        bb = _cumsum_mm(tri_u16, gb)
        bf_mid, bf_end = bf[mid_f:mid_f + 1], bf[C - 1:C]
        bb_mid, bb_end = bb[mid_b:mid_b + 1], bb[0:1]
        qd_f, kd_f = q * jnp.exp(bf - bf_mid), kf * jnp.exp(bf_mid - bf)
        qd_b, kd_b = q * jnp.exp(bb - bb_mid), kb * jnp.exp(bb_mid - bb)
        kend = jnp.concatenate([kf * jnp.exp(bf_end - bf), kb * jnp.exp(bb_end - bb)], axis=1)
        qi_scr[rows, :] = jnp.concatenate([q * jnp.exp(bf), q * jnp.exp(bb)], axis=1)
        dec_scr[n] = jnp.broadcast_to(jnp.concatenate([jnp.exp(bf_end), jnp.exp(bb_end)], axis=1), (8, 2 * LANES))
        for h in range(hpb):
            v = v_ref[rows, h * LANES:(h + 1) * LANES]
            if masks[h] is None:
                qf_h, qb_h, kend_h = qd_f, qd_b, kend
            else:
                qf_h, qb_h = jnp.where(masks[h], qd_f, 0.0), jnp.where(masks[h], qd_b, 0.0)
                kend_h = jnp.where(jnp.concatenate([masks[h], masks[h]], axis=1), kend, 0.0)
            s = jnp.where(tri_l, _dot_nt(qf_h, kd_f), 0.0) + jnp.where(tri_u, _dot_nt(qb_h, kd_b), 0.0)
            o_scr[rows, h * LANES:(h + 1) * LANES] = _dot(s, v)
            u_scr[h, n] = _dot(v.T, kend_h)

    for h in range(hpb):
        if has_s0:
            s_f, s_b = s0_ref[0, 0, 0].T, s0_ref[0, 1, 0].T
            if masks[h] is not None:
                s_f, s_b = jnp.where(masks[h], s_f, 0.0), jnp.where(masks[h], s_b, 0.0)
        else:
            s_f = s_b = jnp.zeros((LANES, LANES), F32)
        for n in range(nc):
            sin_scr[h, n, :, 0:LANES] = s_f
            s_f = s_f * dec_scr[n, 0:1, 0:LANES] + u_scr[h, n, :, 0:LANES]
        for n in reversed(range(nc)):
            sin_scr[h, n, :, LANES:2 * LANES] = s_b
            s_b = s_b * dec_scr[n, 0:1, LANES:2 * LANES] + u_scr[h, n, :, LANES:2 * LANES]
        if want_final:
            if h == 0:
                fin_f, fin_b = s_f, s_b
            else:
                fin_f, fin_b = fin_f + s_f, fin_b + s_b
    if want_final:
        sfin_ref[0, 0, 0] = fin_f.T
        sfin_ref[0, 1, 0] = fin_b.T

    gain = gain_ref[...]
    for n in range(nc):
        rows = pl.ds(n * C, C)
        qi = qi_scr[rows, :]
        for h in range(hpb):
            cols = slice(h * LANES, (h + 1) * LANES)
            o = o_scr[rows, cols] + _dot_nt(qi, sin_scr[h, n])
            y_ref[rows, cols] = (_rms(o, gain) * _silu(og_ref[rows, cols])).astype(y_ref.dtype)


def _scan_scratch(T, hpb):
    nc = T // SCAN_CHUNK
    return [
        pltpu.VMEM((T, 2 * LANES), F32),
        pltpu.VMEM((T, hpb * LANES), F32),
        pltpu.VMEM((hpb, nc, LANES, 2 * LANES), F32),
        pltpu.VMEM((nc, 8, 2 * LANES), F32),
        pltpu.VMEM((hpb, nc, LANES, 2 * LANES), F32),
    ]


def _gla_scan(proj, gdec, gain, s0, y_alias, row0, B, T, want_final):
    rb0 = row0 // T
    n_tok = proj.shape[0]
    has_s0 = s0 is not None

    def col(base, width=LANES):
        return lambda b, j: (rb0 + b, base // width + j)

    args = [proj, proj, proj, proj, gdec, gdec, gain.reshape(1, GLA_DV)]
    specs = [pl.BlockSpec((T, LANES), col(_GQ)), pl.BlockSpec((T, LANES), col(_GK)),
             pl.BlockSpec((T, 2 * LANES), col(_GV, 2 * LANES)), pl.BlockSpec((T, 2 * LANES), col(_GG, 2 * LANES)),
             pl.BlockSpec((T, LANES), col(0)), pl.BlockSpec((T, LANES), col(GLA_HEADS * GLA_DK)),
             pl.BlockSpec((1, GLA_DV), lambda b, j: (0, 0))]
    st_spec = pl.BlockSpec((1, 2, 1, LANES, LANES), lambda b, j: (b, 0, j, 0, 0))
    if has_s0:
        args.append(s0.reshape(B, 2, GLA_HEADS // 2, 2 * GLA_DK, GLA_DV))
        specs.append(st_spec)
    out_shapes = [jax.ShapeDtypeStruct((n_tok, GLA_HEADS * GLA_DV), BF16)]
    out_specs = [pl.BlockSpec((T, 2 * LANES), lambda b, j: (rb0 + b, j))]
    if want_final:
        out_shapes.append(jax.ShapeDtypeStruct((B, 2, GLA_HEADS // 2, 2 * GLA_DK, GLA_DV), F32))
        out_specs.append(st_spec)
    aliases = {}
    if y_alias is not None:
        args.append(y_alias)
        specs.append(pl.BlockSpec(memory_space=pl.ANY))
        aliases = {len(args) - 1: 0}
    kern = functools.partial(_gated_scan_kernel, variant="gla", T=T, has_s0=has_s0, want_final=want_final, hpb=2)
    if y_alias is not None:
        kern = _drop_alias_ref(kern, len(args) - 1)
    return pl.pallas_call(
        kern, grid=(B, GLA_HEADS // 2), in_specs=specs, out_specs=out_specs, out_shape=out_shapes,
        scratch_shapes=_scan_scratch(T, 2), input_output_aliases=aliases,
        compiler_params=_cparams(("arbitrary", "arbitrary")), name=f"gla_scan_T{T}",
    )(*args)


def _drop_alias_ref(kern, pos):
    def wrapped(*refs):
        return kern(*(refs[:pos] + refs[pos + 1:]))
    return wrapped


def _hgrn_scan(proj, lb_logits, gain, s0, y_alias, row0, B, T, want_final):
    rb0 = row0 // T
    n_tok = proj.shape[0]
    has_s0 = s0 is not None

    def col(base):
        return lambda b, j: (rb0 + b, base // LANES + j)

    args = [proj, proj, proj, proj, proj, lb_logits, gain.reshape(1, HG_DV)]
    specs = [pl.BlockSpec((T, LANES), col(_HQ)), pl.BlockSpec((T, LANES), col(_HFF)),
             pl.BlockSpec((T, LANES), col(_HFB)), pl.BlockSpec((T, LANES), col(_HI)),
             pl.BlockSpec((T, LANES), col(_HG)),
             pl.BlockSpec((DEPTH, 2, LANES), lambda b, j: (0, 0, j)),
             pl.BlockSpec((1, HG_DV), lambda b, j: (0, 0))]
    st_spec = pl.BlockSpec((1, 2, 1, LANES, LANES), lambda b, j: (b, 0, j, 0, 0))
    if has_s0:
        args.append(s0.reshape(B, 2, HG_HEADS, HG_DK, HG_DV))
        specs.append(st_spec)
    out_shapes = [jax.ShapeDtypeStruct((n_tok, HG_HEADS * HG_DV), BF16)]
    out_specs = [pl.BlockSpec((T, LANES), lambda b, j: (rb0 + b, j))]
    if want_final:
        out_shapes.append(jax.ShapeDtypeStruct((B, 2, HG_HEADS, HG_DK, HG_DV), F32))
        out_specs.append(st_spec)
    aliases = {}
    if y_alias is not None:
        args.append(y_alias)
        specs.append(pl.BlockSpec(memory_space=pl.ANY))
        aliases = {len(args) - 1: 0}
    kern = functools.partial(_gated_scan_kernel, variant="hgrn", T=T, has_s0=has_s0, want_final=want_final, hpb=1)
    if y_alias is not None:
        kern = _drop_alias_ref(kern, len(args) - 1)
    return pl.pallas_call(
        kern, grid=(B, HG_HEADS), in_specs=specs, out_specs=out_specs, out_shape=out_shapes,
        scratch_shapes=_scan_scratch(T, 1), input_output_aliases=aliases,
        compiler_params=_cparams(("arbitrary", "arbitrary")), name=f"hgrn_scan_T{T}",
    )(*args)


def _ret_scan_kernel(*refs, T, has_s0, want_final, rope):
    C = SCAN_CHUNK
    nc = T // C
    it = iter(refs)
    q_ref, k_ref, v_ref, og_ref, dexp_ref = (next(it) for _ in range(5))
    if rope:
        cos_ref, sin_ref = next(it), next(it)
    s0_ref = next(it) if has_s0 else None
    y_ref = next(it)
    sfin_ref = next(it) if want_final else None
    qi_scr, o_scr, u_scr, sin_scr = (next(it) for _ in range(4))

    lg = jnp.log1p(-jnp.exp2(-dexp_ref[0]))
    lg_f, lg_b = lg[0:1], lg[1:2]
    r = lax.broadcasted_iota(jnp.int32, (C, C), 0)
    s = lax.broadcasted_iota(jnp.int32, (C, C), 1)
    dist = (r - s).astype(F32)
    dmask = (jnp.where(r >= s, jnp.exp(jnp.maximum(dist, 0.0) * lg_f[:, 0:1]), 0.0)
             + jnp.where(r <= s, jnp.exp(jnp.maximum(-dist, 0.0) * lg_b[:, 0:1]), 0.0))
    pos = lax.broadcasted_iota(jnp.int32, (C, LANES), 0).astype(F32)
    xi = jnp.concatenate([jnp.exp((pos + 1.0) * lg_f), jnp.exp((C - pos) * lg_b)], axis=1)
    zeta = jnp.concatenate([jnp.exp((C - 1.0 - pos) * lg_f), jnp.exp(pos * lg_b)], axis=1)
    d_f, d_b = jnp.exp(C * lg_f), jnp.exp(C * lg_b)

    def rot(x, rows):
        if not rope:
            return x
        return x * cos_ref[rows, :] + pltpu.roll(x, RET_DK // 2, axis=1) * sin_ref[rows, :]

    for n in range(nc):
        rows = pl.ds(n * C, C)
        q = rot(q_ref[rows, :], rows)
        k = rot(k_ref[rows, :] * (RET_DK ** -0.5), rows)
        v = v_ref[rows, :]
        o_scr[rows, :] = _dot(_dot_nt(q, k) * dmask, v)
        qi_scr[rows, :] = jnp.concatenate([q, q], axis=1) * xi
        u_scr[n] = _dot(v.T, jnp.concatenate([k, k], axis=1) * zeta)

    if has_s0:
        s_f, s_b = s0_ref[0, 0, 0].T, s0_ref[0, 1, 0].T
    else:
        s_f = s_b = jnp.zeros((LANES, LANES), F32)
    for n in range(nc):
        sin_scr[n, :, 0:LANES] = s_f
        s_f = s_f * d_f + u_scr[n, :, 0:LANES]
    for n in reversed(range(nc)):
        sin_scr[n, :, LANES:2 * LANES] = s_b
        s_b = s_b * d_b + u_scr[n, :, LANES:2 * LANES]
    if want_final:
        sfin_ref[0, 0, 0] = s_f.T
        sfin_ref[0, 1, 0] = s_b.T

    for n in range(nc):
        rows = pl.ds(n * C, C)
        o = o_scr[rows, :] + _dot_nt(qi_scr[rows, :], sin_scr[n])
        y_ref[rows, :] = (_rms(o) * _silu(og_ref[rows, :])).astype(y_ref.dtype)


def _ret_scan(proj, dexp, rope_tabs, s0, y_alias, row0, B, T, want_final):
    rb0 = row0 // T
    n_tok = proj.shape[0]
    nc = T // SCAN_CHUNK
    has_s0 = s0 is not None
    rope = rope_tabs is not None

    def col(base):
        return lambda b, j: (rb0 + b, base // LANES + j)

    args = [proj, proj, proj, proj, dexp]
    specs = [pl.BlockSpec((T, LANES), col(_RQ)), pl.BlockSpec((T, LANES), col(_RK)),
             pl.BlockSpec((T, LANES), col(_RV)), pl.BlockSpec((T, LANES), col(_RG)),
             pl.BlockSpec((1, 2, LANES), lambda b, j: (j, 0, 0))]
    if rope:
        args += list(rope_tabs)
        specs += [pl.BlockSpec((T, LANES), lambda b, j: (0, 0))] * 2
    st_spec = pl.BlockSpec((1, 2, 1, LANES, LANES), lambda b, j: (b, 0, j, 0, 0))
    if has_s0:
        args.append(s0.reshape(B, 2, RET_HEADS, RET_DK, RET_DV))
        specs.append(st_spec)
    out_shapes = [jax.ShapeDtypeStruct((n_tok, RET_HEADS * RET_DV), BF16)]
    out_specs = [pl.BlockSpec((T, LANES), lambda b, j: (rb0 + b, j))]
    if want_final:
        out_shapes.append(jax.ShapeDtypeStruct((B, 2, RET_HEADS, RET_DK, RET_DV), F32))
        out_specs.append(st_spec)
    aliases = {}
    if y_alias is not None:
        args.append(y_alias)
        specs.append(pl.BlockSpec(memory_space=pl.ANY))
        aliases = {len(args) - 1: 0}
    kern = functools.partial(_ret_scan_kernel, T=T, has_s0=has_s0, want_final=want_final, rope=rope)
    if y_alias is not None:
        kern = _drop_alias_ref(kern, len(args) - 1)
    scratch = [pltpu.VMEM((T, 2 * LANES), F32), pltpu.VMEM((T, LANES), F32),
               pltpu.VMEM((nc, LANES, 2 * LANES), F32), pltpu.VMEM((nc, LANES, 2 * LANES), F32)]
    return pl.pallas_call(
        kern, grid=(B, RET_HEADS), in_specs=specs, out_specs=out_specs, out_shape=out_shapes,
        scratch_shapes=scratch, input_output_aliases=aliases,
        compiler_params=_cparams(("arbitrary", "arbitrary")), name=f"ret_scan_T{T}",
    )(*args)


def _out_proj_kernel(*refs, n_mix):
    it = iter(refs)
    y_refs = [next(it) for _ in range(n_mix)]
    wo_ref, x_ref, mod_ref, gain_ref, rw_ref, rb_ref = (next(it) for _ in range(6))
    xnew_ref, hn_ref, idx_ref, wt_ref, rank_ref, cnt_ref = (next(it) for _ in range(6))
    cnt_scr = next(it)
    D = D_MODEL
    mix = None
    r0 = 0
    for y_ref in y_refs:
        w = y_ref.shape[1]
        part = _dot(y_ref[...], wo_ref[r0:r0 + w, :])
        mix = part if mix is None else mix + part
        r0 += w
    m = mod_ref[0]
    xn = x_ref[...] + m[:, 2 * D:3 * D] * mix
    xnew_ref[...] = xn
    hn = _rms(xn, gain_ref[...]) * (1.0 + m[:, 4 * D:5 * D]) + m[:, 3 * D:4 * D]
    hn_hi = hn.astype(BF16)
    hn_ref[...] = hn_hi
    hn_lo = (hn - hn_hi.astype(F32)).astype(BF16)
    rw = rw_ref[...]
    rw_hi = rw.astype(BF16)
    rw_lo = (rw - rw_hi.astype(F32)).astype(BF16)
    logits = (_dot(hn_hi, rw_hi) + (_dot(hn_lo, rw_hi) + _dot(hn_hi, rw_lo))) + rb_ref[...]
    lane = lax.broadcasted_iota(jnp.int32, logits.shape, 1).astype(F32)
    vals, idxs = [], []
    cur = logits
    for _ in range(TOP_K):
        mx = jnp.max(cur, axis=-1, keepdims=True)
        ik = jnp.min(jnp.where(cur == mx, lane, float(LANES)), axis=-1, keepdims=True)
        vals.append(mx)
        idxs.append(ik)
        cur = jnp.where(lane == ik, -jnp.inf, cur)
    es = [jnp.exp(v - vals[0]) for v in vals]
    tot = (es[0] + es[1]) + (es[2] + es[3])
    idx_out = jnp.zeros(logits.shape, F32)
    wt_out = jnp.zeros(logits.shape, F32)
    for k in range(TOP_K):
        idx_out = jnp.where(lane == float(k), idxs[k], idx_out)
        wt_out = jnp.where(lane == float(k), es[k] / tot, wt_out)
    idx_ref[...] = idx_out.astype(jnp.int32)
    wt_ref[...] = wt_out

    @pl.when(pl.program_id(0) == 0)
    def _():
        cnt_scr[...] = jnp.zeros(cnt_scr.shape, F32)

    tm = logits.shape[0]
    hits = [lane == idxs[k] for k in range(TOP_K)]
    sel = jnp.zeros(logits.shape, F32)
    for k in range(TOP_K):
        sel = sel + jnp.where(hits[k], 1.0, 0.0)
    rr = lax.broadcasted_iota(jnp.int32, (tm, tm), 0)
    cc = lax.broadcasted_iota(jnp.int32, (tm, tm), 1)
    before = jnp.where(rr > cc, 1.0, 0.0).astype(BF16)
    rank_all = cnt_scr[0:1, :] + _dot(before, sel.astype(BF16))
    rank_out = jnp.zeros(logits.shape, F32)
    for k in range(TOP_K):
        rk = jnp.sum(jnp.where(hits[k], rank_all, 0.0), axis=-1, keepdims=True)
        rank_out = jnp.where(lane == float(k), rk, rank_out)
    rank_ref[...] = rank_out.astype(jnp.int32)
    total = cnt_scr[...] + jnp.sum(sel, axis=0, keepdims=True)
    cnt_scr[...] = total
    cnt_ref[...] = total.astype(jnp.int32)


def _out_proj(layer, n_ctx, n_tok, lat_seq, ys, w_out, x_cur, mods, gain, rw_pad, rb_pad):
    tm = TOKEN_TILE
    nb, nb_ctx, tpl = n_tok // tm, n_ctx // tm, lat_seq // tm
    D = D_MODEL
    row = lambda i: (i, 0)
    specs = [pl.BlockSpec((tm, y.shape[1]), row) for y in ys]
    specs += [_resident((D, D)), pl.BlockSpec((tm, D), row),
              pl.BlockSpec((1, 1, 6 * D), lambda i: (layer * 8 + _cond_row(i, nb_ctx, tpl), 0, 0)),
              _resident((1, D)), _resident((D, LANES)), _resident((1, LANES))]
    return pl.pallas_call(
        functools.partial(_out_proj_kernel, n_mix=len(ys)),
        grid=(nb,),
        in_specs=specs,
        out_specs=[pl.BlockSpec((tm, D), row), pl.BlockSpec((tm, D), row),
                   pl.BlockSpec((tm, LANES), row), pl.BlockSpec((tm, LANES), row),
                   pl.BlockSpec((tm, LANES), row), pl.BlockSpec((8, LANES), lambda i: (0, 0))],
        out_shape=[jax.ShapeDtypeStruct((n_tok, D), F32), jax.ShapeDtypeStruct((n_tok, D), BF16),
                   jax.ShapeDtypeStruct((n_tok, LANES), jnp.int32), jax.ShapeDtypeStruct((n_tok, LANES), F32),
                   jax.ShapeDtypeStruct((n_tok, LANES), jnp.int32), jax.ShapeDtypeStruct((8, LANES), jnp.int32)],
        scratch_shapes=[pltpu.VMEM((8, LANES), F32)],
        compiler_params=_cparams(("arbitrary",)),
        name=f"out_proj_l{layer}",
    )(*ys, w_out, x_cur, mods, gain.reshape(1, D), rw_pad, rb_pad)


W1_SPLIT = 4
W2_SPLIT = 2


def _moe_kernel(te_ref, nv_ref, x_ref, gate_ref, *refs):
    w1_refs = refs[:W1_SPLIT]
    b1_ref = refs[W1_SPLIT]
    w2_refs = refs[W1_SPLIT + 1:W1_SPLIT + 1 + W2_SPLIT]
    b2_ref, o_ref, w1_scr, w2_scr = refs[W1_SPLIT + 1 + W2_SPLIT:]
    g = pl.program_id(0)
    valid = g < nv_ref[0]
    new_expert = jnp.logical_or(g == 0, te_ref[g] != te_ref[jnp.maximum(g - 1, 0)])

    @pl.when(jnp.logical_and(valid, new_expert))
    def _():
        cw = 2 * D_FF // W1_SPLIT
        for c, w_ref in enumerate(w1_refs):
            w1_scr[:, c * cw:(c + 1) * cw] = w_ref[0, 0].astype(BF16)
        rh = D_FF // W2_SPLIT
        for c, w_ref in enumerate(w2_refs):
            w2_scr[c * rh:(c + 1) * rh, :] = w_ref[0, 0].astype(BF16)

    @pl.when(valid)
    def _():
        hid = _dot(x_ref[...], w1_scr[...]) + b1_ref[0, 0]
        nxt = pltpu.roll(hid, 2 * D_FF - 1, axis=1)
        glu = jnp.minimum(hid, SWIGLU_LIMIT)
        lin = jnp.clip(nxt, -SWIGLU_LIMIT, SWIGLU_LIMIT)
        gate = gate_ref[...]
        act = (glu * _sigmoid(SWIGLU_ALPHA * glu) * (lin + 1.0)) * gate
        blk = 4 * LANES
        rr = lax.broadcasted_iota(jnp.int32, (blk, blk // 2), 0)
        cc = lax.broadcasted_iota(jnp.int32, (blk, blk // 2), 1)
        sel = (rr == 2 * cc).astype(BF16)
        parts = [_dot(act[:, c * blk:(c + 1) * blk].astype(BF16), sel) for c in range(2 * D_FF // blk)]
        act_c = jnp.concatenate(parts, axis=1)
        o_ref[...] = _dot(act_c, w2_scr[...]) + gate * b2_ref[0, 0]

    @pl.when(jnp.logical_not(valid))
    def _():
        o_ref[...] = jnp.zeros(o_ref.shape, o_ref.dtype)


def _moe_experts(layer, n_tiles, tile_expert, n_valid, xs, gates, w1, b1, w2, b2):
    tm = MOE_TILE
    D, F2 = D_MODEL, 2 * D_FF
    cw, rh = F2 // W1_SPLIT, D_FF // W2_SPLIT
    w1_specs = [pl.BlockSpec((1, 1, D, cw), (lambda c: (lambda g, te, nv: (layer, te[g], 0, c)))(c))
                for c in range(W1_SPLIT)]
    w2_specs = [pl.BlockSpec((1, 1, rh, D), (lambda c: (lambda g, te, nv: (layer, te[g], c, 0)))(c))
                for c in range(W2_SPLIT)]
    grid_spec = pltpu.PrefetchScalarGridSpec(
        num_scalar_prefetch=2,
        grid=(n_tiles,),
        in_specs=[pl.BlockSpec((tm, D), lambda g, te, nv: (g, 0)),
                  pl.BlockSpec((tm, 1), lambda g, te, nv: (g, 0))]
        + w1_specs + [pl.BlockSpec((1, 1, 1, F2), lambda g, te, nv: (layer, te[g], 0, 0))]
        + w2_specs + [pl.BlockSpec((1, 1, 1, D), lambda g, te, nv: (layer, te[g], 0, 0))],
        out_specs=pl.BlockSpec((tm, D), lambda g, te, nv: (g, 0)),
        scratch_shapes=[pltpu.VMEM((D, F2), BF16), pltpu.VMEM((D_FF, D), BF16)],
    )
    return pl.pallas_call(
        _moe_kernel,
        grid_spec=grid_spec,
        out_shape=jax.ShapeDtypeStruct((n_tiles * tm, D), F32),
        compiler_params=_cparams(("arbitrary",)),
        name="moe_experts",
    )(tile_expert, n_valid, xs, gates, *([w1] * W1_SPLIT), b1.reshape(DEPTH, N_EXPERTS, 1, F2),
      *([w2] * W2_SPLIT), b2.reshape(DEPTH, N_EXPERTS, 1, D))


def _moe_layer(layer, hn_bf16, idx, wts, rank, counts, w1, b1, w2, b2):
    n_tok = hn_bf16.shape[0]
    tm = MOE_TILE
    n_pairs = n_tok * TOP_K
    n_tiles = n_pairs // tm + N_EXPERTS
    experts = jnp.arange(N_EXPERTS, dtype=jnp.int32)
    tiles_per_e = (counts + tm - 1) // tm
    tile_end = jnp.cumsum(tiles_per_e)
    tile_start = tile_end - tiles_per_e
    n_valid = tile_end[-1]
    tile_ids = jnp.arange(n_tiles, dtype=jnp.int32)
    tile_expert = jnp.sum((tile_end[None, :] <= jnp.minimum(tile_ids, n_valid - 1)[:, None]).astype(jnp.int32), axis=1)
    tile_expert = jnp.minimum(tile_expert, N_EXPERTS - 1)
    seg_start = jnp.sum(jnp.where(idx[:, :, None] == experts[None, None, :], (tile_start * tm)[None, None, :], 0), axis=-1)
    pos = seg_start + rank
    pair_ids = jnp.arange(n_pairs, dtype=jnp.int32).reshape(n_tok, TOP_K)
    packed = jnp.stack([pair_ids, lax.bitcast_convert_type(wts, jnp.int32)], axis=-1).reshape(n_pairs, 2)
    slot_tab = jnp.zeros((n_tiles * tm, 2), jnp.int32).at[pos.reshape(-1)].set(packed)
    slot_tok = slot_tab[:, 0] // TOP_K
    gate = lax.bitcast_convert_type(slot_tab[:, 1], F32).reshape(-1, 1)
    xs = jnp.take(hn_bf16, slot_tok, axis=0)
    ys = _moe_experts(layer, n_tiles, tile_expert, n_valid.reshape(1), xs, gate, w1, b1, w2, b2)
    return jnp.take(ys, pos.T.reshape(-1), axis=0)


def _final_kernel(*refs, nb_ctx):
    x_ref = refs[0]
    y_refs = refs[1:1 + TOP_K]
    mod_ref, gain_ref, oa_ref, ob_ref = refs[1 + TOP_K:]
    D = D_MODEL
    i = pl.program_id(0)
    g2 = mod_ref[0][:, 5 * D:6 * D]
    x = x_ref[...] + g2 * ((y_refs[0][...] + y_refs[1][...]) + (y_refs[2][...] + y_refs[3][...]))
    out = _rms(x, gain_ref[...])

    @pl.when(i < nb_ctx)
    def _():
        oa_ref[...] = out

    @pl.when(i >= nb_ctx)
    def _():
        ob_ref[...] = out


def _final(n_ctx, n_tok, lat_seq, x_new, y4, mods, gain):
    tm = TOKEN_TILE
    nb, nb_ctx, tpl = n_tok // tm, n_ctx // tm, lat_seq // tm
    D = D_MODEL
    specs = [pl.BlockSpec((tm, D), lambda i: (i, 0))]
    specs += [pl.BlockSpec((tm, D), (lambda k: (lambda i: (k * nb + i, 0)))(k)) for k in range(TOP_K)]
    specs += [pl.BlockSpec((1, 1, 6 * D), lambda i: ((DEPTH - 1) * 8 + _cond_row(i, nb_ctx, tpl), 0, 0)),
              _resident((1, D))]
    return pl.pallas_call(
        functools.partial(_final_kernel, nb_ctx=nb_ctx),
        grid=(nb,),
        in_specs=specs,
        out_specs=[pl.BlockSpec((tm, D), lambda i: (jnp.minimum(i, nb_ctx - 1), 0)),
                   pl.BlockSpec((tm, D), lambda i: (jnp.maximum(i - nb_ctx, 0), 0))],
        out_shape=[jax.ShapeDtypeStruct((n_ctx, D), F32), jax.ShapeDtypeStruct((n_tok - n_ctx, D), F32)],
        compiler_params=_cparams(("arbitrary",)),
        name="final_norm",
    )(x_new, *([y4] * TOP_K), mods, gain.reshape(1, D))


def _rope_tables(n_tokens):
    rows = n_tokens // GRID_W
    r = jnp.repeat(jnp.arange(rows, dtype=F32), GRID_W)
    col = jnp.tile(jnp.arange(GRID_W, dtype=F32), rows)
    nf = RET_DK // 4
    inv = ROPE_BASE ** (-jnp.arange(nf, dtype=F32) / nf)
    ang = jnp.concatenate([r[:, None] * inv, col[:, None] * inv], axis=-1)
    cos, sin = jnp.cos(ang), jnp.sin(ang)
    return jnp.concatenate([cos, cos], axis=-1), jnp.concatenate([-sin, sin], axis=-1)


def kernel(x_prompt, x_sample, c, state_gla, state_ret, state_hgrn, c_ctx, norm_mix, norm_ffn, ada_w, ada_b,
           w_in_even, w_out_even, gla_gk_w, gla_gk_b, gla_gain, ret_decay_exp, w_in_odd, w_out_odd,
           hgrn_lb_logits, hgrn_gain, router_w, router_b, moe_w1, moe_b1, moe_w2, moe_b2, final_norm):
    D = D_MODEL
    B_ctx, T_ctx, _ = x_prompt.shape
    B_lat, T_lat, _ = x_sample.shape
    n_ctx, n_lat = B_ctx * T_ctx, B_lat * T_lat
    n_tok = n_ctx + n_lat
    assert n_ctx % TOKEN_TILE == 0 and T_lat % TOKEN_TILE == 0 and B_lat + 1 <= 8
    assert T_ctx % SCAN_CHUNK == 0 and T_lat % SCAN_CHUNK == 0 and n_ctx % T_lat == 0

    cond8 = jnp.concatenate([c_ctx[None, :], c, jnp.zeros((8 - 1 - B_lat, D), F32)], axis=0)
    mods = _ada_mods(cond8, ada_w, ada_b).reshape(DEPTH * 8, 1, 6 * D)

    rw_pad = jnp.pad(router_w, ((0, 0), (0, 0), (0, LANES - N_EXPERTS)))
    rb_pad = jnp.pad(router_b, ((0, 0), (0, LANES - N_EXPERTS)), constant_values=-1e30)

    w_even = w_in_even[0]
    lr0 = GLA_HEADS * (2 * GLA_DK + GLA_DV)
    lr1 = lr0 + 2 * GLA_LOW_RANK
    w_main = jnp.concatenate([w_even[:, :lr0], w_even[:, lr1:]], axis=1)
    w_lr = jnp.pad(w_even[:, lr0:lr1], ((0, 0), (0, LANES - 2 * GLA_LOW_RANK)))
    nqk = GLA_HEADS * GLA_DK
    gkw = jnp.zeros((LANES, 2 * nqk), F32)
    gkw = gkw.at[0:GLA_LOW_RANK, 0:nqk].set(gla_gk_w[0, 0])
    gkw = gkw.at[GLA_LOW_RANK:2 * GLA_LOW_RANK, nqk:].set(gla_gk_w[0, 1])
    gkb = gla_gk_b[0].reshape(1, 2 * nqk)
    proj, gdec, x_cur = _in_proj(0, n_ctx, n_tok, T_lat, mods, norm_mix[0], w_main,
                                 x_parts=(x_prompt.reshape(n_ctx, D), x_sample.reshape(n_lat, D)),
                                 even_extra=(w_lr, gkw, gkb))

    y_gla, fin_gla = _gla_scan(proj, gdec, gla_gain[0], None, None, 0, B_ctx, T_ctx, True)
    (y_gla,) = _gla_scan(proj, gdec, gla_gain[0], state_gla[:, 0], y_gla, n_ctx, B_lat, T_lat, False)
    dexp = jnp.broadcast_to(ret_decay_exp[0].T[:, :, None], (RET_HEADS, 2, LANES))
    y_ret, fin_ret = _ret_scan(proj, dexp, None, None, None, 0, B_ctx, T_ctx, True)
    (y_ret,) = _ret_scan(proj, dexp, _rope_tables(T_lat), state_ret[:, 0], y_ret, n_ctx, B_lat, T_lat, False)

    x_new, hn, idx, wts, rank, cnt = _out_proj(0, n_ctx, n_tok, T_lat, [y_gla, y_ret], w_out_even[0], x_cur, mods,
                                               norm_ffn[0], rw_pad[0], rb_pad[0:1])
    y4 = _moe_layer(0, hn, idx[:, :TOP_K], wts[:, :TOP_K], rank[:, :TOP_K], cnt[0, :N_EXPERTS],
                    moe_w1, moe_b1, moe_w2, moe_b2)

    (proj, x_cur) = _in_proj(1, n_ctx, n_tok, T_lat, mods, norm_mix[1], w_in_odd[0], x_prev=x_new, y4=y4)
    y_h, fin_h = _hgrn_scan(proj, hgrn_lb_logits, hgrn_gain[0], None, None, 0, B_ctx, T_ctx, True)
    (y_h,) = _hgrn_scan(proj, hgrn_lb_logits, hgrn_gain[0], state_hgrn[:, 0], y_h, n_ctx, B_lat, T_lat, False)
    x_new, hn, idx, wts, rank, cnt = _out_proj(1, n_ctx, n_tok, T_lat, [y_h], w_out_odd[0], x_cur, mods,
                                               norm_ffn[1], rw_pad[1], rb_pad[1:2])
    y4 = _moe_layer(1, hn, idx[:, :TOP_K], wts[:, :TOP_K], rank[:, :TOP_K], cnt[0, :N_EXPERTS],
                    moe_w1, moe_b1, moe_w2, moe_b2)

    y_ctx, y_lat = _final(n_ctx, n_tok, T_lat, x_new, y4, mods, final_norm)

    new_state_gla = fin_gla.reshape(B_ctx, 1, 2, GLA_HEADS, GLA_DK, GLA_DV)
    new_state_ret = fin_ret.reshape(B_ctx, 1, 2, RET_HEADS, RET_DK, RET_DV)
    new_state_hgrn = fin_h.reshape(B_ctx, 1, 2, HG_HEADS, HG_DK, HG_DV)
    return (y_ctx.reshape(B_ctx, T_ctx, D), y_lat.reshape(B_lat, T_lat, D), new_state_gla, new_state_ret,
            new_state_hgrn)
```

```python
import functools

import jax
import jax.numpy as jnp
from jax import lax
from jax.experimental import pallas as pl
from jax.experimental.pallas import tpu as pltpu

F32 = jnp.float32
BF16 = jnp.bfloat16

D_MODEL = 1024
DEPTH = 2
GRID_W = 64
GLA_HEADS, GLA_DK, GLA_DV, GLA_LOW_RANK = 4, 64, 128, 16
GLA_NORMALIZER = 16.0
RET_HEADS, RET_DK, RET_DV = 4, 128, 128
ROPE_BASE = 10000.0
HG_HEADS, HG_DK, HG_DV = 8, 128, 128
N_EXPERTS, TOP_K, D_FF = 32, 4, 1024
SWIGLU_ALPHA, SWIGLU_LIMIT = 1.702, 7.0
EPS = 1e-6

LANES = 128
SCAN_CHUNK = 128
TOKEN_TILE = 256
MOE_TILE = 256
VMEM_LIMIT = 56 * 1024 * 1024

_GQ, _GK, _GV, _GG, _RQ, _RK, _RV, _RG, _EVEN_MAIN = 0, 256, 512, 1024, 1536, 2048, 2560, 3072, 3584
_HQ, _HFF, _HFB, _HI, _HG, _ODD_MAIN = 0, 1024, 2048, 3072, 4096, 5120


def _dot(a, b):
    return jnp.dot(a, b, preferred_element_type=F32)


def _dot_nt(a, b):
    return lax.dot_general(a, b, (((1,), (1,)), ((), ())), preferred_element_type=F32)


def _rms(x, gain=None):
    y = x * lax.rsqrt(jnp.mean(x * x, axis=-1, keepdims=True) + EPS)
    if gain is not None:
        y = y * gain
    return y


def _sigmoid(x):
    return 1.0 / (1.0 + jnp.exp(-x))


def _silu(x):
    return x * _sigmoid(x)


def _cparams(sem, vmem=VMEM_LIMIT):
    return pltpu.CompilerParams(dimension_semantics=sem, vmem_limit_bytes=vmem)


def _resident(shape):
    nd = len(shape)
    return pl.BlockSpec(shape, lambda *_: (0,) * nd, pipeline_mode=pl.Buffered(1))


def _ada_kernel(c_ref, w_ref, b_ref, o_ref):
    o_ref[0] = _dot(_silu(c_ref[...]), w_ref[0]) + b_ref[0]


def _ada_mods(cond8, ada_w, ada_b):
    tn = 1536
    return pl.pallas_call(
        _ada_kernel,
        grid=(DEPTH, 6 * D_MODEL // tn),
        in_specs=[
            pl.BlockSpec((8, D_MODEL), lambda l, j: (0, 0)),
            pl.BlockSpec((1, D_MODEL, tn), lambda l, j: (l, 0, j)),
            pl.BlockSpec((1, 1, tn), lambda l, j: (l, 0, j)),
        ],
        out_specs=pl.BlockSpec((1, 8, tn), lambda l, j: (l, 0, j)),
        out_shape=jax.ShapeDtypeStruct((DEPTH, 8, 6 * D_MODEL), F32),
        compiler_params=_cparams(("arbitrary", "arbitrary")),
        name="ada_mods",
    )(cond8, ada_w, ada_b.reshape(DEPTH, 1, 6 * D_MODEL))


def _moe_combine(y_refs, wt_ref):
    w = wt_ref[...]
    terms = [w[:, k:k + 1] * y_refs[k][...] for k in range(TOP_K)]
    return (terms[0] + terms[1]) + (terms[2] + terms[3])


def _in_proj_kernel(*refs, first, even, nb_ctx):
    it = iter(refs)
    if first:
        xa_ref, xb_ref = next(it), next(it)
    else:
        xp_ref = next(it)
        y_refs = [next(it) for _ in range(TOP_K)]
        wt_ref, modp_ref = next(it), next(it)
    gain_ref, mod_ref, w_ref = next(it), next(it), next(it)
    if even:
        wlr_ref, gkw_ref, gkb_ref = next(it), next(it), next(it)
    proj_ref = next(it)
    if even:
        g_ref = next(it)
    xcur_ref = next(it)

    D = D_MODEL
    i = pl.program_id(0)
    if first:
        x = jnp.where(i < nb_ctx, xa_ref[...], xb_ref[...])
    else:
        x = xp_ref[...] + modp_ref[0][:, 5 * D:6 * D] * _moe_combine(y_refs, wt_ref)
    xcur_ref[...] = x
    m = mod_ref[0]
    hn = _rms(x, gain_ref[...]) * (1.0 + m[:, D:2 * D]) + m[:, 0:D]
    proj_ref[...] = _dot(hn, w_ref[...])
    if even:
        z = _dot(_dot(hn, wlr_ref[...]), gkw_ref[...]) + gkb_ref[...]
        g_ref[...] = (jnp.minimum(z, 0.0) - jnp.log(1.0 + jnp.exp(-jnp.abs(z)))) * (1.0 / GLA_NORMALIZER)


def _cond_row(i, nb_ctx, tiles_per_lat_seq):
    return jnp.where(i < nb_ctx, 0, 1 + (i - nb_ctx) // tiles_per_lat_seq)


def _in_proj(layer, n_ctx, n_tok, lat_seq, mods, gain, w_main, *, x_parts=None, x_prev=None, y4=None, even_extra=None):
    first = x_parts is not None
    even = even_extra is not None
    tm = TOKEN_TILE
    nb = n_tok // tm
    nb_ctx = n_ctx // tm
    tpl = lat_seq // tm
    np_cols = w_main.shape[1]
    D = D_MODEL

    def mod_map(l):
        return lambda i: (l * 8 + _cond_row(i, nb_ctx, tpl), 0, 0)

    row = lambda i: (i, 0)
    args, specs = [], []
    if first:
        xa, xb = x_parts
        args += [xa, xb]
        specs += [pl.BlockSpec((tm, D), lambda i: (jnp.minimum(i, nb_ctx - 1), 0)),
                  pl.BlockSpec((tm, D), lambda i: (jnp.maximum(i - nb_ctx, 0), 0))]
    else:
        y4, wts = y4
        args += [x_prev] + [y4] * TOP_K + [wts, mods]
        specs += [pl.BlockSpec((tm, D), row)]
        specs += [pl.BlockSpec((tm, D), (lambda k: (lambda i: (k * nb + i, 0)))(k)) for k in range(TOP_K)]
        specs += [pl.BlockSpec((tm, LANES), row), pl.BlockSpec((1, 1, 6 * D), mod_map(layer - 1))]
    args += [gain.reshape(1, D), mods, w_main]
    specs += [_resident((1, D)), pl.BlockSpec((1, 1, 6 * D), mod_map(layer)), _resident((D, np_cols))]
    out_shapes = [jax.ShapeDtypeStruct((n_tok, np_cols), F32)]
    out_specs = [pl.BlockSpec((tm, np_cols), row)]
    if even:
        w_lr, gkw, gkb = even_extra
        args += [w_lr, gkw, gkb]
        specs += [_resident(w_lr.shape), _resident(gkw.shape), _resident(gkb.shape)]
        out_shapes.append(jax.ShapeDtypeStruct((n_tok, 2 * GLA_HEADS * GLA_DK), F32))
        out_specs.append(pl.BlockSpec((tm, 2 * GLA_HEADS * GLA_DK), row))
    out_shapes.append(jax.ShapeDtypeStruct((n_tok, D), F32))
    out_specs.append(pl.BlockSpec((tm, D), row))
    return pl.pallas_call(
        functools.partial(_in_proj_kernel, first=first, even=even, nb_ctx=nb_ctx),
        grid=(nb,),
        in_specs=specs,
        out_specs=out_specs,
        out_shape=out_shapes,
        compiler_params=_cparams(("arbitrary",)),
        name=f"in_proj_l{layer}",
    )(*args)


def _tri(c, lower):
    r = lax.broadcasted_iota(jnp.int32, (c, c), 0)
    s = lax.broadcasted_iota(jnp.int32, (c, c), 1)
    return (r >= s) if lower else (r <= s)


def _cumsum_mm(tri_bf16, g):
    g_hi = g.astype(BF16)
    g_lo = (g - g_hi.astype(F32)).astype(BF16)
    r = _dot(tri_bf16, jnp.concatenate([g_hi, g_lo], axis=1))
    w = g.shape[1]
    return r[:, :w] + r[:, w:]


def _gated_scan_kernel(*refs, variant, T, has_s0, want_final, hpb):
    C = SCAN_CHUNK
    nc = T // C
    it = iter(refs)
    if variant == "gla":
        q_ref, k_ref, v_ref, og_ref, gf_ref, gb_ref, gain_ref = (next(it) for _ in range(7))
    else:
        q_ref, ff_ref, fb_ref, v_ref, og_ref, lbl_ref, gain_ref = (next(it) for _ in range(7))
    s0_ref = next(it) if has_s0 else None
    y_ref = next(it)
    sfin_ref = next(it) if want_final else None
    qi_scr, o_scr, u_scr, dec_scr, sin_scr = (next(it) for _ in range(5))

    lane = lax.broadcasted_iota(jnp.int32, (1, LANES), 1)
    if hpb == 2:
        masks = [lane < GLA_DK, lane >= GLA_DK]
    else:
        masks = [None]
    tri_l, tri_u = _tri(C, True), _tri(C, False)
    tri_l16, tri_u16 = tri_l.astype(F32).astype(BF16), tri_u.astype(F32).astype(BF16)
    mid_f, mid_b = C // 2 - 1, C // 2

    if variant == "hgrn":
        lgs = [lbl_ref[l] for l in range(DEPTH)]
        mx = functools.reduce(jnp.maximum, lgs)
        es = [jnp.exp(l - mx) for l in lgs]
        tot = functools.reduce(lambda a, b: a + b, es)
        ps = [e / tot for e in es]
        layer = DEPTH - 1
        lb = functools.reduce(lambda a, b: a + b, ps[:layer + 1]) - ps[0]
        lb_f, lb_b = lb[0:1], lb[1:2]

    for n in range(nc):
        rows = pl.ds(n * C, C)
        if variant == "gla":
            q = q_ref[rows, :] * (GLA_DK ** -0.5)
            kf = kb = k_ref[rows, :]
            gf, gb = gf_ref[rows, :], gb_ref[rows, :]
        else:
            q = _silu(q_ref[rows, :])
            f_f = lb_f + (1.0 - lb_f) * _sigmoid(ff_ref[rows, :])
            f_b = lb_b + (1.0 - lb_b) * _sigmoid(fb_ref[rows, :])
            kf, kb = 1.0 - f_f, 1.0 - f_b
            gf, gb = jnp.log(f_f), jnp.log(f_b)
        bf = _cumsum_mm(tri_l16, gf)
        bb = _cumsum_mm(tri_u16, gb)
        bf_mid, bf_end = bf[mid_f:mid_f + 1], bf[C - 1:C]
        bb_mid, bb_end = bb[mid_b:mid_b + 1], bb[0:1]
        qd_f, kd_f = q * jnp.exp(bf - bf_mid), kf * jnp.exp(bf_mid - bf)
        qd_b, kd_b = q * jnp.exp(bb - bb_mid), kb * jnp.exp(bb_mid - bb)
        kend = jnp.concatenate([kf * jnp.exp(bf_end - bf), kb * jnp.exp(bb_end - bb)], axis=1)
        qi_scr[rows, :] = jnp.concatenate([q * jnp.exp(bf), q * jnp.exp(bb)], axis=1)
        dec_scr[n] = jnp.broadcast_to(jnp.concatenate([jnp.exp(bf_end), jnp.exp(bb_end)], axis=1), (8, 2 * LANES))
        for h in range(hpb):
            v = v_ref[rows, h * LANES:(h + 1) * LANES]
            if masks[h] is None:
                qf_h, qb_h, kend_h = qd_f, qd_b, kend
            else:
                qf_h, qb_h = jnp.where(masks[h], qd_f, 0.0), jnp.where(masks[h], qd_b, 0.0)
                kend_h = jnp.where(jnp.concatenate([masks[h], masks[h]], axis=1), kend, 0.0)
            s = jnp.where(tri_l, _dot_nt(qf_h, kd_f), 0.0) + jnp.where(tri_u, _dot_nt(qb_h, kd_b), 0.0)
            o_scr[rows, h * LANES:(h + 1) * LANES] = _dot(s, v)
            u_scr[h, n] = _dot(v.T, kend_h)

    for h in range(hpb):
        if has_s0:
            s_f, s_b = s0_ref[0, 0, 0].T, s0_ref[0, 1, 0].T
            if masks[h] is not None:
                s_f, s_b = jnp.where(masks[h], s_f, 0.0), jnp.where(masks[h], s_b, 0.0)
        else:
            s_f = s_b = jnp.zeros((LANES, LANES), F32)
        for n in range(nc):
            sin_scr[h, n, :, 0:LANES] = s_f
            s_f = s_f * dec_scr[n, 0:1, 0:LANES] + u_scr[h, n, :, 0:LANES]
        for n in reversed(range(nc)):
            sin_scr[h, n, :, LANES:2 * LANES] = s_b
            s_b = s_b * dec_scr[n, 0:1, LANES:2 * LANES] + u_scr[h, n, :, LANES:2 * LANES]
        if want_final:
            if h == 0:
                fin_f, fin_b = s_f, s_b
            else:
                fin_f, fin_b = fin_f + s_f, fin_b + s_b
    if want_final:
        sfin_ref[0, 0, 0] = fin_f.T
        sfin_ref[0, 1, 0] = fin_b.T

    gain = gain_ref[...]
    for n in range(nc):
        rows = pl.ds(n * C, C)
        qi = qi_scr[rows, :]
        for h in range(hpb):
            cols = slice(h * LANES, (h + 1) * LANES)
            o = o_scr[rows, cols] + _dot_nt(qi, sin_scr[h, n])
            y_ref[rows, cols] = (_rms(o, gain) * _silu(og_ref[rows, cols])).astype(y_ref.dtype)


def _scan_scratch(T, hpb):
    nc = T // SCAN_CHUNK
    return [
        pltpu.VMEM((T, 2 * LANES), F32),
        pltpu.VMEM((T, hpb * LANES), F32),
        pltpu.VMEM((hpb, nc, LANES, 2 * LANES), F32),
        pltpu.VMEM((nc, 8, 2 * LANES), F32),
        pltpu.VMEM((hpb, nc, LANES, 2 * LANES), F32),
    ]


def _gla_scan(proj, gdec, gain, s0, row0, B, T, want_final):
    rb0 = row0 // T
    has_s0 = s0 is not None

    def col(base, width=LANES):
        return lambda b, j: (rb0 + b, base // width + j)

    args = [proj, proj, proj, proj, gdec, gdec, gain.reshape(1, GLA_DV)]
    specs = [pl.BlockSpec((T, LANES), col(_GQ)), pl.BlockSpec((T, LANES), col(_GK)),
             pl.BlockSpec((T, 2 * LANES), col(_GV, 2 * LANES)), pl.BlockSpec((T, 2 * LANES), col(_GG, 2 * LANES)),
             pl.BlockSpec((T, LANES), col(0)), pl.BlockSpec((T, LANES), col(GLA_HEADS * GLA_DK)),
             pl.BlockSpec((1, GLA_DV), lambda b, j: (0, 0))]
    st_spec = pl.BlockSpec((1, 2, 1, LANES, LANES), lambda b, j: (b, 0, j, 0, 0))
    if has_s0:
        args.append(s0.reshape(B, 2, GLA_HEADS // 2, 2 * GLA_DK, GLA_DV))
        specs.append(st_spec)
    out_shapes = [jax.ShapeDtypeStruct((B * T, GLA_HEADS * GLA_DV), BF16)]
    out_specs = [pl.BlockSpec((T, 2 * LANES), lambda b, j: (b, j))]
    if want_final:
        out_shapes.append(jax.ShapeDtypeStruct((B, 2, GLA_HEADS // 2, 2 * GLA_DK, GLA_DV), F32))
        out_specs.append(st_spec)
    kern = functools.partial(_gated_scan_kernel, variant="gla", T=T, has_s0=has_s0, want_final=want_final, hpb=2)
    return pl.pallas_call(
        kern, grid=(B, GLA_HEADS // 2), in_specs=specs, out_specs=out_specs, out_shape=out_shapes,
        scratch_shapes=_scan_scratch(T, 2),
        compiler_params=_cparams(("arbitrary", "arbitrary")), name=f"gla_scan_T{T}",
    )(*args)


def _hgrn_scan(proj, lb_logits, gain, s0, row0, B, T, want_final):
    rb0 = row0 // T
    has_s0 = s0 is not None

    def col(base):
        return lambda b, j: (rb0 + b, base // LANES + j)

    args = [proj, proj, proj, proj, proj, lb_logits, gain.reshape(1, HG_DV)]
    specs = [pl.BlockSpec((T, LANES), col(_HQ)), pl.BlockSpec((T, LANES), col(_HFF)),
             pl.BlockSpec((T, LANES), col(_HFB)), pl.BlockSpec((T, LANES), col(_HI)),
             pl.BlockSpec((T, LANES), col(_HG)),
             pl.BlockSpec((DEPTH, 2, LANES), lambda b, j: (0, 0, j)),
             pl.BlockSpec((1, HG_DV), lambda b, j: (0, 0))]
    st_spec = pl.BlockSpec((1, 2, 1, LANES, LANES), lambda b, j: (b, 0, j, 0, 0))
    if has_s0:
        args.append(s0.reshape(B, 2, HG_HEADS, HG_DK, HG_DV))
        specs.append(st_spec)
    out_shapes = [jax.ShapeDtypeStruct((B * T, HG_HEADS * HG_DV), BF16)]
    out_specs = [pl.BlockSpec((T, LANES), lambda b, j: (b, j))]
    if want_final:
        out_shapes.append(jax.ShapeDtypeStruct((B, 2, HG_HEADS, HG_DK, HG_DV), F32))
        out_specs.append(st_spec)
    kern = functools.partial(_gated_scan_kernel, variant="hgrn", T=T, has_s0=has_s0, want_final=want_final, hpb=1)
    return pl.pallas_call(
        kern, grid=(B, HG_HEADS), in_specs=specs, out_specs=out_specs, out_shape=out_shapes,
        scratch_shapes=_scan_scratch(T, 1),
        compiler_params=_cparams(("arbitrary", "arbitrary")), name=f"hgrn_scan_T{T}",
    )(*args)


def _ret_scan_kernel(*refs, T, has_s0, want_final, rope):
    C = SCAN_CHUNK
    nc = T // C
    it = iter(refs)
    q_ref, k_ref, v_ref, og_ref, dexp_ref = (next(it) for _ in range(5))
    if rope:
        cos_ref, sin_ref = next(it), next(it)
    s0_ref = next(it) if has_s0 else None
    y_ref = next(it)
    sfin_ref = next(it) if want_final else None
    qi_scr, o_scr, u_scr, sin_scr = (next(it) for _ in range(4))

    lg = jnp.log1p(-jnp.exp2(-dexp_ref[0]))
    lg_f, lg_b = lg[0:1], lg[1:2]
    r = lax.broadcasted_iota(jnp.int32, (C, C), 0)
    s = lax.broadcasted_iota(jnp.int32, (C, C), 1)
    dist = (r - s).astype(F32)
    dmask = (jnp.where(r >= s, jnp.exp(jnp.maximum(dist, 0.0) * lg_f[:, 0:1]), 0.0)
             + jnp.where(r <= s, jnp.exp(jnp.maximum(-dist, 0.0) * lg_b[:, 0:1]), 0.0))
    pos = lax.broadcasted_iota(jnp.int32, (C, LANES), 0).astype(F32)
    xi = jnp.concatenate([jnp.exp((pos + 1.0) * lg_f), jnp.exp((C - pos) * lg_b)], axis=1)
    zeta = jnp.concatenate([jnp.exp((C - 1.0 - pos) * lg_f), jnp.exp(pos * lg_b)], axis=1)
    d_f, d_b = jnp.exp(C * lg_f), jnp.exp(C * lg_b)

    def rot(x, rows):
        if not rope:
            return x
        return x * cos_ref[rows, :] + pltpu.roll(x, RET_DK // 2, axis=1) * sin_ref[rows, :]

    for n in range(nc):
        rows = pl.ds(n * C, C)
        q = rot(q_ref[rows, :], rows)
        k = rot(k_ref[rows, :] * (RET_DK ** -0.5), rows)
        v = v_ref[rows, :]
        o_scr[rows, :] = _dot(_dot_nt(q, k) * dmask, v)
        qi_scr[rows, :] = jnp.concatenate([q, q], axis=1) * xi
        u_scr[n] = _dot(v.T, jnp.concatenate([k, k], axis=1) * zeta)

    if has_s0:
        s_f, s_b = s0_ref[0, 0, 0].T, s0_ref[0, 1, 0].T
    else:
        s_f = s_b = jnp.zeros((LANES, LANES), F32)
    for n in range(nc):
        sin_scr[n, :, 0:LANES] = s_f
        s_f = s_f * d_f + u_scr[n, :, 0:LANES]
    for n in reversed(range(nc)):
        sin_scr[n, :, LANES:2 * LANES] = s_b
        s_b = s_b * d_b + u_scr[n, :, LANES:2 * LANES]
    if want_final:
        sfin_ref[0, 0, 0] = s_f.T
        sfin_ref[0, 1, 0] = s_b.T

    for n in range(nc):
        rows = pl.ds(n * C, C)
        o = o_scr[rows, :] + _dot_nt(qi_scr[rows, :], sin_scr[n])
        y_ref[rows, :] = (_rms(o) * _silu(og_ref[rows, :])).astype(y_ref.dtype)


def _ret_scan(proj, dexp, rope_tabs, s0, row0, B, T, want_final):
    rb0 = row0 // T
    nc = T // SCAN_CHUNK
    has_s0 = s0 is not None
    rope = rope_tabs is not None

    def col(base):
        return lambda b, j: (rb0 + b, base // LANES + j)

    args = [proj, proj, proj, proj, dexp]
    specs = [pl.BlockSpec((T, LANES), col(_RQ)), pl.BlockSpec((T, LANES), col(_RK)),
             pl.BlockSpec((T, LANES), col(_RV)), pl.BlockSpec((T, LANES), col(_RG)),
             pl.BlockSpec((1, 2, LANES), lambda b, j: (j, 0, 0))]
    if rope:
        args += list(rope_tabs)
        specs += [pl.BlockSpec((T, LANES), lambda b, j: (0, 0))] * 2
    st_spec = pl.BlockSpec((1, 2, 1, LANES, LANES), lambda b, j: (b, 0, j, 0, 0))
    if has_s0:
        args.append(s0.reshape(B, 2, RET_HEADS, RET_DK, RET_DV))
        specs.append(st_spec)
    out_shapes = [jax.ShapeDtypeStruct((B * T, RET_HEADS * RET_DV), BF16)]
    out_specs = [pl.BlockSpec((T, LANES), lambda b, j: (b, j))]
    if want_final:
        out_shapes.append(jax.ShapeDtypeStruct((B, 2, RET_HEADS, RET_DK, RET_DV), F32))
        out_specs.append(st_spec)
    kern = functools.partial(_ret_scan_kernel, T=T, has_s0=has_s0, want_final=want_final, rope=rope)
    scratch = [pltpu.VMEM((T, 2 * LANES), F32), pltpu.VMEM((T, LANES), F32),
               pltpu.VMEM((nc, LANES, 2 * LANES), F32), pltpu.VMEM((nc, LANES, 2 * LANES), F32)]
    return pl.pallas_call(
        kern, grid=(B, RET_HEADS), in_specs=specs, out_specs=out_specs, out_shape=out_shapes,
        scratch_shapes=scratch,
        compiler_params=_cparams(("arbitrary", "arbitrary")), name=f"ret_scan_T{T}",
    )(*args)


def _out_proj_kernel(*refs, n_mix, nb_ctx):
    it = iter(refs)
    y_refs = [(next(it), next(it)) for _ in range(n_mix)]
    wo_ref, x_ref, mod_ref, gain_ref, rw_ref, rb_ref = (next(it) for _ in range(6))
    xnew_ref, hn_ref, idx_ref, wt_ref, rank_ref, cnt_ref = (next(it) for _ in range(6))
    cnt_scr = next(it)
    D = D_MODEL
    is_ctx = pl.program_id(0) < nb_ctx
    mix = None
    r0 = 0
    for ya_ref, yb_ref in y_refs:
        w = ya_ref.shape[1]
        part = _dot(jnp.where(is_ctx, ya_ref[...], yb_ref[...]), wo_ref[r0:r0 + w, :])
        mix = part if mix is None else mix + part
        r0 += w
    m = mod_ref[0]
    xn = x_ref[...] + m[:, 2 * D:3 * D] * mix
    xnew_ref[...] = xn
    hn = _rms(xn, gain_ref[...]) * (1.0 + m[:, 4 * D:5 * D]) + m[:, 3 * D:4 * D]
    hn_hi = hn.astype(BF16)
    hn_ref[...] = _pack_bf16_pairs(hn)
    hn_lo = (hn - hn_hi.astype(F32)).astype(BF16)
    rw = rw_ref[...]
    rw_hi = rw.astype(BF16)
    rw_lo = (rw - rw_hi.astype(F32)).astype(BF16)
    logits = (_dot(hn_hi, rw_hi) + (_dot(hn_lo, rw_hi) + _dot(hn_hi, rw_lo))) + rb_ref[...]
    lane = lax.broadcasted_iota(jnp.int32, logits.shape, 1).astype(F32)
    vals, idxs = [], []
    cur = logits
    for _ in range(TOP_K):
        mx = jnp.max(cur, axis=-1, keepdims=True)
        ik = jnp.min(jnp.where(cur == mx, lane, float(LANES)), axis=-1, keepdims=True)
        vals.append(mx)
        idxs.append(ik)
        cur = jnp.where(lane == ik, -jnp.inf, cur)
    es = [jnp.exp(v - vals[0]) for v in vals]
    tot = (es[0] + es[1]) + (es[2] + es[3])
    idx_out = jnp.zeros(logits.shape, F32)
    wt_out = jnp.zeros(logits.shape, F32)
    for k in range(TOP_K):
        idx_out = jnp.where(lane == float(k), idxs[k], idx_out)
        wt_out = jnp.where(lane == float(k), es[k] / tot, wt_out)
    idx_ref[...] = idx_out.astype(jnp.int32)
    wt_ref[...] = wt_out

    @pl.when(pl.program_id(0) == 0)
    def _():
        cnt_scr[...] = jnp.zeros(cnt_scr.shape, F32)

    tm = logits.shape[0]
    hits = [lane == idxs[k] for k in range(TOP_K)]
    sel = jnp.zeros(logits.shape, F32)
    for k in range(TOP_K):
        sel = sel + jnp.where(hits[k], 1.0, 0.0)
    rr = lax.broadcasted_iota(jnp.int32, (tm, tm), 0)
    cc = lax.broadcasted_iota(jnp.int32, (tm, tm), 1)
    before = jnp.where(rr > cc, 1.0, 0.0).astype(BF16)
    rank_all = cnt_scr[0:1, :] + _dot(before, sel.astype(BF16))
    rank_out = jnp.zeros(logits.shape, F32)
    for k in range(TOP_K):
        rk = jnp.sum(jnp.where(hits[k], rank_all, 0.0), axis=-1, keepdims=True)
        rank_out = jnp.where(lane == float(k), rk, rank_out)
    rank_ref[...] = rank_out.astype(jnp.int32)
    total = cnt_scr[...] + jnp.sum(sel, axis=0, keepdims=True)
    cnt_scr[...] = total
    cnt_ref[...] = total.astype(jnp.int32)


def _out_proj(layer, n_ctx, n_tok, lat_seq, ys, w_out, x_cur, mods, gain, rw_pad, rb_pad):
    tm = TOKEN_TILE
    nb, nb_ctx, tpl = n_tok // tm, n_ctx // tm, lat_seq // tm
    D = D_MODEL
    row = lambda i: (i, 0)
    specs = []
    for ya, _ in ys:
        specs += [pl.BlockSpec((tm, ya.shape[1]), lambda i: (jnp.minimum(i, nb_ctx - 1), 0)),
                  pl.BlockSpec((tm, ya.shape[1]), lambda i: (jnp.maximum(i - nb_ctx, 0), 0))]
    specs += [_resident((D, D)), pl.BlockSpec((tm, D), row),
              pl.BlockSpec((1, 1, 6 * D), lambda i: (layer * 8 + _cond_row(i, nb_ctx, tpl), 0, 0)),
              _resident((1, D)), _resident((D, LANES)), _resident((1, LANES))]
    return pl.pallas_call(
        functools.partial(_out_proj_kernel, n_mix=len(ys), nb_ctx=nb_ctx),
        grid=(nb,),
        in_specs=specs,
        out_specs=[pl.BlockSpec((tm, D), row), pl.BlockSpec((tm, D // 2), row),
                   pl.BlockSpec((tm, LANES), row), pl.BlockSpec((tm, LANES), row),
                   pl.BlockSpec((tm, LANES), row), pl.BlockSpec((8, LANES), lambda i: (0, 0))],
        out_shape=[jax.ShapeDtypeStruct((n_tok, D), F32), jax.ShapeDtypeStruct((n_tok, D // 2), jnp.uint32),
                   jax.ShapeDtypeStruct((n_tok, LANES), jnp.int32), jax.ShapeDtypeStruct((n_tok, LANES), F32),
                   jax.ShapeDtypeStruct((n_tok, LANES), jnp.int32), jax.ShapeDtypeStruct((8, LANES), jnp.int32)],
        scratch_shapes=[pltpu.VMEM((8, LANES), F32)],
        compiler_params=_cparams(("arbitrary",)),
        name=f"out_proj_l{layer}",
    )(*[y for pair in ys for y in pair], w_out, x_cur, mods, gain.reshape(1, D), rw_pad, rb_pad)


W1_SPLIT = 4
W2_SPLIT = 2


def _pack_bf16_pairs(x):
    w = x.shape[1] // 2
    bits = lax.bitcast_convert_type(x.astype(BF16).astype(F32), jnp.uint32)
    return (bits[:, w:] & jnp.uint32(0xFFFF0000)) | (bits[:, :w] >> 16)


def _unpack_bf16_pairs(p):
    lo = lax.bitcast_convert_type(p << 16, F32)
    hi = lax.bitcast_convert_type(p & jnp.uint32(0xFFFF0000), F32)
    return jnp.concatenate([lo, hi], axis=1).astype(BF16)


def _moe_kernel(ts_ref, nt_ref, xs_hbm, *refs):
    w1_refs = refs[:W1_SPLIT]
    b1_ref = refs[W1_SPLIT]
    w2_refs = refs[W1_SPLIT + 1:W1_SPLIT + 1 + W2_SPLIT]
    b2_ref, ys_hbm, w1_scr, w2_scr, xbuf, obuf, xsem, osem = refs[W1_SPLIT + 1 + W2_SPLIT:]
    tm = MOE_TILE
    e = pl.program_id(0)
    nt = nt_ref[e]
    row0 = ts_ref[e] * tm

    def x_copy(t, slot):
        return pltpu.make_async_copy(xs_hbm.at[pl.ds(row0 + t * tm, tm)], xbuf.at[slot], xsem.at[slot])

    def o_copy(t, slot):
        return pltpu.make_async_copy(obuf.at[slot], ys_hbm.at[pl.ds(row0 + t * tm, tm)], osem.at[slot])

    @pl.when(nt > 0)
    def _():
        x_copy(0, 0).start()
        cw = 2 * D_FF // W1_SPLIT
        for c, w_ref in enumerate(w1_refs):
            w1_scr[:, c * cw:(c + 1) * cw] = w_ref[0, 0].astype(BF16)
        rh = D_FF // W2_SPLIT
        for c, w_ref in enumerate(w2_refs):
            w2_scr[c * rh:(c + 1) * rh, :] = w_ref[0, 0].astype(BF16)
        blk = 4 * LANES
        rr = lax.broadcasted_iota(jnp.int32, (blk, blk // 2), 0)
        cc = lax.broadcasted_iota(jnp.int32, (blk, blk // 2), 1)
        sel = jnp.where(rr == 2 * cc, 1.0, 0.0).astype(BF16)

        def tile(t, carry):
            slot = lax.rem(t, 2)

            @pl.when(t + 1 < nt)
            def _():
                x_copy(t + 1, 1 - slot).start()

            x_copy(t, slot).wait()

            @pl.when(t >= 2)
            def _():
                o_copy(t - 2, slot).wait()

            x = _unpack_bf16_pairs(xbuf[slot])
            hid = _dot(x, w1_scr[...]) + b1_ref[0, 0]
            nxt = pltpu.roll(hid, 2 * D_FF - 1, axis=1)
            glu = jnp.minimum(hid, SWIGLU_LIMIT)
            lin = jnp.clip(nxt, -SWIGLU_LIMIT, SWIGLU_LIMIT)
            act = glu * _sigmoid(SWIGLU_ALPHA * glu) * (lin + 1.0)
            parts = [_dot(act[:, c * blk:(c + 1) * blk].astype(BF16), sel) for c in range(2 * D_FF // blk)]
            act_c = jnp.concatenate(parts, axis=1)
            obuf[slot] = _dot(act_c, w2_scr[...]) + b2_ref[0, 0]
            o_copy(t, slot).start()
            return carry

        lax.fori_loop(0, nt, tile, 0)

        @pl.when(nt >= 2)
        def _():
            o_copy(nt - 2, lax.rem(nt, 2)).wait()

        o_copy(nt - 1, lax.rem(nt - 1, 2)).wait()

    @pl.when(e == N_EXPERTS - 1)
    def _():
        obuf[0] = jnp.zeros(obuf.shape[1:], obuf.dtype)

        def fill(t, carry):
            cp = pltpu.make_async_copy(obuf.at[0], ys_hbm.at[pl.ds(t * tm, tm)], osem.at[0])
            cp.start()
            cp.wait()
            return carry

        lax.fori_loop(ts_ref[e] + nt, ys_hbm.shape[0] // tm, fill, 0)


def _moe_experts(layer, n_slots, tile_start, tiles_per_e, xs, w1, b1, w2, b2):
    tm = MOE_TILE
    D, F2 = D_MODEL, 2 * D_FF
    cw, rh = F2 // W1_SPLIT, D_FF // W2_SPLIT
    w1_specs = [pl.BlockSpec((1, 1, D, cw), (lambda c: (lambda e, ts, nt: (layer, e, 0, c)))(c))
                for c in range(W1_SPLIT)]
    w2_specs = [pl.BlockSpec((1, 1, rh, D), (lambda c: (lambda e, ts, nt: (layer, e, c, 0)))(c))
                for c in range(W2_SPLIT)]
    grid_spec = pltpu.PrefetchScalarGridSpec(
        num_scalar_prefetch=2,
        grid=(N_EXPERTS,),
        in_specs=[pl.BlockSpec(memory_space=pl.ANY)]
        + w1_specs + [pl.BlockSpec((1, 1, 1, F2), lambda e, ts, nt: (layer, e, 0, 0))]
        + w2_specs + [pl.BlockSpec((1, 1, 1, D), lambda e, ts, nt: (layer, e, 0, 0))],
        out_specs=pl.BlockSpec(memory_space=pl.ANY),
        scratch_shapes=[pltpu.VMEM((D, F2), BF16), pltpu.VMEM((D_FF, D), BF16),
                        pltpu.VMEM((2, tm, D // 2), jnp.uint32), pltpu.VMEM((2, tm, D), F32),
                        pltpu.SemaphoreType.DMA((2,)), pltpu.SemaphoreType.DMA((2,))],
    )
    return pl.pallas_call(
        _moe_kernel,
        grid_spec=grid_spec,
        out_shape=jax.ShapeDtypeStruct((n_slots, D), F32),
        compiler_params=_cparams(("arbitrary",)),
        name="moe_experts",
    )(tile_start, tiles_per_e, xs, *([w1] * W1_SPLIT), b1.reshape(DEPTH, N_EXPERTS, 1, F2),
      *([w2] * W2_SPLIT), b2.reshape(DEPTH, N_EXPERTS, 1, D))


def _pos_kernel(idx_ref, rank_ref, seg_ref, pos_ref):
    lane = lax.broadcasted_iota(jnp.int32, idx_ref.shape, 1)
    idx, rank, seg = idx_ref[...], rank_ref[...], seg_ref[...]
    out = jnp.zeros(idx.shape, jnp.int32)
    for k in range(TOP_K):
        hit = lane == idx[:, k:k + 1]
        seg_k = jnp.sum(jnp.where(hit, seg, 0.0), axis=-1, keepdims=True).astype(jnp.int32)
        out = jnp.where(lane == k, seg_k + rank[:, k:k + 1], out)
    pos_ref[...] = out


def _pair_slots(idx, rank, seg_start):
    n_tok = idx.shape[0]
    tm = TOKEN_TILE
    row = lambda i: (i, 0)
    return pl.pallas_call(
        _pos_kernel,
        grid=(n_tok // tm,),
        in_specs=[pl.BlockSpec((tm, LANES), row), pl.BlockSpec((tm, LANES), row),
                  pl.BlockSpec((1, LANES), lambda i: (0, 0))],
        out_specs=pl.BlockSpec((tm, LANES), row),
        out_shape=jax.ShapeDtypeStruct((n_tok, LANES), jnp.int32),
        compiler_params=_cparams(("arbitrary",)),
        name="pair_slots",
    )(idx, rank, seg_start)


def _moe_layer(layer, hn_packed, idx, rank, counts, w1, b1, w2, b2):
    n_tok = hn_packed.shape[0]
    tm = MOE_TILE
    n_pairs = n_tok * TOP_K
    n_slots = n_pairs + N_EXPERTS * tm
    tiles_per_e = (counts + tm - 1) // tm
    tile_start = jnp.cumsum(tiles_per_e) - tiles_per_e
    seg_start = jnp.pad((tile_start * tm).astype(F32), (0, LANES - N_EXPERTS)).reshape(1, LANES)
    pos = _pair_slots(idx, rank, seg_start)[:, :TOP_K]
    tok_ids = jnp.broadcast_to(jnp.arange(n_tok, dtype=jnp.int32)[:, None], (n_tok, TOP_K))
    slot_tok = jnp.zeros((n_slots,), jnp.int32).at[pos.reshape(-1)].set(tok_ids.reshape(-1))
    xs = jnp.take(hn_packed, slot_tok, axis=0)
    ys = _moe_experts(layer, n_slots, tile_start, tiles_per_e, xs, w1, b1, w2, b2)
    return jnp.take(ys, pos.T.reshape(-1), axis=0)


def _final_kernel(*refs, nb_ctx):
    x_ref = refs[0]
    y_refs = refs[1:1 + TOP_K]
    wt_ref, mod_ref, gain_ref, oa_ref, ob_ref = refs[1 + TOP_K:]
    D = D_MODEL
    i = pl.program_id(0)
    x = x_ref[...] + mod_ref[0][:, 5 * D:6 * D] * _moe_combine(y_refs, wt_ref)
    out = _rms(x, gain_ref[...])

    @pl.when(i < nb_ctx)
    def _():
        oa_ref[...] = out

    @pl.when(i >= nb_ctx)
    def _():
        ob_ref[...] = out


def _final(n_ctx, n_tok, lat_seq, x_new, y4, wts, mods, gain):
    tm = TOKEN_TILE
    nb, nb_ctx, tpl = n_tok // tm, n_ctx // tm, lat_seq // tm
    D = D_MODEL
    specs = [pl.BlockSpec((tm, D), lambda i: (i, 0))]
    specs += [pl.BlockSpec((tm, D), (lambda k: (lambda i: (k * nb + i, 0)))(k)) for k in range(TOP_K)]
    specs += [pl.BlockSpec((tm, LANES), lambda i: (i, 0)),
              pl.BlockSpec((1, 1, 6 * D), lambda i: ((DEPTH - 1) * 8 + _cond_row(i, nb_ctx, tpl), 0, 0)),
              _resident((1, D))]
    return pl.pallas_call(
        functools.partial(_final_kernel, nb_ctx=nb_ctx),
        grid=(nb,),
        in_specs=specs,
        out_specs=[pl.BlockSpec((tm, D), lambda i: (jnp.minimum(i, nb_ctx - 1), 0)),
                   pl.BlockSpec((tm, D), lambda i: (jnp.maximum(i - nb_ctx, 0), 0))],
        out_shape=[jax.ShapeDtypeStruct((n_ctx, D), F32), jax.ShapeDtypeStruct((n_tok - n_ctx, D), F32)],
        compiler_params=_cparams(("arbitrary",)),
        name="final_norm",
    )(x_new, *([y4] * TOP_K), wts, mods, gain.reshape(1, D))


def _rope_tables(n_tokens):
    rows = n_tokens // GRID_W
    r = jnp.repeat(jnp.arange(rows, dtype=F32), GRID_W)
    col = jnp.tile(jnp.arange(GRID_W, dtype=F32), rows)
    nf = RET_DK // 4
    inv = ROPE_BASE ** (-jnp.arange(nf, dtype=F32) / nf)
    ang = jnp.concatenate([r[:, None] * inv, col[:, None] * inv], axis=-1)
    cos, sin = jnp.cos(ang), jnp.sin(ang)
    return jnp.concatenate([cos, cos], axis=-1), jnp.concatenate([-sin, sin], axis=-1)


def kernel(x_prompt, x_sample, c, state_gla, state_ret, state_hgrn, c_ctx, norm_mix, norm_ffn, ada_w, ada_b,
           w_in_even, w_out_even, gla_gk_w, gla_gk_b, gla_gain, ret_decay_exp, w_in_odd, w_out_odd,
           hgrn_lb_logits, hgrn_gain, router_w, router_b, moe_w1, moe_b1, moe_w2, moe_b2, final_norm):
    D = D_MODEL
    B_ctx, T_ctx, _ = x_prompt.shape
    B_lat, T_lat, _ = x_sample.shape
    n_ctx, n_lat = B_ctx * T_ctx, B_lat * T_lat
    n_tok = n_ctx + n_lat
    assert n_ctx % TOKEN_TILE == 0 and T_lat % TOKEN_TILE == 0 and B_lat + 1 <= 8
    assert T_ctx % SCAN_CHUNK == 0 and T_lat % SCAN_CHUNK == 0 and n_ctx % T_lat == 0

    cond8 = jnp.concatenate([c_ctx[None, :], c, jnp.zeros((8 - 1 - B_lat, D), F32)], axis=0)
    mods = _ada_mods(cond8, ada_w, ada_b).reshape(DEPTH * 8, 1, 6 * D)

    rw_pad = jnp.pad(router_w, ((0, 0), (0, 0), (0, LANES - N_EXPERTS)))
    rb_pad = jnp.pad(router_b, ((0, 0), (0, LANES - N_EXPERTS)), constant_values=-1e30)

    w_even = w_in_even[0]
    lr0 = GLA_HEADS * (2 * GLA_DK + GLA_DV)
    lr1 = lr0 + 2 * GLA_LOW_RANK
    w_main = jnp.concatenate([w_even[:, :lr0], w_even[:, lr1:]], axis=1)
    w_lr = jnp.pad(w_even[:, lr0:lr1], ((0, 0), (0, LANES - 2 * GLA_LOW_RANK)))
    nqk = GLA_HEADS * GLA_DK
    gkw = jnp.zeros((LANES, 2 * nqk), F32)
    gkw = gkw.at[0:GLA_LOW_RANK, 0:nqk].set(gla_gk_w[0, 0])
    gkw = gkw.at[GLA_LOW_RANK:2 * GLA_LOW_RANK, nqk:].set(gla_gk_w[0, 1])
    gkb = gla_gk_b[0].reshape(1, 2 * nqk)
    proj, gdec, x_cur = _in_proj(0, n_ctx, n_tok, T_lat, mods, norm_mix[0], w_main,
                                 x_parts=(x_prompt.reshape(n_ctx, D), x_sample.reshape(n_lat, D)),
                                 even_extra=(w_lr, gkw, gkb))

    y_gla_c, fin_gla = _gla_scan(proj, gdec, gla_gain[0], None, 0, B_ctx, T_ctx, True)
    (y_gla_l,) = _gla_scan(proj, gdec, gla_gain[0], state_gla[:, 0], n_ctx, B_lat, T_lat, False)
    dexp = jnp.broadcast_to(ret_decay_exp[0].T[:, :, None], (RET_HEADS, 2, LANES))
    y_ret_c, fin_ret = _ret_scan(proj, dexp, None, None, 0, B_ctx, T_ctx, True)
    (y_ret_l,) = _ret_scan(proj, dexp, _rope_tables(T_lat), state_ret[:, 0], n_ctx, B_lat, T_lat, False)

    x_new, hn, idx, wts, rank, cnt = _out_proj(0, n_ctx, n_tok, T_lat, [(y_gla_c, y_gla_l), (y_ret_c, y_ret_l)],
                                               w_out_even[0], x_cur, mods, norm_ffn[0], rw_pad[0], rb_pad[0:1])
    y4 = _moe_layer(0, hn, idx, rank, cnt[0, :N_EXPERTS], moe_w1, moe_b1, moe_w2, moe_b2)

    (proj, x_cur) = _in_proj(1, n_ctx, n_tok, T_lat, mods, norm_mix[1], w_in_odd[0], x_prev=x_new, y4=(y4, wts))
    y_h_c, fin_h = _hgrn_scan(proj, hgrn_lb_logits, hgrn_gain[0], None, 0, B_ctx, T_ctx, True)
    (y_h_l,) = _hgrn_scan(proj, hgrn_lb_logits, hgrn_gain[0], state_hgrn[:, 0], n_ctx, B_lat, T_lat, False)
    x_new, hn, idx, wts, rank, cnt = _out_proj(1, n_ctx, n_tok, T_lat, [(y_h_c, y_h_l)], w_out_odd[0], x_cur, mods,
                                               norm_ffn[1], rw_pad[1], rb_pad[1:2])
    y4 = _moe_layer(1, hn, idx, rank, cnt[0, :N_EXPERTS], moe_w1, moe_b1, moe_w2, moe_b2)

    y_ctx, y_lat = _final(n_ctx, n_tok, T_lat, x_new, y4, wts, mods, final_norm)

    new_state_gla = fin_gla.reshape(B_ctx, 1, 2, GLA_HEADS, GLA_DK, GLA_DV)
    new_state_ret = fin_ret.reshape(B_ctx, 1, 2, RET_HEADS, RET_DK, RET_DV)
    new_state_hgrn = fin_h.reshape(B_ctx, 1, 2, HG_HEADS, HG_DK, HG_DV)
    return (y_ctx.reshape(B_ctx, T_ctx, D), y_lat.reshape(B_lat, T_lat, D), new_state_gla, new_state_ret,
            new_state_hgrn)
```

```python
import functools

import jax
import jax.numpy as jnp
from jax import lax
from jax.experimental import pallas as pl
from jax.experimental.pallas import tpu as pltpu

F32 = jnp.float32
BF16 = jnp.bfloat16

D_MODEL = 1024
DEPTH = 2
GRID_W = 64
GLA_HEADS, GLA_DK, GLA_DV, GLA_LOW_RANK = 4, 64, 128, 16
GLA_NORMALIZER = 16.0
RET_HEADS, RET_DK, RET_DV = 4, 128, 128
ROPE_BASE = 10000.0
HG_HEADS, HG_DK, HG_DV = 8, 128, 128
N_EXPERTS, TOP_K, D_FF = 32, 4, 1024
SWIGLU_ALPHA, SWIGLU_LIMIT = 1.702, 7.0
EPS = 1e-6

LANES = 128
SCAN_CHUNK = 128
TOKEN_TILE = 256
MOE_TILE = 512
VMEM_LIMIT = 56 * 1024 * 1024

_GQ, _GK, _GV, _GG, _RQ, _RK, _RV, _RG, _EVEN_MAIN = 0, 256, 512, 1024, 1536, 2048, 2560, 3072, 3584
_HQ, _HFF, _HFB, _HI, _HG, _ODD_MAIN = 0, 1024, 2048, 3072, 4096, 5120


def _dot(a, b):
    return jnp.dot(a, b, preferred_element_type=F32)


def _dot_nt(a, b):
    return lax.dot_general(a, b, (((1,), (1,)), ((), ())), preferred_element_type=F32)


def _rms(x, gain=None):
    y = x * lax.rsqrt(jnp.mean(x * x, axis=-1, keepdims=True) + EPS)
    if gain is not None:
        y = y * gain
    return y


def _sigmoid(x):
    return 1.0 / (1.0 + jnp.exp(-x))


def _silu(x):
    return x * _sigmoid(x)


def _cparams(sem, vmem=VMEM_LIMIT):
    return pltpu.CompilerParams(dimension_semantics=sem, vmem_limit_bytes=vmem)


def _resident(shape):
    nd = len(shape)
    return pl.BlockSpec(shape, lambda *_: (0,) * nd, pipeline_mode=pl.Buffered(1))


def _ada_kernel(c_ref, w_ref, b_ref, o_ref):
    o_ref[0] = _dot(_silu(c_ref[...]), w_ref[0]) + b_ref[0]


def _ada_mods(cond8, ada_w, ada_b):
    tn = 1536
    return pl.pallas_call(
        _ada_kernel,
        grid=(DEPTH, 6 * D_MODEL // tn),
        in_specs=[
            pl.BlockSpec((8, D_MODEL), lambda l, j: (0, 0)),
            pl.BlockSpec((1, D_MODEL, tn), lambda l, j: (l, 0, j)),
            pl.BlockSpec((1, 1, tn), lambda l, j: (l, 0, j)),
        ],
        out_specs=pl.BlockSpec((1, 8, tn), lambda l, j: (l, 0, j)),
        out_shape=jax.ShapeDtypeStruct((DEPTH, 8, 6 * D_MODEL), F32),
        compiler_params=_cparams(("arbitrary", "arbitrary")),
        name="ada_mods",
    )(cond8, ada_w, ada_b.reshape(DEPTH, 1, 6 * D_MODEL))


ROW_UNITS = D_MODEL // LANES


def _rows_from_units(ref, n_rows):
    return jnp.concatenate([ref[pl.ds(c, n_rows, stride=ROW_UNITS), :] for c in range(ROW_UNITS)], axis=1)


def _rows_to_units(ref, val):
    n_rows = val.shape[0]
    for c in range(ROW_UNITS):
        ref[pl.ds(c, n_rows, stride=ROW_UNITS), :] = val[:, c * LANES:(c + 1) * LANES]


def _moe_combine(y_refs, wt_ref):
    w = wt_ref[...]
    terms = [w[:, k:k + 1] * _rows_from_units(y_refs[k], w.shape[0]) for k in range(TOP_K)]
    return (terms[0] + terms[1]) + (terms[2] + terms[3])


def _in_proj_kernel(*refs, first, even, nb_ctx):
    it = iter(refs)
    if first:
        xa_ref, xb_ref = next(it), next(it)
    else:
        xp_ref = next(it)
        y_refs = [next(it) for _ in range(TOP_K)]
        wt_ref, modp_ref = next(it), next(it)
    gain_ref, mod_ref, w_ref = next(it), next(it), next(it)
    if even:
        wlr_ref, gkw_ref, gkb_ref = next(it), next(it), next(it)
    proj_ref = next(it)
    if even:
        g_ref = next(it)
    xcur_ref = next(it)

    D = D_MODEL
    i = pl.program_id(0)
    if first:
        x = jnp.where(i < nb_ctx, xa_ref[...], xb_ref[...])
    else:
        x = xp_ref[...] + modp_ref[0][:, 5 * D:6 * D] * _moe_combine(y_refs, wt_ref)
    xcur_ref[...] = x
    m = mod_ref[0]
    hn = _rms(x, gain_ref[...]) * (1.0 + m[:, D:2 * D]) + m[:, 0:D]
    proj_ref[...] = _dot(hn, w_ref[...])
    if even:
        z = _dot(_dot(hn, wlr_ref[...]), gkw_ref[...]) + gkb_ref[...]
        g_ref[...] = (jnp.minimum(z, 0.0) - jnp.log(1.0 + jnp.exp(-jnp.abs(z)))) * (1.0 / GLA_NORMALIZER)


def _cond_row(i, nb_ctx, tiles_per_lat_seq):
    return jnp.where(i < nb_ctx, 0, 1 + (i - nb_ctx) // tiles_per_lat_seq)


def _in_proj(layer, n_ctx, n_tok, lat_seq, mods, gain, w_main, *, x_parts=None, x_prev=None, y4=None, even_extra=None):
    first = x_parts is not None
    even = even_extra is not None
    tm = TOKEN_TILE
    nb = n_tok // tm
    nb_ctx = n_ctx // tm
    tpl = lat_seq // tm
    np_cols = w_main.shape[1]
    D = D_MODEL

    def mod_map(l):
        return lambda i: (l * 8 + _cond_row(i, nb_ctx, tpl), 0, 0)

    row = lambda i: (i, 0)
    args, specs = [], []
    if first:
        xa, xb = x_parts
        args += [xa, xb]
        specs += [pl.BlockSpec((tm, D), lambda i: (jnp.minimum(i, nb_ctx - 1), 0)),
                  pl.BlockSpec((tm, D), lambda i: (jnp.maximum(i - nb_ctx, 0), 0))]
    else:
        y4, wts = y4
        args += [x_prev] + [y4] * TOP_K + [wts, mods]
        specs += [pl.BlockSpec((tm, D), row)]
        specs += [pl.BlockSpec((tm * ROW_UNITS, LANES), (lambda k: (lambda i: (k * nb + i, 0)))(k))
                  for k in range(TOP_K)]
        specs += [pl.BlockSpec((tm, LANES), row), pl.BlockSpec((1, 1, 6 * D), mod_map(layer - 1))]
    args += [gain.reshape(1, D), mods, w_main]
    specs += [_resident((1, D)), pl.BlockSpec((1, 1, 6 * D), mod_map(layer)), _resident((D, np_cols))]
    out_shapes = [jax.ShapeDtypeStruct((n_tok, np_cols), F32)]
    out_specs = [pl.BlockSpec((tm, np_cols), row)]
    if even:
        w_lr, gkw, gkb = even_extra
        args += [w_lr, gkw, gkb]
        specs += [_resident(w_lr.shape), _resident(gkw.shape), _resident(gkb.shape)]
        out_shapes.append(jax.ShapeDtypeStruct((n_tok, 2 * GLA_HEADS * GLA_DK), F32))
        out_specs.append(pl.BlockSpec((tm, 2 * GLA_HEADS * GLA_DK), row))
    out_shapes.append(jax.ShapeDtypeStruct((n_tok, D), F32))
    out_specs.append(pl.BlockSpec((tm, D), row))
    return pl.pallas_call(
        functools.partial(_in_proj_kernel, first=first, even=even, nb_ctx=nb_ctx),
        grid=(nb,),
        in_specs=specs,
        out_specs=out_specs,
        out_shape=out_shapes,
        compiler_params=_cparams(("arbitrary",)),
        name=f"in_proj_l{layer}",
    )(*args)


def _tri(c, lower):
    r = lax.broadcasted_iota(jnp.int32, (c, c), 0)
    s = lax.broadcasted_iota(jnp.int32, (c, c), 1)
    return (r >= s) if lower else (r <= s)


def _cumsum_mm(tri_bf16, g):
    g_hi = g.astype(BF16)
    g_lo = (g - g_hi.astype(F32)).astype(BF16)
    r = _dot(tri_bf16, jnp.concatenate([g_hi, g_lo], axis=1))
    w = g.shape[1]
    return r[:, :w] + r[:, w:]


def _gated_scan_kernel(*refs, variant, T, has_s0, want_final, hpb):
    C = SCAN_CHUNK
    nc = T // C
    it = iter(refs)
    if variant == "gla":
        q_ref, k_ref, v_ref, og_ref, gf_ref, gb_ref, gain_ref = (next(it) for _ in range(7))
    else:
        q_ref, ff_ref, fb_ref, v_ref, og_ref, lbl_ref, gain_ref = (next(it) for _ in range(7))
    s0_ref = next(it) if has_s0 else None
    y_ref = next(it)
    sfin_ref = next(it) if want_final else None
    qi_scr, o_scr, u_scr, dec_scr, sin_scr = (next(it) for _ in range(5))

    lane = lax.broadcasted_iota(jnp.int32, (1, LANES), 1)
    if hpb == 2:
        masks = [lane < GLA_DK, lane >= GLA_DK]
    else:
        masks = [None]
    tri_l, tri_u = _tri(C, True), _tri(C, False)
    tri_l16, tri_u16 = tri_l.astype(F32).astype(BF16), tri_u.astype(F32).astype(BF16)
    mid_f, mid_b = C // 2 - 1, C // 2

    if variant == "hgrn":
        lgs = [lbl_ref[l] for l in range(DEPTH)]
        mx = functools.reduce(jnp.maximum, lgs)
        es = [jnp.exp(l - mx) for l in lgs]
        tot = functools.reduce(lambda a, b: a + b, es)
        ps = [e / tot for e in es]
        layer = DEPTH - 1
        lb = functools.reduce(lambda a, b: a + b, ps[:layer + 1]) - ps[0]
        lb_f, lb_b = lb[0:1], lb[1:2]

    for n in range(nc):
        rows = pl.ds(n * C, C)
        if variant == "gla":
            q = q_ref[rows, :] * (GLA_DK ** -0.5)
            kf = kb = k_ref[rows, :]
            gf, gb = gf_ref[rows, :], gb_ref[rows, :]
        else:
            q = _silu(q_ref[rows, :])
            f_f = lb_f + (1.0 - lb_f) * _sigmoid(ff_ref[rows, :])
            f_b = lb_b + (1.0 - lb_b) * _sigmoid(fb_ref[rows, :])
            kf, kb = 1.0 - f_f, 1.0 - f_b
            gf, gb = jnp.log(f_f), jnp.log(f_b)
        bf = _cumsum_mm(tri_l16, gf)
        bb = _cumsum_mm(tri_u16, gb)
        bf_mid, bf_end = bf[mid_f:mid_f + 1], bf[C - 1:C]
        bb_mid, bb_end = bb[mid_b:mid_b + 1], bb[0:1]
        qd_f, kd_f = q * jnp.exp(bf - bf_mid), kf * jnp.exp(bf_mid - bf)
        qd_b, kd_b = q * jnp.exp(bb - bb_mid), kb * jnp.exp(bb_mid - bb)
        kend = jnp.concatenate([kf * jnp.exp(bf_end - bf), kb * jnp.exp(bb_end - bb)], axis=1)
        qi_scr[rows, :] = jnp.concatenate([q * jnp.exp(bf), q * jnp.exp(bb)], axis=1)
        dec_scr[n] = jnp.broadcast_to(jnp.concatenate([jnp.exp(bf_end), jnp.exp(bb_end)], axis=1), (8, 2 * LANES))
        for h in range(hpb):
            v = v_ref[rows, h * LANES:(h + 1) * LANES]
            if masks[h] is None:
                qf_h, qb_h, kend_h = qd_f, qd_b, kend
            else:
                qf_h, qb_h = jnp.where(masks[h], qd_f, 0.0), jnp.where(masks[h], qd_b, 0.0)
                kend_h = jnp.where(jnp.concatenate([masks[h], masks[h]], axis=1), kend, 0.0)
            s = jnp.where(tri_l, _dot_nt(qf_h, kd_f), 0.0) + jnp.where(tri_u, _dot_nt(qb_h, kd_b), 0.0)
            o_scr[rows, h * LANES:(h + 1) * LANES] = _dot(s, v)
            u_scr[h, n] = _dot(v.T, kend_h)

    for h in range(hpb):
        if has_s0:
            s_f, s_b = s0_ref[0, 0, 0].T, s0_ref[0, 1, 0].T
            if masks[h] is not None:
                s_f, s_b = jnp.where(masks[h], s_f, 0.0), jnp.where(masks[h], s_b, 0.0)
        else:
            s_f = s_b = jnp.zeros((LANES, LANES), F32)
        for n in range(nc):
            sin_scr[h, n, :, 0:LANES] = s_f
            s_f = s_f * dec_scr[n, 0:1, 0:LANES] + u_scr[h, n, :, 0:LANES]
        for n in reversed(range(nc)):
            sin_scr[h, n, :, LANES:2 * LANES] = s_b
            s_b = s_b * dec_scr[n, 0:1, LANES:2 * LANES] + u_scr[h, n, :, LANES:2 * LANES]
        if want_final:
            if h == 0:
                fin_f, fin_b = s_f, s_b
            else:
                fin_f, fin_b = fin_f + s_f, fin_b + s_b
    if want_final:
        sfin_ref[0, 0, 0] = fin_f.T
        sfin_ref[0, 1, 0] = fin_b.T

    gain = gain_ref[...]
    for n in range(nc):
        rows = pl.ds(n * C, C)
        qi = qi_scr[rows, :]
        for h in range(hpb):
            cols = slice(h * LANES, (h + 1) * LANES)
            o = o_scr[rows, cols] + _dot_nt(qi, sin_scr[h, n])
            y_ref[rows, cols] = (_rms(o, gain) * _silu(og_ref[rows, cols])).astype(y_ref.dtype)


def _scan_scratch(T, hpb):
    nc = T // SCAN_CHUNK
    return [
        pltpu.VMEM((T, 2 * LANES), F32),
        pltpu.VMEM((T, hpb * LANES), F32),
        pltpu.VMEM((hpb, nc, LANES, 2 * LANES), F32),
        pltpu.VMEM((nc, 8, 2 * LANES), F32),
        pltpu.VMEM((hpb, nc, LANES, 2 * LANES), F32),
    ]


def _gla_scan(proj, gdec, gain, s0, row0, B, T, want_final):
    rb0 = row0 // T
    has_s0 = s0 is not None

    def col(base, width=LANES):
        return lambda b, j: (rb0 + b, base // width + j)

    args = [proj, proj, proj, proj, gdec, gdec, gain.reshape(1, GLA_DV)]
    specs = [pl.BlockSpec((T, LANES), col(_GQ)), pl.BlockSpec((T, LANES), col(_GK)),
             pl.BlockSpec((T, 2 * LANES), col(_GV, 2 * LANES)), pl.BlockSpec((T, 2 * LANES), col(_GG, 2 * LANES)),
             pl.BlockSpec((T, LANES), col(0)), pl.BlockSpec((T, LANES), col(GLA_HEADS * GLA_DK)),
             pl.BlockSpec((1, GLA_DV), lambda b, j: (0, 0))]
    st_spec = pl.BlockSpec((1, 2, 1, LANES, LANES), lambda b, j: (b, 0, j, 0, 0))
    if has_s0:
        args.append(s0.reshape(B, 2, GLA_HEADS // 2, 2 * GLA_DK, GLA_DV))
        specs.append(st_spec)
    out_shapes = [jax.ShapeDtypeStruct((B * T, GLA_HEADS * GLA_DV), BF16)]
    out_specs = [pl.BlockSpec((T, 2 * LANES), lambda b, j: (b, j))]
    if want_final:
        out_shapes.append(jax.ShapeDtypeStruct((B, 2, GLA_HEADS // 2, 2 * GLA_DK, GLA_DV), F32))
        out_specs.append(st_spec)
    kern = functools.partial(_gated_scan_kernel, variant="gla", T=T, has_s0=has_s0, want_final=want_final, hpb=2)
    return pl.pallas_call(
        kern, grid=(B, GLA_HEADS // 2), in_specs=specs, out_specs=out_specs, out_shape=out_shapes,
        scratch_shapes=_scan_scratch(T, 2),
        compiler_params=_cparams(("arbitrary", "arbitrary")), name=f"gla_scan_T{T}",
    )(*args)


def _hgrn_scan(proj, lb_logits, gain, s0, row0, B, T, want_final):
    rb0 = row0 // T
    has_s0 = s0 is not None

    def col(base):
        return lambda b, j: (rb0 + b, base // LANES + j)

    args = [proj, proj, proj, proj, proj, lb_logits, gain.reshape(1, HG_DV)]
    specs = [pl.BlockSpec((T, LANES), col(_HQ)), pl.BlockSpec((T, LANES), col(_HFF)),
             pl.BlockSpec((T, LANES), col(_HFB)), pl.BlockSpec((T, LANES), col(_HI)),
             pl.BlockSpec((T, LANES), col(_HG)),
             pl.BlockSpec((DEPTH, 2, LANES), lambda b, j: (0, 0, j)),
             pl.BlockSpec((1, HG_DV), lambda b, j: (0, 0))]
    st_spec = pl.BlockSpec((1, 2, 1, LANES, LANES), lambda b, j: (b, 0, j, 0, 0))
    if has_s0:
        args.append(s0.reshape(B, 2, HG_HEADS, HG_DK, HG_DV))
        specs.append(st_spec)
    out_shapes = [jax.ShapeDtypeStruct((B * T, HG_HEADS * HG_DV), BF16)]
    out_specs = [pl.BlockSpec((T, LANES), lambda b, j: (b, j))]
    if want_final:
        out_shapes.append(jax.ShapeDtypeStruct((B, 2, HG_HEADS, HG_DK, HG_DV), F32))
        out_specs.append(st_spec)
    kern = functools.partial(_gated_scan_kernel, variant="hgrn", T=T, has_s0=has_s0, want_final=want_final, hpb=1)
    return pl.pallas_call(
        kern, grid=(B, HG_HEADS), in_specs=specs, out_specs=out_specs, out_shape=out_shapes,
        scratch_shapes=_scan_scratch(T, 1),
        compiler_params=_cparams(("arbitrary", "arbitrary")), name=f"hgrn_scan_T{T}",
    )(*args)


def _ret_scan_kernel(*refs, T, has_s0, want_final, rope):
    C = SCAN_CHUNK
    nc = T // C
    it = iter(refs)
    q_ref, k_ref, v_ref, og_ref, dexp_ref = (next(it) for _ in range(5))
    if rope:
        cos_ref, sin_ref = next(it), next(it)
    s0_ref = next(it) if has_s0 else None
    y_ref = next(it)
    sfin_ref = next(it) if want_final else None
    qi_scr, o_scr, u_scr, sin_scr = (next(it) for _ in range(4))

    lg = jnp.log1p(-jnp.exp2(-dexp_ref[0]))
    lg_f, lg_b = lg[0:1], lg[1:2]
    r = lax.broadcasted_iota(jnp.int32, (C, C), 0)
    s = lax.broadcasted_iota(jnp.int32, (C, C), 1)
    dist = (r - s).astype(F32)
    dmask = (jnp.where(r >= s, jnp.exp(jnp.maximum(dist, 0.0) * lg_f[:, 0:1]), 0.0)
             + jnp.where(r <= s, jnp.exp(jnp.maximum(-dist, 0.0) * lg_b[:, 0:1]), 0.0))
    pos = lax.broadcasted_iota(jnp.int32, (C, LANES), 0).astype(F32)
    xi = jnp.concatenate([jnp.exp((pos + 1.0) * lg_f), jnp.exp((C - pos) * lg_b)], axis=1)
    zeta = jnp.concatenate([jnp.exp((C - 1.0 - pos) * lg_f), jnp.exp(pos * lg_b)], axis=1)
    d_f, d_b = jnp.exp(C * lg_f), jnp.exp(C * lg_b)

    def rot(x, rows):
        if not rope:
            return x
        return x * cos_ref[rows, :] + pltpu.roll(x, RET_DK // 2, axis=1) * sin_ref[rows, :]

    for n in range(nc):
        rows = pl.ds(n * C, C)
        q = rot(q_ref[rows, :], rows)
        k = rot(k_ref[rows, :] * (RET_DK ** -0.5), rows)
        v = v_ref[rows, :]
        o_scr[rows, :] = _dot(_dot_nt(q, k) * dmask, v)
        qi_scr[rows, :] = jnp.concatenate([q, q], axis=1) * xi
        u_scr[n] = _dot(v.T, jnp.concatenate([k, k], axis=1) * zeta)

    if has_s0:
        s_f, s_b = s0_ref[0, 0, 0].T, s0_ref[0, 1, 0].T
    else:
        s_f = s_b = jnp.zeros((LANES, LANES), F32)
    for n in range(nc):
        sin_scr[n, :, 0:LANES] = s_f
        s_f = s_f * d_f + u_scr[n, :, 0:LANES]
    for n in reversed(range(nc)):
        sin_scr[n, :, LANES:2 * LANES] = s_b
        s_b = s_b * d_b + u_scr[n, :, LANES:2 * LANES]
    if want_final:
        sfin_ref[0, 0, 0] = s_f.T
        sfin_ref[0, 1, 0] = s_b.T

    for n in range(nc):
        rows = pl.ds(n * C, C)
        o = o_scr[rows, :] + _dot_nt(qi_scr[rows, :], sin_scr[n])
        y_ref[rows, :] = (_rms(o) * _silu(og_ref[rows, :])).astype(y_ref.dtype)


def _ret_scan(proj, dexp, rope_tabs, s0, row0, B, T, want_final):
    rb0 = row0 // T
    nc = T // SCAN_CHUNK
    has_s0 = s0 is not None
    rope = rope_tabs is not None

    def col(base):
        return lambda b, j: (rb0 + b, base // LANES + j)

    args = [proj, proj, proj, proj, dexp]
    specs = [pl.BlockSpec((T, LANES), col(_RQ)), pl.BlockSpec((T, LANES), col(_RK)),
             pl.BlockSpec((T, LANES), col(_RV)), pl.BlockSpec((T, LANES), col(_RG)),
             pl.BlockSpec((1, 2, LANES), lambda b, j: (j, 0, 0))]
    if rope:
        args += list(rope_tabs)
        specs += [pl.BlockSpec((T, LANES), lambda b, j: (0, 0))] * 2
    st_spec = pl.BlockSpec((1, 2, 1, LANES, LANES), lambda b, j: (b, 0, j, 0, 0))
    if has_s0:
        args.append(s0.reshape(B, 2, RET_HEADS, RET_DK, RET_DV))
        specs.append(st_spec)
    out_shapes = [jax.ShapeDtypeStruct((B * T, RET_HEADS * RET_DV), BF16)]
    out_specs = [pl.BlockSpec((T, LANES), lambda b, j: (b, j))]
    if want_final:
        out_shapes.append(jax.ShapeDtypeStruct((B, 2, RET_HEADS, RET_DK, RET_DV), F32))
        out_specs.append(st_spec)
    kern = functools.partial(_ret_scan_kernel, T=T, has_s0=has_s0, want_final=want_final, rope=rope)
    scratch = [pltpu.VMEM((T, 2 * LANES), F32), pltpu.VMEM((T, LANES), F32),
               pltpu.VMEM((nc, LANES, 2 * LANES), F32), pltpu.VMEM((nc, LANES, 2 * LANES), F32)]
    return pl.pallas_call(
        kern, grid=(B, RET_HEADS), in_specs=specs, out_specs=out_specs, out_shape=out_shapes,
        scratch_shapes=scratch,
        compiler_params=_cparams(("arbitrary", "arbitrary")), name=f"ret_scan_T{T}",
    )(*args)


def _out_proj_kernel(*refs, n_mix, nb_ctx):
    it = iter(refs)
    y_refs = [(next(it), next(it)) for _ in range(n_mix)]
    wo_ref, x_ref, mod_ref, gain_ref, rw_ref, rb_ref = (next(it) for _ in range(6))
    xnew_ref, hn_ref, idx_ref, wt_ref, rank_ref, cnt_ref = (next(it) for _ in range(6))
    cnt_scr = next(it)
    D = D_MODEL
    is_ctx = pl.program_id(0) < nb_ctx
    mix = None
    r0 = 0
    for ya_ref, yb_ref in y_refs:
        w = ya_ref.shape[1]
        part = _dot(jnp.where(is_ctx, ya_ref[...], yb_ref[...]), wo_ref[r0:r0 + w, :])
        mix = part if mix is None else mix + part
        r0 += w
    m = mod_ref[0]
    xn = x_ref[...] + m[:, 2 * D:3 * D] * mix
    xnew_ref[...] = xn
    hn = _rms(xn, gain_ref[...]) * (1.0 + m[:, 4 * D:5 * D]) + m[:, 3 * D:4 * D]
    hn_hi = hn.astype(BF16)
    _rows_to_units(hn_ref, hn)
    hn_lo = (hn - hn_hi.astype(F32)).astype(BF16)
    rw = rw_ref[...]
    rw_hi = rw.astype(BF16)
    rw_lo = (rw - rw_hi.astype(F32)).astype(BF16)
    logits = (_dot(hn_hi, rw_hi) + (_dot(hn_lo, rw_hi) + _dot(hn_hi, rw_lo))) + rb_ref[...]
    lane = lax.broadcasted_iota(jnp.int32, logits.shape, 1).astype(F32)
    vals, idxs = [], []
    cur = logits
    for _ in range(TOP_K):
        mx = jnp.max(cur, axis=-1, keepdims=True)
        ik = jnp.min(jnp.where(cur == mx, lane, float(LANES)), axis=-1, keepdims=True)
        vals.append(mx)
        idxs.append(ik)
        cur = jnp.where(lane == ik, -jnp.inf, cur)
    es = [jnp.exp(v - vals[0]) for v in vals]
    tot = (es[0] + es[1]) + (es[2] + es[3])
    idx_out = jnp.zeros(logits.shape, F32)
    wt_out = jnp.zeros(logits.shape, F32)
    for k in range(TOP_K):
        idx_out = jnp.where(lane == float(k), idxs[k], idx_out)
        wt_out = jnp.where(lane == float(k), es[k] / tot, wt_out)
    idx_ref[...] = idx_out.astype(jnp.int32)
    wt_ref[...] = wt_out

    @pl.when(pl.program_id(0) == 0)
    def _():
        cnt_scr[...] = jnp.zeros(cnt_scr.shape, F32)

    tm = logits.shape[0]
    hits = [lane == idxs[k] for k in range(TOP_K)]
    sel = jnp.zeros(logits.shape, F32)
    for k in range(TOP_K):
        sel = sel + jnp.where(hits[k], 1.0, 0.0)
    rr = lax.broadcasted_iota(jnp.int32, (tm, tm), 0)
    cc = lax.broadcasted_iota(jnp.int32, (tm, tm), 1)
    before = jnp.where(rr > cc, 1.0, 0.0).astype(BF16)
    rank_all = cnt_scr[0:1, :] + _dot(before, sel.astype(BF16))
    rank_out = jnp.zeros(logits.shape, F32)
    for k in range(TOP_K):
        rk = jnp.sum(jnp.where(hits[k], rank_all, 0.0), axis=-1, keepdims=True)
        rank_out = jnp.where(lane == float(k), rk, rank_out)
    rank_ref[...] = rank_out.astype(jnp.int32)
    total = cnt_scr[...] + jnp.sum(sel, axis=0, keepdims=True)
    cnt_scr[...] = total
    cnt_ref[...] = total.astype(jnp.int32)


def _out_proj(layer, n_ctx, n_tok, lat_seq, ys, w_out, x_cur, mods, gain, rw_pad, rb_pad):
    tm = TOKEN_TILE
    nb, nb_ctx, tpl = n_tok // tm, n_ctx // tm, lat_seq // tm
    D = D_MODEL
    row = lambda i: (i, 0)
    specs = []
    for ya, _ in ys:
        specs += [pl.BlockSpec((tm, ya.shape[1]), lambda i: (jnp.minimum(i, nb_ctx - 1), 0)),
                  pl.BlockSpec((tm, ya.shape[1]), lambda i: (jnp.maximum(i - nb_ctx, 0), 0))]
    specs += [_resident((D, D)), pl.BlockSpec((tm, D), row),
              pl.BlockSpec((1, 1, 6 * D), lambda i: (layer * 8 + _cond_row(i, nb_ctx, tpl), 0, 0)),
              _resident((1, D)), _resident((D, LANES)), _resident((1, LANES))]
    return pl.pallas_call(
        functools.partial(_out_proj_kernel, n_mix=len(ys), nb_ctx=nb_ctx),
        grid=(nb,),
        in_specs=specs,
        out_specs=[pl.BlockSpec((tm, D), row), pl.BlockSpec((tm * ROW_UNITS, LANES), row),
                   pl.BlockSpec((tm, LANES), row), pl.BlockSpec((tm, LANES), row),
                   pl.BlockSpec((tm, LANES), row), pl.BlockSpec((8, LANES), lambda i: (0, 0))],
        out_shape=[jax.ShapeDtypeStruct((n_tok, D), F32), jax.ShapeDtypeStruct((n_tok * ROW_UNITS, LANES), F32),
                   jax.ShapeDtypeStruct((n_tok, LANES), jnp.int32), jax.ShapeDtypeStruct((n_tok, LANES), F32),
                   jax.ShapeDtypeStruct((n_tok, LANES), jnp.int32), jax.ShapeDtypeStruct((8, LANES), jnp.int32)],
        scratch_shapes=[pltpu.VMEM((8, LANES), F32)],
        compiler_params=_cparams(("arbitrary",)),
        name=f"out_proj_l{layer}",
    )(*[y for pair in ys for y in pair], w_out, x_cur, mods, gain.reshape(1, D), rw_pad, rb_pad)


W1_SPLIT = 4
W2_SPLIT = 2


def _moe_kernel(seg_ref, nt_ref, code_hbm, x_hbm, *refs, n_tok):
    w1_refs = refs[:W1_SPLIT]
    b1_ref = refs[W1_SPLIT]
    w2_refs = refs[W1_SPLIT + 1:W1_SPLIT + 1 + W2_SPLIT]
    b2_ref, y_hbm, w1_scr, w2_scr, act_scr, xbuf, obuf, code_smem, csem, gsem, ssem = refs[W1_SPLIT + 1 + W2_SPLIT:]
    tm = MOE_TILE
    U = ROW_UNITS
    e = pl.program_id(0)
    nt = nt_ref[e]
    slot0 = seg_ref[e]

    def code_copy(t, b):
        first = pl.multiple_of(slot0 + t * tm, tm)
        return pltpu.make_async_copy(code_hbm.at[pl.ds(first, tm)], code_smem.at[b], csem.at[b])

    def gather_start(b):
        for r in range(tm):
            tok = jnp.bitwise_and(code_smem[b, r], n_tok - 1)
            pltpu.make_async_copy(x_hbm.at[pl.ds(pl.multiple_of(tok * U, U), U)],
                                  xbuf.at[b, pl.ds(r * U, U)], gsem.at[b]).start()

    def gather_wait(b):
        pltpu.make_async_copy(x_hbm.at[pl.ds(0, tm * U)], xbuf.at[b], gsem.at[b]).wait()

    def scatter_start(b):
        for r in range(tm):
            dst = code_smem[b, r]
            pltpu.make_async_copy(obuf.at[b, pl.ds(r * U, U)],
                                  y_hbm.at[pl.ds(pl.multiple_of(dst * U, U), U)], ssem.at[b]).start()

    def scatter_wait(b):
        pltpu.make_async_copy(obuf.at[b], y_hbm.at[pl.ds(0, tm * U)], ssem.at[b]).wait()

    @pl.when(e == 0)
    def _():
        obuf[0] = jnp.zeros(obuf.shape[1:], obuf.dtype)
        for h in range(2):
            cp = pltpu.make_async_copy(obuf.at[0], y_hbm.at[pl.ds((TOP_K * n_tok + h * tm) * U, tm * U)], ssem.at[0])
            cp.start()
            cp.wait()

    @pl.when(nt > 0)
    def _():
        code_copy(0, 0).start()
        cw = 2 * D_FF // W1_SPLIT
        for c, w_ref in enumerate(w1_refs):
            w1_scr[:, c * cw:(c + 1) * cw] = w_ref[0, 0].astype(BF16)
        rh = D_FF // W2_SPLIT
        for c, w_ref in enumerate(w2_refs):
            w2_scr[c * rh:(c + 1) * rh, :] = w_ref[0, 0].astype(BF16)
        code_copy(0, 0).wait()
        gather_start(0)

        @pl.when(nt > 1)
        def _():
            code_copy(1, 1).start()

        blk = 4 * LANES
        rr = lax.broadcasted_iota(jnp.int32, (blk, blk // 2), 0)
        cc = lax.broadcasted_iota(jnp.int32, (blk, blk // 2), 1)
        sel = jnp.where(rr == 2 * cc, 1.0, 0.0).astype(BF16)

        def tile(t, carry):
            b = lax.rem(t, 2)

            @pl.when(t + 1 < nt)
            def _():
                code_copy(t + 1, 1 - b).wait()
                gather_start(1 - b)

            gather_wait(b)
            x = _rows_from_units(xbuf.at[b], tm).astype(BF16)
            for c in range(2 * D_FF // blk):
                cols = slice(c * blk, (c + 1) * blk)
                hid = _dot(x, w1_scr[:, cols]) + b1_ref[0, 0][:, cols]
                nxt = pltpu.roll(hid, blk - 1, axis=1)
                glu = jnp.minimum(hid, SWIGLU_LIMIT)
                lin = jnp.clip(nxt, -SWIGLU_LIMIT, SWIGLU_LIMIT)
                act = glu * _sigmoid(SWIGLU_ALPHA * glu) * (lin + 1.0)
                act_scr[:, c * (blk // 2):(c + 1) * (blk // 2)] = _dot(act.astype(BF16), sel).astype(BF16)
            y = _dot(act_scr[...], w2_scr[...]) + b2_ref[0, 0]

            @pl.when(t >= 2)
            def _():
                scatter_wait(b)

            _rows_to_units(obuf.at[b], y)
            scatter_start(b)

            @pl.when(t + 2 < nt)
            def _():
                code_copy(t + 2, b).start()

            return carry

        lax.fori_loop(0, nt, tile, 0)

        @pl.when(nt >= 2)
        def _():
            scatter_wait(lax.rem(nt, 2))

        scatter_wait(lax.rem(nt - 1, 2))


def _moe_experts(layer, n_tok, seg_start, tiles_per_e, code, x_units, w1, b1, w2, b2):
    tm = MOE_TILE
    D, F2, U = D_MODEL, 2 * D_FF, ROW_UNITS
    cw, rh = F2 // W1_SPLIT, D_FF // W2_SPLIT
    w1_specs = [pl.BlockSpec((1, 1, D, cw), (lambda c: (lambda e, sg, nt: (layer, e, 0, c)))(c))
                for c in range(W1_SPLIT)]
    w2_specs = [pl.BlockSpec((1, 1, rh, D), (lambda c: (lambda e, sg, nt: (layer, e, c, 0)))(c))
                for c in range(W2_SPLIT)]
    grid_spec = pltpu.PrefetchScalarGridSpec(
        num_scalar_prefetch=2,
        grid=(N_EXPERTS,),
        in_specs=[pl.BlockSpec(memory_space=pl.ANY), pl.BlockSpec(memory_space=pl.ANY)]
        + w1_specs + [pl.BlockSpec((1, 1, 1, F2), lambda e, sg, nt: (layer, e, 0, 0))]
        + w2_specs + [pl.BlockSpec((1, 1, 1, D), lambda e, sg, nt: (layer, e, 0, 0))],
        out_specs=pl.BlockSpec(memory_space=pl.ANY),
        scratch_shapes=[pltpu.VMEM((D, F2), BF16), pltpu.VMEM((D_FF, D), BF16), pltpu.VMEM((tm, D_FF), BF16),
                        pltpu.VMEM((2, tm * U, LANES), F32), pltpu.VMEM((2, tm * U, LANES), F32),
                        pltpu.SMEM((2, tm), jnp.int32),
                        pltpu.SemaphoreType.DMA((2,)), pltpu.SemaphoreType.DMA((2,)), pltpu.SemaphoreType.DMA((2,))],
    )
    n_out_rows = TOP_K * n_tok + 2 * tm
    return pl.pallas_call(
        functools.partial(_moe_kernel, n_tok=n_tok),
        grid_spec=grid_spec,
        out_shape=jax.ShapeDtypeStruct((n_out_rows * U, LANES), F32),
        compiler_params=_cparams(("arbitrary",)),
        name="moe_experts",
    )(seg_start, tiles_per_e, code, x_units, *([w1] * W1_SPLIT), b1.reshape(DEPTH, N_EXPERTS, 1, F2),
      *([w2] * W2_SPLIT), b2.reshape(DEPTH, N_EXPERTS, 1, D))


def _pos_kernel(idx_ref, rank_ref, seg_ref, pos_ref):
    lane = lax.broadcasted_iota(jnp.int32, idx_ref.shape, 1)
    idx, rank, seg = idx_ref[...], rank_ref[...], seg_ref[...]
    out = jnp.zeros(idx.shape, jnp.int32)
    for k in range(TOP_K):
        hit = lane == idx[:, k:k + 1]
        seg_k = jnp.sum(jnp.where(hit, seg, 0.0), axis=-1, keepdims=True).astype(jnp.int32)
        out = jnp.where(lane == k, seg_k + rank[:, k:k + 1], out)
    pos_ref[...] = out


def _pair_slots(idx, rank, seg_start):
    n_tok = idx.shape[0]
    tm = TOKEN_TILE
    row = lambda i: (i, 0)
    return pl.pallas_call(
        _pos_kernel,
        grid=(n_tok // tm,),
        in_specs=[pl.BlockSpec((tm, LANES), row), pl.BlockSpec((tm, LANES), row),
                  pl.BlockSpec((1, LANES), lambda i: (0, 0))],
        out_specs=pl.BlockSpec((tm, LANES), row),
        out_shape=jax.ShapeDtypeStruct((n_tok, LANES), jnp.int32),
        compiler_params=_cparams(("arbitrary",)),
        name="pair_slots",
    )(idx, rank, seg_start)


def _moe_layer(layer, hn_units, idx, rank, counts, w1, b1, w2, b2):
    n_tok = hn_units.shape[0] // ROW_UNITS
    assert n_tok & (n_tok - 1) == 0
    tm = MOE_TILE
    n_pairs = n_tok * TOP_K
    n_slots = n_pairs + N_EXPERTS * tm
    tiles_per_e = (counts + tm - 1) // tm
    seg_start = (jnp.cumsum(tiles_per_e) - tiles_per_e) * tm
    seg_f32 = jnp.pad(seg_start.astype(F32), (0, LANES - N_EXPERTS)).reshape(1, LANES)
    pos = _pair_slots(idx, rank, seg_f32)[:, :TOP_K]
    pair_code = (jnp.arange(TOP_K, dtype=jnp.int32)[None, :] * n_tok + jnp.arange(n_tok, dtype=jnp.int32)[:, None])
    unused = n_pairs + jnp.arange(n_slots, dtype=jnp.int32) % (2 * tm)
    code = unused.at[pos.reshape(-1)].set(pair_code.reshape(-1), mode="promise_in_bounds", unique_indices=True)
    return _moe_experts(layer, n_tok, seg_start, tiles_per_e, code, hn_units, w1, b1, w2, b2)


def _final_kernel(*refs, nb_ctx):
    x_ref = refs[0]
    y_refs = refs[1:1 + TOP_K]
    wt_ref, mod_ref, gain_ref, oa_ref, ob_ref = refs[1 + TOP_K:]
    D = D_MODEL
    i = pl.program_id(0)
    x = x_ref[...] + mod_ref[0][:, 5 * D:6 * D] * _moe_combine(y_refs, wt_ref)
    out = _rms(x, gain_ref[...])

    @pl.when(i < nb_ctx)
    def _():
        oa_ref[...] = out

    @pl.when(i >= nb_ctx)
    def _():
        ob_ref[...] = out


def _final(n_ctx, n_tok, lat_seq, x_new, y4, wts, mods, gain):
    tm = TOKEN_TILE
    nb, nb_ctx, tpl = n_tok // tm, n_ctx // tm, lat_seq // tm
    D = D_MODEL
    specs = [pl.BlockSpec((tm, D), lambda i: (i, 0))]
    specs += [pl.BlockSpec((tm * ROW_UNITS, LANES), (lambda k: (lambda i: (k * nb + i, 0)))(k)) for k in range(TOP_K)]
    specs += [pl.BlockSpec((tm, LANES), lambda i: (i, 0)),
              pl.BlockSpec((1, 1, 6 * D), lambda i: ((DEPTH - 1) * 8 + _cond_row(i, nb_ctx, tpl), 0, 0)),
              _resident((1, D))]
    return pl.pallas_call(
        functools.partial(_final_kernel, nb_ctx=nb_ctx),
        grid=(nb,),
        in_specs=specs,
        out_specs=[pl.BlockSpec((tm, D), lambda i: (jnp.minimum(i, nb_ctx - 1), 0)),
                   pl.BlockSpec((tm, D), lambda i: (jnp.maximum(i - nb_ctx, 0), 0))],
        out_shape=[jax.ShapeDtypeStruct((n_ctx, D), F32), jax.ShapeDtypeStruct((n_tok - n_ctx, D), F32)],
        compiler_params=_cparams(("arbitrary",)),
        name="final_norm",
    )(x_new, *([y4] * TOP_K), wts, mods, gain.reshape(1, D))


def _rope_tables(n_tokens):
    rows = n_tokens // GRID_W
    r = jnp.repeat(jnp.arange(rows, dtype=F32), GRID_W)
    col = jnp.tile(jnp.arange(GRID_W, dtype=F32), rows)
    nf = RET_DK // 4
    inv = ROPE_BASE ** (-jnp.arange(nf, dtype=F32) / nf)
    ang = jnp.concatenate([r[:, None] * inv, col[:, None] * inv], axis=-1)
    cos, sin = jnp.cos(ang), jnp.sin(ang)
    return jnp.concatenate([cos, cos], axis=-1), jnp.concatenate([-sin, sin], axis=-1)


def kernel(x_prompt, x_sample, c, state_gla, state_ret, state_hgrn, c_ctx, norm_mix, norm_ffn, ada_w, ada_b,
           w_in_even, w_out_even, gla_gk_w, gla_gk_b, gla_gain, ret_decay_exp, w_in_odd, w_out_odd,
           hgrn_lb_logits, hgrn_gain, router_w, router_b, moe_w1, moe_b1, moe_w2, moe_b2, final_norm):
    D = D_MODEL
    B_ctx, T_ctx, _ = x_prompt.shape
    B_lat, T_lat, _ = x_sample.shape
    n_ctx, n_lat = B_ctx * T_ctx, B_lat * T_lat
    n_tok = n_ctx + n_lat
    assert n_ctx % TOKEN_TILE == 0 and T_lat % TOKEN_TILE == 0 and B_lat + 1 <= 8
    assert T_ctx % SCAN_CHUNK == 0 and T_lat % SCAN_CHUNK == 0 and n_ctx % T_lat == 0

    cond8 = jnp.concatenate([c_ctx[None, :], c, jnp.zeros((8 - 1 - B_lat, D), F32)], axis=0)
    mods = _ada_mods(cond8, ada_w, ada_b).reshape(DEPTH * 8, 1, 6 * D)

    rw_pad = jnp.pad(router_w, ((0, 0), (0, 0), (0, LANES - N_EXPERTS)))
    rb_pad = jnp.pad(router_b, ((0, 0), (0, LANES - N_EXPERTS)), constant_values=-1e30)

    w_even = w_in_even[0]
    lr0 = GLA_HEADS * (2 * GLA_DK + GLA_DV)
    lr1 = lr0 + 2 * GLA_LOW_RANK
    w_main = jnp.concatenate([w_even[:, :lr0], w_even[:, lr1:]], axis=1)
    w_lr = jnp.pad(w_even[:, lr0:lr1], ((0, 0), (0, LANES - 2 * GLA_LOW_RANK)))
    nqk = GLA_HEADS * GLA_DK
    gkw = jnp.zeros((LANES, 2 * nqk), F32)
    gkw = gkw.at[0:GLA_LOW_RANK, 0:nqk].set(gla_gk_w[0, 0])
    gkw = gkw.at[GLA_LOW_RANK:2 * GLA_LOW_RANK, nqk:].set(gla_gk_w[0, 1])
    gkb = gla_gk_b[0].reshape(1, 2 * nqk)
    proj, gdec, x_cur = _in_proj(0, n_ctx, n_tok, T_lat, mods, norm_mix[0], w_main,
                                 x_parts=(x_prompt.reshape(n_ctx, D), x_sample.reshape(n_lat, D)),
                                 even_extra=(w_lr, gkw, gkb))

    y_gla_c, fin_gla = _gla_scan(proj, gdec, gla_gain[0], None, 0, B_ctx, T_ctx, True)
    (y_gla_l,) = _gla_scan(proj, gdec, gla_gain[0], state_gla[:, 0], n_ctx, B_lat, T_lat, False)
    dexp = jnp.broadcast_to(ret_decay_exp[0].T[:, :, None], (RET_HEADS, 2, LANES))
    y_ret_c, fin_ret = _ret_scan(proj, dexp, None, None, 0, B_ctx, T_ctx, True)
    (y_ret_l,) = _ret_scan(proj, dexp, _rope_tables(T_lat), state_ret[:, 0], n_ctx, B_lat, T_lat, False)

    x_new, hn, idx, wts, rank, cnt = _out_proj(0, n_ctx, n_tok, T_lat, [(y_gla_c, y_gla_l), (y_ret_c, y_ret_l)],
                                               w_out_even[0], x_cur, mods, norm_ffn[0], rw_pad[0], rb_pad[0:1])
    y4 = _moe_layer(0, hn, idx, rank, cnt[0, :N_EXPERTS], moe_w1, moe_b1, moe_w2, moe_b2)

    (proj, x_cur) = _in_proj(1, n_ctx, n_tok, T_lat, mods, norm_mix[1], w_in_odd[0], x_prev=x_new, y4=(y4, wts))
    y_h_c, fin_h = _hgrn_scan(proj, hgrn_lb_logits, hgrn_gain[0], None, 0, B_ctx, T_ctx, True)
    (y_h_l,) = _hgrn_scan(proj, hgrn_lb_logits, hgrn_gain[0], state_hgrn[:, 0], n_ctx, B_lat, T_lat, False)
    x_new, hn, idx, wts, rank, cnt = _out_proj(1, n_ctx, n_tok, T_lat, [(y_h_c, y_h_l)], w_out_odd[0], x_cur, mods,
                                               norm_ffn[1], rw_pad[1], rb_pad[1:2])
    y4 = _moe_layer(1, hn, idx, rank, cnt[0, :N_EXPERTS], moe_w1, moe_b1, moe_w2, moe_b2)

    y_ctx, y_lat = _final(n_ctx, n_tok, T_lat, x_new, y4, wts, mods, final_norm)

    new_state_gla = fin_gla.reshape(B_ctx, 1, 2, GLA_HEADS, GLA_DK, GLA_DV)
    new_state_ret = fin_ret.reshape(B_ctx, 1, 2, RET_HEADS, RET_DK, RET_DV)
    new_state_hgrn = fin_h.reshape(B_ctx, 1, 2, HG_HEADS, HG_DK, HG_DV)
    return (y_ctx.reshape(B_ctx, T_ctx, D), y_lat.reshape(B_lat, T_lat, D), new_state_gla, new_state_ret,
            new_state_hgrn)
```

```python
import functools

import jax
import jax.numpy as jnp
from jax import lax
from jax.experimental import pallas as pl
from jax.experimental.pallas import tpu as pltpu

F32 = jnp.float32
BF16 = jnp.bfloat16

D_MODEL = 1024
DEPTH = 2
GRID_W = 64
GLA_HEADS, GLA_DK, GLA_DV, GLA_LOW_RANK = 4, 64, 128, 16
GLA_NORMALIZER = 16.0
RET_HEADS, RET_DK, RET_DV = 4, 128, 128
ROPE_BASE = 10000.0
HG_HEADS, HG_DK, HG_DV = 8, 128, 128
N_EXPERTS, TOP_K, D_FF = 32, 4, 1024
SWIGLU_ALPHA, SWIGLU_LIMIT = 1.702, 7.0
EPS = 1e-6

LANES = 128
SCAN_CHUNK = 128
SCAN_ROWS = 1024
TOKEN_TILE = 256
MOE_TILE = 512
VMEM_LIMIT = 56 * 1024 * 1024

_GQ, _GK, _GV, _GG, _RQ, _RK, _RV, _RG, _EVEN_MAIN = 0, 256, 512, 1024, 1536, 2048, 2560, 3072, 3584
_HQ, _HFF, _HFB, _HI, _HG, _ODD_MAIN = 0, 1024, 2048, 3072, 4096, 5120


def _dot(a, b):
    return jnp.dot(a, b, preferred_element_type=F32)


def _dot_nt(a, b):
    return lax.dot_general(a, b, (((1,), (1,)), ((), ())), preferred_element_type=F32)


def _rms(x, gain=None):
    y = x * lax.rsqrt(jnp.mean(x * x, axis=-1, keepdims=True) + EPS)
    if gain is not None:
        y = y * gain
    return y


def _sigmoid(x):
    return 1.0 / (1.0 + jnp.exp(-x))


def _silu(x):
    return x * _sigmoid(x)


def _cparams(sem, vmem=VMEM_LIMIT):
    return pltpu.CompilerParams(dimension_semantics=sem, vmem_limit_bytes=vmem)


def _resident(shape):
    nd = len(shape)
    return pl.BlockSpec(shape, lambda *_: (0,) * nd, pipeline_mode=pl.Buffered(1))


def _ada_kernel(c_ref, w_ref, b_ref, o_ref):
    o_ref[0] = _dot(_silu(c_ref[...]), w_ref[0]) + b_ref[0]


def _ada_mods(cond8, ada_w, ada_b):
    tn = 1536
    return pl.pallas_call(
        _ada_kernel,
        grid=(DEPTH, 6 * D_MODEL // tn),
        in_specs=[
            pl.BlockSpec((8, D_MODEL), lambda l, j: (0, 0)),
            pl.BlockSpec((1, D_MODEL, tn), lambda l, j: (l, 0, j)),
            pl.BlockSpec((1, 1, tn), lambda l, j: (l, 0, j)),
        ],
        out_specs=pl.BlockSpec((1, 8, tn), lambda l, j: (l, 0, j)),
        out_shape=jax.ShapeDtypeStruct((DEPTH, 8, 6 * D_MODEL), F32),
        compiler_params=_cparams(("arbitrary", "arbitrary")),
        name="ada_mods",
    )(cond8, ada_w, ada_b.reshape(DEPTH, 1, 6 * D_MODEL))


ROW_UNITS = D_MODEL // LANES


def _rows_from_units(ref, n_rows):
    return jnp.concatenate([ref[pl.ds(c, n_rows, stride=ROW_UNITS), :] for c in range(ROW_UNITS)], axis=1)


def _rows_to_units(ref, val):
    n_rows = val.shape[0]
    for c in range(ROW_UNITS):
        ref[pl.ds(c, n_rows, stride=ROW_UNITS), :] = val[:, c * LANES:(c + 1) * LANES]


def _moe_combine(y_refs, wt_ref):
    w = wt_ref[...]
    terms = [w[:, k:k + 1] * _rows_from_units(y_refs[k], w.shape[0]) for k in range(TOP_K)]
    return (terms[0] + terms[1]) + (terms[2] + terms[3])


def _in_proj_kernel(*refs, first, even, nb_ctx):
    it = iter(refs)
    if first:
        xa_ref, xb_ref = next(it), next(it)
    else:
        xp_ref = next(it)
        y_refs = [next(it) for _ in range(TOP_K)]
        wt_ref, modp_ref = next(it), next(it)
    gain_ref, mod_ref, w_ref = next(it), next(it), next(it)
    if even:
        wlr_ref, gkw_ref, gkb_ref = next(it), next(it), next(it)
    proj_ref = next(it)
    if even:
        g_ref = next(it)
    xcur_ref = next(it)

    D = D_MODEL
    i = pl.program_id(0)
    if first:
        x = jnp.where(i < nb_ctx, xa_ref[...], xb_ref[...])
    else:
        x = xp_ref[...] + modp_ref[0][:, 5 * D:6 * D] * _moe_combine(y_refs, wt_ref)
    xcur_ref[...] = x
    m = mod_ref[0]
    hn = _rms(x, gain_ref[...]) * (1.0 + m[:, D:2 * D]) + m[:, 0:D]
    proj_ref[...] = _dot(hn, w_ref[...])
    if even:
        z = _dot(_dot(hn, wlr_ref[...]), gkw_ref[...]) + gkb_ref[...]
        g_ref[...] = (jnp.minimum(z, 0.0) - jnp.log(1.0 + jnp.exp(-jnp.abs(z)))) * (1.0 / GLA_NORMALIZER)


def _cond_row(i, nb_ctx, tiles_per_lat_seq):
    return jnp.where(i < nb_ctx, 0, 1 + (i - nb_ctx) // tiles_per_lat_seq)


def _in_proj(layer, n_ctx, n_tok, lat_seq, mods, gain, w_main, *, x_parts=None, x_prev=None, y4=None, even_extra=None):
    first = x_parts is not None
    even = even_extra is not None
    tm = TOKEN_TILE
    nb = n_tok // tm
    nb_ctx = n_ctx // tm
    tpl = lat_seq // tm
    np_cols = w_main.shape[1]
    D = D_MODEL

    def mod_map(l):
        return lambda i: (l * 8 + _cond_row(i, nb_ctx, tpl), 0, 0)

    row = lambda i: (i, 0)
    args, specs = [], []
    if first:
        xa, xb = x_parts
        args += [xa, xb]
        specs += [pl.BlockSpec((tm, D), lambda i: (jnp.minimum(i, nb_ctx - 1), 0)),
                  pl.BlockSpec((tm, D), lambda i: (jnp.maximum(i - nb_ctx, 0), 0))]
    else:
        y4, wts = y4
        args += [x_prev] + [y4] * TOP_K + [wts, mods]
        specs += [pl.BlockSpec((tm, D), row)]
        specs += [pl.BlockSpec((tm * ROW_UNITS, LANES), (lambda k: (lambda i: (k * nb + i, 0)))(k))
                  for k in range(TOP_K)]
        specs += [pl.BlockSpec((tm, LANES), row), pl.BlockSpec((1, 1, 6 * D), mod_map(layer - 1))]
    args += [gain.reshape(1, D), mods, w_main]
    specs += [_resident((1, D)), pl.BlockSpec((1, 1, 6 * D), mod_map(layer)), _resident((D, np_cols))]
    out_shapes = [jax.ShapeDtypeStruct((n_tok, np_cols), F32)]
    out_specs = [pl.BlockSpec((tm, np_cols), row)]
    if even:
        w_lr, gkw, gkb = even_extra
        args += [w_lr, gkw, gkb]
        specs += [_resident(w_lr.shape), _resident(gkw.shape), _resident(gkb.shape)]
        out_shapes.append(jax.ShapeDtypeStruct((n_tok, 2 * GLA_HEADS * GLA_DK), F32))
        out_specs.append(pl.BlockSpec((tm, 2 * GLA_HEADS * GLA_DK), row))
    out_shapes.append(jax.ShapeDtypeStruct((n_tok, D), F32))
    out_specs.append(pl.BlockSpec((tm, D), row))
    return pl.pallas_call(
        functools.partial(_in_proj_kernel, first=first, even=even, nb_ctx=nb_ctx),
        grid=(nb,),
        in_specs=specs,
        out_specs=out_specs,
        out_shape=out_shapes,
        compiler_params=_cparams(("arbitrary",)),
        name=f"in_proj_l{layer}",
    )(*args)


def _tri(c, lower):
    r = lax.broadcasted_iota(jnp.int32, (c, c), 0)
    s = lax.broadcasted_iota(jnp.int32, (c, c), 1)
    return (r >= s) if lower else (r <= s)


def _cumsum_mm(tri_bf16, g):
    g_hi = g.astype(BF16)
    g_lo = (g - g_hi.astype(F32)).astype(BF16)
    r = _dot(tri_bf16, jnp.concatenate([g_hi, g_lo], axis=1))
    w = g.shape[1]
    return r[:, :w] + r[:, w:]


def _gated_scan_kernel(*refs, variant, T, nseq, has_s0, want_final, hpb):
    C = SCAN_CHUNK
    ncs = T // C
    nc = nseq * ncs
    it = iter(refs)
    if variant == "gla":
        q_ref, k_ref, v_ref, og_ref, gf_ref, gb_ref, gain_ref = (next(it) for _ in range(7))
    else:
        q_ref, ff_ref, fb_ref, v_ref, og_ref, lbl_ref, gain_ref = (next(it) for _ in range(7))
    s0_ref = next(it) if has_s0 else None
    y_ref = next(it)
    sfin_ref = next(it) if want_final else None
    qi_scr, o_scr, u_scr, dec_scr, sin_scr = (next(it) for _ in range(5))

    lane = lax.broadcasted_iota(jnp.int32, (1, LANES), 1)
    if hpb == 2:
        masks = [lane < GLA_DK, lane >= GLA_DK]
    else:
        masks = [None]
    tri_l, tri_u = _tri(C, True), _tri(C, False)
    tri_l16, tri_u16 = tri_l.astype(F32).astype(BF16), tri_u.astype(F32).astype(BF16)
    mid_f, mid_b = C // 2 - 1, C // 2

    if variant == "hgrn":
        lgs = [lbl_ref[l] for l in range(DEPTH)]
        mx = functools.reduce(jnp.maximum, lgs)
        es = [jnp.exp(l - mx) for l in lgs]
        tot = functools.reduce(lambda a, b: a + b, es)
        ps = [e / tot for e in es]
        layer = DEPTH - 1
        lb = functools.reduce(lambda a, b: a + b, ps[:layer + 1]) - ps[0]
        lb_f, lb_b = lb[0:1], lb[1:2]

    for n in range(nc):
        rows = pl.ds(n * C, C)
        if variant == "gla":
            q = q_ref[rows, :] * (GLA_DK ** -0.5)
            kf = kb = k_ref[rows, :]
            gf, gb = gf_ref[rows, :], gb_ref[rows, :]
        else:
            q = _silu(q_ref[rows, :])
            f_f = lb_f + (1.0 - lb_f) * _sigmoid(ff_ref[rows, :])
            f_b = lb_b + (1.0 - lb_b) * _sigmoid(fb_ref[rows, :])
            kf, kb = 1.0 - f_f, 1.0 - f_b
            gf, gb = jnp.log(f_f), jnp.log(f_b)
        bf = _cumsum_mm(tri_l16, gf)
        bb = _cumsum_mm(tri_u16, gb)
        bf_mid, bf_end = bf[mid_f:mid_f + 1], bf[C - 1:C]
        bb_mid, bb_end = bb[mid_b:mid_b + 1], bb[0:1]
        qd_f, kd_f = q * jnp.exp(bf - bf_mid), kf * jnp.exp(bf_mid - bf)
        qd_b, kd_b = q * jnp.exp(bb - bb_mid), kb * jnp.exp(bb_mid - bb)
        kend = jnp.concatenate([kf * jnp.exp(bf_end - bf), kb * jnp.exp(bb_end - bb)], axis=1)
        qi_scr[rows, :] = jnp.concatenate([q * jnp.exp(bf), q * jnp.exp(bb)], axis=1)
        dec_scr[n] = jnp.broadcast_to(jnp.concatenate([jnp.exp(bf_end), jnp.exp(bb_end)], axis=1), (8, 2 * LANES))
        for h in range(hpb):
            v = v_ref[rows, h * LANES:(h + 1) * LANES]
            if masks[h] is None:
                qf_h, qb_h, kend_h = qd_f, qd_b, kend
            else:
                qf_h, qb_h = jnp.where(masks[h], qd_f, 0.0), jnp.where(masks[h], qd_b, 0.0)
                kend_h = jnp.where(jnp.concatenate([masks[h], masks[h]], axis=1), kend, 0.0)
            s = jnp.where(tri_l, _dot_nt(qf_h, kd_f), 0.0) + jnp.where(tri_u, _dot_nt(qb_h, kd_b), 0.0)
            o_scr[rows, h * LANES:(h + 1) * LANES] = _dot(s, v)
            u_scr[h, n] = _dot(v.T, kend_h)

    for sq in range(nseq):
        for h in range(hpb):
            if has_s0:
                s_f, s_b = s0_ref[sq, 0, 0].T, s0_ref[sq, 1, 0].T
                if masks[h] is not None:
                    s_f, s_b = jnp.where(masks[h], s_f, 0.0), jnp.where(masks[h], s_b, 0.0)
            else:
                s_f = s_b = jnp.zeros((LANES, LANES), F32)
            for n in range(sq * ncs, (sq + 1) * ncs):
                sin_scr[h, n, :, 0:LANES] = s_f
                s_f = s_f * dec_scr[n, 0:1, 0:LANES] + u_scr[h, n, :, 0:LANES]
            for n in reversed(range(sq * ncs, (sq + 1) * ncs)):
                sin_scr[h, n, :, LANES:2 * LANES] = s_b
                s_b = s_b * dec_scr[n, 0:1, LANES:2 * LANES] + u_scr[h, n, :, LANES:2 * LANES]
            if want_final:
                if h == 0:
                    fin_f, fin_b = s_f, s_b
                else:
                    fin_f, fin_b = fin_f + s_f, fin_b + s_b
        if want_final:
            sfin_ref[sq, 0, 0] = fin_f.T
            sfin_ref[sq, 1, 0] = fin_b.T

    gain = gain_ref[...]
    for n in range(nc):
        rows = pl.ds(n * C, C)
        qi = qi_scr[rows, :]
        for h in range(hpb):
            cols = slice(h * LANES, (h + 1) * LANES)
            o = o_scr[rows, cols] + _dot_nt(qi, sin_scr[h, n])
            y_ref[rows, cols] = (_rms(o, gain) * _silu(og_ref[rows, cols])).astype(y_ref.dtype)


def _scan_scratch(T, hpb):
    nc = T // SCAN_CHUNK
    return [
        pltpu.VMEM((T, 2 * LANES), F32),
        pltpu.VMEM((T, hpb * LANES), F32),
        pltpu.VMEM((hpb, nc, LANES, 2 * LANES), F32),
        pltpu.VMEM((nc, 8, 2 * LANES), F32),
        pltpu.VMEM((hpb, nc, LANES, 2 * LANES), F32),
    ]


def _seqs_per_step(B, T):
    nseq = max(1, min(B, SCAN_ROWS // T))
    assert B % nseq == 0
    return nseq


def _gla_scan(proj, gdec, gain, s0, row0, B, T, want_final):
    nseq = _seqs_per_step(B, T)
    R = nseq * T
    rb0 = row0 // R
    assert row0 % R == 0
    has_s0 = s0 is not None

    def col(base, width=LANES):
        return lambda b, j: (rb0 + b, base // width + j)

    args = [proj, proj, proj, proj, gdec, gdec, gain.reshape(1, GLA_DV)]
    specs = [pl.BlockSpec((R, LANES), col(_GQ)), pl.BlockSpec((R, LANES), col(_GK)),
             pl.BlockSpec((R, 2 * LANES), col(_GV, 2 * LANES)), pl.BlockSpec((R, 2 * LANES), col(_GG, 2 * LANES)),
             pl.BlockSpec((R, LANES), col(0)), pl.BlockSpec((R, LANES), col(GLA_HEADS * GLA_DK)),
             pl.BlockSpec((1, GLA_DV), lambda b, j: (0, 0))]
    st_spec = pl.BlockSpec((nseq, 2, 1, LANES, LANES), lambda b, j: (b, 0, j, 0, 0))
    if has_s0:
        args.append(s0.reshape(B, 2, GLA_HEADS // 2, 2 * GLA_DK, GLA_DV))
        specs.append(st_spec)
    out_shapes = [jax.ShapeDtypeStruct((B * T, GLA_HEADS * GLA_DV), BF16)]
    out_specs = [pl.BlockSpec((R, 2 * LANES), lambda b, j: (b, j))]
    if want_final:
        out_shapes.append(jax.ShapeDtypeStruct((B, 2, GLA_HEADS // 2, 2 * GLA_DK, GLA_DV), F32))
        out_specs.append(st_spec)
    kern = functools.partial(_gated_scan_kernel, variant="gla", T=T, nseq=nseq, has_s0=has_s0,
                             want_final=want_final, hpb=2)
    return pl.pallas_call(
        kern, grid=(B // nseq, GLA_HEADS // 2), in_specs=specs, out_specs=out_specs, out_shape=out_shapes,
        scratch_shapes=_scan_scratch(R, 2),
        compiler_params=_cparams(("arbitrary", "arbitrary")), name=f"gla_scan_T{T}",
    )(*args)


def _hgrn_scan(proj, lb_logits, gain, s0, row0, B, T, want_final):
    nseq = _seqs_per_step(B, T)
    R = nseq * T
    rb0 = row0 // R
    assert row0 % R == 0
    has_s0 = s0 is not None

    def col(base):
        return lambda b, j: (rb0 + b, base // LANES + j)

    args = [proj, proj, proj, proj, proj, lb_logits, gain.reshape(1, HG_DV)]
    specs = [pl.BlockSpec((R, LANES), col(_HQ)), pl.BlockSpec((R, LANES), col(_HFF)),
             pl.BlockSpec((R, LANES), col(_HFB)), pl.BlockSpec((R, LANES), col(_HI)),
             pl.BlockSpec((R, LANES), col(_HG)),
             pl.BlockSpec((DEPTH, 2, LANES), lambda b, j: (0, 0, j)),
             pl.BlockSpec((1, HG_DV), lambda b, j: (0, 0))]
    st_spec = pl.BlockSpec((nseq, 2, 1, LANES, LANES), lambda b, j: (b, 0, j, 0, 0))
    if has_s0:
        args.append(s0.reshape(B, 2, HG_HEADS, HG_DK, HG_DV))
        specs.append(st_spec)
    out_shapes = [jax.ShapeDtypeStruct((B * T, HG_HEADS * HG_DV), BF16)]
    out_specs = [pl.BlockSpec((R, LANES), lambda b, j: (b, j))]
    if want_final:
        out_shapes.append(jax.ShapeDtypeStruct((B, 2, HG_HEADS, HG_DK, HG_DV), F32))
        out_specs.append(st_spec)
    kern = functools.partial(_gated_scan_kernel, variant="hgrn", T=T, nseq=nseq, has_s0=has_s0,
                             want_final=want_final, hpb=1)
    return pl.pallas_call(
        kern, grid=(B // nseq, HG_HEADS), in_specs=specs, out_specs=out_specs, out_shape=out_shapes,
        scratch_shapes=_scan_scratch(R, 1),
        compiler_params=_cparams(("arbitrary", "arbitrary")), name=f"hgrn_scan_T{T}",
    )(*args)


def _ret_scan_kernel(*refs, T, nseq, has_s0, want_final, rope):
    C = SCAN_CHUNK
    ncs = T // C
    nc = nseq * ncs
    it = iter(refs)
    q_ref, k_ref, v_ref, og_ref, dexp_ref = (next(it) for _ in range(5))
    if rope:
        cos_ref, sin_ref = next(it), next(it)
    s0_ref = next(it) if has_s0 else None
    y_ref = next(it)
    sfin_ref = next(it) if want_final else None
    qi_scr, o_scr, u_scr, sin_scr = (next(it) for _ in range(4))

    lg = jnp.log1p(-jnp.exp2(-dexp_ref[0]))
    lg_f, lg_b = lg[0:1], lg[1:2]
    r = lax.broadcasted_iota(jnp.int32, (C, C), 0)
    s = lax.broadcasted_iota(jnp.int32, (C, C), 1)
    dist = (r - s).astype(F32)
    dmask = (jnp.where(r >= s, jnp.exp(jnp.maximum(dist, 0.0) * lg_f[:, 0:1]), 0.0)
             + jnp.where(r <= s, jnp.exp(jnp.maximum(-dist, 0.0) * lg_b[:, 0:1]), 0.0))
    pos = lax.broadcasted_iota(jnp.int32, (C, LANES), 0).astype(F32)
    xi = jnp.concatenate([jnp.exp((pos + 1.0) * lg_f), jnp.exp((C - pos) * lg_b)], axis=1)
    zeta = jnp.concatenate([jnp.exp((C - 1.0 - pos) * lg_f), jnp.exp(pos * lg_b)], axis=1)
    d_f, d_b = jnp.exp(C * lg_f), jnp.exp(C * lg_b)

    def rot(x, seq_rows):
        if not rope:
            return x
        return x * cos_ref[seq_rows, :] + pltpu.roll(x, RET_DK // 2, axis=1) * sin_ref[seq_rows, :]

    for n in range(nc):
        rows = pl.ds(n * C, C)
        seq_rows = pl.ds((n % ncs) * C, C)
        q = rot(q_ref[rows, :], seq_rows)
        k = rot(k_ref[rows, :] * (RET_DK ** -0.5), seq_rows)
        v = v_ref[rows, :]
        o_scr[rows, :] = _dot(_dot_nt(q, k) * dmask, v)
        qi_scr[rows, :] = jnp.concatenate([q, q], axis=1) * xi
        u_scr[n] = _dot(v.T, jnp.concatenate([k, k], axis=1) * zeta)

    for sq in range(nseq):
        if has_s0:
            s_f, s_b = s0_ref[sq, 0, 0].T, s0_ref[sq, 1, 0].T
        else:
            s_f = s_b = jnp.zeros((LANES, LANES), F32)
        for n in range(sq * ncs, (sq + 1) * ncs):
            sin_scr[n, :, 0:LANES] = s_f
            s_f = s_f * d_f + u_scr[n, :, 0:LANES]
        for n in reversed(range(sq * ncs, (sq + 1) * ncs)):
            sin_scr[n, :, LANES:2 * LANES] = s_b
            s_b = s_b * d_b + u_scr[n, :, LANES:2 * LANES]
        if want_final:
            sfin_ref[sq, 0, 0] = s_f.T
            sfin_ref[sq, 1, 0] = s_b.T

    for n in range(nc):
        rows = pl.ds(n * C, C)
        o = o_scr[rows, :] + _dot_nt(qi_scr[rows, :], sin_scr[n])
        y_ref[rows, :] = (_rms(o) * _silu(og_ref[rows, :])).astype(y_ref.dtype)


def _ret_scan(proj, dexp, rope_tabs, s0, row0, B, T, want_final):
    nseq = _seqs_per_step(B, T)
    R = nseq * T
    rb0 = row0 // R
    assert row0 % R == 0
    nc = R // SCAN_CHUNK
    has_s0 = s0 is not None
    rope = rope_tabs is not None

    def col(base):
        return lambda b, j: (rb0 + b, base // LANES + j)

    args = [proj, proj, proj, proj, dexp]
    specs = [pl.BlockSpec((R, LANES), col(_RQ)), pl.BlockSpec((R, LANES), col(_RK)),
             pl.BlockSpec((R, LANES), col(_RV)), pl.BlockSpec((R, LANES), col(_RG)),
             pl.BlockSpec((1, 2, LANES), lambda b, j: (j, 0, 0))]
    if rope:
        args += list(rope_tabs)
        specs += [pl.BlockSpec((T, LANES), lambda b, j: (0, 0))] * 2
    st_spec = pl.BlockSpec((nseq, 2, 1, LANES, LANES), lambda b, j: (b, 0, j, 0, 0))
    if has_s0:
        args.append(s0.reshape(B, 2, RET_HEADS, RET_DK, RET_DV))
        specs.append(st_spec)
    out_shapes = [jax.ShapeDtypeStruct((B * T, RET_HEADS * RET_DV), BF16)]
    out_specs = [pl.BlockSpec((R, LANES), lambda b, j: (b, j))]
    if want_final:
        out_shapes.append(jax.ShapeDtypeStruct((B, 2, RET_HEADS, RET_DK, RET_DV), F32))
        out_specs.append(st_spec)
    kern = functools.partial(_ret_scan_kernel, T=T, nseq=nseq, has_s0=has_s0, want_final=want_final, rope=rope)
    scratch = [pltpu.VMEM((R, 2 * LANES), F32), pltpu.VMEM((R, LANES), F32),
               pltpu.VMEM((nc, LANES, 2 * LANES), F32), pltpu.VMEM((nc, LANES, 2 * LANES), F32)]
    return pl.pallas_call(
        kern, grid=(B // nseq, RET_HEADS), in_specs=specs, out_specs=out_specs, out_shape=out_shapes,
        scratch_shapes=scratch,
        compiler_params=_cparams(("arbitrary", "arbitrary")), name=f"ret_scan_T{T}",
    )(*args)


def _out_proj_kernel(*refs, n_mix, nb_ctx):
    it = iter(refs)
    y_refs = [(next(it), next(it)) for _ in range(n_mix)]
    wo_ref, x_ref, mod_ref, gain_ref, rw_ref, rb_ref = (next(it) for _ in range(6))
    xnew_ref, hn_ref, idx_ref, wt_ref, rank_ref, cnt_ref = (next(it) for _ in range(6))
    cnt_scr = next(it)
    D = D_MODEL
    is_ctx = pl.program_id(0) < nb_ctx
    mix = None
    r0 = 0
    for ya_ref, yb_ref in y_refs:
        w = ya_ref.shape[1]
        part = _dot(jnp.where(is_ctx, ya_ref[...], yb_ref[...]), wo_ref[r0:r0 + w, :])
        mix = part if mix is None else mix + part
        r0 += w
    m = mod_ref[0]
    xn = x_ref[...] + m[:, 2 * D:3 * D] * mix
    xnew_ref[...] = xn
    hn = _rms(xn, gain_ref[...]) * (1.0 + m[:, 4 * D:5 * D]) + m[:, 3 * D:4 * D]
    hn_hi = hn.astype(BF16)
    _rows_to_units(hn_ref, hn)
    hn_lo = (hn - hn_hi.astype(F32)).astype(BF16)
    rw = rw_ref[...]
    rw_hi = rw.astype(BF16)
    rw_lo = (rw - rw_hi.astype(F32)).astype(BF16)
    logits = (_dot(hn_hi, rw_hi) + (_dot(hn_lo, rw_hi) + _dot(hn_hi, rw_lo))) + rb_ref[...]
    lane = lax.broadcasted_iota(jnp.int32, logits.shape, 1).astype(F32)
    vals, idxs = [], []
    cur = logits
    for _ in range(TOP_K):
        mx = jnp.max(cur, axis=-1, keepdims=True)
        ik = jnp.min(jnp.where(cur == mx, lane, float(LANES)), axis=-1, keepdims=True)
        vals.append(mx)
        idxs.append(ik)
        cur = jnp.where(lane == ik, -jnp.inf, cur)
    es = [jnp.exp(v - vals[0]) for v in vals]
    tot = (es[0] + es[1]) + (es[2] + es[3])
    idx_out = jnp.zeros(logits.shape, F32)
    wt_out = jnp.zeros(logits.shape, F32)
    for k in range(TOP_K):
        idx_out = jnp.where(lane == float(k), idxs[k], idx_out)
        wt_out = jnp.where(lane == float(k), es[k] / tot, wt_out)
    idx_ref[...] = idx_out.astype(jnp.int32)
    wt_ref[...] = wt_out

    @pl.when(pl.program_id(0) == 0)
    def _():
        cnt_scr[...] = jnp.zeros(cnt_scr.shape, F32)

    tm = logits.shape[0]
    hits = [lane == idxs[k] for k in range(TOP_K)]
    sel = jnp.zeros(logits.shape, F32)
    for k in range(TOP_K):
        sel = sel + jnp.where(hits[k], 1.0, 0.0)
    rr = lax.broadcasted_iota(jnp.int32, (tm, tm), 0)
    cc = lax.broadcasted_iota(jnp.int32, (tm, tm), 1)
    before = jnp.where(rr > cc, 1.0, 0.0).astype(BF16)
    rank_all = cnt_scr[0:1, :] + _dot(before, sel.astype(BF16))
    rank_out = jnp.zeros(logits.shape, F32)
    for k in range(TOP_K):
        rk = jnp.sum(jnp.where(hits[k], rank_all, 0.0), axis=-1, keepdims=True)
        rank_out = jnp.where(lane == float(k), rk, rank_out)
    rank_ref[...] = rank_out.astype(jnp.int32)
    total = cnt_scr[...] + jnp.sum(sel, axis=0, keepdims=True)
    cnt_scr[...] = total
    cnt_ref[...] = total.astype(jnp.int32)


def _out_proj(layer, n_ctx, n_tok, lat_seq, ys, w_out, x_cur, mods, gain, rw_pad, rb_pad):
    tm = TOKEN_TILE
    nb, nb_ctx, tpl = n_tok // tm, n_ctx // tm, lat_seq // tm
    D = D_MODEL
    row = lambda i: (i, 0)
    specs = []
    for ya, _ in ys:
        specs += [pl.BlockSpec((tm, ya.shape[1]), lambda i: (jnp.minimum(i, nb_ctx - 1), 0)),
                  pl.BlockSpec((tm, ya.shape[1]), lambda i: (jnp.maximum(i - nb_ctx, 0), 0))]
    specs += [_resident((D, D)), pl.BlockSpec((tm, D), row),
              pl.BlockSpec((1, 1, 6 * D), lambda i: (layer * 8 + _cond_row(i, nb_ctx, tpl), 0, 0)),
              _resident((1, D)), _resident((D, LANES)), _resident((1, LANES))]
    return pl.pallas_call(
        functools.partial(_out_proj_kernel, n_mix=len(ys), nb_ctx=nb_ctx),
        grid=(nb,),
        in_specs=specs,
        out_specs=[pl.BlockSpec((tm, D), row), pl.BlockSpec((tm * ROW_UNITS, LANES), row),
                   pl.BlockSpec((tm, LANES), row), pl.BlockSpec((tm, LANES), row),
                   pl.BlockSpec((tm, LANES), row), pl.BlockSpec((8, LANES), lambda i: (0, 0))],
        out_shape=[jax.ShapeDtypeStruct((n_tok, D), F32), jax.ShapeDtypeStruct((n_tok * ROW_UNITS, LANES), F32),
                   jax.ShapeDtypeStruct((n_tok, LANES), jnp.int32), jax.ShapeDtypeStruct((n_tok, LANES), F32),
                   jax.ShapeDtypeStruct((n_tok, LANES), jnp.int32), jax.ShapeDtypeStruct((8, LANES), jnp.int32)],
        scratch_shapes=[pltpu.VMEM((8, LANES), F32)],
        compiler_params=_cparams(("arbitrary",)),
        name=f"out_proj_l{layer}",
    )(*[y for pair in ys for y in pair], w_out, x_cur, mods, gain.reshape(1, D), rw_pad, rb_pad)


W1_SPLIT = 4
W2_SPLIT = 2


def _moe_kernel(seg_ref, nt_ref, code_hbm, x_hbm, *refs, n_tok):
    w1_refs = refs[:W1_SPLIT]
    b1_ref = refs[W1_SPLIT]
    w2_refs = refs[W1_SPLIT + 1:W1_SPLIT + 1 + W2_SPLIT]
    b2_ref, y_hbm, w1_scr, w2_scr, act_scr, xbuf, obuf, code_smem, csem, gsem, ssem = refs[W1_SPLIT + 1 + W2_SPLIT:]
    tm = MOE_TILE
    U = ROW_UNITS
    n_code_tiles = code_hbm.shape[0] // tm
    e = pl.program_id(0)
    nt = nt_ref[e]
    g_first = seg_ref[e] // tm

    def code_copy(g, q):
        first = pl.multiple_of(jnp.minimum(g, n_code_tiles - 1) * tm, tm)
        return pltpu.make_async_copy(code_hbm.at[pl.ds(first, tm)], code_smem.at[q], csem.at[q])

    def gather_start(b, q, rows=range(MOE_TILE)):
        for r in rows:
            tok = jnp.bitwise_and(code_smem[q, r], n_tok - 1)
            pltpu.make_async_copy(x_hbm.at[pl.ds(pl.multiple_of(tok * U, U), U)],
                                  xbuf.at[b, pl.ds(r * U, U)], gsem.at[b]).start()

    def gather_wait(b):
        pltpu.make_async_copy(x_hbm.at[pl.ds(0, tm * U)], xbuf.at[b], gsem.at[b]).wait()

    def scatter_start(b, q, rows=range(MOE_TILE)):
        for r in rows:
            dst = code_smem[q, r]
            pltpu.make_async_copy(obuf.at[b, pl.ds(r * U, U)],
                                  y_hbm.at[pl.ds(pl.multiple_of(dst * U, U), U)], ssem.at[b]).start()

    def scatter_wait(b):
        pltpu.make_async_copy(obuf.at[b], y_hbm.at[pl.ds(0, tm * U)], ssem.at[b]).wait()

    @pl.when(e == 0)
    def _():
        obuf[...] = jnp.zeros(obuf.shape, obuf.dtype)
        for h in range(2):
            cp = pltpu.make_async_copy(obuf.at[0], y_hbm.at[pl.ds((TOP_K * n_tok + h * tm) * U, tm * U)], ssem.at[0])
            cp.start()
            cp.wait()
        code_copy(0, 0).start()
        code_copy(1, 1).start()
        code_copy(n_code_tiles - 1, 3).start()
        code_copy(0, 0).wait()
        code_copy(n_code_tiles - 1, 3).wait()
        gather_start(0, 0)

    @pl.when(nt > 0)
    def _():
        cw = 2 * D_FF // W1_SPLIT
        for c, w_ref in enumerate(w1_refs):
            w1_scr[:, c * cw:(c + 1) * cw] = w_ref[0, 0].astype(BF16)
        rh = D_FF // W2_SPLIT
        for c, w_ref in enumerate(w2_refs):
            w2_scr[c * rh:(c + 1) * rh, :] = w_ref[0, 0].astype(BF16)

        blk = 4 * LANES
        rr = lax.broadcasted_iota(jnp.int32, (blk, blk // 2), 0)
        cc = lax.broadcasted_iota(jnp.int32, (blk, blk // 2), 1)
        sel = jnp.where(rr == 2 * cc, 1.0, 0.0).astype(BF16)

        def tile(t, carry):
            g = g_first + t
            b = lax.rem(g, 2)
            @pl.when(g >= 0)
            def _():
                code_copy(g + 1, lax.rem(g + 1, 4)).wait()
                gather_start(1 - b, lax.rem(g + 1, 4))
                scatter_start(1 - b, lax.rem(g + 3, 4))

            gather_wait(b)
            x = _rows_from_units(xbuf.at[b], tm).astype(BF16)
            for c in range(2 * D_FF // blk):
                cols = slice(c * blk, (c + 1) * blk)
                hid = _dot(x, w1_scr[:, cols]) + b1_ref[0, 0][:, cols]
                nxt = pltpu.roll(hid, blk - 1, axis=1)
                glu = jnp.minimum(hid, SWIGLU_LIMIT)
                lin = jnp.clip(nxt, -SWIGLU_LIMIT, SWIGLU_LIMIT)
                act = glu * _sigmoid(SWIGLU_ALPHA * glu) * (lin + 1.0)
                act_scr[:, c * (blk // 2):(c + 1) * (blk // 2)] = _dot(act.astype(BF16), sel).astype(BF16)
            y = _dot(act_scr[...], w2_scr[...]) + b2_ref[0, 0]

            @pl.when(g >= 1)
            def _():
                scatter_wait(b)

            _rows_to_units(obuf.at[b], y)
            code_copy(g + 2, lax.rem(g + 2, 4)).start()
            return carry

        lax.fori_loop(0, nt, tile, 0)

    @pl.when(e == N_EXPERTS - 1)
    def _():
        g_end = g_first + nt
        last = lax.rem(g_end + 1, 2)
        scatter_start(last, lax.rem(g_end + 3, 4))
        scatter_wait(last)
        scatter_wait(1 - last)
        gather_wait(1 - last)
        code_copy(g_end + 1, lax.rem(g_end + 1, 4)).wait()


def _moe_experts(layer, n_tok, seg_start, tiles_per_e, code, x_units, w1, b1, w2, b2):
    tm = MOE_TILE
    D, F2, U = D_MODEL, 2 * D_FF, ROW_UNITS
    cw, rh = F2 // W1_SPLIT, D_FF // W2_SPLIT
    w1_specs = [pl.BlockSpec((1, 1, D, cw), (lambda c: (lambda e, sg, nt: (layer, e, 0, c)))(c))
                for c in range(W1_SPLIT)]
    w2_specs = [pl.BlockSpec((1, 1, rh, D), (lambda c: (lambda e, sg, nt: (layer, e, c, 0)))(c))
                for c in range(W2_SPLIT)]
    grid_spec = pltpu.PrefetchScalarGridSpec(
        num_scalar_prefetch=2,
        grid=(N_EXPERTS,),
        in_specs=[pl.BlockSpec(memory_space=pl.ANY), pl.BlockSpec(memory_space=pl.ANY)]
        + w1_specs + [pl.BlockSpec((1, 1, 1, F2), lambda e, sg, nt: (layer, e, 0, 0))]
        + w2_specs + [pl.BlockSpec((1, 1, 1, D), lambda e, sg, nt: (layer, e, 0, 0))],
        out_specs=pl.BlockSpec(memory_space=pl.ANY),
        scratch_shapes=[pltpu.VMEM((D, F2), BF16), pltpu.VMEM((D_FF, D), BF16), pltpu.VMEM((tm, D_FF), BF16),
                        pltpu.VMEM((2, tm * U, LANES), F32), pltpu.VMEM((2, tm * U, LANES), F32),
                        pltpu.SMEM((4, tm), jnp.int32),
                        pltpu.SemaphoreType.DMA((4,)), pltpu.SemaphoreType.DMA((2,)), pltpu.SemaphoreType.DMA((2,))],
    )
    n_out_rows = TOP_K * n_tok + 2 * tm
    return pl.pallas_call(
        functools.partial(_moe_kernel, n_tok=n_tok),
        grid_spec=grid_spec,
        out_shape=jax.ShapeDtypeStruct((n_out_rows * U, LANES), F32),
        compiler_params=_cparams(("arbitrary",)),
        name="moe_experts",
    )(seg_start, tiles_per_e, code, x_units, *([w1] * W1_SPLIT), b1.reshape(DEPTH, N_EXPERTS, 1, F2),
      *([w2] * W2_SPLIT), b2.reshape(DEPTH, N_EXPERTS, 1, D))


def _pos_kernel(idx_ref, rank_ref, seg_ref, pos_ref):
    lane = lax.broadcasted_iota(jnp.int32, idx_ref.shape, 1)
    idx, rank, seg = idx_ref[...], rank_ref[...], seg_ref[...]
    out = jnp.zeros(idx.shape, jnp.int32)
    for k in range(TOP_K):
        hit = lane == idx[:, k:k + 1]
        seg_k = jnp.sum(jnp.where(hit, seg, 0.0), axis=-1, keepdims=True).astype(jnp.int32)
        out = jnp.where(lane == k, seg_k + rank[:, k:k + 1], out)
    pos_ref[...] = out


def _pair_slots(idx, rank, seg_start):
    n_tok = idx.shape[0]
    tm = TOKEN_TILE
    row = lambda i: (i, 0)
    return pl.pallas_call(
        _pos_kernel,
        grid=(n_tok // tm,),
        in_specs=[pl.BlockSpec((tm, LANES), row), pl.BlockSpec((tm, LANES), row),
                  pl.BlockSpec((1, LANES), lambda i: (0, 0))],
        out_specs=pl.BlockSpec((tm, LANES), row),
        out_shape=jax.ShapeDtypeStruct((n_tok, LANES), jnp.int32),
        compiler_params=_cparams(("arbitrary",)),
        name="pair_slots",
    )(idx, rank, seg_start)


def _slot_code_kernel(pos_ref, unused_hbm, code_ref, sem, *, n_tok):
    i = pl.program_id(0)
    tm = pos_ref.shape[0] // TOP_K

    @pl.when(i == 0)
    def _():
        cp = pltpu.make_async_copy(unused_hbm, code_ref, sem)
        cp.start()
        cp.wait()

    base = i * tm
    for t in range(tm):
        for k in range(TOP_K):
            code_ref[pos_ref[t * TOP_K + k]] = base + (k * n_tok + t)


def _slot_codes(pos, unused):
    n_tok = pos.shape[0]
    tm = TOKEN_TILE
    return pl.pallas_call(
        functools.partial(_slot_code_kernel, n_tok=n_tok),
        grid=(n_tok // tm,),
        in_specs=[pl.BlockSpec((tm * TOP_K,), lambda i: (i,), memory_space=pltpu.SMEM),
                  pl.BlockSpec(memory_space=pl.ANY)],
        out_specs=pl.BlockSpec(memory_space=pltpu.SMEM),
        out_shape=jax.ShapeDtypeStruct(unused.shape, jnp.int32),
        scratch_shapes=[pltpu.SemaphoreType.DMA(())],
        compiler_params=_cparams(("arbitrary",)),
        name="slot_codes",
    )(pos.reshape(-1), unused)


def _moe_layer(layer, hn_units, idx, rank, counts, w1, b1, w2, b2):
    n_tok = hn_units.shape[0] // ROW_UNITS
    assert n_tok & (n_tok - 1) == 0
    tm = MOE_TILE
    n_pairs = n_tok * TOP_K
    n_slots = n_pairs + N_EXPERTS * tm
    tiles_per_e = (counts + tm - 1) // tm
    seg_start = (jnp.cumsum(tiles_per_e) - tiles_per_e) * tm
    seg_f32 = jnp.pad(seg_start.astype(F32), (0, LANES - N_EXPERTS)).reshape(1, LANES)
    pos = _pair_slots(idx, rank, seg_f32)[:, :TOP_K]
    unused = n_pairs + jnp.arange(n_slots, dtype=jnp.int32) % (2 * tm)
    code = _slot_codes(pos, unused)
    return _moe_experts(layer, n_tok, seg_start, tiles_per_e, code, hn_units, w1, b1, w2, b2)


def _final_kernel(*refs, nb_ctx):
    x_ref = refs[0]
    y_refs = refs[1:1 + TOP_K]
    wt_ref, mod_ref, gain_ref, oa_ref, ob_ref = refs[1 + TOP_K:]
    D = D_MODEL
    i = pl.program_id(0)
    x = x_ref[...] + mod_ref[0][:, 5 * D:6 * D] * _moe_combine(y_refs, wt_ref)
    out = _rms(x, gain_ref[...])

    @pl.when(i < nb_ctx)
    def _():
        oa_ref[...] = out

    @pl.when(i >= nb_ctx)
    def _():
        ob_ref[...] = out


def _final(n_ctx, n_tok, lat_seq, x_new, y4, wts, mods, gain):
    tm = TOKEN_TILE
    nb, nb_ctx, tpl = n_tok // tm, n_ctx // tm, lat_seq // tm
    D = D_MODEL
    specs = [pl.BlockSpec((tm, D), lambda i: (i, 0))]
    specs += [pl.BlockSpec((tm * ROW_UNITS, LANES), (lambda k: (lambda i: (k * nb + i, 0)))(k)) for k in range(TOP_K)]
    specs += [pl.BlockSpec((tm, LANES), lambda i: (i, 0)),
              pl.BlockSpec((1, 1, 6 * D), lambda i: ((DEPTH - 1) * 8 + _cond_row(i, nb_ctx, tpl), 0, 0)),
              _resident((1, D))]
    return pl.pallas_call(
        functools.partial(_final_kernel, nb_ctx=nb_ctx),
        grid=(nb,),
        in_specs=specs,
        out_specs=[pl.BlockSpec((tm, D), lambda i: (jnp.minimum(i, nb_ctx - 1), 0)),
                   pl.BlockSpec((tm, D), lambda i: (jnp.maximum(i - nb_ctx, 0), 0))],
        out_shape=[jax.ShapeDtypeStruct((n_ctx, D), F32), jax.ShapeDtypeStruct((n_tok - n_ctx, D), F32)],
        compiler_params=_cparams(("arbitrary",)),
        name="final_norm",
    )(x_new, *([y4] * TOP_K), wts, mods, gain.reshape(1, D))


def _rope_tables(n_tokens):
    rows = n_tokens // GRID_W
    r = jnp.repeat(jnp.arange(rows, dtype=F32), GRID_W)
    col = jnp.tile(jnp.arange(GRID_W, dtype=F32), rows)
    nf = RET_DK // 4
    inv = ROPE_BASE ** (-jnp.arange(nf, dtype=F32) / nf)
    ang = jnp.concatenate([r[:, None] * inv, col[:, None] * inv], axis=-1)
    cos, sin = jnp.cos(ang), jnp.sin(ang)
    return jnp.concatenate([cos, cos], axis=-1), jnp.concatenate([-sin, sin], axis=-1)


def kernel(x_prompt, x_sample, c, state_gla, state_ret, state_hgrn, c_ctx, norm_mix, norm_ffn, ada_w, ada_b,
           w_in_even, w_out_even, gla_gk_w, gla_gk_b, gla_gain, ret_decay_exp, w_in_odd, w_out_odd,
           hgrn_lb_logits, hgrn_gain, router_w, router_b, moe_w1, moe_b1, moe_w2, moe_b2, final_norm):
    D = D_MODEL
    B_ctx, T_ctx, _ = x_prompt.shape
    B_lat, T_lat, _ = x_sample.shape
    n_ctx, n_lat = B_ctx * T_ctx, B_lat * T_lat
    n_tok = n_ctx + n_lat
    assert n_ctx % TOKEN_TILE == 0 and T_lat % TOKEN_TILE == 0 and B_lat + 1 <= 8
    assert T_ctx % SCAN_CHUNK == 0 and T_lat % SCAN_CHUNK == 0 and n_ctx % T_lat == 0

    cond8 = jnp.concatenate([c_ctx[None, :], c, jnp.zeros((8 - 1 - B_lat, D), F32)], axis=0)
    mods = _ada_mods(cond8, ada_w, ada_b).reshape(DEPTH * 8, 1, 6 * D)

    rw_pad = jnp.pad(router_w, ((0, 0), (0, 0), (0, LANES - N_EXPERTS)))
    rb_pad = jnp.pad(router_b, ((0, 0), (0, LANES - N_EXPERTS)), constant_values=-1e30)

    w_even = w_in_even[0]
    lr0 = GLA_HEADS * (2 * GLA_DK + GLA_DV)
    lr1 = lr0 + 2 * GLA_LOW_RANK
    w_main = jnp.concatenate([w_even[:, :lr0], w_even[:, lr1:]], axis=1)
    w_lr = jnp.pad(w_even[:, lr0:lr1], ((0, 0), (0, LANES - 2 * GLA_LOW_RANK)))
    nqk = GLA_HEADS * GLA_DK
    gkw = jnp.zeros((LANES, 2 * nqk), F32)
    gkw = gkw.at[0:GLA_LOW_RANK, 0:nqk].set(gla_gk_w[0, 0])
    gkw = gkw.at[GLA_LOW_RANK:2 * GLA_LOW_RANK, nqk:].set(gla_gk_w[0, 1])
    gkb = gla_gk_b[0].reshape(1, 2 * nqk)
    proj, gdec, x_cur = _in_proj(0, n_ctx, n_tok, T_lat, mods, norm_mix[0], w_main,
                                 x_parts=(x_prompt.reshape(n_ctx, D), x_sample.reshape(n_lat, D)),
                                 even_extra=(w_lr, gkw, gkb))

    y_gla_c, fin_gla = _gla_scan(proj, gdec, gla_gain[0], None, 0, B_ctx, T_ctx, True)
    (y_gla_l,) = _gla_scan(proj, gdec, gla_gain[0], state_gla[:, 0], n_ctx, B_lat, T_lat, False)
    dexp = jnp.broadcast_to(ret_decay_exp[0].T[:, :, None], (RET_HEADS, 2, LANES))
    y_ret_c, fin_ret = _ret_scan(proj, dexp, None, None, 0, B_ctx, T_ctx, True)
    (y_ret_l,) = _ret_scan(proj, dexp, _rope_tables(T_lat), state_ret[:, 0], n_ctx, B_lat, T_lat, False)

    x_new, hn, idx, wts, rank, cnt = _out_proj(0, n_ctx, n_tok, T_lat, [(y_gla_c, y_gla_l), (y_ret_c, y_ret_l)],
                                               w_out_even[0], x_cur, mods, norm_ffn[0], rw_pad[0], rb_pad[0:1])
    y4 = _moe_layer(0, hn, idx, rank, cnt[0, :N_EXPERTS], moe_w1, moe_b1, moe_w2, moe_b2)

    (proj, x_cur) = _in_proj(1, n_ctx, n_tok, T_lat, mods, norm_mix[1], w_in_odd[0], x_prev=x_new, y4=(y4, wts))
    y_h_c, fin_h = _hgrn_scan(proj, hgrn_lb_logits, hgrn_gain[0], None, 0, B_ctx, T_ctx, True)
    (y_h_l,) = _hgrn_scan(proj, hgrn_lb_logits, hgrn_gain[0], state_hgrn[:, 0], n_ctx, B_lat, T_lat, False)
    x_new, hn, idx, wts, rank, cnt = _out_proj(1, n_ctx, n_tok, T_lat, [(y_h_c, y_h_l)], w_out_odd[0], x_cur, mods,
                                               norm_ffn[1], rw_pad[1], rb_pad[1:2])
    y4 = _moe_layer(1, hn, idx, rank, cnt[0, :N_EXPERTS], moe_w1, moe_b1, moe_w2, moe_b2)

    y_ctx, y_lat = _final(n_ctx, n_tok, T_lat, x_new, y4, wts, mods, final_norm)

    new_state_gla = fin_gla.reshape(B_ctx, 1, 2, GLA_HEADS, GLA_DK, GLA_DV)
    new_state_ret = fin_ret.reshape(B_ctx, 1, 2, RET_HEADS, RET_DK, RET_DV)
    new_state_hgrn = fin_h.reshape(B_ctx, 1, 2, HG_HEADS, HG_DK, HG_DV)
    return (y_ctx.reshape(B_ctx, T_ctx, D), y_lat.reshape(B_lat, T_lat, D), new_state_gla, new_state_ret,
            new_state_hgrn)
```

```python
import functools

import jax
import jax.numpy as jnp
from jax import lax
from jax.experimental import pallas as pl
from jax.experimental.pallas import tpu as pltpu

F32 = jnp.float32
BF16 = jnp.bfloat16

D_MODEL = 1024
DEPTH = 2
GRID_W = 64
GLA_HEADS, GLA_DK, GLA_DV, GLA_LOW_RANK = 4, 64, 128, 16
GLA_NORMALIZER = 16.0
RET_HEADS, RET_DK, RET_DV = 4, 128, 128
ROPE_BASE = 10000.0
HG_HEADS, HG_DK, HG_DV = 8, 128, 128
N_EXPERTS, TOP_K, D_FF = 32, 4, 1024
SWIGLU_ALPHA, SWIGLU_LIMIT = 1.702, 7.0
EPS = 1e-6

LANES = 128
SCAN_CHUNK = 128
SCAN_ROWS = 1024
TOKEN_TILE = 256
BIG_TOKEN_TILE = 512
MOE_TILE = 512
VMEM_LIMIT = 56 * 1024 * 1024

_GQ, _GK, _GV, _GG, _RQ, _RK, _RV, _RG, _EVEN_MAIN = 0, 256, 512, 1024, 1536, 2048, 2560, 3072, 3584
_HQ, _HFF, _HFB, _HI, _HG, _ODD_MAIN = 0, 1024, 2048, 3072, 4096, 5120


def _dot(a, b):
    return jnp.dot(a, b, preferred_element_type=F32)


def _dot_nt(a, b):
    return lax.dot_general(a, b, (((1,), (1,)), ((), ())), preferred_element_type=F32)


def _rms(x, gain=None):
    y = x * lax.rsqrt(jnp.mean(x * x, axis=-1, keepdims=True) + EPS)
    if gain is not None:
        y = y * gain
    return y


def _sigmoid(x):
    return 1.0 / (1.0 + jnp.exp(-x))


def _silu(x):
    return x * _sigmoid(x)


def _cparams(sem, vmem=VMEM_LIMIT):
    return pltpu.CompilerParams(dimension_semantics=sem, vmem_limit_bytes=vmem)


def _resident(shape):
    nd = len(shape)
    return pl.BlockSpec(shape, lambda *_: (0,) * nd, pipeline_mode=pl.Buffered(1))


def _ada_kernel(c_ref, w_ref, b_ref, o_ref):
    o_ref[0] = _dot(_silu(c_ref[...]), w_ref[0]) + b_ref[0]


def _ada_mods(cond8, ada_w, ada_b):
    tn = 1536
    return pl.pallas_call(
        _ada_kernel,
        grid=(DEPTH, 6 * D_MODEL // tn),
        in_specs=[
            pl.BlockSpec((8, D_MODEL), lambda l, j: (0, 0)),
            pl.BlockSpec((1, D_MODEL, tn), lambda l, j: (l, 0, j)),
            pl.BlockSpec((1, 1, tn), lambda l, j: (l, 0, j)),
        ],
        out_specs=pl.BlockSpec((1, 8, tn), lambda l, j: (l, 0, j)),
        out_shape=jax.ShapeDtypeStruct((DEPTH, 8, 6 * D_MODEL), F32),
        compiler_params=_cparams(("arbitrary", "arbitrary")),
        name="ada_mods",
    )(cond8, ada_w, ada_b.reshape(DEPTH, 1, 6 * D_MODEL))


ROW_UNITS = D_MODEL // LANES


def _rows_from_units(ref, n_rows):
    return jnp.concatenate([ref[pl.ds(c, n_rows, stride=ROW_UNITS), :] for c in range(ROW_UNITS)], axis=1)


def _rows_to_units(ref, val):
    n_rows = val.shape[0]
    for c in range(ROW_UNITS):
        ref[pl.ds(c, n_rows, stride=ROW_UNITS), :] = val[:, c * LANES:(c + 1) * LANES]


def _moe_combine(y_refs, wt_ref):
    w = wt_ref[...]
    terms = [w[:, k:k + 1] * _rows_from_units(y_refs[k], w.shape[0]) for k in range(TOP_K)]
    return (terms[0] + terms[1]) + (terms[2] + terms[3])


def _in_proj_kernel(*refs, first, even, nb_ctx):
    it = iter(refs)
    if first:
        xa_ref, xb_ref = next(it), next(it)
    else:
        xp_ref = next(it)
        y_refs = [next(it) for _ in range(TOP_K)]
        wt_ref, modp_ref = next(it), next(it)
    gain_ref, mod_ref, w_ref = next(it), next(it), next(it)
    if even:
        wlr_ref, gkw_ref, gkb_ref = next(it), next(it), next(it)
    proj_ref = next(it)
    if even:
        g_ref = next(it)
    xcur_ref = next(it)

    D = D_MODEL
    i = pl.program_id(0)
    if first:
        x = jnp.where(i < nb_ctx, xa_ref[...], xb_ref[...])
    else:
        x = xp_ref[...] + modp_ref[0][:, 5 * D:6 * D] * _moe_combine(y_refs, wt_ref)
    xcur_ref[...] = x
    m = mod_ref[0]
    hn = _rms(x, gain_ref[...]) * (1.0 + m[:, D:2 * D]) + m[:, 0:D]
    proj_ref[...] = _dot(hn, w_ref[...])
    if even:
        z = _dot(_dot(hn, wlr_ref[...]), gkw_ref[...]) + gkb_ref[...]
        g_ref[...] = (jnp.minimum(z, 0.0) - jnp.log(1.0 + jnp.exp(-jnp.abs(z)))) * (1.0 / GLA_NORMALIZER)


def _cond_row(i, nb_ctx, tiles_per_lat_seq):
    return jnp.where(i < nb_ctx, 0, 1 + (i - nb_ctx) // tiles_per_lat_seq)


def _in_proj(layer, n_ctx, n_tok, lat_seq, mods, gain, w_main, *, x_parts=None, x_prev=None, y4=None, even_extra=None):
    first = x_parts is not None
    even = even_extra is not None
    tm = BIG_TOKEN_TILE if first else TOKEN_TILE
    nb = n_tok // tm
    nb_ctx = n_ctx // tm
    tpl = lat_seq // tm
    np_cols = w_main.shape[1]
    D = D_MODEL

    def mod_map(l):
        return lambda i: (l * 8 + _cond_row(i, nb_ctx, tpl), 0, 0)

    row = lambda i: (i, 0)
    args, specs = [], []
    if first:
        xa, xb = x_parts
        args += [xa, xb]
        specs += [pl.BlockSpec((tm, D), lambda i: (jnp.minimum(i, nb_ctx - 1), 0)),
                  pl.BlockSpec((tm, D), lambda i: (jnp.maximum(i - nb_ctx, 0), 0))]
    else:
        y4, wts = y4
        args += [x_prev] + [y4] * TOP_K + [wts, mods]
        specs += [pl.BlockSpec((tm, D), row)]
        specs += [pl.BlockSpec((tm * ROW_UNITS, LANES), (lambda k: (lambda i: (k * nb + i, 0)))(k))
                  for k in range(TOP_K)]
        specs += [pl.BlockSpec((tm, LANES), row), pl.BlockSpec((1, 1, 6 * D), mod_map(layer - 1))]
    args += [gain.reshape(1, D), mods, w_main]
    specs += [_resident((1, D)), pl.BlockSpec((1, 1, 6 * D), mod_map(layer)), _resident((D, np_cols))]
    out_shapes = [jax.ShapeDtypeStruct((n_tok, np_cols), F32)]
    out_specs = [pl.BlockSpec((tm, np_cols), row)]
    if even:
        w_lr, gkw, gkb = even_extra
        args += [w_lr, gkw, gkb]
        specs += [_resident(w_lr.shape), _resident(gkw.shape), _resident(gkb.shape)]
        out_shapes.append(jax.ShapeDtypeStruct((n_tok, 2 * GLA_HEADS * GLA_DK), F32))
        out_specs.append(pl.BlockSpec((tm, 2 * GLA_HEADS * GLA_DK), row))
    out_shapes.append(jax.ShapeDtypeStruct((n_tok, D), F32))
    out_specs.append(pl.BlockSpec((tm, D), row))
    return pl.pallas_call(
        functools.partial(_in_proj_kernel, first=first, even=even, nb_ctx=nb_ctx),
        grid=(nb,),
        in_specs=specs,
        out_specs=out_specs,
        out_shape=out_shapes,
        compiler_params=_cparams(("arbitrary",)),
        name=f"in_proj_l{layer}",
    )(*args)


def _tri(c, lower):
    r = lax.broadcasted_iota(jnp.int32, (c, c), 0)
    s = lax.broadcasted_iota(jnp.int32, (c, c), 1)
    return (r >= s) if lower else (r <= s)


def _cumsum_mm(tri_bf16, g):
    g_hi = g.astype(BF16)
    g_lo = (g - g_hi.astype(F32)).astype(BF16)
    r = _dot(tri_bf16, jnp.concatenate([g_hi, g_lo], axis=1))
    w = g.shape[1]
    return r[:, :w] + r[:, w:]


def _gated_scan_kernel(*refs, variant, T, nseq, has_s0, want_final, hpb):
    C = SCAN_CHUNK
    ncs = T // C
    nc = nseq * ncs
    it = iter(refs)
    if variant == "gla":
        q_ref, k_ref, v_ref, og_ref, gf_ref, gb_ref, gain_ref = (next(it) for _ in range(7))
    else:
        q_ref, ff_ref, fb_ref, v_ref, og_ref, lbl_ref, gain_ref = (next(it) for _ in range(7))
    s0_ref = next(it) if has_s0 else None
    y_ref = next(it)
    sfin_ref = next(it) if want_final else None
    qi_scr, o_scr, u_scr, dec_scr, sin_scr = (next(it) for _ in range(5))

    lane = lax.broadcasted_iota(jnp.int32, (1, LANES), 1)
    if hpb == 2:
        masks = [lane < GLA_DK, lane >= GLA_DK]
    else:
        masks = [None]
    tri_l, tri_u = _tri(C, True), _tri(C, False)
    tri_l16, tri_u16 = tri_l.astype(F32).astype(BF16), tri_u.astype(F32).astype(BF16)
    mid_f, mid_b = C // 2 - 1, C // 2

    if variant == "hgrn":
        lgs = [lbl_ref[l] for l in range(DEPTH)]
        mx = functools.reduce(jnp.maximum, lgs)
        es = [jnp.exp(l - mx) for l in lgs]
        tot = functools.reduce(lambda a, b: a + b, es)
        ps = [e / tot for e in es]
        layer = DEPTH - 1
        lb = functools.reduce(lambda a, b: a + b, ps[:layer + 1]) - ps[0]
        lb_f, lb_b = lb[0:1], lb[1:2]

    for n in range(nc):
        rows = pl.ds(n * C, C)
        if variant == "gla":
            q = q_ref[rows, :] * (GLA_DK ** -0.5)
            kf = kb = k_ref[rows, :]
            gf, gb = gf_ref[rows, :], gb_ref[rows, :]
        else:
            q = _silu(q_ref[rows, :])
            f_f = lb_f + (1.0 - lb_f) * _sigmoid(ff_ref[rows, :])
            f_b = lb_b + (1.0 - lb_b) * _sigmoid(fb_ref[rows, :])
            kf, kb = 1.0 - f_f, 1.0 - f_b
            gf, gb = jnp.log(f_f), jnp.log(f_b)
        bf = _cumsum_mm(tri_l16, gf)
        bb = _cumsum_mm(tri_u16, gb)
        bf_mid, bf_end = bf[mid_f:mid_f + 1], bf[C - 1:C]
        bb_mid, bb_end = bb[mid_b:mid_b + 1], bb[0:1]
        qd_f, kd_f = q * jnp.exp(bf - bf_mid), kf * jnp.exp(bf_mid - bf)
        qd_b, kd_b = q * jnp.exp(bb - bb_mid), kb * jnp.exp(bb_mid - bb)
        kend = jnp.concatenate([kf * jnp.exp(bf_end - bf), kb * jnp.exp(bb_end - bb)], axis=1)
        qi_scr[rows, :] = jnp.concatenate([q * jnp.exp(bf), q * jnp.exp(bb)], axis=1)
        dec_scr[n] = jnp.broadcast_to(jnp.concatenate([jnp.exp(bf_end), jnp.exp(bb_end)], axis=1), (8, 2 * LANES))
        for h in range(hpb):
            v = v_ref[rows, h * LANES:(h + 1) * LANES]
            if masks[h] is None:
                qf_h, qb_h, kend_h = qd_f, qd_b, kend
            else:
                qf_h, qb_h = jnp.where(masks[h], qd_f, 0.0), jnp.where(masks[h], qd_b, 0.0)
                kend_h = jnp.where(jnp.concatenate([masks[h], masks[h]], axis=1), kend, 0.0)
            s = jnp.where(tri_l, _dot_nt(qf_h, kd_f), 0.0) + jnp.where(tri_u, _dot_nt(qb_h, kd_b), 0.0)
            o_scr[rows, h * LANES:(h + 1) * LANES] = _dot(s, v)
            u_scr[h, n] = _dot(v.T, kend_h)

    for sq in range(nseq):
        for h in range(hpb):
            if has_s0:
                s_f, s_b = s0_ref[sq, 0, 0].T, s0_ref[sq, 1, 0].T
                if masks[h] is not None:
                    s_f, s_b = jnp.where(masks[h], s_f, 0.0), jnp.where(masks[h], s_b, 0.0)
            else:
                s_f = s_b = jnp.zeros((LANES, LANES), F32)
            for n in range(sq * ncs, (sq + 1) * ncs):
                sin_scr[h, n, :, 0:LANES] = s_f
                s_f = s_f * dec_scr[n, 0:1, 0:LANES] + u_scr[h, n, :, 0:LANES]
            for n in reversed(range(sq * ncs, (sq + 1) * ncs)):
                sin_scr[h, n, :, LANES:2 * LANES] = s_b
                s_b = s_b * dec_scr[n, 0:1, LANES:2 * LANES] + u_scr[h, n, :, LANES:2 * LANES]
            if want_final:
                if h == 0:
                    fin_f, fin_b = s_f, s_b
                else:
                    fin_f, fin_b = fin_f + s_f, fin_b + s_b
        if want_final:
            sfin_ref[sq, 0, 0] = fin_f.T
            sfin_ref[sq, 1, 0] = fin_b.T

    gain = gain_ref[...]
    for n in range(nc):
        rows = pl.ds(n * C, C)
        qi = qi_scr[rows, :]
        for h in range(hpb):
            cols = slice(h * LANES, (h + 1) * LANES)
            o = o_scr[rows, cols] + _dot_nt(qi, sin_scr[h, n])
            y_ref[rows, cols] = (_rms(o, gain) * _silu(og_ref[rows, cols])).astype(y_ref.dtype)


def _scan_scratch(T, hpb):
    nc = T // SCAN_CHUNK
    return [
        pltpu.VMEM((T, 2 * LANES), F32),
        pltpu.VMEM((T, hpb * LANES), F32),
        pltpu.VMEM((hpb, nc, LANES, 2 * LANES), F32),
        pltpu.VMEM((nc, 8, 2 * LANES), F32),
        pltpu.VMEM((hpb, nc, LANES, 2 * LANES), F32),
    ]


def _seqs_per_step(B, T):
    nseq = max(1, min(B, SCAN_ROWS // T))
    assert B % nseq == 0
    return nseq


def _gla_scan(proj, gdec, gain, s0, row0, B, T, want_final):
    nseq = _seqs_per_step(B, T)
    R = nseq * T
    rb0 = row0 // R
    assert row0 % R == 0
    has_s0 = s0 is not None

    def col(base, width=LANES):
        return lambda b, j: (rb0 + b, base // width + j)

    args = [proj, proj, proj, proj, gdec, gdec, gain.reshape(1, GLA_DV)]
    specs = [pl.BlockSpec((R, LANES), col(_GQ)), pl.BlockSpec((R, LANES), col(_GK)),
             pl.BlockSpec((R, 2 * LANES), col(_GV, 2 * LANES)), pl.BlockSpec((R, 2 * LANES), col(_GG, 2 * LANES)),
             pl.BlockSpec((R, LANES), col(0)), pl.BlockSpec((R, LANES), col(GLA_HEADS * GLA_DK)),
             pl.BlockSpec((1, GLA_DV), lambda b, j: (0, 0))]
    st_spec = pl.BlockSpec((nseq, 2, 1, LANES, LANES), lambda b, j: (b, 0, j, 0, 0))
    if has_s0:
        args.append(s0.reshape(B, 2, GLA_HEADS // 2, 2 * GLA_DK, GLA_DV))
        specs.append(st_spec)
    out_shapes = [jax.ShapeDtypeStruct((B * T, GLA_HEADS * GLA_DV), BF16)]
    out_specs = [pl.BlockSpec((R, 2 * LANES), lambda b, j: (b, j))]
    if want_final:
        out_shapes.append(jax.ShapeDtypeStruct((B, 2, GLA_HEADS // 2, 2 * GLA_DK, GLA_DV), F32))
        out_specs.append(st_spec)
    kern = functools.partial(_gated_scan_kernel, variant="gla", T=T, nseq=nseq, has_s0=has_s0,
                             want_final=want_final, hpb=2)
    return pl.pallas_call(
        kern, grid=(B // nseq, GLA_HEADS // 2), in_specs=specs, out_specs=out_specs, out_shape=out_shapes,
        scratch_shapes=_scan_scratch(R, 2),
        compiler_params=_cparams(("arbitrary", "arbitrary")), name=f"gla_scan_T{T}",
    )(*args)


def _hgrn_scan(proj, lb_logits, gain, s0, row0, B, T, want_final):
    nseq = _seqs_per_step(B, T)
    R = nseq * T
    rb0 = row0 // R
    assert row0 % R == 0
    has_s0 = s0 is not None

    def col(base):
        return lambda b, j: (rb0 + b, base // LANES + j)

    args = [proj, proj, proj, proj, proj, lb_logits, gain.reshape(1, HG_DV)]
    specs = [pl.BlockSpec((R, LANES), col(_HQ)), pl.BlockSpec((R, LANES), col(_HFF)),
             pl.BlockSpec((R, LANES), col(_HFB)), pl.BlockSpec((R, LANES), col(_HI)),
             pl.BlockSpec((R, LANES), col(_HG)),
             pl.BlockSpec((DEPTH, 2, LANES), lambda b, j: (0, 0, j)),
             pl.BlockSpec((1, HG_DV), lambda b, j: (0, 0))]
    st_spec = pl.BlockSpec((nseq, 2, 1, LANES, LANES), lambda b, j: (b, 0, j, 0, 0))
    if has_s0:
        args.append(s0.reshape(B, 2, HG_HEADS, HG_DK, HG_DV))
        specs.append(st_spec)
    out_shapes = [jax.ShapeDtypeStruct((B * T, HG_HEADS * HG_DV), BF16)]
    out_specs = [pl.BlockSpec((R, LANES), lambda b, j: (b, j))]
    if want_final:
        out_shapes.append(jax.ShapeDtypeStruct((B, 2, HG_HEADS, HG_DK, HG_DV), F32))
        out_specs.append(st_spec)
    kern = functools.partial(_gated_scan_kernel, variant="hgrn", T=T, nseq=nseq, has_s0=has_s0,
                             want_final=want_final, hpb=1)
    return pl.pallas_call(
        kern, grid=(B // nseq, HG_HEADS), in_specs=specs, out_specs=out_specs, out_shape=out_shapes,
        scratch_shapes=_scan_scratch(R, 1),
        compiler_params=_cparams(("arbitrary", "arbitrary")), name=f"hgrn_scan_T{T}",
    )(*args)


def _ret_scan_kernel(*refs, T, nseq, has_s0, want_final, rope):
    C = SCAN_CHUNK
    ncs = T // C
    nc = nseq * ncs
    it = iter(refs)
    q_ref, k_ref, v_ref, og_ref, dexp_ref = (next(it) for _ in range(5))
    if rope:
        cos_ref, sin_ref = next(it), next(it)
    s0_ref = next(it) if has_s0 else None
    y_ref = next(it)
    sfin_ref = next(it) if want_final else None
    qi_scr, o_scr, u_scr, sin_scr = (next(it) for _ in range(4))

    lg = jnp.log1p(-jnp.exp2(-dexp_ref[0]))
    lg_f, lg_b = lg[0:1], lg[1:2]
    r = lax.broadcasted_iota(jnp.int32, (C, C), 0)
    s = lax.broadcasted_iota(jnp.int32, (C, C), 1)
    dist = (r - s).astype(F32)
    dmask = (jnp.where(r >= s, jnp.exp(jnp.maximum(dist, 0.0) * lg_f[:, 0:1]), 0.0)
             + jnp.where(r <= s, jnp.exp(jnp.maximum(-dist, 0.0) * lg_b[:, 0:1]), 0.0))
    pos = lax.broadcasted_iota(jnp.int32, (C, LANES), 0).astype(F32)
    xi = jnp.concatenate([jnp.exp((pos + 1.0) * lg_f), jnp.exp((C - pos) * lg_b)], axis=1)
    zeta = jnp.concatenate([jnp.exp((C - 1.0 - pos) * lg_f), jnp.exp(pos * lg_b)], axis=1)
    d_f, d_b = jnp.exp(C * lg_f), jnp.exp(C * lg_b)

    def rot(x, seq_rows):
        if not rope:
            return x
        return x * cos_ref[seq_rows, :] + pltpu.roll(x, RET_DK // 2, axis=1) * sin_ref[seq_rows, :]

    for n in range(nc):
        rows = pl.ds(n * C, C)
        seq_rows = pl.ds((n % ncs) * C, C)
        q = rot(q_ref[rows, :], seq_rows)
        k = rot(k_ref[rows, :] * (RET_DK ** -0.5), seq_rows)
        v = v_ref[rows, :]
        o_scr[rows, :] = _dot(_dot_nt(q, k) * dmask, v)
        qi_scr[rows, :] = jnp.concatenate([q, q], axis=1) * xi
        u_scr[n] = _dot(v.T, jnp.concatenate([k, k], axis=1) * zeta)

    for sq in range(nseq):
        if has_s0:
            s_f, s_b = s0_ref[sq, 0, 0].T, s0_ref[sq, 1, 0].T
        else:
            s_f = s_b = jnp.zeros((LANES, LANES), F32)
        for n in range(sq * ncs, (sq + 1) * ncs):
            sin_scr[n, :, 0:LANES] = s_f
            s_f = s_f * d_f + u_scr[n, :, 0:LANES]
        for n in reversed(range(sq * ncs, (sq + 1) * ncs)):
            sin_scr[n, :, LANES:2 * LANES] = s_b
            s_b = s_b * d_b + u_scr[n, :, LANES:2 * LANES]
        if want_final:
            sfin_ref[sq, 0, 0] = s_f.T
            sfin_ref[sq, 1, 0] = s_b.T

    for n in range(nc):
        rows = pl.ds(n * C, C)
        o = o_scr[rows, :] + _dot_nt(qi_scr[rows, :], sin_scr[n])
        y_ref[rows, :] = (_rms(o) * _silu(og_ref[rows, :])).astype(y_ref.dtype)


def _ret_scan(proj, dexp, rope_tabs, s0, row0, B, T, want_final):
    nseq = _seqs_per_step(B, T)
    R = nseq * T
    rb0 = row0 // R
    assert row0 % R == 0
    nc = R // SCAN_CHUNK
    has_s0 = s0 is not None
    rope = rope_tabs is not None

    def col(base):
        return lambda b, j: (rb0 + b, base // LANES + j)

    args = [proj, proj, proj, proj, dexp]
    specs = [pl.BlockSpec((R, LANES), col(_RQ)), pl.BlockSpec((R, LANES), col(_RK)),
             pl.BlockSpec((R, LANES), col(_RV)), pl.BlockSpec((R, LANES), col(_RG)),
             pl.BlockSpec((1, 2, LANES), lambda b, j: (j, 0, 0))]
    if rope:
        args += list(rope_tabs)
        specs += [pl.BlockSpec((T, LANES), lambda b, j: (0, 0))] * 2
    st_spec = pl.BlockSpec((nseq, 2, 1, LANES, LANES), lambda b, j: (b, 0, j, 0, 0))
    if has_s0:
        args.append(s0.reshape(B, 2, RET_HEADS, RET_DK, RET_DV))
        specs.append(st_spec)
    out_shapes = [jax.ShapeDtypeStruct((B * T, RET_HEADS * RET_DV), BF16)]
    out_specs = [pl.BlockSpec((R, LANES), lambda b, j: (b, j))]
    if want_final:
        out_shapes.append(jax.ShapeDtypeStruct((B, 2, RET_HEADS, RET_DK, RET_DV), F32))
        out_specs.append(st_spec)
    kern = functools.partial(_ret_scan_kernel, T=T, nseq=nseq, has_s0=has_s0, want_final=want_final, rope=rope)
    scratch = [pltpu.VMEM((R, 2 * LANES), F32), pltpu.VMEM((R, LANES), F32),
               pltpu.VMEM((nc, LANES, 2 * LANES), F32), pltpu.VMEM((nc, LANES, 2 * LANES), F32)]
    return pl.pallas_call(
        kern, grid=(B // nseq, RET_HEADS), in_specs=specs, out_specs=out_specs, out_shape=out_shapes,
        scratch_shapes=scratch,
        compiler_params=_cparams(("arbitrary", "arbitrary")), name=f"ret_scan_T{T}",
    )(*args)


def _out_proj_kernel(*refs, n_mix, nb_ctx):
    it = iter(refs)
    y_refs = [(next(it), next(it)) for _ in range(n_mix)]
    wo_ref, x_ref, mod_ref, gain_ref, rw_ref, rb_ref = (next(it) for _ in range(6))
    xnew_ref, hn_ref, idx_ref, wt_ref, rank_ref, cnt_ref = (next(it) for _ in range(6))
    cnt_scr = next(it)
    D = D_MODEL
    is_ctx = pl.program_id(0) < nb_ctx
    mix = None
    r0 = 0
    for ya_ref, yb_ref in y_refs:
        w = ya_ref.shape[1]
        part = _dot(jnp.where(is_ctx, ya_ref[...], yb_ref[...]), wo_ref[r0:r0 + w, :])
        mix = part if mix is None else mix + part
        r0 += w
    m = mod_ref[0]
    xn = x_ref[...] + m[:, 2 * D:3 * D] * mix
    xnew_ref[...] = xn
    hn = _rms(xn, gain_ref[...]) * (1.0 + m[:, 4 * D:5 * D]) + m[:, 3 * D:4 * D]
    hn_hi = hn.astype(BF16)
    _rows_to_units(hn_ref, hn)
    hn_lo = (hn - hn_hi.astype(F32)).astype(BF16)
    rw = rw_ref[...]
    rw_hi = rw.astype(BF16)
    rw_lo = (rw - rw_hi.astype(F32)).astype(BF16)
    logits = (_dot(hn_hi, rw_hi) + (_dot(hn_lo, rw_hi) + _dot(hn_hi, rw_lo))) + rb_ref[...]
    lane = lax.broadcasted_iota(jnp.int32, logits.shape, 1).astype(F32)
    vals, idxs = [], []
    cur = logits
    for _ in range(TOP_K):
        mx = jnp.max(cur, axis=-1, keepdims=True)
        ik = jnp.min(jnp.where(cur == mx, lane, float(LANES)), axis=-1, keepdims=True)
        vals.append(mx)
        idxs.append(ik)
        cur = jnp.where(lane == ik, -jnp.inf, cur)
    es = [jnp.exp(v - vals[0]) for v in vals]
    tot = (es[0] + es[1]) + (es[2] + es[3])
    idx_out = jnp.zeros(logits.shape, F32)
    wt_out = jnp.zeros(logits.shape, F32)
    for k in range(TOP_K):
        idx_out = jnp.where(lane == float(k), idxs[k], idx_out)
        wt_out = jnp.where(lane == float(k), es[k] / tot, wt_out)
    idx_ref[...] = idx_out.astype(jnp.int32)
    wt_ref[...] = wt_out

    @pl.when(pl.program_id(0) == 0)
    def _():
        cnt_scr[...] = jnp.zeros(cnt_scr.shape, F32)

    tm = logits.shape[0]
    hits = [lane == idxs[k] for k in range(TOP_K)]
    sel = jnp.zeros(logits.shape, F32)
    for k in range(TOP_K):
        sel = sel + jnp.where(hits[k], 1.0, 0.0)
    rr = lax.broadcasted_iota(jnp.int32, (tm, tm), 0)
    cc = lax.broadcasted_iota(jnp.int32, (tm, tm), 1)
    before = jnp.where(rr > cc, 1.0, 0.0).astype(BF16)
    rank_all = cnt_scr[0:1, :] + _dot(before, sel.astype(BF16))
    rank_out = jnp.zeros(logits.shape, F32)
    for k in range(TOP_K):
        rk = jnp.sum(jnp.where(hits[k], rank_all, 0.0), axis=-1, keepdims=True)
        rank_out = jnp.where(lane == float(k), rk, rank_out)
    rank_ref[...] = rank_out.astype(jnp.int32)
    total = cnt_scr[...] + jnp.sum(sel, axis=0, keepdims=True)
    cnt_scr[...] = total
    cnt_ref[...] = total.astype(jnp.int32)


def _out_proj(layer, n_ctx, n_tok, lat_seq, ys, w_out, x_cur, mods, gain, rw_pad, rb_pad):
    tm = BIG_TOKEN_TILE
    nb, nb_ctx, tpl = n_tok // tm, n_ctx // tm, lat_seq // tm
    D = D_MODEL
    row = lambda i: (i, 0)
    specs = []
    for ya, _ in ys:
        specs += [pl.BlockSpec((tm, ya.shape[1]), lambda i: (jnp.minimum(i, nb_ctx - 1), 0)),
                  pl.BlockSpec((tm, ya.shape[1]), lambda i: (jnp.maximum(i - nb_ctx, 0), 0))]
    specs += [_resident((D, D)), pl.BlockSpec((tm, D), row),
              pl.BlockSpec((1, 1, 6 * D), lambda i: (layer * 8 + _cond_row(i, nb_ctx, tpl), 0, 0)),
              _resident((1, D)), _resident((D, LANES)), _resident((1, LANES))]
    return pl.pallas_call(
        functools.partial(_out_proj_kernel, n_mix=len(ys), nb_ctx=nb_ctx),
        grid=(nb,),
        in_specs=specs,
        out_specs=[pl.BlockSpec((tm, D), row), pl.BlockSpec((tm * ROW_UNITS, LANES), row),
                   pl.BlockSpec((tm, LANES), row), pl.BlockSpec((tm, LANES), row),
                   pl.BlockSpec((tm, LANES), row), pl.BlockSpec((8, LANES), lambda i: (0, 0))],
        out_shape=[jax.ShapeDtypeStruct((n_tok, D), F32), jax.ShapeDtypeStruct((n_tok * ROW_UNITS, LANES), F32),
                   jax.ShapeDtypeStruct((n_tok, LANES), jnp.int32), jax.ShapeDtypeStruct((n_tok, LANES), F32),
                   jax.ShapeDtypeStruct((n_tok, LANES), jnp.int32), jax.ShapeDtypeStruct((8, LANES), jnp.int32)],
        scratch_shapes=[pltpu.VMEM((8, LANES), F32)],
        compiler_params=_cparams(("arbitrary",)),
        name=f"out_proj_l{layer}",
    )(*[y for pair in ys for y in pair], w_out, x_cur, mods, gain.reshape(1, D), rw_pad, rb_pad)


W1_SPLIT = 4
W2_SPLIT = 2


def _moe_kernel(seg_ref, nt_ref, code_hbm, x_hbm, *refs, n_tok):
    w1_refs = refs[:W1_SPLIT]
    b1_ref = refs[W1_SPLIT]
    w2_refs = refs[W1_SPLIT + 1:W1_SPLIT + 1 + W2_SPLIT]
    b2_ref, y_hbm, w1_scr, w2_scr, act_scr, xbuf, obuf, code_smem, csem, gsem, ssem = refs[W1_SPLIT + 1 + W2_SPLIT:]
    tm = MOE_TILE
    U = ROW_UNITS
    n_code_tiles = code_hbm.shape[0] // tm
    e = pl.program_id(0)
    nt = nt_ref[e]
    g_first = seg_ref[e] // tm

    GATHER, SCATTER = 0, 1

    def code_copy(kind, t, p):
        first = pl.multiple_of(jnp.clip(t, 0, n_code_tiles - 1) * tm, tm)
        return pltpu.make_async_copy(code_hbm.at[pl.ds(first, tm)], code_smem.at[kind, p], csem.at[kind, p])

    def gather_start(b, p):
        for r in range(tm):
            tok = jnp.bitwise_and(code_smem[GATHER, p, r], n_tok - 1)
            pltpu.make_async_copy(x_hbm.at[pl.ds(pl.multiple_of(tok * U, U), U)],
                                  xbuf.at[b, pl.ds(r * U, U)], gsem.at[b]).start()

    def gather_wait(b):
        pltpu.make_async_copy(x_hbm.at[pl.ds(0, tm * U)], xbuf.at[b], gsem.at[b]).wait()

    def scatter_start(b, p):
        for r in range(tm):
            dst = code_smem[SCATTER, p, r]
            pltpu.make_async_copy(obuf.at[b, pl.ds(r * U, U)],
                                  y_hbm.at[pl.ds(pl.multiple_of(dst * U, U), U)], ssem.at[b]).start()

    def scatter_wait(b):
        pltpu.make_async_copy(obuf.at[b], y_hbm.at[pl.ds(0, tm * U)], ssem.at[b]).wait()

    @pl.when(e == 0)
    def _():
        obuf[...] = jnp.zeros(obuf.shape, obuf.dtype)
        for h in range(2):
            cp = pltpu.make_async_copy(obuf.at[0], y_hbm.at[pl.ds((TOP_K * n_tok + h * tm) * U, tm * U)], ssem.at[0])
            cp.start()
            cp.wait()
        code_copy(GATHER, 0, 0).start()
        code_copy(GATHER, 1, 1).start()
        code_copy(SCATTER, n_code_tiles - 1, 1).start()
        code_copy(SCATTER, 0, 0).start()
        code_copy(GATHER, 0, 0).wait()
        gather_start(0, 0)

    @pl.when(nt > 0)
    def _():
        cw = 2 * D_FF // W1_SPLIT
        for c, w_ref in enumerate(w1_refs):
            w1_scr[:, c * cw:(c + 1) * cw] = w_ref[0, 0].astype(BF16)
        rh = D_FF // W2_SPLIT
        for c, w_ref in enumerate(w2_refs):
            w2_scr[c * rh:(c + 1) * rh, :] = w_ref[0, 0].astype(BF16)

        blk = 4 * LANES
        rr = lax.broadcasted_iota(jnp.int32, (blk, blk // 2), 0)
        cc = lax.broadcasted_iota(jnp.int32, (blk, blk // 2), 1)
        sel = jnp.where(rr == 2 * cc, 1.0, 0.0).astype(BF16)

        def tile(g, b):
            o = 1 - b

            @pl.when(g >= 0)
            def _():
                code_copy(GATHER, g + 1, o).wait()
                code_copy(SCATTER, g - 1, o).wait()
                gather_start(o, o)
                scatter_start(o, o)

            gather_wait(b)
            x = _rows_from_units(xbuf.at[b], tm).astype(BF16)
            for c in range(2 * D_FF // blk):
                cols = slice(c * blk, (c + 1) * blk)
                hid = _dot(x, w1_scr[:, cols]) + b1_ref[0, 0][:, cols]
                nxt = pltpu.roll(hid, blk - 1, axis=1)
                glu = jnp.minimum(hid, SWIGLU_LIMIT)
                lin = jnp.clip(nxt, -SWIGLU_LIMIT, SWIGLU_LIMIT)
                act = glu * _sigmoid(SWIGLU_ALPHA * glu) * (lin + 1.0)
                act_scr[:, c * (blk // 2):(c + 1) * (blk // 2)] = _dot(act.astype(BF16), sel).astype(BF16)
            y = _dot(act_scr[...], w2_scr[...]) + b2_ref[0, 0]

            @pl.when(g >= 1)
            def _():
                scatter_wait(b)

            _rows_to_units(obuf.at[b], y)
            code_copy(GATHER, g + 2, b).start()
            code_copy(SCATTER, g + 1, o).start()

        g_end = g_first + nt

        def tile_pair(m, carry):
            for b in range(2):
                g = 2 * m + b

                @pl.when(jnp.logical_and(g >= g_first, g < g_end))
                def _():
                    tile(g, b)

            return carry

        lax.fori_loop(g_first // 2, (g_end + 1) // 2, tile_pair, 0)

    @pl.when(e == N_EXPERTS - 1)
    def _():
        g_end = g_first + nt
        last = lax.rem(g_end + 1, 2)
        code_copy(SCATTER, g_end - 1, last).wait()
        scatter_start(last, last)
        scatter_wait(last)
        scatter_wait(1 - last)
        gather_wait(1 - last)
        code_copy(GATHER, g_end + 1, last).wait()
        code_copy(SCATTER, g_end, 1 - last).wait()


def _moe_experts(layer, n_tok, seg_start, tiles_per_e, code, x_units, w1, b1, w2, b2):
    tm = MOE_TILE
    D, F2, U = D_MODEL, 2 * D_FF, ROW_UNITS
    cw, rh = F2 // W1_SPLIT, D_FF // W2_SPLIT
    w1_specs = [pl.BlockSpec((1, 1, D, cw), (lambda c: (lambda e, sg, nt: (layer, e, 0, c)))(c))
                for c in range(W1_SPLIT)]
    w2_specs = [pl.BlockSpec((1, 1, rh, D), (lambda c: (lambda e, sg, nt: (layer, e, c, 0)))(c))
                for c in range(W2_SPLIT)]
    grid_spec = pltpu.PrefetchScalarGridSpec(
        num_scalar_prefetch=2,
        grid=(N_EXPERTS,),
        in_specs=[pl.BlockSpec(memory_space=pl.ANY), pl.BlockSpec(memory_space=pl.ANY)]
        + w1_specs + [pl.BlockSpec((1, 1, 1, F2), lambda e, sg, nt: (layer, e, 0, 0))]
        + w2_specs + [pl.BlockSpec((1, 1, 1, D), lambda e, sg, nt: (layer, e, 0, 0))],
        out_specs=pl.BlockSpec(memory_space=pl.ANY),
        scratch_shapes=[pltpu.VMEM((D, F2), BF16), pltpu.VMEM((D_FF, D), BF16), pltpu.VMEM((tm, D_FF), BF16),
                        pltpu.VMEM((2, tm * U, LANES), F32), pltpu.VMEM((2, tm * U, LANES), F32),
                        pltpu.SMEM((2, 2, tm), jnp.int32),
                        pltpu.SemaphoreType.DMA((2, 2)), pltpu.SemaphoreType.DMA((2,)), pltpu.SemaphoreType.DMA((2,))],
    )
    n_out_rows = TOP_K * n_tok + 2 * tm
    return pl.pallas_call(
        functools.partial(_moe_kernel, n_tok=n_tok),
        grid_spec=grid_spec,
        out_shape=jax.ShapeDtypeStruct((n_out_rows * U, LANES), F32),
        compiler_params=_cparams(("arbitrary",)),
        name="moe_experts",
    )(seg_start, tiles_per_e, code, x_units, *([w1] * W1_SPLIT), b1.reshape(DEPTH, N_EXPERTS, 1, F2),
      *([w2] * W2_SPLIT), b2.reshape(DEPTH, N_EXPERTS, 1, D))


def _pos_kernel(idx_ref, rank_ref, seg_ref, pos_ref):
    lane = lax.broadcasted_iota(jnp.int32, idx_ref.shape, 1)
    idx, rank, seg = idx_ref[...], rank_ref[...], seg_ref[...]
    out = jnp.zeros(idx.shape, jnp.int32)
    for k in range(TOP_K):
        hit = lane == idx[:, k:k + 1]
        seg_k = jnp.sum(jnp.where(hit, seg, 0.0), axis=-1, keepdims=True).astype(jnp.int32)
        out = jnp.where(lane == k, seg_k + rank[:, k:k + 1], out)
    pos_ref[...] = out


def _pair_slots(idx, rank, seg_start):
    n_tok = idx.shape[0]
    tm = TOKEN_TILE
    row = lambda i: (i, 0)
    return pl.pallas_call(
        _pos_kernel,
        grid=(n_tok // tm,),
        in_specs=[pl.BlockSpec((tm, LANES), row), pl.BlockSpec((tm, LANES), row),
                  pl.BlockSpec((1, LANES), lambda i: (0, 0))],
        out_specs=pl.BlockSpec((tm, LANES), row),
        out_shape=jax.ShapeDtypeStruct((n_tok, LANES), jnp.int32),
        compiler_params=_cparams(("arbitrary",)),
        name="pair_slots",
    )(idx, rank, seg_start)


def _slot_code_kernel(pos_ref, unused_hbm, code_ref, sem, *, n_tok):
    i = pl.program_id(0)
    tm = pos_ref.shape[0] // TOP_K

    @pl.when(i == 0)
    def _():
        cp = pltpu.make_async_copy(unused_hbm, code_ref, sem)
        cp.start()
        cp.wait()

    base = i * tm
    for t in range(tm):
        for k in range(TOP_K):
            code_ref[pos_ref[t * TOP_K + k]] = base + (k * n_tok + t)


def _slot_codes(pos, unused):
    n_tok = pos.shape[0]
    tm = TOKEN_TILE
    return pl.pallas_call(
        functools.partial(_slot_code_kernel, n_tok=n_tok),
        grid=(n_tok // tm,),
        in_specs=[pl.BlockSpec((tm * TOP_K,), lambda i: (i,), memory_space=pltpu.SMEM),
                  pl.BlockSpec(memory_space=pl.ANY)],
        out_specs=pl.BlockSpec(memory_space=pltpu.SMEM),
        out_shape=jax.ShapeDtypeStruct(unused.shape, jnp.int32),
        scratch_shapes=[pltpu.SemaphoreType.DMA(())],
        compiler_params=_cparams(("arbitrary",)),
        name="slot_codes",
    )(pos.reshape(-1), unused)


def _moe_layer(layer, hn_units, idx, rank, counts, w1, b1, w2, b2):
    n_tok = hn_units.shape[0] // ROW_UNITS
    assert n_tok & (n_tok - 1) == 0
    tm = MOE_TILE
    n_pairs = n_tok * TOP_K
    n_slots = n_pairs + N_EXPERTS * tm
    tiles_per_e = (counts + tm - 1) // tm
    seg_start = (jnp.cumsum(tiles_per_e) - tiles_per_e) * tm
    seg_f32 = jnp.pad(seg_start.astype(F32), (0, LANES - N_EXPERTS)).reshape(1, LANES)
    pos = _pair_slots(idx, rank, seg_f32)[:, :TOP_K]
    unused = n_pairs + jnp.arange(n_slots, dtype=jnp.int32) % (2 * tm)
    code = _slot_codes(pos, unused)
    return _moe_experts(layer, n_tok, seg_start, tiles_per_e, code, hn_units, w1, b1, w2, b2)


def _final_kernel(*refs, nb_ctx):
    x_ref = refs[0]
    y_refs = refs[1:1 + TOP_K]
    wt_ref, mod_ref, gain_ref, oa_ref, ob_ref = refs[1 + TOP_K:]
    D = D_MODEL
    i = pl.program_id(0)
    x = x_ref[...] + mod_ref[0][:, 5 * D:6 * D] * _moe_combine(y_refs, wt_ref)
    out = _rms(x, gain_ref[...])

    @pl.when(i < nb_ctx)
    def _():
        oa_ref[...] = out

    @pl.when(i >= nb_ctx)
    def _():
        ob_ref[...] = out


def _final(n_ctx, n_tok, lat_seq, x_new, y4, wts, mods, gain):
    tm = TOKEN_TILE
    nb, nb_ctx, tpl = n_tok // tm, n_ctx // tm, lat_seq // tm
    D = D_MODEL
    specs = [pl.BlockSpec((tm, D), lambda i: (i, 0))]
    specs += [pl.BlockSpec((tm * ROW_UNITS, LANES), (lambda k: (lambda i: (k * nb + i, 0)))(k)) for k in range(TOP_K)]
    specs += [pl.BlockSpec((tm, LANES), lambda i: (i, 0)),
              pl.BlockSpec((1, 1, 6 * D), lambda i: ((DEPTH - 1) * 8 + _cond_row(i, nb_ctx, tpl), 0, 0)),
              _resident((1, D))]
    return pl.pallas_call(
        functools.partial(_final_kernel, nb_ctx=nb_ctx),
        grid=(nb,),
        in_specs=specs,
        out_specs=[pl.BlockSpec((tm, D), lambda i: (jnp.minimum(i, nb_ctx - 1), 0)),
                   pl.BlockSpec((tm, D), lambda i: (jnp.maximum(i - nb_ctx, 0), 0))],
        out_shape=[jax.ShapeDtypeStruct((n_ctx, D), F32), jax.ShapeDtypeStruct((n_tok - n_ctx, D), F32)],
        compiler_params=_cparams(("arbitrary",)),
        name="final_norm",
    )(x_new, *([y4] * TOP_K), wts, mods, gain.reshape(1, D))


def _rope_tables(n_tokens):
    rows = n_tokens // GRID_W
    r = jnp.repeat(jnp.arange(rows, dtype=F32), GRID_W)
    col = jnp.tile(jnp.arange(GRID_W, dtype=F32), rows)
    nf = RET_DK // 4
    inv = ROPE_BASE ** (-jnp.arange(nf, dtype=F32) / nf)
    ang = jnp.concatenate([r[:, None] * inv, col[:, None] * inv], axis=-1)
    cos, sin = jnp.cos(ang), jnp.sin(ang)
    return jnp.concatenate([cos, cos], axis=-1), jnp.concatenate([-sin, sin], axis=-1)


def kernel(x_prompt, x_sample, c, state_gla, state_ret, state_hgrn, c_ctx, norm_mix, norm_ffn, ada_w, ada_b,
           w_in_even, w_out_even, gla_gk_w, gla_gk_b, gla_gain, ret_decay_exp, w_in_odd, w_out_odd,
           hgrn_lb_logits, hgrn_gain, router_w, router_b, moe_w1, moe_b1, moe_w2, moe_b2, final_norm):
    D = D_MODEL
    B_ctx, T_ctx, _ = x_prompt.shape
    B_lat, T_lat, _ = x_sample.shape
    n_ctx, n_lat = B_ctx * T_ctx, B_lat * T_lat
    n_tok = n_ctx + n_lat
    assert n_ctx % BIG_TOKEN_TILE == 0 and T_lat % BIG_TOKEN_TILE == 0 and B_lat + 1 <= 8
    assert T_ctx % SCAN_CHUNK == 0 and T_lat % SCAN_CHUNK == 0 and n_ctx % T_lat == 0

    cond8 = jnp.concatenate([c_ctx[None, :], c, jnp.zeros((8 - 1 - B_lat, D), F32)], axis=0)
    mods = _ada_mods(cond8, ada_w, ada_b).reshape(DEPTH * 8, 1, 6 * D)

    rw_pad = jnp.pad(router_w, ((0, 0), (0, 0), (0, LANES - N_EXPERTS)))
    rb_pad = jnp.pad(router_b, ((0, 0), (0, LANES - N_EXPERTS)), constant_values=-1e30)

    w_even = w_in_even[0]
    lr0 = GLA_HEADS * (2 * GLA_DK + GLA_DV)
    lr1 = lr0 + 2 * GLA_LOW_RANK
    w_main = jnp.concatenate([w_even[:, :lr0], w_even[:, lr1:]], axis=1)
    w_lr = jnp.pad(w_even[:, lr0:lr1], ((0, 0), (0, LANES - 2 * GLA_LOW_RANK)))
    nqk = GLA_HEADS * GLA_DK
    gkw = jnp.zeros((LANES, 2 * nqk), F32)
    gkw = gkw.at[0:GLA_LOW_RANK, 0:nqk].set(gla_gk_w[0, 0])
    gkw = gkw.at[GLA_LOW_RANK:2 * GLA_LOW_RANK, nqk:].set(gla_gk_w[0, 1])
    gkb = gla_gk_b[0].reshape(1, 2 * nqk)
    proj, gdec, x_cur = _in_proj(0, n_ctx, n_tok, T_lat, mods, norm_mix[0], w_main,
                                 x_parts=(x_prompt.reshape(n_ctx, D), x_sample.reshape(n_lat, D)),
                                 even_extra=(w_lr, gkw, gkb))

    y_gla_c, fin_gla = _gla_scan(proj, gdec, gla_gain[0], None, 0, B_ctx, T_ctx, True)
    (y_gla_l,) = _gla_scan(proj, gdec, gla_gain[0], state_gla[:, 0], n_ctx, B_lat, T_lat, False)
    dexp = jnp.broadcast_to(ret_decay_exp[0].T[:, :, None], (RET_HEADS, 2, LANES))
    y_ret_c, fin_ret = _ret_scan(proj, dexp, None, None, 0, B_ctx, T_ctx, True)
    (y_ret_l,) = _ret_scan(proj, dexp, _rope_tables(T_lat), state_ret[:, 0], n_ctx, B_lat, T_lat, False)

    x_new, hn, idx, wts, rank, cnt = _out_proj(0, n_ctx, n_tok, T_lat, [(y_gla_c, y_gla_l), (y_ret_c, y_ret_l)],
                                               w_out_even[0], x_cur, mods, norm_ffn[0], rw_pad[0], rb_pad[0:1])
    y4 = _moe_layer(0, hn, idx, rank, cnt[0, :N_EXPERTS], moe_w1, moe_b1, moe_w2, moe_b2)

    (proj, x_cur) = _in_proj(1, n_ctx, n_tok, T_lat, mods, norm_mix[1], w_in_odd[0], x_prev=x_new, y4=(y4, wts))
    y_h_c, fin_h = _hgrn_scan(proj, hgrn_lb_logits, hgrn_gain[0], None, 0, B_ctx, T_ctx, True)
    (y_h_l,) = _hgrn_scan(proj, hgrn_lb_logits, hgrn_gain[0], state_hgrn[:, 0], n_ctx, B_lat, T_lat, False)
    x_new, hn, idx, wts, rank, cnt = _out_proj(1, n_ctx, n_tok, T_lat, [(y_h_c, y_h_l)], w_out_odd[0], x_cur, mods,
                                               norm_ffn[1], rw_pad[1], rb_pad[1:2])
    y4 = _moe_layer(1, hn, idx, rank, cnt[0, :N_EXPERTS], moe_w1, moe_b1, moe_w2, moe_b2)

    y_ctx, y_lat = _final(n_ctx, n_tok, T_lat, x_new, y4, wts, mods, final_norm)

    new_state_gla = fin_gla.reshape(B_ctx, 1, 2, GLA_HEADS, GLA_DK, GLA_DV)
    new_state_ret = fin_ret.reshape(B_ctx, 1, 2, RET_HEADS, RET_DK, RET_DV)
    new_state_hgrn = fin_h.reshape(B_ctx, 1, 2, HG_HEADS, HG_DK, HG_DV)
    return (y_ctx.reshape(B_ctx, T_ctx, D), y_lat.reshape(B_lat, T_lat, D), new_state_gla, new_state_ret,
            new_state_hgrn)
```

```python
import functools

import jax
import jax.numpy as jnp
from jax import lax
from jax.experimental import pallas as pl
from jax.experimental.pallas import tpu as pltpu

F32 = jnp.float32
BF16 = jnp.bfloat16

D_MODEL = 1024
DEPTH = 2
GRID_W = 64
GLA_HEADS, GLA_DK, GLA_DV, GLA_LOW_RANK = 4, 64, 128, 16
GLA_NORMALIZER = 16.0
RET_HEADS, RET_DK, RET_DV = 4, 128, 128
ROPE_BASE = 10000.0
HG_HEADS, HG_DK, HG_DV = 8, 128, 128
N_EXPERTS, TOP_K, D_FF = 32, 4, 1024
SWIGLU_ALPHA, SWIGLU_LIMIT = 1.702, 7.0
EPS = 1e-6

LANES = 128
SCAN_CHUNK = 128
SCAN_ROWS = 1024
TOKEN_TILE = 256
BIG_TOKEN_TILE = 512
MOE_TILE = 384
VMEM_LIMIT = 56 * 1024 * 1024

_GQ, _GK, _GV, _GG, _RQ, _RK, _RV, _RG, _EVEN_MAIN = 0, 256, 512, 1024, 1536, 2048, 2560, 3072, 3584
_HQ, _HFF, _HFB, _HI, _HG, _ODD_MAIN = 0, 1024, 2048, 3072, 4096, 5120


def _dot(a, b):
    return jnp.dot(a, b, preferred_element_type=F32)


def _dot_nt(a, b):
    return lax.dot_general(a, b, (((1,), (1,)), ((), ())), preferred_element_type=F32)


def _rms(x, gain=None):
    y = x * lax.rsqrt(jnp.mean(x * x, axis=-1, keepdims=True) + EPS)
    if gain is not None:
        y = y * gain
    return y


def _sigmoid(x):
    return 1.0 / (1.0 + jnp.exp(-x))


def _silu(x):
    return x * _sigmoid(x)


def _cparams(sem, vmem=VMEM_LIMIT):
    return pltpu.CompilerParams(dimension_semantics=sem, vmem_limit_bytes=vmem)


def _resident(shape):
    nd = len(shape)
    return pl.BlockSpec(shape, lambda *_: (0,) * nd, pipeline_mode=pl.Buffered(1))


def _ada_kernel(c_ref, w_ref, b_ref, o_ref):
    o_ref[0] = _dot(_silu(c_ref[...]), w_ref[0]) + b_ref[0]


def _ada_mods(cond8, ada_w, ada_b):
    tn = 1536
    return pl.pallas_call(
        _ada_kernel,
        grid=(DEPTH, 6 * D_MODEL // tn),
        in_specs=[
            pl.BlockSpec((8, D_MODEL), lambda l, j: (0, 0)),
            pl.BlockSpec((1, D_MODEL, tn), lambda l, j: (l, 0, j)),
            pl.BlockSpec((1, 1, tn), lambda l, j: (l, 0, j)),
        ],
        out_specs=pl.BlockSpec((1, 8, tn), lambda l, j: (l, 0, j)),
        out_shape=jax.ShapeDtypeStruct((DEPTH, 8, 6 * D_MODEL), F32),
        compiler_params=_cparams(("arbitrary", "arbitrary")),
        name="ada_mods",
    )(cond8, ada_w, ada_b.reshape(DEPTH, 1, 6 * D_MODEL))


ROW_UNITS = D_MODEL // LANES


def _rows_from_units(ref, n_rows):
    return jnp.concatenate([ref[pl.ds(c, n_rows, stride=ROW_UNITS), :] for c in range(ROW_UNITS)], axis=1)


def _rows_to_units(ref, val):
    n_rows = val.shape[0]
    for c in range(ROW_UNITS):
        ref[pl.ds(c, n_rows, stride=ROW_UNITS), :] = val[:, c * LANES:(c + 1) * LANES]


def _moe_combine(y_refs, wt_ref):
    w = wt_ref[...]
    terms = [w[:, k:k + 1] * _rows_from_units(y_refs[k], w.shape[0]) for k in range(TOP_K)]
    return (terms[0] + terms[1]) + (terms[2] + terms[3])


def _in_proj_kernel(*refs, first, even, nb_ctx):
    it = iter(refs)
    if first:
        xa_ref, xb_ref = next(it), next(it)
    else:
        xp_ref = next(it)
        y_refs = [next(it) for _ in range(TOP_K)]
        wt_ref, modp_ref = next(it), next(it)
    gain_ref, mod_ref, w_ref = next(it), next(it), next(it)
    if even:
        wlr_ref, gkw_ref, gkb_ref = next(it), next(it), next(it)
    proj_ref = next(it)
    if even:
        g_ref = next(it)
    xcur_ref = next(it)

    D = D_MODEL
    i = pl.program_id(0)
    if first:
        x = jnp.where(i < nb_ctx, xa_ref[...], xb_ref[...])
    else:
        x = xp_ref[...] + modp_ref[0][:, 5 * D:6 * D] * _moe_combine(y_refs, wt_ref)
    xcur_ref[...] = x
    m = mod_ref[0]
    hn = _rms(x, gain_ref[...]) * (1.0 + m[:, D:2 * D]) + m[:, 0:D]
    proj_ref[...] = _dot(hn, w_ref[...])
    if even:
        z = _dot(_dot(hn, wlr_ref[...]), gkw_ref[...]) + gkb_ref[...]
        g_ref[...] = (jnp.minimum(z, 0.0) - jnp.log(1.0 + jnp.exp(-jnp.abs(z)))) * (1.0 / GLA_NORMALIZER)


def _cond_row(i, nb_ctx, tiles_per_lat_seq):
    return jnp.where(i < nb_ctx, 0, 1 + (i - nb_ctx) // tiles_per_lat_seq)


def _in_proj(layer, n_ctx, n_tok, lat_seq, mods, gain, w_main, *, x_parts=None, x_prev=None, y4=None, even_extra=None):
    first = x_parts is not None
    even = even_extra is not None
    tm = BIG_TOKEN_TILE if first else TOKEN_TILE
    nb = n_tok // tm
    nb_ctx = n_ctx // tm
    tpl = lat_seq // tm
    np_cols = w_main.shape[1]
    D = D_MODEL

    def mod_map(l):
        return lambda i: (l * 8 + _cond_row(i, nb_ctx, tpl), 0, 0)

    row = lambda i: (i, 0)
    args, specs = [], []
    if first:
        xa, xb = x_parts
        args += [xa, xb]
        specs += [pl.BlockSpec((tm, D), lambda i: (jnp.minimum(i, nb_ctx - 1), 0)),
                  pl.BlockSpec((tm, D), lambda i: (jnp.maximum(i - nb_ctx, 0), 0))]
    else:
        y4, wts = y4
        args += [x_prev] + [y4] * TOP_K + [wts, mods]
        specs += [pl.BlockSpec((tm, D), row)]
        specs += [pl.BlockSpec((tm * ROW_UNITS, LANES), (lambda k: (lambda i: (k * nb + i, 0)))(k))
                  for k in range(TOP_K)]
        specs += [pl.BlockSpec((tm, LANES), row), pl.BlockSpec((1, 1, 6 * D), mod_map(layer - 1))]
    args += [gain.reshape(1, D), mods, w_main]
    specs += [_resident((1, D)), pl.BlockSpec((1, 1, 6 * D), mod_map(layer)), _resident((D, np_cols))]
    out_shapes = [jax.ShapeDtypeStruct((n_tok, np_cols), F32)]
    out_specs = [pl.BlockSpec((tm, np_cols), row)]
    if even:
        w_lr, gkw, gkb = even_extra
        args += [w_lr, gkw, gkb]
        specs += [_resident(w_lr.shape), _resident(gkw.shape), _resident(gkb.shape)]
        out_shapes.append(jax.ShapeDtypeStruct((n_tok, 2 * GLA_HEADS * GLA_DK), F32))
        out_specs.append(pl.BlockSpec((tm, 2 * GLA_HEADS * GLA_DK), row))
    out_shapes.append(jax.ShapeDtypeStruct((n_tok, D), F32))
    out_specs.append(pl.BlockSpec((tm, D), row))
    return pl.pallas_call(
        functools.partial(_in_proj_kernel, first=first, even=even, nb_ctx=nb_ctx),
        grid=(nb,),
        in_specs=specs,
        out_specs=out_specs,
        out_shape=out_shapes,
        compiler_params=_cparams(("arbitrary",)),
        name=f"in_proj_l{layer}",
    )(*args)


def _tri(c, lower):
    r = lax.broadcasted_iota(jnp.int32, (c, c), 0)
    s = lax.broadcasted_iota(jnp.int32, (c, c), 1)
    return (r >= s) if lower else (r <= s)


def _cumsum_mm(tri_bf16, g):
    g_hi = g.astype(BF16)
    g_lo = (g - g_hi.astype(F32)).astype(BF16)
    r = _dot(tri_bf16, jnp.concatenate([g_hi, g_lo], axis=1))
    w = g.shape[1]
    return r[:, :w] + r[:, w:]


def _gated_scan_kernel(*refs, variant, T, nseq, has_s0, want_final, hpb):
    C = SCAN_CHUNK
    ncs = T // C
    nc = nseq * ncs
    it = iter(refs)
    if variant == "gla":
        q_ref, k_ref, v_ref, og_ref, gf_ref, gb_ref, gain_ref = (next(it) for _ in range(7))
    else:
        q_ref, ff_ref, fb_ref, v_ref, og_ref, lbl_ref, gain_ref = (next(it) for _ in range(7))
    s0_ref = next(it) if has_s0 else None
    y_ref = next(it)
    sfin_ref = next(it) if want_final else None
    qi_scr, o_scr, u_scr, dec_scr, sin_scr = (next(it) for _ in range(5))

    lane = lax.broadcasted_iota(jnp.int32, (1, LANES), 1)
    if hpb == 2:
        masks = [lane < GLA_DK, lane >= GLA_DK]
    else:
        masks = [None]
    tri_l, tri_u = _tri(C, True), _tri(C, False)
    tri_l16, tri_u16 = tri_l.astype(F32).astype(BF16), tri_u.astype(F32).astype(BF16)
    mid_f, mid_b = C // 2 - 1, C // 2

    if variant == "hgrn":
        lgs = [lbl_ref[l] for l in range(DEPTH)]
        mx = functools.reduce(jnp.maximum, lgs)
        es = [jnp.exp(l - mx) for l in lgs]
        tot = functools.reduce(lambda a, b: a + b, es)
        ps = [e / tot for e in es]
        layer = DEPTH - 1
        lb = functools.reduce(lambda a, b: a + b, ps[:layer + 1]) - ps[0]
        lb_f, lb_b = lb[0:1], lb[1:2]

    for n in range(nc):
        rows = pl.ds(n * C, C)
        if variant == "gla":
            q = q_ref[rows, :] * (GLA_DK ** -0.5)
            kf = kb = k_ref[rows, :]
            gf, gb = gf_ref[rows, :], gb_ref[rows, :]
        else:
            q = _silu(q_ref[rows, :])
            f_f = lb_f + (1.0 - lb_f) * _sigmoid(ff_ref[rows, :])
            f_b = lb_b + (1.0 - lb_b) * _sigmoid(fb_ref[rows, :])
            kf, kb = 1.0 - f_f, 1.0 - f_b
            gf, gb = jnp.log(f_f), jnp.log(f_b)
        bf = _cumsum_mm(tri_l16, gf)
        bb = _cumsum_mm(tri_u16, gb)
        bf_mid, bf_end = bf[mid_f:mid_f + 1], bf[C - 1:C]
        bb_mid, bb_end = bb[mid_b:mid_b + 1], bb[0:1]
        qd_f, kd_f = q * jnp.exp(bf - bf_mid), kf * jnp.exp(bf_mid - bf)
        qd_b, kd_b = q * jnp.exp(bb - bb_mid), kb * jnp.exp(bb_mid - bb)
        kend = jnp.concatenate([kf * jnp.exp(bf_end - bf), kb * jnp.exp(bb_end - bb)], axis=1)
        qi_scr[rows, :] = jnp.concatenate([q * jnp.exp(bf), q * jnp.exp(bb)], axis=1)
        dec_scr[n] = jnp.broadcast_to(jnp.concatenate([jnp.exp(bf_end), jnp.exp(bb_end)], axis=1), (8, 2 * LANES))
        for h in range(hpb):
            v = v_ref[rows, h * LANES:(h + 1) * LANES]
            if masks[h] is None:
                qf_h, qb_h, kend_h = qd_f, qd_b, kend
            else:
                qf_h, qb_h = jnp.where(masks[h], qd_f, 0.0), jnp.where(masks[h], qd_b, 0.0)
                kend_h = jnp.where(jnp.concatenate([masks[h], masks[h]], axis=1), kend, 0.0)
            s = jnp.where(tri_l, _dot_nt(qf_h, kd_f), 0.0) + jnp.where(tri_u, _dot_nt(qb_h, kd_b), 0.0)
            o_scr[rows, h * LANES:(h + 1) * LANES] = _dot(s, v)
            u_scr[h, n] = _dot(v.T, kend_h)

    for sq in range(nseq):
        for h in range(hpb):
            if has_s0:
                s_f, s_b = s0_ref[sq, 0, 0].T, s0_ref[sq, 1, 0].T
                if masks[h] is not None:
                    s_f, s_b = jnp.where(masks[h], s_f, 0.0), jnp.where(masks[h], s_b, 0.0)
            else:
                s_f = s_b = jnp.zeros((LANES, LANES), F32)
            for n in range(sq * ncs, (sq + 1) * ncs):
                sin_scr[h, n, :, 0:LANES] = s_f
                s_f = s_f * dec_scr[n, 0:1, 0:LANES] + u_scr[h, n, :, 0:LANES]
            for n in reversed(range(sq * ncs, (sq + 1) * ncs)):
                sin_scr[h, n, :, LANES:2 * LANES] = s_b
                s_b = s_b * dec_scr[n, 0:1, LANES:2 * LANES] + u_scr[h, n, :, LANES:2 * LANES]
            if want_final:
                if h == 0:
                    fin_f, fin_b = s_f, s_b
                else:
                    fin_f, fin_b = fin_f + s_f, fin_b + s_b
        if want_final:
            sfin_ref[sq, 0, 0] = fin_f.T
            sfin_ref[sq, 1, 0] = fin_b.T

    gain = gain_ref[...]
    for n in range(nc):
        rows = pl.ds(n * C, C)
        qi = qi_scr[rows, :]
        for h in range(hpb):
            cols = slice(h * LANES, (h + 1) * LANES)
            o = o_scr[rows, cols] + _dot_nt(qi, sin_scr[h, n])
            y_ref[rows, cols] = (_rms(o, gain) * _silu(og_ref[rows, cols])).astype(y_ref.dtype)


def _scan_scratch(T, hpb):
    nc = T // SCAN_CHUNK
    return [
        pltpu.VMEM((T, 2 * LANES), F32),
        pltpu.VMEM((T, hpb * LANES), F32),
        pltpu.VMEM((hpb, nc, LANES, 2 * LANES), F32),
        pltpu.VMEM((nc, 8, 2 * LANES), F32),
        pltpu.VMEM((hpb, nc, LANES, 2 * LANES), F32),
    ]


def _seqs_per_step(B, T):
    nseq = max(1, min(B, SCAN_ROWS // T))
    assert B % nseq == 0
    return nseq


def _gla_scan(proj, gdec, gain, s0, row0, B, T, want_final):
    nseq = _seqs_per_step(B, T)
    R = nseq * T
    rb0 = row0 // R
    assert row0 % R == 0
    has_s0 = s0 is not None

    def col(base, width=LANES):
        return lambda b, j: (rb0 + b, base // width + j)

    args = [proj, proj, proj, proj, gdec, gdec, gain.reshape(1, GLA_DV)]
    specs = [pl.BlockSpec((R, LANES), col(_GQ)), pl.BlockSpec((R, LANES), col(_GK)),
             pl.BlockSpec((R, 2 * LANES), col(_GV, 2 * LANES)), pl.BlockSpec((R, 2 * LANES), col(_GG, 2 * LANES)),
             pl.BlockSpec((R, LANES), col(0)), pl.BlockSpec((R, LANES), col(GLA_HEADS * GLA_DK)),
             pl.BlockSpec((1, GLA_DV), lambda b, j: (0, 0))]
    st_spec = pl.BlockSpec((nseq, 2, 1, LANES, LANES), lambda b, j: (b, 0, j, 0, 0))
    if has_s0:
        args.append(s0.reshape(B, 2, GLA_HEADS // 2, 2 * GLA_DK, GLA_DV))
        specs.append(st_spec)
    out_shapes = [jax.ShapeDtypeStruct((B * T, GLA_HEADS * GLA_DV), BF16)]
    out_specs = [pl.BlockSpec((R, 2 * LANES), lambda b, j: (b, j))]
    if want_final:
        out_shapes.append(jax.ShapeDtypeStruct((B, 2, GLA_HEADS // 2, 2 * GLA_DK, GLA_DV), F32))
        out_specs.append(st_spec)
    kern = functools.partial(_gated_scan_kernel, variant="gla", T=T, nseq=nseq, has_s0=has_s0,
                             want_final=want_final, hpb=2)
    return pl.pallas_call(
        kern, grid=(B // nseq, GLA_HEADS // 2), in_specs=specs, out_specs=out_specs, out_shape=out_shapes,
        scratch_shapes=_scan_scratch(R, 2),
        compiler_params=_cparams(("arbitrary", "arbitrary")), name=f"gla_scan_T{T}",
    )(*args)


def _hgrn_scan(proj, lb_logits, gain, s0, row0, B, T, want_final):
    nseq = _seqs_per_step(B, T)
    R = nseq * T
    rb0 = row0 // R
    assert row0 % R == 0
    has_s0 = s0 is not None

    def col(base):
        return lambda b, j: (rb0 + b, base // LANES + j)

    args = [proj, proj, proj, proj, proj, lb_logits, gain.reshape(1, HG_DV)]
    specs = [pl.BlockSpec((R, LANES), col(_HQ)), pl.BlockSpec((R, LANES), col(_HFF)),
             pl.BlockSpec((R, LANES), col(_HFB)), pl.BlockSpec((R, LANES), col(_HI)),
             pl.BlockSpec((R, LANES), col(_HG)),
             pl.BlockSpec((DEPTH, 2, LANES), lambda b, j: (0, 0, j)),
             pl.BlockSpec((1, HG_DV), lambda b, j: (0, 0))]
    st_spec = pl.BlockSpec((nseq, 2, 1, LANES, LANES), lambda b, j: (b, 0, j, 0, 0))
    if has_s0:
        args.append(s0.reshape(B, 2, HG_HEADS, HG_DK, HG_DV))
        specs.append(st_spec)
    out_shapes = [jax.ShapeDtypeStruct((B * T, HG_HEADS * HG_DV), BF16)]
    out_specs = [pl.BlockSpec((R, LANES), lambda b, j: (b, j))]
    if want_final:
        out_shapes.append(jax.ShapeDtypeStruct((B, 2, HG_HEADS, HG_DK, HG_DV), F32))
        out_specs.append(st_spec)
    kern = functools.partial(_gated_scan_kernel, variant="hgrn", T=T, nseq=nseq, has_s0=has_s0,
                             want_final=want_final, hpb=1)
    return pl.pallas_call(
        kern, grid=(B // nseq, HG_HEADS), in_specs=specs, out_specs=out_specs, out_shape=out_shapes,
        scratch_shapes=_scan_scratch(R, 1),
        compiler_params=_cparams(("arbitrary", "arbitrary")), name=f"hgrn_scan_T{T}",
    )(*args)


def _ret_scan_kernel(*refs, T, nseq, has_s0, want_final, rope):
    C = SCAN_CHUNK
    ncs = T // C
    nc = nseq * ncs
    it = iter(refs)
    q_ref, k_ref, v_ref, og_ref, dexp_ref = (next(it) for _ in range(5))
    if rope:
        cos_ref, sin_ref = next(it), next(it)
    s0_ref = next(it) if has_s0 else None
    y_ref = next(it)
    sfin_ref = next(it) if want_final else None
    qi_scr, o_scr, u_scr, sin_scr = (next(it) for _ in range(4))

    lg = jnp.log1p(-jnp.exp2(-dexp_ref[0]))
    lg_f, lg_b = lg[0:1], lg[1:2]
    r = lax.broadcasted_iota(jnp.int32, (C, C), 0)
    s = lax.broadcasted_iota(jnp.int32, (C, C), 1)
    dist = (r - s).astype(F32)
    dmask = (jnp.where(r >= s, jnp.exp(jnp.maximum(dist, 0.0) * lg_f[:, 0:1]), 0.0)
             + jnp.where(r <= s, jnp.exp(jnp.maximum(-dist, 0.0) * lg_b[:, 0:1]), 0.0))
    pos = lax.broadcasted_iota(jnp.int32, (C, LANES), 0).astype(F32)
    xi = jnp.concatenate([jnp.exp((pos + 1.0) * lg_f), jnp.exp((C - pos) * lg_b)], axis=1)
    zeta = jnp.concatenate([jnp.exp((C - 1.0 - pos) * lg_f), jnp.exp(pos * lg_b)], axis=1)
    d_f, d_b = jnp.exp(C * lg_f), jnp.exp(C * lg_b)

    def rot(x, seq_rows):
        if not rope:
            return x
        return x * cos_ref[seq_rows, :] + pltpu.roll(x, RET_DK // 2, axis=1) * sin_ref[seq_rows, :]

    for n in range(nc):
        rows = pl.ds(n * C, C)
        seq_rows = pl.ds((n % ncs) * C, C)
        q = rot(q_ref[rows, :], seq_rows)
        k = rot(k_ref[rows, :] * (RET_DK ** -0.5), seq_rows)
        v = v_ref[rows, :]
        o_scr[rows, :] = _dot(_dot_nt(q, k) * dmask, v)
        qi_scr[rows, :] = jnp.concatenate([q, q], axis=1) * xi
        u_scr[n] = _dot(v.T, jnp.concatenate([k, k], axis=1) * zeta)

    for sq in range(nseq):
        if has_s0:
            s_f, s_b = s0_ref[sq, 0, 0].T, s0_ref[sq, 1, 0].T
        else:
            s_f = s_b = jnp.zeros((LANES, LANES), F32)
        for n in range(sq * ncs, (sq + 1) * ncs):
            sin_scr[n, :, 0:LANES] = s_f
            s_f = s_f * d_f + u_scr[n, :, 0:LANES]
        for n in reversed(range(sq * ncs, (sq + 1) * ncs)):
            sin_scr[n, :, LANES:2 * LANES] = s_b
            s_b = s_b * d_b + u_scr[n, :, LANES:2 * LANES]
        if want_final:
            sfin_ref[sq, 0, 0] = s_f.T
            sfin_ref[sq, 1, 0] = s_b.T

    for n in range(nc):
        rows = pl.ds(n * C, C)
        o = o_scr[rows, :] + _dot_nt(qi_scr[rows, :], sin_scr[n])
        y_ref[rows, :] = (_rms(o) * _silu(og_ref[rows, :])).astype(y_ref.dtype)


def _ret_scan(proj, dexp, rope_tabs, s0, row0, B, T, want_final):
    nseq = _seqs_per_step(B, T)
    R = nseq * T
    rb0 = row0 // R
    assert row0 % R == 0
    nc = R // SCAN_CHUNK
    has_s0 = s0 is not None
    rope = rope_tabs is not None

    def col(base):
        return lambda b, j: (rb0 + b, base // LANES + j)

    args = [proj, proj, proj, proj, dexp]
    specs = [pl.BlockSpec((R, LANES), col(_RQ)), pl.BlockSpec((R, LANES), col(_RK)),
             pl.BlockSpec((R, LANES), col(_RV)), pl.BlockSpec((R, LANES), col(_RG)),
             pl.BlockSpec((1, 2, LANES), lambda b, j: (j, 0, 0))]
    if rope:
        args += list(rope_tabs)
        specs += [pl.BlockSpec((T, LANES), lambda b, j: (0, 0))] * 2
    st_spec = pl.BlockSpec((nseq, 2, 1, LANES, LANES), lambda b, j: (b, 0, j, 0, 0))
    if has_s0:
        args.append(s0.reshape(B, 2, RET_HEADS, RET_DK, RET_DV))
        specs.append(st_spec)
    out_shapes = [jax.ShapeDtypeStruct((B * T, RET_HEADS * RET_DV), BF16)]
    out_specs = [pl.BlockSpec((R, LANES), lambda b, j: (b, j))]
    if want_final:
        out_shapes.append(jax.ShapeDtypeStruct((B, 2, RET_HEADS, RET_DK, RET_DV), F32))
        out_specs.append(st_spec)
    kern = functools.partial(_ret_scan_kernel, T=T, nseq=nseq, has_s0=has_s0, want_final=want_final, rope=rope)
    scratch = [pltpu.VMEM((R, 2 * LANES), F32), pltpu.VMEM((R, LANES), F32),
               pltpu.VMEM((nc, LANES, 2 * LANES), F32), pltpu.VMEM((nc, LANES, 2 * LANES), F32)]
    return pl.pallas_call(
        kern, grid=(B // nseq, RET_HEADS), in_specs=specs, out_specs=out_specs, out_shape=out_shapes,
        scratch_shapes=scratch,
        compiler_params=_cparams(("arbitrary", "arbitrary")), name=f"ret_scan_T{T}",
    )(*args)


def _out_proj_kernel(*refs, n_mix, nb_ctx):
    it = iter(refs)
    y_refs = [(next(it), next(it)) for _ in range(n_mix)]
    wo_ref, x_ref, mod_ref, gain_ref, rw_ref, rb_ref = (next(it) for _ in range(6))
    xnew_ref, hn_ref, idx_ref, wt_ref, rank_ref, cnt_ref = (next(it) for _ in range(6))
    cnt_scr = next(it)
    D = D_MODEL
    is_ctx = pl.program_id(0) < nb_ctx
    mix = None
    r0 = 0
    for ya_ref, yb_ref in y_refs:
        w = ya_ref.shape[1]
        part = _dot(jnp.where(is_ctx, ya_ref[...], yb_ref[...]), wo_ref[r0:r0 + w, :])
        mix = part if mix is None else mix + part
        r0 += w
    m = mod_ref[0]
    xn = x_ref[...] + m[:, 2 * D:3 * D] * mix
    xnew_ref[...] = xn
    hn = _rms(xn, gain_ref[...]) * (1.0 + m[:, 4 * D:5 * D]) + m[:, 3 * D:4 * D]
    hn_hi = hn.astype(BF16)
    _rows_to_units(hn_ref, hn)
    hn_lo = (hn - hn_hi.astype(F32)).astype(BF16)
    rw = rw_ref[...]
    rw_hi = rw.astype(BF16)
    rw_lo = (rw - rw_hi.astype(F32)).astype(BF16)
    logits = (_dot(hn_hi, rw_hi) + (_dot(hn_lo, rw_hi) + _dot(hn_hi, rw_lo))) + rb_ref[...]
    lane = lax.broadcasted_iota(jnp.int32, logits.shape, 1).astype(F32)
    vals, idxs = [], []
    cur = logits
    for _ in range(TOP_K):
        mx = jnp.max(cur, axis=-1, keepdims=True)
        ik = jnp.min(jnp.where(cur == mx, lane, float(LANES)), axis=-1, keepdims=True)
        vals.append(mx)
        idxs.append(ik)
        cur = jnp.where(lane == ik, -jnp.inf, cur)
    es = [jnp.exp(v - vals[0]) for v in vals]
    tot = (es[0] + es[1]) + (es[2] + es[3])
    idx_out = jnp.zeros(logits.shape, F32)
    wt_out = jnp.zeros(logits.shape, F32)
    for k in range(TOP_K):
        idx_out = jnp.where(lane == float(k), idxs[k], idx_out)
        wt_out = jnp.where(lane == float(k), es[k] / tot, wt_out)
    idx_ref[...] = idx_out.astype(jnp.int32)
    wt_ref[...] = wt_out

    @pl.when(pl.program_id(0) == 0)
    def _():
        cnt_scr[...] = jnp.zeros(cnt_scr.shape, F32)

    tm = logits.shape[0]
    hits = [lane == idxs[k] for k in range(TOP_K)]
    sel = jnp.zeros(logits.shape, F32)
    for k in range(TOP_K):
        sel = sel + jnp.where(hits[k], 1.0, 0.0)
    rr = lax.broadcasted_iota(jnp.int32, (tm, tm), 0)
    cc = lax.broadcasted_iota(jnp.int32, (tm, tm), 1)
    before = jnp.where(rr > cc, 1.0, 0.0).astype(BF16)
    rank_all = cnt_scr[0:1, :] + _dot(before, sel.astype(BF16))
    rank_out = jnp.zeros(logits.shape, F32)
    for k in range(TOP_K):
        rk = jnp.sum(jnp.where(hits[k], rank_all, 0.0), axis=-1, keepdims=True)
        rank_out = jnp.where(lane == float(k), rk, rank_out)
    rank_ref[...] = rank_out.astype(jnp.int32)
    total = cnt_scr[...] + jnp.sum(sel, axis=0, keepdims=True)
    cnt_scr[...] = total
    cnt_ref[...] = total.astype(jnp.int32)


def _out_proj(layer, n_ctx, n_tok, lat_seq, ys, w_out, x_cur, mods, gain, rw_pad, rb_pad):
    tm = BIG_TOKEN_TILE
    nb, nb_ctx, tpl = n_tok // tm, n_ctx // tm, lat_seq // tm
    D = D_MODEL
    row = lambda i: (i, 0)
    specs = []
    for ya, _ in ys:
        specs += [pl.BlockSpec((tm, ya.shape[1]), lambda i: (jnp.minimum(i, nb_ctx - 1), 0)),
                  pl.BlockSpec((tm, ya.shape[1]), lambda i: (jnp.maximum(i - nb_ctx, 0), 0))]
    specs += [_resident((D, D)), pl.BlockSpec((tm, D), row),
              pl.BlockSpec((1, 1, 6 * D), lambda i: (layer * 8 + _cond_row(i, nb_ctx, tpl), 0, 0)),
              _resident((1, D)), _resident((D, LANES)), _resident((1, LANES))]
    return pl.pallas_call(
        functools.partial(_out_proj_kernel, n_mix=len(ys), nb_ctx=nb_ctx),
        grid=(nb,),
        in_specs=specs,
        out_specs=[pl.BlockSpec((tm, D), row), pl.BlockSpec((tm * ROW_UNITS, LANES), row),
                   pl.BlockSpec((tm, LANES), row), pl.BlockSpec((tm, LANES), row),
                   pl.BlockSpec((tm, LANES), row), pl.BlockSpec((8, LANES), lambda i: (0, 0))],
        out_shape=[jax.ShapeDtypeStruct((n_tok, D), F32), jax.ShapeDtypeStruct((n_tok * ROW_UNITS, LANES), F32),
                   jax.ShapeDtypeStruct((n_tok, LANES), jnp.int32), jax.ShapeDtypeStruct((n_tok, LANES), F32),
                   jax.ShapeDtypeStruct((n_tok, LANES), jnp.int32), jax.ShapeDtypeStruct((8, LANES), jnp.int32)],
        scratch_shapes=[pltpu.VMEM((8, LANES), F32)],
        compiler_params=_cparams(("arbitrary",)),
        name=f"out_proj_l{layer}",
    )(*[y for pair in ys for y in pair], w_out, x_cur, mods, gain.reshape(1, D), rw_pad, rb_pad)


W1_SPLIT = 4
W2_SPLIT = 2


def _moe_kernel(seg_ref, nt_ref, code_hbm, x_hbm, *refs, n_tok):
    w1_refs = refs[:W1_SPLIT]
    b1_ref = refs[W1_SPLIT]
    w2_refs = refs[W1_SPLIT + 1:W1_SPLIT + 1 + W2_SPLIT]
    b2_ref, y_hbm, w1_scr, w2_scr, act_scr, xbuf, obuf, code_smem, csem, gsem, ssem = refs[W1_SPLIT + 1 + W2_SPLIT:]
    tm = MOE_TILE
    U = ROW_UNITS
    n_code_tiles = code_hbm.shape[0] // tm
    e = pl.program_id(0)
    nt = nt_ref[e]
    g_first = seg_ref[e] // tm

    GATHER, SCATTER = 0, 1

    def code_copy(kind, t, p):
        first = pl.multiple_of(jnp.clip(t, 0, n_code_tiles - 1) * tm, tm)
        return pltpu.make_async_copy(code_hbm.at[pl.ds(first, tm)], code_smem.at[kind, p], csem.at[kind, p])

    def gather_start(b, p):
        for r in range(tm):
            tok = jnp.bitwise_and(code_smem[GATHER, p, r], n_tok - 1)
            pltpu.make_async_copy(x_hbm.at[pl.ds(pl.multiple_of(tok * U, U), U)],
                                  xbuf.at[b, pl.ds(r * U, U)], gsem.at[b]).start()

    def gather_wait(b):
        pltpu.make_async_copy(x_hbm.at[pl.ds(0, tm * U)], xbuf.at[b], gsem.at[b]).wait()

    def scatter_start(b, p):
        for r in range(tm):
            dst = code_smem[SCATTER, p, r]
            pltpu.make_async_copy(obuf.at[b, pl.ds(r * U, U)],
                                  y_hbm.at[pl.ds(pl.multiple_of(dst * U, U), U)], ssem.at[b]).start()

    def scatter_wait(b):
        pltpu.make_async_copy(obuf.at[b], y_hbm.at[pl.ds(0, tm * U)], ssem.at[b]).wait()

    @pl.when(e == 0)
    def _():
        obuf[...] = jnp.zeros(obuf.shape, obuf.dtype)
        for h in range(2):
            cp = pltpu.make_async_copy(obuf.at[0], y_hbm.at[pl.ds((TOP_K * n_tok + h * tm) * U, tm * U)], ssem.at[0])
            cp.start()
            cp.wait()
        code_copy(GATHER, 0, 0).start()
        code_copy(GATHER, 1, 1).start()
        code_copy(SCATTER, n_code_tiles - 1, 1).start()
        code_copy(SCATTER, 0, 0).start()
        code_copy(GATHER, 0, 0).wait()
        gather_start(0, 0)

    @pl.when(nt > 0)
    def _():
        cw = 2 * D_FF // W1_SPLIT
        for c, w_ref in enumerate(w1_refs):
            w1_scr[:, c * cw:(c + 1) * cw] = w_ref[0, 0].astype(BF16)
        rh = D_FF // W2_SPLIT
        for c, w_ref in enumerate(w2_refs):
            w2_scr[c * rh:(c + 1) * rh, :] = w_ref[0, 0].astype(BF16)

        blk = 4 * LANES
        rr = lax.broadcasted_iota(jnp.int32, (blk, blk // 2), 0)
        cc = lax.broadcasted_iota(jnp.int32, (blk, blk // 2), 1)
        sel = jnp.where(rr == 2 * cc, 1.0, 0.0).astype(BF16)

        def tile(g, b):
            o = 1 - b

            @pl.when(g >= 0)
            def _():
                code_copy(GATHER, g + 1, o).wait()
                code_copy(SCATTER, g - 1, o).wait()
                gather_start(o, o)
                scatter_start(o, o)

            gather_wait(b)
            x = _rows_from_units(xbuf.at[b], tm).astype(BF16)
            for c in range(2 * D_FF // blk):
                cols = slice(c * blk, (c + 1) * blk)
                hid = _dot(x, w1_scr[:, cols]) + b1_ref[0, 0][:, cols]
                nxt = pltpu.roll(hid, blk - 1, axis=1)
                glu = jnp.minimum(hid, SWIGLU_LIMIT)
                lin = jnp.clip(nxt, -SWIGLU_LIMIT, SWIGLU_LIMIT)
                act = glu * _sigmoid(SWIGLU_ALPHA * glu) * (lin + 1.0)
                act_scr[:, c * (blk // 2):(c + 1) * (blk // 2)] = _dot(act.astype(BF16), sel).astype(BF16)
            y = _dot(act_scr[...], w2_scr[...]) + b2_ref[0, 0]

            @pl.when(g >= 1)
            def _():
                scatter_wait(b)

            _rows_to_units(obuf.at[b], y)
            code_copy(GATHER, g + 2, b).start()
            code_copy(SCATTER, g + 1, o).start()

        g_end = g_first + nt

        def tile_pair(m, carry):
            for b in range(2):
                g = 2 * m + b

                @pl.when(jnp.logical_and(g >= g_first, g < g_end))
                def _():
                    tile(g, b)

            return carry

        lax.fori_loop(g_first // 2, (g_end + 1) // 2, tile_pair, 0)

    @pl.when(e == N_EXPERTS - 1)
    def _():
        g_end = g_first + nt
        last = lax.rem(g_end + 1, 2)
        code_copy(SCATTER, g_end - 1, last).wait()
        scatter_start(last, last)
        scatter_wait(last)
        scatter_wait(1 - last)
        gather_wait(1 - last)
        code_copy(GATHER, g_end + 1, last).wait()
        code_copy(SCATTER, g_end, 1 - last).wait()


def _moe_experts(layer, n_tok, seg_start, tiles_per_e, code, x_units, w1, b1, w2, b2):
    tm = MOE_TILE
    D, F2, U = D_MODEL, 2 * D_FF, ROW_UNITS
    cw, rh = F2 // W1_SPLIT, D_FF // W2_SPLIT
    w1_specs = [pl.BlockSpec((1, 1, D, cw), (lambda c: (lambda e, sg, nt: (layer, e, 0, c)))(c))
                for c in range(W1_SPLIT)]
    w2_specs = [pl.BlockSpec((1, 1, rh, D), (lambda c: (lambda e, sg, nt: (layer, e, c, 0)))(c))
                for c in range(W2_SPLIT)]
    grid_spec = pltpu.PrefetchScalarGridSpec(
        num_scalar_prefetch=2,
        grid=(N_EXPERTS,),
        in_specs=[pl.BlockSpec(memory_space=pl.ANY), pl.BlockSpec(memory_space=pl.ANY)]
        + w1_specs + [pl.BlockSpec((1, 1, 1, F2), lambda e, sg, nt: (layer, e, 0, 0))]
        + w2_specs + [pl.BlockSpec((1, 1, 1, D), lambda e, sg, nt: (layer, e, 0, 0))],
        out_specs=pl.BlockSpec(memory_space=pl.ANY),
        scratch_shapes=[pltpu.VMEM((D, F2), BF16), pltpu.VMEM((D_FF, D), BF16), pltpu.VMEM((tm, D_FF), BF16),
                        pltpu.VMEM((2, tm * U, LANES), F32), pltpu.VMEM((2, tm * U, LANES), F32),
                        pltpu.SMEM((2, 2, tm), jnp.int32),
                        pltpu.SemaphoreType.DMA((2, 2)), pltpu.SemaphoreType.DMA((2,)), pltpu.SemaphoreType.DMA((2,))],
    )
    n_out_rows = TOP_K * n_tok + 2 * tm
    return pl.pallas_call(
        functools.partial(_moe_kernel, n_tok=n_tok),
        grid_spec=grid_spec,
        out_shape=jax.ShapeDtypeStruct((n_out_rows * U, LANES), F32),
        compiler_params=_cparams(("arbitrary",)),
        name="moe_experts",
    )(seg_start, tiles_per_e, code, x_units, *([w1] * W1_SPLIT), b1.reshape(DEPTH, N_EXPERTS, 1, F2),
      *([w2] * W2_SPLIT), b2.reshape(DEPTH, N_EXPERTS, 1, D))


def _pos_kernel(idx_ref, rank_ref, seg_ref, pos_ref):
    lane = lax.broadcasted_iota(jnp.int32, idx_ref.shape, 1)
    idx, rank, seg = idx_ref[...], rank_ref[...], seg_ref[...]
    out = jnp.zeros(idx.shape, jnp.int32)
    for k in range(TOP_K):
        hit = lane == idx[:, k:k + 1]
        seg_k = jnp.sum(jnp.where(hit, seg, 0.0), axis=-1, keepdims=True).astype(jnp.int32)
        out = jnp.where(lane == k, seg_k + rank[:, k:k + 1], out)
    pos_ref[...] = out


def _pair_slots(idx, rank, seg_start):
    n_tok = idx.shape[0]
    tm = TOKEN_TILE
    row = lambda i: (i, 0)
    return pl.pallas_call(
        _pos_kernel,
        grid=(n_tok // tm,),
        in_specs=[pl.BlockSpec((tm, LANES), row), pl.BlockSpec((tm, LANES), row),
                  pl.BlockSpec((1, LANES), lambda i: (0, 0))],
        out_specs=pl.BlockSpec((tm, LANES), row),
        out_shape=jax.ShapeDtypeStruct((n_tok, LANES), jnp.int32),
        compiler_params=_cparams(("arbitrary",)),
        name="pair_slots",
    )(idx, rank, seg_start)


def _slot_code_kernel(pos_ref, unused_hbm, code_ref, sem, *, n_tok):
    i = pl.program_id(0)
    tm = pos_ref.shape[0] // TOP_K

    @pl.when(i == 0)
    def _():
        cp = pltpu.make_async_copy(unused_hbm, code_ref, sem)
        cp.start()
        cp.wait()

    base = i * tm
    for t in range(tm):
        for k in range(TOP_K):
            code_ref[pos_ref[t * TOP_K + k]] = base + (k * n_tok + t)


def _slot_codes(pos, unused):
    n_tok = pos.shape[0]
    tm = TOKEN_TILE
    return pl.pallas_call(
        functools.partial(_slot_code_kernel, n_tok=n_tok),
        grid=(n_tok // tm,),
        in_specs=[pl.BlockSpec((tm * TOP_K,), lambda i: (i,), memory_space=pltpu.SMEM),
                  pl.BlockSpec(memory_space=pl.ANY)],
        out_specs=pl.BlockSpec(memory_space=pltpu.SMEM),
        out_shape=jax.ShapeDtypeStruct(unused.shape, jnp.int32),
        scratch_shapes=[pltpu.SemaphoreType.DMA(())],
        compiler_params=_cparams(("arbitrary",)),
        name="slot_codes",
    )(pos.reshape(-1), unused)


def _moe_layer(layer, hn_units, idx, rank, counts, w1, b1, w2, b2):
    n_tok = hn_units.shape[0] // ROW_UNITS
    assert n_tok & (n_tok - 1) == 0
    tm = MOE_TILE
    n_pairs = n_tok * TOP_K
    n_slots = ((n_pairs + N_EXPERTS * (tm - 1)) // tm + 1) * tm
    tiles_per_e = (counts + tm - 1) // tm
    seg_start = (jnp.cumsum(tiles_per_e) - tiles_per_e) * tm
    seg_f32 = jnp.pad(seg_start.astype(F32), (0, LANES - N_EXPERTS)).reshape(1, LANES)
    pos = _pair_slots(idx, rank, seg_f32)[:, :TOP_K]
    unused = n_pairs + jnp.arange(n_slots, dtype=jnp.int32) % (2 * tm)
    code = _slot_codes(pos, unused)
    return _moe_experts(layer, n_tok, seg_start, tiles_per_e, code, hn_units, w1, b1, w2, b2)


def _final_kernel(*refs, nb_ctx):
    x_ref = refs[0]
    y_refs = refs[1:1 + TOP_K]
    wt_ref, mod_ref, gain_ref, oa_ref, ob_ref = refs[1 + TOP_K:]
    D = D_MODEL
    i = pl.program_id(0)
    x = x_ref[...] + mod_ref[0][:, 5 * D:6 * D] * _moe_combine(y_refs, wt_ref)
    out = _rms(x, gain_ref[...])

    @pl.when(i < nb_ctx)
    def _():
        oa_ref[...] = out

    @pl.when(i >= nb_ctx)
    def _():
        ob_ref[...] = out


def _final(n_ctx, n_tok, lat_seq, x_new, y4, wts, mods, gain):
    tm = TOKEN_TILE
    nb, nb_ctx, tpl = n_tok // tm, n_ctx // tm, lat_seq // tm
    D = D_MODEL
    specs = [pl.BlockSpec((tm, D), lambda i: (i, 0))]
    specs += [pl.BlockSpec((tm * ROW_UNITS, LANES), (lambda k: (lambda i: (k * nb + i, 0)))(k)) for k in range(TOP_K)]
    specs += [pl.BlockSpec((tm, LANES), lambda i: (i, 0)),
              pl.BlockSpec((1, 1, 6 * D), lambda i: ((DEPTH - 1) * 8 + _cond_row(i, nb_ctx, tpl), 0, 0)),
              _resident((1, D))]
    return pl.pallas_call(
        functools.partial(_final_kernel, nb_ctx=nb_ctx),
        grid=(nb,),
        in_specs=specs,
        out_specs=[pl.BlockSpec((tm, D), lambda i: (jnp.minimum(i, nb_ctx - 1), 0)),
                   pl.BlockSpec((tm, D), lambda i: (jnp.maximum(i - nb_ctx, 0), 0))],
        out_shape=[jax.ShapeDtypeStruct((n_ctx, D), F32), jax.ShapeDtypeStruct((n_tok - n_ctx, D), F32)],
        compiler_params=_cparams(("arbitrary",)),
        name="final_norm",
    )(x_new, *([y4] * TOP_K), wts, mods, gain.reshape(1, D))


def _rope_tables(n_tokens):
    rows = n_tokens // GRID_W
    r = jnp.repeat(jnp.arange(rows, dtype=F32), GRID_W)
    col = jnp.tile(jnp.arange(GRID_W, dtype=F32), rows)
    nf = RET_DK // 4
    inv = ROPE_BASE ** (-jnp.arange(nf, dtype=F32) / nf)
    ang = jnp.concatenate([r[:, None] * inv, col[:, None] * inv], axis=-1)
    cos, sin = jnp.cos(ang), jnp.sin(ang)
    return jnp.concatenate([cos, cos], axis=-1), jnp.concatenate([-sin, sin], axis=-1)


def kernel(x_prompt, x_sample, c, state_gla, state_ret, state_hgrn, c_ctx, norm_mix, norm_ffn, ada_w, ada_b,
           w_in_even, w_out_even, gla_gk_w, gla_gk_b, gla_gain, ret_decay_exp, w_in_odd, w_out_odd,
           hgrn_lb_logits, hgrn_gain, router_w, router_b, moe_w1, moe_b1, moe_w2, moe_b2, final_norm):
    D = D_MODEL
    B_ctx, T_ctx, _ = x_prompt.shape
    B_lat, T_lat, _ = x_sample.shape
    n_ctx, n_lat = B_ctx * T_ctx, B_lat * T_lat
    n_tok = n_ctx + n_lat
    assert n_ctx % BIG_TOKEN_TILE == 0 and T_lat % BIG_TOKEN_TILE == 0 and B_lat + 1 <= 8
    assert T_ctx % SCAN_CHUNK == 0 and T_lat % SCAN_CHUNK == 0 and n_ctx % T_lat == 0

    cond8 = jnp.concatenate([c_ctx[None, :], c, jnp.zeros((8 - 1 - B_lat, D), F32)], axis=0)
    mods = _ada_mods(cond8, ada_w, ada_b).reshape(DEPTH * 8, 1, 6 * D)

    rw_pad = jnp.pad(router_w, ((0, 0), (0, 0), (0, LANES - N_EXPERTS)))
    rb_pad = jnp.pad(router_b, ((0, 0), (0, LANES - N_EXPERTS)), constant_values=-1e30)

    w_even = w_in_even[0]
    lr0 = GLA_HEADS * (2 * GLA_DK + GLA_DV)
    lr1 = lr0 + 2 * GLA_LOW_RANK
    w_main = jnp.concatenate([w_even[:, :lr0], w_even[:, lr1:]], axis=1)
    w_lr = jnp.pad(w_even[:, lr0:lr1], ((0, 0), (0, LANES - 2 * GLA_LOW_RANK)))
    nqk = GLA_HEADS * GLA_DK
    gkw = jnp.zeros((LANES, 2 * nqk), F32)
    gkw = gkw.at[0:GLA_LOW_RANK, 0:nqk].set(gla_gk_w[0, 0])
    gkw = gkw.at[GLA_LOW_RANK:2 * GLA_LOW_RANK, nqk:].set(gla_gk_w[0, 1])
    gkb = gla_gk_b[0].reshape(1, 2 * nqk)
    proj, gdec, x_cur = _in_proj(0, n_ctx, n_tok, T_lat, mods, norm_mix[0], w_main,
                                 x_parts=(x_prompt.reshape(n_ctx, D), x_sample.reshape(n_lat, D)),
                                 even_extra=(w_lr, gkw, gkb))

    y_gla_c, fin_gla = _gla_scan(proj, gdec, gla_gain[0], None, 0, B_ctx, T_ctx, True)
    (y_gla_l,) = _gla_scan(proj, gdec, gla_gain[0], state_gla[:, 0], n_ctx, B_lat, T_lat, False)
    dexp = jnp.broadcast_to(ret_decay_exp[0].T[:, :, None], (RET_HEADS, 2, LANES))
    y_ret_c, fin_ret = _ret_scan(proj, dexp, None, None, 0, B_ctx, T_ctx, True)
    (y_ret_l,) = _ret_scan(proj, dexp, _rope_tables(T_lat), state_ret[:, 0], n_ctx, B_lat, T_lat, False)

    x_new, hn, idx, wts, rank, cnt = _out_proj(0, n_ctx, n_tok, T_lat, [(y_gla_c, y_gla_l), (y_ret_c, y_ret_l)],
                                               w_out_even[0], x_cur, mods, norm_ffn[0], rw_pad[0], rb_pad[0:1])
    y4 = _moe_layer(0, hn, idx, rank, cnt[0, :N_EXPERTS], moe_w1, moe_b1, moe_w2, moe_b2)

    (proj, x_cur) = _in_proj(1, n_ctx, n_tok, T_lat, mods, norm_mix[1], w_in_odd[0], x_prev=x_new, y4=(y4, wts))
    y_h_c, fin_h = _hgrn_scan(proj, hgrn_lb_logits, hgrn_gain[0], None, 0, B_ctx, T_ctx, True)
    (y_h_l,) = _hgrn_scan(proj, hgrn_lb_logits, hgrn_gain[0], state_hgrn[:, 0], n_ctx, B_lat, T_lat, False)
    x_new, hn, idx, wts, rank, cnt = _out_proj(1, n_ctx, n_tok, T_lat, [(y_h_c, y_h_l)], w_out_odd[0], x_cur, mods,
                                               norm_ffn[1], rw_pad[1], rb_pad[1:2])
    y4 = _moe_layer(1, hn, idx, rank, cnt[0, :N_EXPERTS], moe_w1, moe_b1, moe_w2, moe_b2)

    y_ctx, y_lat = _final(n_ctx, n_tok, T_lat, x_new, y4, wts, mods, final_norm)

    new_state_gla = fin_gla.reshape(B_ctx, 1, 2, GLA_HEADS, GLA_DK, GLA_DV)
    new_state_ret = fin_ret.reshape(B_ctx, 1, 2, RET_HEADS, RET_DK, RET_DV)
    new_state_hgrn = fin_h.reshape(B_ctx, 1, 2, HG_HEADS, HG_DK, HG_DV)
    return (y_ctx.reshape(B_ctx, T_ctx, D), y_lat.reshape(B_lat, T_lat, D), new_state_gla, new_state_ret,
            new_state_hgrn)
```

```python
import functools

import jax
import jax.numpy as jnp
from jax import lax
from jax.experimental import pallas as pl
from jax.experimental.pallas import tpu as pltpu

F32 = jnp.float32
BF16 = jnp.bfloat16

D_MODEL = 1024
DEPTH = 2
GRID_W = 64
GLA_HEADS, GLA_DK, GLA_DV, GLA_LOW_RANK = 4, 64, 128, 16
GLA_NORMALIZER = 16.0
RET_HEADS, RET_DK, RET_DV = 4, 128, 128
ROPE_BASE = 10000.0
HG_HEADS, HG_DK, HG_DV = 8, 128, 128
N_EXPERTS, TOP_K, D_FF = 32, 4, 1024
SWIGLU_ALPHA, SWIGLU_LIMIT = 1.702, 7.0
EPS = 1e-6

LANES = 128
SCAN_CHUNK = 128
SCAN_ROWS = 1024
TOKEN_TILE = 256
BIG_TOKEN_TILE = 512
MOE_TILE = 384
VMEM_LIMIT = 56 * 1024 * 1024

_GQ, _GK, _GV, _GG, _RQ, _RK, _RV, _RG, _EVEN_MAIN = 0, 256, 512, 1024, 1536, 2048, 2560, 3072, 3584
_HQ, _HFF, _HFB, _HI, _HG, _ODD_MAIN = 0, 1024, 2048, 3072, 4096, 5120


def _dot(a, b):
    return jnp.dot(a, b, preferred_element_type=F32)


def _dot_nt(a, b):
    return lax.dot_general(a, b, (((1,), (1,)), ((), ())), preferred_element_type=F32)


def _rms(x, gain=None):
    y = x * lax.rsqrt(jnp.mean(x * x, axis=-1, keepdims=True) + EPS)
    if gain is not None:
        y = y * gain
    return y


def _sigmoid(x):
    return 1.0 / (1.0 + jnp.exp(-x))


def _silu(x):
    return x * _sigmoid(x)


def _cparams(sem, vmem=VMEM_LIMIT):
    return pltpu.CompilerParams(dimension_semantics=sem, vmem_limit_bytes=vmem)


def _resident(shape):
    nd = len(shape)
    return pl.BlockSpec(shape, lambda *_: (0,) * nd, pipeline_mode=pl.Buffered(1))


def _ada_kernel(c_ref, w_ref, b_ref, o_ref):
    o_ref[0] = _dot(_silu(c_ref[...]), w_ref[0]) + b_ref[0]


def _ada_mods(cond8, ada_w, ada_b):
    tn = 1536
    return pl.pallas_call(
        _ada_kernel,
        grid=(DEPTH, 6 * D_MODEL // tn),
        in_specs=[
            pl.BlockSpec((8, D_MODEL), lambda l, j: (0, 0)),
            pl.BlockSpec((1, D_MODEL, tn), lambda l, j: (l, 0, j)),
            pl.BlockSpec((1, 1, tn), lambda l, j: (l, 0, j)),
        ],
        out_specs=pl.BlockSpec((1, 8, tn), lambda l, j: (l, 0, j)),
        out_shape=jax.ShapeDtypeStruct((DEPTH, 8, 6 * D_MODEL), F32),
        compiler_params=_cparams(("arbitrary", "arbitrary")),
        name="ada_mods",
    )(cond8, ada_w, ada_b.reshape(DEPTH, 1, 6 * D_MODEL))


ROW_UNITS = D_MODEL // LANES


def _rows_from_units(ref, n_rows):
    return jnp.concatenate([ref[pl.ds(c, n_rows, stride=ROW_UNITS), :] for c in range(ROW_UNITS)], axis=1)


def _rows_to_units(ref, val):
    n_rows = val.shape[0]
    for c in range(ROW_UNITS):
        ref[pl.ds(c, n_rows, stride=ROW_UNITS), :] = val[:, c * LANES:(c + 1) * LANES]


def _moe_combine(y_refs, wt_ref):
    w = wt_ref[...]
    terms = [w[:, k:k + 1] * _rows_from_units(y_refs[k], w.shape[0]) for k in range(TOP_K)]
    return (terms[0] + terms[1]) + (terms[2] + terms[3])


def _in_proj_kernel(*refs, first, even, nb_ctx):
    it = iter(refs)
    if first:
        xa_ref, xb_ref = next(it), next(it)
    else:
        xp_ref = next(it)
        y_refs = [next(it) for _ in range(TOP_K)]
        wt_ref, modp_ref = next(it), next(it)
    gain_ref, mod_ref, w_ref = next(it), next(it), next(it)
    if even:
        wlr_ref, gkw_ref, gkb_ref = next(it), next(it), next(it)
    proj_ref = next(it)
    if even:
        g_ref = next(it)
    xcur_ref = next(it)

    D = D_MODEL
    i = pl.program_id(0)
    if first:
        x = jnp.where(i < nb_ctx, xa_ref[...], xb_ref[...])
    else:
        x = xp_ref[...] + modp_ref[0][:, 5 * D:6 * D] * _moe_combine(y_refs, wt_ref)
    xcur_ref[...] = x
    m = mod_ref[0]
    hn = _rms(x, gain_ref[...]) * (1.0 + m[:, D:2 * D]) + m[:, 0:D]
    proj_ref[...] = _dot(hn, w_ref[...])
    if even:
        z = _dot(_dot(hn, wlr_ref[...]), gkw_ref[...]) + gkb_ref[...]
        g_ref[...] = (jnp.minimum(z, 0.0) - jnp.log(1.0 + jnp.exp(-jnp.abs(z)))) * (1.0 / GLA_NORMALIZER)


def _cond_row(i, nb_ctx, tiles_per_lat_seq):
    return jnp.where(i < nb_ctx, 0, 1 + (i - nb_ctx) // tiles_per_lat_seq)


def _in_proj(layer, n_ctx, n_tok, lat_seq, mods, gain, w_main, *, x_parts=None, x_prev=None, y4=None, even_extra=None):
    first = x_parts is not None
    even = even_extra is not None
    tm = BIG_TOKEN_TILE if first else TOKEN_TILE
    nb = n_tok // tm
    nb_ctx = n_ctx // tm
    tpl = lat_seq // tm
    np_cols = w_main.shape[1]
    D = D_MODEL

    def mod_map(l):
        return lambda i: (l * 8 + _cond_row(i, nb_ctx, tpl), 0, 0)

    row = lambda i: (i, 0)
    args, specs = [], []
    if first:
        xa, xb = x_parts
        args += [xa, xb]
        specs += [pl.BlockSpec((tm, D), lambda i: (jnp.minimum(i, nb_ctx - 1), 0)),
                  pl.BlockSpec((tm, D), lambda i: (jnp.maximum(i - nb_ctx, 0), 0))]
    else:
        y4, wts = y4
        args += [x_prev] + [y4] * TOP_K + [wts, mods]
        specs += [pl.BlockSpec((tm, D), row)]
        specs += [pl.BlockSpec((tm * ROW_UNITS, LANES), (lambda k: (lambda i: (k * nb + i, 0)))(k))
                  for k in range(TOP_K)]
        specs += [pl.BlockSpec((tm, LANES), row), pl.BlockSpec((1, 1, 6 * D), mod_map(layer - 1))]
    args += [gain.reshape(1, D), mods, w_main]
    specs += [_resident((1, D)), pl.BlockSpec((1, 1, 6 * D), mod_map(layer)), _resident((D, np_cols))]
    out_shapes = [jax.ShapeDtypeStruct((n_tok, np_cols), F32)]
    out_specs = [pl.BlockSpec((tm, np_cols), row)]
    if even:
        w_lr, gkw, gkb = even_extra
        args += [w_lr, gkw, gkb]
        specs += [_resident(w_lr.shape), _resident(gkw.shape), _resident(gkb.shape)]
        out_shapes.append(jax.ShapeDtypeStruct((n_tok, 2 * GLA_HEADS * GLA_DK), F32))
        out_specs.append(pl.BlockSpec((tm, 2 * GLA_HEADS * GLA_DK), row))
    out_shapes.append(jax.ShapeDtypeStruct((n_tok, D), F32))
    out_specs.append(pl.BlockSpec((tm, D), row))
    return pl.pallas_call(
        functools.partial(_in_proj_kernel, first=first, even=even, nb_ctx=nb_ctx),
        grid=(nb,),
        in_specs=specs,
        out_specs=out_specs,
        out_shape=out_shapes,
        compiler_params=_cparams(("arbitrary",)),
        name=f"in_proj_l{layer}",
    )(*args)


def _tri(c, lower):
    r = lax.broadcasted_iota(jnp.int32, (c, c), 0)
    s = lax.broadcasted_iota(jnp.int32, (c, c), 1)
    return (r >= s) if lower else (r <= s)


def _cumsum_mm(tri_bf16, g):
    g_hi = g.astype(BF16)
    g_lo = (g - g_hi.astype(F32)).astype(BF16)
    r = _dot(tri_bf16, jnp.concatenate([g_hi, g_lo], axis=1))
    w = g.shape[1]
    return r[:, :w] + r[:, w:]


def _gated_scan_kernel(*refs, variant, T, nseq, has_s0, want_final, hpb):
    C = SCAN_CHUNK
    ncs = T // C
    nc = nseq * ncs
    it = iter(refs)
    if variant == "gla":
        q_ref, k_ref, v_ref, og_ref, gf_ref, gb_ref, gain_ref = (next(it) for _ in range(7))
    else:
        q_ref, ff_ref, fb_ref, v_ref, og_ref, lbl_ref, gain_ref = (next(it) for _ in range(7))
    s0_ref = next(it) if has_s0 else None
    y_ref = next(it)
    sfin_ref = next(it) if want_final else None
    qi_scr, o_scr, u_scr, dec_scr, sin_scr = (next(it) for _ in range(5))

    lane = lax.broadcasted_iota(jnp.int32, (1, LANES), 1)
    if hpb == 2:
        masks = [lane < GLA_DK, lane >= GLA_DK]
    else:
        masks = [None]
    tri_l, tri_u = _tri(C, True), _tri(C, False)
    tri_l16, tri_u16 = tri_l.astype(F32).astype(BF16), tri_u.astype(F32).astype(BF16)
    mid_f, mid_b = C // 2 - 1, C // 2

    if variant == "hgrn":
        lgs = [lbl_ref[l] for l in range(DEPTH)]
        mx = functools.reduce(jnp.maximum, lgs)
        es = [jnp.exp(l - mx) for l in lgs]
        tot = functools.reduce(lambda a, b: a + b, es)
        ps = [e / tot for e in es]
        layer = DEPTH - 1
        lb = functools.reduce(lambda a, b: a + b, ps[:layer + 1]) - ps[0]
        lb_f, lb_b = lb[0:1], lb[1:2]

    for n in range(nc):
        rows = pl.ds(n * C, C)
        if variant == "gla":
            q = q_ref[rows, :] * (GLA_DK ** -0.5)
            kf = kb = k_ref[rows, :]
            gf, gb = gf_ref[rows, :], gb_ref[rows, :]
        else:
            q = _silu(q_ref[rows, :])
            f_f = lb_f + (1.0 - lb_f) * _sigmoid(ff_ref[rows, :])
            f_b = lb_b + (1.0 - lb_b) * _sigmoid(fb_ref[rows, :])
            kf, kb = 1.0 - f_f, 1.0 - f_b
            gf, gb = jnp.log(f_f), jnp.log(f_b)
        bf = _cumsum_mm(tri_l16, gf)
        bb = _cumsum_mm(tri_u16, gb)
        bf_mid, bf_end = bf[mid_f:mid_f + 1], bf[C - 1:C]
        bb_mid, bb_end = bb[mid_b:mid_b + 1], bb[0:1]
        qd_f, kd_f = q * jnp.exp(bf - bf_mid), kf * jnp.exp(bf_mid - bf)
        qd_b, kd_b = q * jnp.exp(bb - bb_mid), kb * jnp.exp(bb_mid - bb)
        kend = jnp.concatenate([kf * jnp.exp(bf_end - bf), kb * jnp.exp(bb_end - bb)], axis=1)
        qi_scr[rows, :] = jnp.concatenate([q * jnp.exp(bf), q * jnp.exp(bb)], axis=1)
        dec_scr[n] = jnp.broadcast_to(jnp.concatenate([jnp.exp(bf_end), jnp.exp(bb_end)], axis=1), (8, 2 * LANES))
        for h in range(hpb):
            v = v_ref[rows, h * LANES:(h + 1) * LANES]
            if masks[h] is None:
                qf_h, qb_h, kend_h = qd_f, qd_b, kend
            else:
                qf_h, qb_h = jnp.where(masks[h], qd_f, 0.0), jnp.where(masks[h], qd_b, 0.0)
                kend_h = jnp.where(jnp.concatenate([masks[h], masks[h]], axis=1), kend, 0.0)
            s = jnp.where(tri_l, _dot_nt(qf_h, kd_f), 0.0) + jnp.where(tri_u, _dot_nt(qb_h, kd_b), 0.0)
            o_scr[rows, h * LANES:(h + 1) * LANES] = _dot(s, v)
            u_scr[h, n] = _dot(v.T, kend_h)

    for sq in range(nseq):
        for h in range(hpb):
            if has_s0:
                s_f, s_b = s0_ref[sq, 0, 0].T, s0_ref[sq, 1, 0].T
                if masks[h] is not None:
                    s_f, s_b = jnp.where(masks[h], s_f, 0.0), jnp.where(masks[h], s_b, 0.0)
            else:
                s_f = s_b = jnp.zeros((LANES, LANES), F32)
            for n in range(sq * ncs, (sq + 1) * ncs):
                sin_scr[h, n, :, 0:LANES] = s_f
                s_f = s_f * dec_scr[n, 0:1, 0:LANES] + u_scr[h, n, :, 0:LANES]
            for n in reversed(range(sq * ncs, (sq + 1) * ncs)):
                sin_scr[h, n, :, LANES:2 * LANES] = s_b
                s_b = s_b * dec_scr[n, 0:1, LANES:2 * LANES] + u_scr[h, n, :, LANES:2 * LANES]
            if want_final:
                if h == 0:
                    fin_f, fin_b = s_f, s_b
                else:
                    fin_f, fin_b = fin_f + s_f, fin_b + s_b
        if want_final:
            sfin_ref[sq, 0, 0] = fin_f.T
            sfin_ref[sq, 1, 0] = fin_b.T

    gain = gain_ref[...]
    for n in range(nc):
        rows = pl.ds(n * C, C)
        qi = qi_scr[rows, :]
        for h in range(hpb):
            cols = slice(h * LANES, (h + 1) * LANES)
            o = o_scr[rows, cols] + _dot_nt(qi, sin_scr[h, n])
            y_ref[rows, cols] = (_rms(o, gain) * _silu(og_ref[rows, cols])).astype(y_ref.dtype)


def _scan_scratch(T, hpb):
    nc = T // SCAN_CHUNK
    return [
        pltpu.VMEM((T, 2 * LANES), F32),
        pltpu.VMEM((T, hpb * LANES), F32),
        pltpu.VMEM((hpb, nc, LANES, 2 * LANES), F32),
        pltpu.VMEM((nc, 8, 2 * LANES), F32),
        pltpu.VMEM((hpb, nc, LANES, 2 * LANES), F32),
    ]


def _seqs_per_step(B, T):
    nseq = max(1, min(B, SCAN_ROWS // T))
    assert B % nseq == 0
    return nseq


def _gla_scan(proj, gdec, gain, s0, row0, B, T, want_final):
    nseq = _seqs_per_step(B, T)
    R = nseq * T
    rb0 = row0 // R
    assert row0 % R == 0
    has_s0 = s0 is not None

    def col(base, width=LANES):
        return lambda b, j: (rb0 + b, base // width + j)

    args = [proj, proj, proj, proj, gdec, gdec, gain.reshape(1, GLA_DV)]
    specs = [pl.BlockSpec((R, LANES), col(_GQ)), pl.BlockSpec((R, LANES), col(_GK)),
             pl.BlockSpec((R, 2 * LANES), col(_GV, 2 * LANES)), pl.BlockSpec((R, 2 * LANES), col(_GG, 2 * LANES)),
             pl.BlockSpec((R, LANES), col(0)), pl.BlockSpec((R, LANES), col(GLA_HEADS * GLA_DK)),
             pl.BlockSpec((1, GLA_DV), lambda b, j: (0, 0))]
    st_spec = pl.BlockSpec((nseq, 2, 1, LANES, LANES), lambda b, j: (b, 0, j, 0, 0))
    if has_s0:
        args.append(s0.reshape(B, 2, GLA_HEADS // 2, 2 * GLA_DK, GLA_DV))
        specs.append(st_spec)
    out_shapes = [jax.ShapeDtypeStruct((B * T, GLA_HEADS * GLA_DV), BF16)]
    out_specs = [pl.BlockSpec((R, 2 * LANES), lambda b, j: (b, j))]
    if want_final:
        out_shapes.append(jax.ShapeDtypeStruct((B, 2, GLA_HEADS // 2, 2 * GLA_DK, GLA_DV), F32))
        out_specs.append(st_spec)
    kern = functools.partial(_gated_scan_kernel, variant="gla", T=T, nseq=nseq, has_s0=has_s0,
                             want_final=want_final, hpb=2)
    return pl.pallas_call(
        kern, grid=(B // nseq, GLA_HEADS // 2), in_specs=specs, out_specs=out_specs, out_shape=out_shapes,
        scratch_shapes=_scan_scratch(R, 2),
        compiler_params=_cparams(("arbitrary", "arbitrary")), name=f"gla_scan_T{T}",
    )(*args)


def _hgrn_scan(proj, lb_logits, gain, s0, row0, B, T, want_final):
    nseq = _seqs_per_step(B, T)
    R = nseq * T
    rb0 = row0 // R
    assert row0 % R == 0
    has_s0 = s0 is not None

    def col(base):
        return lambda b, j: (rb0 + b, base // LANES + j)

    args = [proj, proj, proj, proj, proj, lb_logits, gain.reshape(1, HG_DV)]
    specs = [pl.BlockSpec((R, LANES), col(_HQ)), pl.BlockSpec((R, LANES), col(_HFF)),
             pl.BlockSpec((R, LANES), col(_HFB)), pl.BlockSpec((R, LANES), col(_HI)),
             pl.BlockSpec((R, LANES), col(_HG)),
             pl.BlockSpec((DEPTH, 2, LANES), lambda b, j: (0, 0, j)),
             pl.BlockSpec((1, HG_DV), lambda b, j: (0, 0))]
    st_spec = pl.BlockSpec((nseq, 2, 1, LANES, LANES), lambda b, j: (b, 0, j, 0, 0))
    if has_s0:
        args.append(s0.reshape(B, 2, HG_HEADS, HG_DK, HG_DV))
        specs.append(st_spec)
    out_shapes = [jax.ShapeDtypeStruct((B * T, HG_HEADS * HG_DV), BF16)]
    out_specs = [pl.BlockSpec((R, LANES), lambda b, j: (b, j))]
    if want_final:
        out_shapes.append(jax.ShapeDtypeStruct((B, 2, HG_HEADS, HG_DK, HG_DV), F32))
        out_specs.append(st_spec)
    kern = functools.partial(_gated_scan_kernel, variant="hgrn", T=T, nseq=nseq, has_s0=has_s0,
                             want_final=want_final, hpb=1)
    return pl.pallas_call(
        kern, grid=(B // nseq, HG_HEADS), in_specs=specs, out_specs=out_specs, out_shape=out_shapes,
        scratch_shapes=_scan_scratch(R, 1),
        compiler_params=_cparams(("arbitrary", "arbitrary")), name=f"hgrn_scan_T{T}",
    )(*args)


def _ret_scan_kernel(*refs, T, nseq, has_s0, want_final, rope):
    C = SCAN_CHUNK
    ncs = T // C
    nc = nseq * ncs
    it = iter(refs)
    q_ref, k_ref, v_ref, og_ref, dexp_ref = (next(it) for _ in range(5))
    if rope:
        cos_ref, sin_ref = next(it), next(it)
    s0_ref = next(it) if has_s0 else None
    y_ref = next(it)
    sfin_ref = next(it) if want_final else None
    qi_scr, o_scr, u_scr, sin_scr = (next(it) for _ in range(4))

    lg = jnp.log1p(-jnp.exp2(-dexp_ref[0]))
    lg_f, lg_b = lg[0:1], lg[1:2]
    r = lax.broadcasted_iota(jnp.int32, (C, C), 0)
    s = lax.broadcasted_iota(jnp.int32, (C, C), 1)
    dist = (r - s).astype(F32)
    dmask = (jnp.where(r >= s, jnp.exp(jnp.maximum(dist, 0.0) * lg_f[:, 0:1]), 0.0)
             + jnp.where(r <= s, jnp.exp(jnp.maximum(-dist, 0.0) * lg_b[:, 0:1]), 0.0))
    pos = lax.broadcasted_iota(jnp.int32, (C, LANES), 0).astype(F32)
    xi = jnp.concatenate([jnp.exp((pos + 1.0) * lg_f), jnp.exp((C - pos) * lg_b)], axis=1)
    zeta = jnp.concatenate([jnp.exp((C - 1.0 - pos) * lg_f), jnp.exp(pos * lg_b)], axis=1)
    d_f, d_b = jnp.exp(C * lg_f), jnp.exp(C * lg_b)

    def rot(x, seq_rows):
        if not rope:
            return x
        return x * cos_ref[seq_rows, :] + pltpu.roll(x, RET_DK // 2, axis=1) * sin_ref[seq_rows, :]

    for n in range(nc):
        rows = pl.ds(n * C, C)
        seq_rows = pl.ds((n % ncs) * C, C)
        q = rot(q_ref[rows, :], seq_rows)
        k = rot(k_ref[rows, :] * (RET_DK ** -0.5), seq_rows)
        v = v_ref[rows, :]
        o_scr[rows, :] = _dot(_dot_nt(q, k) * dmask, v)
        qi_scr[rows, :] = jnp.concatenate([q, q], axis=1) * xi
        u_scr[n] = _dot(v.T, jnp.concatenate([k, k], axis=1) * zeta)

    for sq in range(nseq):
        if has_s0:
            s_f, s_b = s0_ref[sq, 0, 0].T, s0_ref[sq, 1, 0].T
        else:
            s_f = s_b = jnp.zeros((LANES, LANES), F32)
        for n in range(sq * ncs, (sq + 1) * ncs):
            sin_scr[n, :, 0:LANES] = s_f
            s_f = s_f * d_f + u_scr[n, :, 0:LANES]
        for n in reversed(range(sq * ncs, (sq + 1) * ncs)):
            sin_scr[n, :, LANES:2 * LANES] = s_b
            s_b = s_b * d_b + u_scr[n, :, LANES:2 * LANES]
        if want_final:
            sfin_ref[sq, 0, 0] = s_f.T
            sfin_ref[sq, 1, 0] = s_b.T

    for n in range(nc):
        rows = pl.ds(n * C, C)
        o = o_scr[rows, :] + _dot_nt(qi_scr[rows, :], sin_scr[n])
        y_ref[rows, :] = (_rms(o) * _silu(og_ref[rows, :])).astype(y_ref.dtype)


def _ret_scan(proj, dexp, rope_tabs, s0, row0, B, T, want_final):
    nseq = _seqs_per_step(B, T)
    R = nseq * T
    rb0 = row0 // R
    assert row0 % R == 0
    nc = R // SCAN_CHUNK
    has_s0 = s0 is not None
    rope = rope_tabs is not None

    def col(base):
        return lambda b, j: (rb0 + b, base // LANES + j)

    args = [proj, proj, proj, proj, dexp]
    specs = [pl.BlockSpec((R, LANES), col(_RQ)), pl.BlockSpec((R, LANES), col(_RK)),
             pl.BlockSpec((R, LANES), col(_RV)), pl.BlockSpec((R, LANES), col(_RG)),
             pl.BlockSpec((1, 2, LANES), lambda b, j: (j, 0, 0))]
    if rope:
        args += list(rope_tabs)
        specs += [pl.BlockSpec((T, LANES), lambda b, j: (0, 0))] * 2
    st_spec = pl.BlockSpec((nseq, 2, 1, LANES, LANES), lambda b, j: (b, 0, j, 0, 0))
    if has_s0:
        args.append(s0.reshape(B, 2, RET_HEADS, RET_DK, RET_DV))
        specs.append(st_spec)
    out_shapes = [jax.ShapeDtypeStruct((B * T, RET_HEADS * RET_DV), BF16)]
    out_specs = [pl.BlockSpec((R, LANES), lambda b, j: (b, j))]
    if want_final:
        out_shapes.append(jax.ShapeDtypeStruct((B, 2, RET_HEADS, RET_DK, RET_DV), F32))
        out_specs.append(st_spec)
    kern = functools.partial(_ret_scan_kernel, T=T, nseq=nseq, has_s0=has_s0, want_final=want_final, rope=rope)
    scratch = [pltpu.VMEM((R, 2 * LANES), F32), pltpu.VMEM((R, LANES), F32),
               pltpu.VMEM((nc, LANES, 2 * LANES), F32), pltpu.VMEM((nc, LANES, 2 * LANES), F32)]
    return pl.pallas_call(
        kern, grid=(B // nseq, RET_HEADS), in_specs=specs, out_specs=out_specs, out_shape=out_shapes,
        scratch_shapes=scratch,
        compiler_params=_cparams(("arbitrary", "arbitrary")), name=f"ret_scan_T{T}",
    )(*args)


def _out_proj_kernel(*refs, n_mix, nb_ctx):
    it = iter(refs)
    y_refs = [(next(it), next(it)) for _ in range(n_mix)]
    wo_ref, x_ref, mod_ref, gain_ref, rw_ref, rb_ref = (next(it) for _ in range(6))
    xnew_ref, hn_ref, idx_ref, wt_ref, rank_ref, cnt_ref = (next(it) for _ in range(6))
    cnt_scr = next(it)
    D = D_MODEL
    is_ctx = pl.program_id(0) < nb_ctx
    mix = None
    r0 = 0
    for ya_ref, yb_ref in y_refs:
        w = ya_ref.shape[1]
        part = _dot(jnp.where(is_ctx, ya_ref[...], yb_ref[...]), wo_ref[r0:r0 + w, :])
        mix = part if mix is None else mix + part
        r0 += w
    m = mod_ref[0]
    xn = x_ref[...] + m[:, 2 * D:3 * D] * mix
    xnew_ref[...] = xn
    hn = _rms(xn, gain_ref[...]) * (1.0 + m[:, 4 * D:5 * D]) + m[:, 3 * D:4 * D]
    hn_hi = hn.astype(BF16)
    _rows_to_units(hn_ref, hn)
    hn_lo = (hn - hn_hi.astype(F32)).astype(BF16)
    rw = rw_ref[...]
    rw_hi = rw.astype(BF16)
    rw_lo = (rw - rw_hi.astype(F32)).astype(BF16)
    logits = (_dot(hn_hi, rw_hi) + (_dot(hn_lo, rw_hi) + _dot(hn_hi, rw_lo))) + rb_ref[...]
    lane = lax.broadcasted_iota(jnp.int32, logits.shape, 1).astype(F32)
    vals, idxs = [], []
    cur = logits
    for _ in range(TOP_K):
        mx = jnp.max(cur, axis=-1, keepdims=True)
        ik = jnp.min(jnp.where(cur == mx, lane, float(LANES)), axis=-1, keepdims=True)
        vals.append(mx)
        idxs.append(ik)
        cur = jnp.where(lane == ik, -jnp.inf, cur)
    es = [jnp.exp(v - vals[0]) for v in vals]
    tot = (es[0] + es[1]) + (es[2] + es[3])
    idx_out = jnp.zeros(logits.shape, F32)
    wt_out = jnp.zeros(logits.shape, F32)
    for k in range(TOP_K):
        idx_out = jnp.where(lane == float(k), idxs[k], idx_out)
        wt_out = jnp.where(lane == float(k), es[k] / tot, wt_out)
    idx_ref[...] = idx_out.astype(jnp.int32)
    wt_ref[...] = wt_out

    @pl.when(pl.program_id(0) == 0)
    def _():
        cnt_scr[...] = jnp.zeros(cnt_scr.shape, F32)

    tm = logits.shape[0]
    hits = [lane == idxs[k] for k in range(TOP_K)]
    sel = jnp.zeros(logits.shape, F32)
    for k in range(TOP_K):
        sel = sel + jnp.where(hits[k], 1.0, 0.0)
    rr = lax.broadcasted_iota(jnp.int32, (tm, tm), 0)
    cc = lax.broadcasted_iota(jnp.int32, (tm, tm), 1)
    before = jnp.where(rr > cc, 1.0, 0.0).astype(BF16)
    rank_all = cnt_scr[0:1, :] + _dot(before, sel.astype(BF16))
    rank_out = jnp.zeros(logits.shape, F32)
    for k in range(TOP_K):
        rk = jnp.sum(jnp.where(hits[k], rank_all, 0.0), axis=-1, keepdims=True)
        rank_out = jnp.where(lane == float(k), rk, rank_out)
    rank_ref[...] = rank_out.astype(jnp.int32)
    total = cnt_scr[...] + jnp.sum(sel, axis=0, keepdims=True)
    cnt_scr[...] = total
    cnt_ref[...] = total.astype(jnp.int32)


def _out_proj(layer, n_ctx, n_tok, lat_seq, ys, w_out, x_cur, mods, gain, rw_pad, rb_pad):
    tm = BIG_TOKEN_TILE
    nb, nb_ctx, tpl = n_tok // tm, n_ctx // tm, lat_seq // tm
    D = D_MODEL
    row = lambda i: (i, 0)
    specs = []
    for ya, _ in ys:
        specs += [pl.BlockSpec((tm, ya.shape[1]), lambda i: (jnp.minimum(i, nb_ctx - 1), 0)),
                  pl.BlockSpec((tm, ya.shape[1]), lambda i: (jnp.maximum(i - nb_ctx, 0), 0))]
    specs += [_resident((D, D)), pl.BlockSpec((tm, D), row),
              pl.BlockSpec((1, 1, 6 * D), lambda i: (layer * 8 + _cond_row(i, nb_ctx, tpl), 0, 0)),
              _resident((1, D)), _resident((D, LANES)), _resident((1, LANES))]
    return pl.pallas_call(
        functools.partial(_out_proj_kernel, n_mix=len(ys), nb_ctx=nb_ctx),
        grid=(nb,),
        in_specs=specs,
        out_specs=[pl.BlockSpec((tm, D), row), pl.BlockSpec((tm * ROW_UNITS, LANES), row),
                   pl.BlockSpec((tm, LANES), row), pl.BlockSpec((tm, LANES), row),
                   pl.BlockSpec((tm, LANES), row), pl.BlockSpec((8, LANES), lambda i: (0, 0))],
        out_shape=[jax.ShapeDtypeStruct((n_tok, D), F32), jax.ShapeDtypeStruct((n_tok * ROW_UNITS, LANES), F32),
                   jax.ShapeDtypeStruct((n_tok, LANES), jnp.int32), jax.ShapeDtypeStruct((n_tok, LANES), F32),
                   jax.ShapeDtypeStruct((n_tok, LANES), jnp.int32), jax.ShapeDtypeStruct((8, LANES), jnp.int32)],
        scratch_shapes=[pltpu.VMEM((8, LANES), F32)],
        compiler_params=_cparams(("arbitrary",)),
        name=f"out_proj_l{layer}",
    )(*[y for pair in ys for y in pair], w_out, x_cur, mods, gain.reshape(1, D), rw_pad, rb_pad)


W1_SPLIT = 4
W2_SPLIT = 2


def _moe_kernel(seg_ref, nt_ref, code_hbm, x_hbm, *refs, n_tok):
    w1_refs = refs[:W1_SPLIT]
    b1_ref = refs[W1_SPLIT]
    w2_refs = refs[W1_SPLIT + 1:W1_SPLIT + 1 + W2_SPLIT]
    (b2_ref, y_hbm, w1_scr, b1_scr, w2_scr, act_scr, xbuf, obuf, code_smem, csem, gsem,
     ssem) = refs[W1_SPLIT + 1 + W2_SPLIT:]
    tm = MOE_TILE
    U = ROW_UNITS
    n_code_tiles = code_hbm.shape[0] // tm
    e = pl.program_id(0)
    nt = nt_ref[e]
    g_first = seg_ref[e] // tm

    GATHER, SCATTER = 0, 1

    def code_copy(kind, t, p):
        first = pl.multiple_of(jnp.clip(t, 0, n_code_tiles - 1) * tm, tm)
        return pltpu.make_async_copy(code_hbm.at[pl.ds(first, tm)], code_smem.at[kind, p], csem.at[kind, p])

    def gather_start(b, p):
        for r in range(tm):
            tok = jnp.bitwise_and(code_smem[GATHER, p, r], n_tok - 1)
            pltpu.make_async_copy(x_hbm.at[pl.ds(pl.multiple_of(tok * U, U), U)],
                                  xbuf.at[b, pl.ds(r * U, U)], gsem.at[b]).start()

    def gather_wait(b):
        pltpu.make_async_copy(x_hbm.at[pl.ds(0, tm * U)], xbuf.at[b], gsem.at[b]).wait()

    def scatter_start(b, p):
        for r in range(tm):
            dst = code_smem[SCATTER, p, r]
            pltpu.make_async_copy(obuf.at[b, pl.ds(r * U, U)],
                                  y_hbm.at[pl.ds(pl.multiple_of(dst * U, U), U)], ssem.at[b]).start()

    def scatter_wait(b):
        pltpu.make_async_copy(obuf.at[b], y_hbm.at[pl.ds(0, tm * U)], ssem.at[b]).wait()

    @pl.when(e == 0)
    def _():
        obuf[...] = jnp.zeros(obuf.shape, obuf.dtype)
        for h in range(2):
            cp = pltpu.make_async_copy(obuf.at[0], y_hbm.at[pl.ds((TOP_K * n_tok + h * tm) * U, tm * U)], ssem.at[0])
            cp.start()
            cp.wait()
        code_copy(GATHER, 0, 0).start()
        code_copy(GATHER, 1, 1).start()
        code_copy(SCATTER, n_code_tiles - 1, 1).start()
        code_copy(SCATTER, 0, 0).start()
        code_copy(GATHER, 0, 0).wait()
        gather_start(0, 0)

    @pl.when(nt > 0)
    def _():
        grp = 2 * LANES
        rr = lax.broadcasted_iota(jnp.int32, (grp, grp), 0)
        cc = lax.broadcasted_iota(jnp.int32, (grp, grp), 1)
        src = jnp.where(cc < LANES, 2 * cc, 2 * (cc - LANES) + 1)
        perm = jnp.where(rr == src, 1.0, 0.0).astype(BF16)
        cw = 2 * D_FF // W1_SPLIT
        for c, w_ref in enumerate(w1_refs):
            for j in range(cw // grp):
                cols = slice(j * grp, (j + 1) * grp)
                w1_scr[:, c * cw + j * grp:c * cw + (j + 1) * grp] = _dot(w_ref[0, 0, :, cols].astype(BF16),
                                                                         perm).astype(BF16)
        bias = jnp.broadcast_to(b1_ref[0, 0], (8, 2 * D_FF))
        for j in range(2 * D_FF // grp):
            cols = slice(j * grp, (j + 1) * grp)
            rest = bias[:, cols]
            acc = jnp.zeros((8, grp), F32)
            for _ in range(3):
                term = rest.astype(BF16)
                acc = acc + _dot(term, perm)
                rest = rest - term.astype(F32)
            b1_scr[:, cols] = acc
        rh = D_FF // W2_SPLIT
        for c, w_ref in enumerate(w2_refs):
            w2_scr[c * rh:(c + 1) * rh, :] = w_ref[0, 0].astype(BF16)
        blk = 4 * LANES

        def tile(g, b):
            o = 1 - b

            @pl.when(g >= 0)
            def _():
                code_copy(GATHER, g + 1, o).wait()
                code_copy(SCATTER, g - 1, o).wait()
                gather_start(o, o)
                scatter_start(o, o)

            gather_wait(b)
            x = _rows_from_units(xbuf.at[b], tm).astype(BF16)
            for c in range(2 * D_FF // blk):
                cols = slice(c * blk, (c + 1) * blk)
                hid = _dot(x, w1_scr[:, cols]) + b1_scr[0:1, cols]
                acts = []
                for j in range(blk // grp):
                    glu = jnp.minimum(hid[:, j * grp:j * grp + LANES], SWIGLU_LIMIT)
                    lin = jnp.clip(hid[:, j * grp + LANES:(j + 1) * grp], -SWIGLU_LIMIT, SWIGLU_LIMIT)
                    acts.append(glu * _sigmoid(SWIGLU_ALPHA * glu) * (lin + 1.0))
                act_scr[:, c * (blk // 2):(c + 1) * (blk // 2)] = jnp.concatenate(acts, axis=1).astype(BF16)
            y = _dot(act_scr[...], w2_scr[...]) + b2_ref[0, 0]

            @pl.when(g >= 1)
            def _():
                scatter_wait(b)

            _rows_to_units(obuf.at[b], y)
            code_copy(GATHER, g + 2, b).start()
            code_copy(SCATTER, g + 1, o).start()

        g_end = g_first + nt

        def tile_pair(m, carry):
            for b in range(2):
                g = 2 * m + b

                @pl.when(jnp.logical_and(g >= g_first, g < g_end))
                def _():
                    tile(g, b)

            return carry

        lax.fori_loop(g_first // 2, (g_end + 1) // 2, tile_pair, 0)

    @pl.when(e == N_EXPERTS - 1)
    def _():
        g_end = g_first + nt
        last = lax.rem(g_end + 1, 2)
        code_copy(SCATTER, g_end - 1, last).wait()
        scatter_start(last, last)
        scatter_wait(last)
        scatter_wait(1 - last)
        gather_wait(1 - last)
        code_copy(GATHER, g_end + 1, last).wait()
        code_copy(SCATTER, g_end, 1 - last).wait()


def _moe_experts(layer, n_tok, seg_start, tiles_per_e, code, x_units, w1, b1, w2, b2):
    tm = MOE_TILE
    D, F2, U = D_MODEL, 2 * D_FF, ROW_UNITS
    cw, rh = F2 // W1_SPLIT, D_FF // W2_SPLIT
    w1_specs = [pl.BlockSpec((1, 1, D, cw), (lambda c: (lambda e, sg, nt: (layer, e, 0, c)))(c))
                for c in range(W1_SPLIT)]
    w2_specs = [pl.BlockSpec((1, 1, rh, D), (lambda c: (lambda e, sg, nt: (layer, e, c, 0)))(c))
                for c in range(W2_SPLIT)]
    grid_spec = pltpu.PrefetchScalarGridSpec(
        num_scalar_prefetch=2,
        grid=(N_EXPERTS,),
        in_specs=[pl.BlockSpec(memory_space=pl.ANY), pl.BlockSpec(memory_space=pl.ANY)]
        + w1_specs + [pl.BlockSpec((1, 1, 1, F2), lambda e, sg, nt: (layer, e, 0, 0))]
        + w2_specs + [pl.BlockSpec((1, 1, 1, D), lambda e, sg, nt: (layer, e, 0, 0))],
        out_specs=pl.BlockSpec(memory_space=pl.ANY),
        scratch_shapes=[pltpu.VMEM((D, F2), BF16), pltpu.VMEM((8, F2), F32), pltpu.VMEM((D_FF, D), BF16),
                        pltpu.VMEM((tm, D_FF), BF16),
                        pltpu.VMEM((2, tm * U, LANES), F32), pltpu.VMEM((2, tm * U, LANES), F32),
                        pltpu.SMEM((2, 2, tm), jnp.int32),
                        pltpu.SemaphoreType.DMA((2, 2)), pltpu.SemaphoreType.DMA((2,)), pltpu.SemaphoreType.DMA((2,))],
    )
    n_out_rows = TOP_K * n_tok + 2 * tm
    return pl.pallas_call(
        functools.partial(_moe_kernel, n_tok=n_tok),
        grid_spec=grid_spec,
        out_shape=jax.ShapeDtypeStruct((n_out_rows * U, LANES), F32),
        compiler_params=_cparams(("arbitrary",)),
        name="moe_experts",
    )(seg_start, tiles_per_e, code, x_units, *([w1] * W1_SPLIT), b1.reshape(DEPTH, N_EXPERTS, 1, F2),
      *([w2] * W2_SPLIT), b2.reshape(DEPTH, N_EXPERTS, 1, D))


def _pos_kernel(idx_ref, rank_ref, seg_ref, pos_ref):
    lane = lax.broadcasted_iota(jnp.int32, idx_ref.shape, 1)
    idx, rank, seg = idx_ref[...], rank_ref[...], seg_ref[...]
    out = jnp.zeros(idx.shape, jnp.int32)
    for k in range(TOP_K):
        hit = lane == idx[:, k:k + 1]
        seg_k = jnp.sum(jnp.where(hit, seg, 0.0), axis=-1, keepdims=True).astype(jnp.int32)
        out = jnp.where(lane == k, seg_k + rank[:, k:k + 1], out)
    pos_ref[...] = out


def _pair_slots(idx, rank, seg_start):
    n_tok = idx.shape[0]
    tm = TOKEN_TILE
    row = lambda i: (i, 0)
    return pl.pallas_call(
        _pos_kernel,
        grid=(n_tok // tm,),
        in_specs=[pl.BlockSpec((tm, LANES), row), pl.BlockSpec((tm, LANES), row),
                  pl.BlockSpec((1, LANES), lambda i: (0, 0))],
        out_specs=pl.BlockSpec((tm, LANES), row),
        out_shape=jax.ShapeDtypeStruct((n_tok, LANES), jnp.int32),
        compiler_params=_cparams(("arbitrary",)),
        name="pair_slots",
    )(idx, rank, seg_start)


def _slot_code_kernel(pos_ref, unused_hbm, code_ref, sem, *, n_tok):
    i = pl.program_id(0)
    tm = pos_ref.shape[0] // TOP_K

    @pl.when(i == 0)
    def _():
        cp = pltpu.make_async_copy(unused_hbm, code_ref, sem)
        cp.start()
        cp.wait()

    base = i * tm
    for t in range(tm):
        for k in range(TOP_K):
            code_ref[pos_ref[t * TOP_K + k]] = base + (k * n_tok + t)


def _slot_codes(pos, unused):
    n_tok = pos.shape[0]
    tm = TOKEN_TILE
    return pl.pallas_call(
        functools.partial(_slot_code_kernel, n_tok=n_tok),
        grid=(n_tok // tm,),
        in_specs=[pl.BlockSpec((tm * TOP_K,), lambda i: (i,), memory_space=pltpu.SMEM),
                  pl.BlockSpec(memory_space=pl.ANY)],
        out_specs=pl.BlockSpec(memory_space=pltpu.SMEM),
        out_shape=jax.ShapeDtypeStruct(unused.shape, jnp.int32),
        scratch_shapes=[pltpu.SemaphoreType.DMA(())],
        compiler_params=_cparams(("arbitrary",)),
        name="slot_codes",
    )(pos.reshape(-1), unused)


def _moe_layer(layer, hn_units, idx, rank, counts, w1, b1, w2, b2):
    n_tok = hn_units.shape[0] // ROW_UNITS
    assert n_tok & (n_tok - 1) == 0
    tm = MOE_TILE
    n_pairs = n_tok * TOP_K
    n_slots = ((n_pairs + N_EXPERTS * (tm - 1)) // tm + 1) * tm
    tiles_per_e = (counts + tm - 1) // tm
    seg_start = (jnp.cumsum(tiles_per_e) - tiles_per_e) * tm
    seg_f32 = jnp.pad(seg_start.astype(F32), (0, LANES - N_EXPERTS)).reshape(1, LANES)
    pos = _pair_slots(idx, rank, seg_f32)[:, :TOP_K]
    unused = n_pairs + jnp.arange(n_slots, dtype=jnp.int32) % (2 * tm)
    code = _slot_codes(pos, unused)
    return _moe_experts(layer, n_tok, seg_start, tiles_per_e, code, hn_units, w1, b1, w2, b2)


def _final_kernel(*refs, nb_ctx):
    x_ref = refs[0]
    y_refs = refs[1:1 + TOP_K]
    wt_ref, mod_ref, gain_ref, oa_ref, ob_ref = refs[1 + TOP_K:]
    D = D_MODEL
    i = pl.program_id(0)
    x = x_ref[...] + mod_ref[0][:, 5 * D:6 * D] * _moe_combine(y_refs, wt_ref)
    out = _rms(x, gain_ref[...])

    @pl.when(i < nb_ctx)
    def _():
        oa_ref[...] = out

    @pl.when(i >= nb_ctx)
    def _():
        ob_ref[...] = out


def _final(n_ctx, n_tok, lat_seq, x_new, y4, wts, mods, gain):
    tm = TOKEN_TILE
    nb, nb_ctx, tpl = n_tok // tm, n_ctx // tm, lat_seq // tm
    D = D_MODEL
    specs = [pl.BlockSpec((tm, D), lambda i: (i, 0))]
    specs += [pl.BlockSpec((tm * ROW_UNITS, LANES), (lambda k: (lambda i: (k * nb + i, 0)))(k)) for k in range(TOP_K)]
    specs += [pl.BlockSpec((tm, LANES), lambda i: (i, 0)),
              pl.BlockSpec((1, 1, 6 * D), lambda i: ((DEPTH - 1) * 8 + _cond_row(i, nb_ctx, tpl), 0, 0)),
              _resident((1, D))]
    return pl.pallas_call(
        functools.partial(_final_kernel, nb_ctx=nb_ctx),
        grid=(nb,),
        in_specs=specs,
        out_specs=[pl.BlockSpec((tm, D), lambda i: (jnp.minimum(i, nb_ctx - 1), 0)),
                   pl.BlockSpec((tm, D), lambda i: (jnp.maximum(i - nb_ctx, 0), 0))],
        out_shape=[jax.ShapeDtypeStruct((n_ctx, D), F32), jax.ShapeDtypeStruct((n_tok - n_ctx, D), F32)],
        compiler_params=_cparams(("arbitrary",)),
        name="final_norm",
    )(x_new, *([y4] * TOP_K), wts, mods, gain.reshape(1, D))


def _rope_tables(n_tokens):
    rows = n_tokens // GRID_W
    r = jnp.repeat(jnp.arange(rows, dtype=F32), GRID_W)
    col = jnp.tile(jnp.arange(GRID_W, dtype=F32), rows)
    nf = RET_DK // 4
    inv = ROPE_BASE ** (-jnp.arange(nf, dtype=F32) / nf)
    ang = jnp.concatenate([r[:, None] * inv, col[:, None] * inv], axis=-1)
    cos, sin = jnp.cos(ang), jnp.sin(ang)
    return jnp.concatenate([cos, cos], axis=-1), jnp.concatenate([-sin, sin], axis=-1)


def kernel(x_prompt, x_sample, c, state_gla, state_ret, state_hgrn, c_ctx, norm_mix, norm_ffn, ada_w, ada_b,
           w_in_even, w_out_even, gla_gk_w, gla_gk_b, gla_gain, ret_decay_exp, w_in_odd, w_out_odd,
           hgrn_lb_logits, hgrn_gain, router_w, router_b, moe_w1, moe_b1, moe_w2, moe_b2, final_norm):
    D = D_MODEL
    B_ctx, T_ctx, _ = x_prompt.shape
    B_lat, T_lat, _ = x_sample.shape
    n_ctx, n_lat = B_ctx * T_ctx, B_lat * T_lat
    n_tok = n_ctx + n_lat
    assert n_ctx % BIG_TOKEN_TILE == 0 and T_lat % BIG_TOKEN_TILE == 0 and B_lat + 1 <= 8
    assert T_ctx % SCAN_CHUNK == 0 and T_lat % SCAN_CHUNK == 0 and n_ctx % T_lat == 0

    cond8 = jnp.concatenate([c_ctx[None, :], c, jnp.zeros((8 - 1 - B_lat, D), F32)], axis=0)
    mods = _ada_mods(cond8, ada_w, ada_b).reshape(DEPTH * 8, 1, 6 * D)

    rw_pad = jnp.pad(router_w, ((0, 0), (0, 0), (0, LANES - N_EXPERTS)))
    rb_pad = jnp.pad(router_b, ((0, 0), (0, LANES - N_EXPERTS)), constant_values=-1e30)

    w_even = w_in_even[0]
    lr0 = GLA_HEADS * (2 * GLA_DK + GLA_DV)
    lr1 = lr0 + 2 * GLA_LOW_RANK
    w_main = jnp.concatenate([w_even[:, :lr0], w_even[:, lr1:]], axis=1)
    w_lr = jnp.pad(w_even[:, lr0:lr1], ((0, 0), (0, LANES - 2 * GLA_LOW_RANK)))
    nqk = GLA_HEADS * GLA_DK
    gkw = jnp.zeros((LANES, 2 * nqk), F32)
    gkw = gkw.at[0:GLA_LOW_RANK, 0:nqk].set(gla_gk_w[0, 0])
    gkw = gkw.at[GLA_LOW_RANK:2 * GLA_LOW_RANK, nqk:].set(gla_gk_w[0, 1])
    gkb = gla_gk_b[0].reshape(1, 2 * nqk)
    proj, gdec, x_cur = _in_proj(0, n_ctx, n_tok, T_lat, mods, norm_mix[0], w_main,
                                 x_parts=(x_prompt.reshape(n_ctx, D), x_sample.reshape(n_lat, D)),
                                 even_extra=(w_lr, gkw, gkb))

    y_gla_c, fin_gla = _gla_scan(proj, gdec, gla_gain[0], None, 0, B_ctx, T_ctx, True)
    (y_gla_l,) = _gla_scan(proj, gdec, gla_gain[0], state_gla[:, 0], n_ctx, B_lat, T_lat, False)
    dexp = jnp.broadcast_to(ret_decay_exp[0].T[:, :, None], (RET_HEADS, 2, LANES))
    y_ret_c, fin_ret = _ret_scan(proj, dexp, None, None, 0, B_ctx, T_ctx, True)
    (y_ret_l,) = _ret_scan(proj, dexp, _rope_tables(T_lat), state_ret[:, 0], n_ctx, B_lat, T_lat, False)

    x_new, hn, idx, wts, rank, cnt = _out_proj(0, n_ctx, n_tok, T_lat, [(y_gla_c, y_gla_l), (y_ret_c, y_ret_l)],
                                               w_out_even[0], x_cur, mods, norm_ffn[0], rw_pad[0], rb_pad[0:1])
    y4 = _moe_layer(0, hn, idx, rank, cnt[0, :N_EXPERTS], moe_w1, moe_b1, moe_w2, moe_b2)

    (proj, x_cur) = _in_proj(1, n_ctx, n_tok, T_lat, mods, norm_mix[1], w_in_odd[0], x_prev=x_new, y4=(y4, wts))
    y_h_c, fin_h = _hgrn_scan(proj, hgrn_lb_logits, hgrn_gain[0], None, 0, B_ctx, T_ctx, True)
    (y_h_l,) = _hgrn_scan(proj, hgrn_lb_logits, hgrn_gain[0], state_hgrn[:, 0], n_ctx, B_lat, T_lat, False)
    x_new, hn, idx, wts, rank, cnt = _out_proj(1, n_ctx, n_tok, T_lat, [(y_h_c, y_h_l)], w_out_odd[0], x_cur, mods,
                                               norm_ffn[1], rw_pad[1], rb_pad[1:2])
    y4 = _moe_layer(1, hn, idx, rank, cnt[0, :N_EXPERTS], moe_w1, moe_b1, moe_w2, moe_b2)

    y_ctx, y_lat = _final(n_ctx, n_tok, T_lat, x_new, y4, wts, mods, final_norm)

    new_state_gla = fin_gla.reshape(B_ctx, 1, 2, GLA_HEADS, GLA_DK, GLA_DV)
    new_state_ret = fin_ret.reshape(B_ctx, 1, 2, RET_HEADS, RET_DK, RET_DV)
    new_state_hgrn = fin_h.reshape(B_ctx, 1, 2, HG_HEADS, HG_DK, HG_DV)
    return (y_ctx.reshape(B_ctx, T_ctx, D), y_lat.reshape(B_lat, T_lat, D), new_state_gla, new_state_ret,
            new_state_hgrn)
```

```python
import functools

import jax
import jax.numpy as jnp
from jax import lax
from jax.experimental import pallas as pl
from jax.experimental.pallas import tpu as pltpu

F32 = jnp.float32
BF16 = jnp.bfloat16

D_MODEL = 1024
DEPTH = 2
GRID_W = 64
GLA_HEADS, GLA_DK, GLA_DV, GLA_LOW_RANK = 4, 64, 128, 16
GLA_NORMALIZER = 16.0
RET_HEADS, RET_DK, RET_DV = 4, 128, 128
ROPE_BASE = 10000.0
HG_HEADS, HG_DK, HG_DV = 8, 128, 128
N_EXPERTS, TOP_K, D_FF = 32, 4, 1024
SWIGLU_ALPHA, SWIGLU_LIMIT = 1.702, 7.0
EPS = 1e-6

LANES = 128
SCAN_CHUNK = 128
SCAN_ROWS = 1024
TOKEN_TILE = 256
BIG_TOKEN_TILE = 512
MOE_TILE = 384
VMEM_LIMIT = 56 * 1024 * 1024

_GQ, _GK, _GV, _GG, _RQ, _RK, _RV, _RG, _EVEN_MAIN = 0, 256, 512, 1024, 1536, 2048, 2560, 3072, 3584
_HQ, _HFF, _HFB, _HI, _HG, _ODD_MAIN = 0, 1024, 2048, 3072, 4096, 5120


def _dot(a, b):
    return jnp.dot(a, b, preferred_element_type=F32)


def _dot_nt(a, b):
    return lax.dot_general(a, b, (((1,), (1,)), ((), ())), preferred_element_type=F32)


def _rms(x, gain=None):
    y = x * lax.rsqrt(jnp.mean(x * x, axis=-1, keepdims=True) + EPS)
    if gain is not None:
        y = y * gain
    return y


def _sigmoid(x):
    return 1.0 / (1.0 + jnp.exp(-x))


def _silu(x):
    return x * _sigmoid(x)


def _cparams(sem, vmem=VMEM_LIMIT):
    return pltpu.CompilerParams(dimension_semantics=sem, vmem_limit_bytes=vmem)


def _resident(shape):
    nd = len(shape)
    return pl.BlockSpec(shape, lambda *_: (0,) * nd, pipeline_mode=pl.Buffered(1))


def _ada_kernel(c_ref, w_ref, b_ref, o_ref):
    o_ref[0] = _dot(_silu(c_ref[...]), w_ref[0]) + b_ref[0]


def _ada_mods(cond8, ada_w, ada_b):
    tn = 1536
    return pl.pallas_call(
        _ada_kernel,
        grid=(DEPTH, 6 * D_MODEL // tn),
        in_specs=[
            pl.BlockSpec((8, D_MODEL), lambda l, j: (0, 0)),
            pl.BlockSpec((1, D_MODEL, tn), lambda l, j: (l, 0, j)),
            pl.BlockSpec((1, 1, tn), lambda l, j: (l, 0, j)),
        ],
        out_specs=pl.BlockSpec((1, 8, tn), lambda l, j: (l, 0, j)),
        out_shape=jax.ShapeDtypeStruct((DEPTH, 8, 6 * D_MODEL), F32),
        compiler_params=_cparams(("arbitrary", "arbitrary")),
        name="ada_mods",
    )(cond8, ada_w, ada_b.reshape(DEPTH, 1, 6 * D_MODEL))


ROW_UNITS = D_MODEL // LANES


def _rows_from_units(ref, n_rows):
    return jnp.concatenate([ref[pl.ds(c, n_rows, stride=ROW_UNITS), :] for c in range(ROW_UNITS)], axis=1)


def _rows_to_units(ref, val):
    n_rows = val.shape[0]
    for c in range(ROW_UNITS):
        ref[pl.ds(c, n_rows, stride=ROW_UNITS), :] = val[:, c * LANES:(c + 1) * LANES]


def _moe_combine(y_refs, wt_ref):
    w = wt_ref[...]
    terms = [w[:, k:k + 1] * _rows_from_units(y_refs[k], w.shape[0]) for k in range(TOP_K)]
    return (terms[0] + terms[1]) + (terms[2] + terms[3])


def _in_proj_kernel(*refs, first, even, nb_ctx):
    it = iter(refs)
    if first:
        xa_ref, xb_ref = next(it), next(it)
    else:
        xp_ref = next(it)
        y_refs = [next(it) for _ in range(TOP_K)]
        wt_ref, modp_ref = next(it), next(it)
    gain_ref, mod_ref, w_ref = next(it), next(it), next(it)
    if even:
        wlr_ref, gkw_ref, gkb_ref = next(it), next(it), next(it)
    proj_ref = next(it)
    if even:
        g_ref = next(it)
    xcur_ref = next(it)

    D = D_MODEL
    i = pl.program_id(0)
    if first:
        x = jnp.where(i < nb_ctx, xa_ref[...], xb_ref[...])
    else:
        x = xp_ref[...] + modp_ref[0][:, 5 * D:6 * D] * _moe_combine(y_refs, wt_ref)
    xcur_ref[...] = x
    m = mod_ref[0]
    hn = _rms(x, gain_ref[...]) * (1.0 + m[:, D:2 * D]) + m[:, 0:D]
    proj_ref[...] = _dot(hn, w_ref[...])
    if even:
        z = _dot(_dot(hn, wlr_ref[...]), gkw_ref[...]) + gkb_ref[...]
        g_ref[...] = (jnp.minimum(z, 0.0) - jnp.log(1.0 + jnp.exp(-jnp.abs(z)))) * (1.0 / GLA_NORMALIZER)


def _cond_row(i, nb_ctx, tiles_per_lat_seq):
    return jnp.where(i < nb_ctx, 0, 1 + (i - nb_ctx) // tiles_per_lat_seq)


def _in_proj(layer, n_ctx, n_tok, lat_seq, mods, gain, w_main, *, x_parts=None, x_prev=None, y4=None, even_extra=None):
    first = x_parts is not None
    even = even_extra is not None
    tm = BIG_TOKEN_TILE if first else TOKEN_TILE
    nb = n_tok // tm
    nb_ctx = n_ctx // tm
    tpl = lat_seq // tm
    np_cols = w_main.shape[1]
    D = D_MODEL

    def mod_map(l):
        return lambda i: (l * 8 + _cond_row(i, nb_ctx, tpl), 0, 0)

    row = lambda i: (i, 0)
    args, specs = [], []
    if first:
        xa, xb = x_parts
        args += [xa, xb]
        specs += [pl.BlockSpec((tm, D), lambda i: (jnp.minimum(i, nb_ctx - 1), 0)),
                  pl.BlockSpec((tm, D), lambda i: (jnp.maximum(i - nb_ctx, 0), 0))]
    else:
        y4, wts = y4
        args += [x_prev] + [y4] * TOP_K + [wts, mods]
        specs += [pl.BlockSpec((tm, D), row)]
        specs += [pl.BlockSpec((tm * ROW_UNITS, LANES), (lambda k: (lambda i: (k * nb + i, 0)))(k))
                  for k in range(TOP_K)]
        specs += [pl.BlockSpec((tm, LANES), row), pl.BlockSpec((1, 1, 6 * D), mod_map(layer - 1))]
    args += [gain.reshape(1, D), mods, w_main]
    specs += [_resident((1, D)), pl.BlockSpec((1, 1, 6 * D), mod_map(layer)), _resident((D, np_cols))]
    out_shapes = [jax.ShapeDtypeStruct((n_tok, np_cols), F32)]
    out_specs = [pl.BlockSpec((tm, np_cols), row)]
    if even:
        w_lr, gkw, gkb = even_extra
        args += [w_lr, gkw, gkb]
        specs += [_resident(w_lr.shape), _resident(gkw.shape), _resident(gkb.shape)]
        out_shapes.append(jax.ShapeDtypeStruct((n_tok, 2 * GLA_HEADS * GLA_DK), F32))
        out_specs.append(pl.BlockSpec((tm, 2 * GLA_HEADS * GLA_DK), row))
    out_shapes.append(jax.ShapeDtypeStruct((n_tok, D), F32))
    out_specs.append(pl.BlockSpec((tm, D), row))
    return pl.pallas_call(
        functools.partial(_in_proj_kernel, first=first, even=even, nb_ctx=nb_ctx),
        grid=(nb,),
        in_specs=specs,
        out_specs=out_specs,
        out_shape=out_shapes,
        compiler_params=_cparams(("arbitrary",)),
        name=f"in_proj_l{layer}",
    )(*args)


def _tri(c, lower):
    r = lax.broadcasted_iota(jnp.int32, (c, c), 0)
    s = lax.broadcasted_iota(jnp.int32, (c, c), 1)
    return (r >= s) if lower else (r <= s)


def _cumsum_mm(tri_bf16, g):
    g_hi = g.astype(BF16)
    g_lo = (g - g_hi.astype(F32)).astype(BF16)
    r = _dot(tri_bf16, jnp.concatenate([g_hi, g_lo], axis=1))
    w = g.shape[1]
    return r[:, :w] + r[:, w:]


def _gated_scan_kernel(*refs, variant, T, nseq, has_s0, want_final, hpb):
    C = SCAN_CHUNK
    ncs = T // C
    nc = nseq * ncs
    it = iter(refs)
    if variant == "gla":
        q_ref, k_ref, v_ref, og_ref, gf_ref, gb_ref, gain_ref = (next(it) for _ in range(7))
    else:
        q_ref, ff_ref, fb_ref, v_ref, og_ref, lbl_ref, gain_ref = (next(it) for _ in range(7))
    s0_ref = next(it) if has_s0 else None
    y_ref = next(it)
    sfin_ref = next(it) if want_final else None
    qi_scr, o_scr, u_scr, dec_scr, sin_scr = (next(it) for _ in range(5))

    lane = lax.broadcasted_iota(jnp.int32, (1, LANES), 1)
    if hpb == 2:
        masks = [lane < GLA_DK, lane >= GLA_DK]
    else:
        masks = [None]
    tri_l, tri_u = _tri(C, True), _tri(C, False)
    tri_l16, tri_u16 = tri_l.astype(F32).astype(BF16), tri_u.astype(F32).astype(BF16)
    mid_f, mid_b = C // 2 - 1, C // 2

    if variant == "hgrn":
        lgs = [lbl_ref[l] for l in range(DEPTH)]
        mx = functools.reduce(jnp.maximum, lgs)
        es = [jnp.exp(l - mx) for l in lgs]
        tot = functools.reduce(lambda a, b: a + b, es)
        ps = [e / tot for e in es]
        layer = DEPTH - 1
        lb = functools.reduce(lambda a, b: a + b, ps[:layer + 1]) - ps[0]
        lb_f, lb_b = lb[0:1], lb[1:2]

    for n in range(nc):
        rows = pl.ds(n * C, C)
        if variant == "gla":
            q = q_ref[rows, :] * (GLA_DK ** -0.5)
            kf = kb = k_ref[rows, :]
            gf, gb = gf_ref[rows, :], gb_ref[rows, :]
        else:
            q = _silu(q_ref[rows, :])
            f_f = lb_f + (1.0 - lb_f) * _sigmoid(ff_ref[rows, :])
            f_b = lb_b + (1.0 - lb_b) * _sigmoid(fb_ref[rows, :])
            kf, kb = 1.0 - f_f, 1.0 - f_b
            gf, gb = jnp.log(f_f), jnp.log(f_b)
        bf = _cumsum_mm(tri_l16, gf)
        bb = _cumsum_mm(tri_u16, gb)
        bf_mid, bf_end = bf[mid_f:mid_f + 1], bf[C - 1:C]
        bb_mid, bb_end = bb[mid_b:mid_b + 1], bb[0:1]
        qd_f, kd_f = q * jnp.exp(bf - bf_mid), kf * jnp.exp(bf_mid - bf)
        qd_b, kd_b = q * jnp.exp(bb - bb_mid), kb * jnp.exp(bb_mid - bb)
        kend = jnp.concatenate([kf * jnp.exp(bf_end - bf), kb * jnp.exp(bb_end - bb)], axis=1)
        qi_scr[rows, :] = jnp.concatenate([q * jnp.exp(bf), q * jnp.exp(bb)], axis=1)
        dec_scr[n] = jnp.broadcast_to(jnp.concatenate([jnp.exp(bf_end), jnp.exp(bb_end)], axis=1), (8, 2 * LANES))
        for h in range(hpb):
            v = v_ref[rows, h * LANES:(h + 1) * LANES]
            if masks[h] is None:
                qf_h, qb_h, kend_h = qd_f, qd_b, kend
            else:
                qf_h, qb_h = jnp.where(masks[h], qd_f, 0.0), jnp.where(masks[h], qd_b, 0.0)
                kend_h = jnp.where(jnp.concatenate([masks[h], masks[h]], axis=1), kend, 0.0)
            s = jnp.where(tri_l, _dot_nt(qf_h, kd_f), 0.0) + jnp.where(tri_u, _dot_nt(qb_h, kd_b), 0.0)
            o_scr[rows, h * LANES:(h + 1) * LANES] = _dot(s, v)
            u_scr[h, n] = _dot(v.T, kend_h)

    for sq in range(nseq):
        for h in range(hpb):
            if has_s0:
                s_f, s_b = s0_ref[sq, 0, 0].T, s0_ref[sq, 1, 0].T
                if masks[h] is not None:
                    s_f, s_b = jnp.where(masks[h], s_f, 0.0), jnp.where(masks[h], s_b, 0.0)
            else:
                s_f = s_b = jnp.zeros((LANES, LANES), F32)
            for n in range(sq * ncs, (sq + 1) * ncs):
                sin_scr[h, n, :, 0:LANES] = s_f
                s_f = s_f * dec_scr[n, 0:1, 0:LANES] + u_scr[h, n, :, 0:LANES]
            for n in reversed(range(sq * ncs, (sq + 1) * ncs)):
                sin_scr[h, n, :, LANES:2 * LANES] = s_b
                s_b = s_b * dec_scr[n, 0:1, LANES:2 * LANES] + u_scr[h, n, :, LANES:2 * LANES]
            if want_final:
                if h == 0:
                    fin_f, fin_b = s_f, s_b
                else:
                    fin_f, fin_b = fin_f + s_f, fin_b + s_b
        if want_final:
            sfin_ref[sq, 0, 0] = fin_f.T
            sfin_ref[sq, 1, 0] = fin_b.T

    gain = gain_ref[...]
    for n in range(nc):
        rows = pl.ds(n * C, C)
        qi = qi_scr[rows, :]
        for h in range(hpb):
            cols = slice(h * LANES, (h + 1) * LANES)
            o = o_scr[rows, cols] + _dot_nt(qi, sin_scr[h, n])
            y_ref[rows, cols] = (_rms(o, gain) * _silu(og_ref[rows, cols])).astype(y_ref.dtype)


def _scan_scratch(T, hpb):
    nc = T // SCAN_CHUNK
    return [
        pltpu.VMEM((T, 2 * LANES), F32),
        pltpu.VMEM((T, hpb * LANES), F32),
        pltpu.VMEM((hpb, nc, LANES, 2 * LANES), F32),
        pltpu.VMEM((nc, 8, 2 * LANES), F32),
        pltpu.VMEM((hpb, nc, LANES, 2 * LANES), F32),
    ]


def _seqs_per_step(B, T):
    nseq = max(1, min(B, SCAN_ROWS // T))
    assert B % nseq == 0
    return nseq


def _gla_scan(proj, gdec, gain, s0, row0, B, T, want_final):
    nseq = _seqs_per_step(B, T)
    R = nseq * T
    rb0 = row0 // R
    assert row0 % R == 0
    has_s0 = s0 is not None

    def col(base, width=LANES):
        return lambda b, j: (rb0 + b, base // width + j)

    args = [proj, proj, proj, proj, gdec, gdec, gain.reshape(1, GLA_DV)]
    specs = [pl.BlockSpec((R, LANES), col(_GQ)), pl.BlockSpec((R, LANES), col(_GK)),
             pl.BlockSpec((R, 2 * LANES), col(_GV, 2 * LANES)), pl.BlockSpec((R, 2 * LANES), col(_GG, 2 * LANES)),
             pl.BlockSpec((R, LANES), col(0)), pl.BlockSpec((R, LANES), col(GLA_HEADS * GLA_DK)),
             pl.BlockSpec((1, GLA_DV), lambda b, j: (0, 0))]
    st_spec = pl.BlockSpec((nseq, 2, 1, LANES, LANES), lambda b, j: (b, 0, j, 0, 0))
    if has_s0:
        args.append(s0.reshape(B, 2, GLA_HEADS // 2, 2 * GLA_DK, GLA_DV))
        specs.append(st_spec)
    out_shapes = [jax.ShapeDtypeStruct((B * T, GLA_HEADS * GLA_DV), BF16)]
    out_specs = [pl.BlockSpec((R, 2 * LANES), lambda b, j: (b, j))]
    if want_final:
        out_shapes.append(jax.ShapeDtypeStruct((B, 2, GLA_HEADS // 2, 2 * GLA_DK, GLA_DV), F32))
        out_specs.append(st_spec)
    kern = functools.partial(_gated_scan_kernel, variant="gla", T=T, nseq=nseq, has_s0=has_s0,
                             want_final=want_final, hpb=2)
    return pl.pallas_call(
        kern, grid=(B // nseq, GLA_HEADS // 2), in_specs=specs, out_specs=out_specs, out_shape=out_shapes,
        scratch_shapes=_scan_scratch(R, 2),
        compiler_params=_cparams(("arbitrary", "arbitrary")), name=f"gla_scan_T{T}",
    )(*args)


def _hgrn_scan(proj, lb_logits, gain, s0, row0, B, T, want_final):
    nseq = _seqs_per_step(B, T)
    R = nseq * T
    rb0 = row0 // R
    assert row0 % R == 0
    has_s0 = s0 is not None

    def col(base):
        return lambda b, j: (rb0 + b, base // LANES + j)

    args = [proj, proj, proj, proj, proj, lb_logits, gain.reshape(1, HG_DV)]
    specs = [pl.BlockSpec((R, LANES), col(_HQ)), pl.BlockSpec((R, LANES), col(_HFF)),
             pl.BlockSpec((R, LANES), col(_HFB)), pl.BlockSpec((R, LANES), col(_HI)),
             pl.BlockSpec((R, LANES), col(_HG)),
             pl.BlockSpec((DEPTH, 2, LANES), lambda b, j: (0, 0, j)),
             pl.BlockSpec((1, HG_DV), lambda b, j: (0, 0))]
    st_spec = pl.BlockSpec((nseq, 2, 1, LANES, LANES), lambda b, j: (b, 0, j, 0, 0))
    if has_s0:
        args.append(s0.reshape(B, 2, HG_HEADS, HG_DK, HG_DV))
        specs.append(st_spec)
    out_shapes = [jax.ShapeDtypeStruct((B * T, HG_HEADS * HG_DV), BF16)]
    out_specs = [pl.BlockSpec((R, LANES), lambda b, j: (b, j))]
    if want_final:
        out_shapes.append(jax.ShapeDtypeStruct((B, 2, HG_HEADS, HG_DK, HG_DV), F32))
        out_specs.append(st_spec)
    kern = functools.partial(_gated_scan_kernel, variant="hgrn", T=T, nseq=nseq, has_s0=has_s0,
                             want_final=want_final, hpb=1)
    return pl.pallas_call(
        kern, grid=(B // nseq, HG_HEADS), in_specs=specs, out_specs=out_specs, out_shape=out_shapes,
        scratch_shapes=_scan_scratch(R, 1),
        compiler_params=_cparams(("arbitrary", "arbitrary")), name=f"hgrn_scan_T{T}",
    )(*args)


def _ret_scan_kernel(*refs, T, nseq, has_s0, want_final, rope):
    C = SCAN_CHUNK
    ncs = T // C
    nc = nseq * ncs
    it = iter(refs)
    q_ref, k_ref, v_ref, og_ref, dexp_ref = (next(it) for _ in range(5))
    if rope:
        cos_ref, sin_ref = next(it), next(it)
    s0_ref = next(it) if has_s0 else None
    y_ref = next(it)
    sfin_ref = next(it) if want_final else None
    qi_scr, o_scr, u_scr, sin_scr = (next(it) for _ in range(4))

    lg = jnp.log1p(-jnp.exp2(-dexp_ref[0]))
    lg_f, lg_b = lg[0:1], lg[1:2]
    r = lax.broadcasted_iota(jnp.int32, (C, C), 0)
    s = lax.broadcasted_iota(jnp.int32, (C, C), 1)
    dist = (r - s).astype(F32)
    dmask = (jnp.where(r >= s, jnp.exp(jnp.maximum(dist, 0.0) * lg_f[:, 0:1]), 0.0)
             + jnp.where(r <= s, jnp.exp(jnp.maximum(-dist, 0.0) * lg_b[:, 0:1]), 0.0))
    pos = lax.broadcasted_iota(jnp.int32, (C, LANES), 0).astype(F32)
    xi = jnp.concatenate([jnp.exp((pos + 1.0) * lg_f), jnp.exp((C - pos) * lg_b)], axis=1)
    zeta = jnp.concatenate([jnp.exp((C - 1.0 - pos) * lg_f), jnp.exp(pos * lg_b)], axis=1)
    d_f, d_b = jnp.exp(C * lg_f), jnp.exp(C * lg_b)

    def rot(x, seq_rows):
        if not rope:
            return x
        return x * cos_ref[seq_rows, :] + pltpu.roll(x, RET_DK // 2, axis=1) * sin_ref[seq_rows, :]

    for n in range(nc):
        rows = pl.ds(n * C, C)
        seq_rows = pl.ds((n % ncs) * C, C)
        q = rot(q_ref[rows, :], seq_rows)
        k = rot(k_ref[rows, :] * (RET_DK ** -0.5), seq_rows)
        v = v_ref[rows, :]
        o_scr[rows, :] = _dot(_dot_nt(q, k) * dmask, v)
        qi_scr[rows, :] = jnp.concatenate([q, q], axis=1) * xi
        u_scr[n] = _dot(v.T, jnp.concatenate([k, k], axis=1) * zeta)

    for sq in range(nseq):
        if has_s0:
            s_f, s_b = s0_ref[sq, 0, 0].T, s0_ref[sq, 1, 0].T
        else:
            s_f = s_b = jnp.zeros((LANES, LANES), F32)
        for n in range(sq * ncs, (sq + 1) * ncs):
            sin_scr[n, :, 0:LANES] = s_f
            s_f = s_f * d_f + u_scr[n, :, 0:LANES]
        for n in reversed(range(sq * ncs, (sq + 1) * ncs)):
            sin_scr[n, :, LANES:2 * LANES] = s_b
            s_b = s_b * d_b + u_scr[n, :, LANES:2 * LANES]
        if want_final:
            sfin_ref[sq, 0, 0] = s_f.T
            sfin_ref[sq, 1, 0] = s_b.T

    for n in range(nc):
        rows = pl.ds(n * C, C)
        o = o_scr[rows, :] + _dot_nt(qi_scr[rows, :], sin_scr[n])
        y_ref[rows, :] = (_rms(o) * _silu(og_ref[rows, :])).astype(y_ref.dtype)


def _ret_scan(proj, dexp, rope_tabs, s0, row0, B, T, want_final):
    nseq = _seqs_per_step(B, T)
    R = nseq * T
    rb0 = row0 // R
    assert row0 % R == 0
    nc = R // SCAN_CHUNK
    has_s0 = s0 is not None
    rope = rope_tabs is not None

    def col(base):
        return lambda b, j: (rb0 + b, base // LANES + j)

    args = [proj, proj, proj, proj, dexp]
    specs = [pl.BlockSpec((R, LANES), col(_RQ)), pl.BlockSpec((R, LANES), col(_RK)),
             pl.BlockSpec((R, LANES), col(_RV)), pl.BlockSpec((R, LANES), col(_RG)),
             pl.BlockSpec((1, 2, LANES), lambda b, j: (j, 0, 0))]
    if rope:
        args += list(rope_tabs)
        specs += [pl.BlockSpec((T, LANES), lambda b, j: (0, 0))] * 2
    st_spec = pl.BlockSpec((nseq, 2, 1, LANES, LANES), lambda b, j: (b, 0, j, 0, 0))
    if has_s0:
        args.append(s0.reshape(B, 2, RET_HEADS, RET_DK, RET_DV))
        specs.append(st_spec)
    out_shapes = [jax.ShapeDtypeStruct((B * T, RET_HEADS * RET_DV), BF16)]
    out_specs = [pl.BlockSpec((R, LANES), lambda b, j: (b, j))]
    if want_final:
        out_shapes.append(jax.ShapeDtypeStruct((B, 2, RET_HEADS, RET_DK, RET_DV), F32))
        out_specs.append(st_spec)
    kern = functools.partial(_ret_scan_kernel, T=T, nseq=nseq, has_s0=has_s0, want_final=want_final, rope=rope)
    scratch = [pltpu.VMEM((R, 2 * LANES), F32), pltpu.VMEM((R, LANES), F32),
               pltpu.VMEM((nc, LANES, 2 * LANES), F32), pltpu.VMEM((nc, LANES, 2 * LANES), F32)]
    return pl.pallas_call(
        kern, grid=(B // nseq, RET_HEADS), in_specs=specs, out_specs=out_specs, out_shape=out_shapes,
        scratch_shapes=scratch,
        compiler_params=_cparams(("arbitrary", "arbitrary")), name=f"ret_scan_T{T}",
    )(*args)


def _out_proj_kernel(*refs, n_mix, nb_ctx):
    it = iter(refs)
    y_refs = [(next(it), next(it)) for _ in range(n_mix)]
    wo_ref, x_ref, mod_ref, gain_ref, rw_ref, rb_ref = (next(it) for _ in range(6))
    xnew_ref, hn_ref, idx_ref, wt_ref, rank_ref, cnt_ref = (next(it) for _ in range(6))
    cnt_scr = next(it)
    D = D_MODEL
    is_ctx = pl.program_id(0) < nb_ctx
    mix = None
    r0 = 0
    for ya_ref, yb_ref in y_refs:
        w = ya_ref.shape[1]
        part = _dot(jnp.where(is_ctx, ya_ref[...], yb_ref[...]), wo_ref[r0:r0 + w, :])
        mix = part if mix is None else mix + part
        r0 += w
    m = mod_ref[0]
    xn = x_ref[...] + m[:, 2 * D:3 * D] * mix
    xnew_ref[...] = xn
    hn = _rms(xn, gain_ref[...]) * (1.0 + m[:, 4 * D:5 * D]) + m[:, 3 * D:4 * D]
    hn_hi = hn.astype(BF16)
    _rows_to_units(hn_ref, hn)
    hn_lo = (hn - hn_hi.astype(F32)).astype(BF16)
    rw = rw_ref[...]
    rw_hi = rw.astype(BF16)
    rw_lo = (rw - rw_hi.astype(F32)).astype(BF16)
    logits = (_dot(hn_hi, rw_hi) + (_dot(hn_lo, rw_hi) + _dot(hn_hi, rw_lo))) + rb_ref[...]
    lane = lax.broadcasted_iota(jnp.int32, logits.shape, 1).astype(F32)
    vals, idxs = [], []
    cur = logits
    for _ in range(TOP_K):
        mx = jnp.max(cur, axis=-1, keepdims=True)
        ik = jnp.min(jnp.where(cur == mx, lane, float(LANES)), axis=-1, keepdims=True)
        vals.append(mx)
        idxs.append(ik)
        cur = jnp.where(lane == ik, -jnp.inf, cur)
    es = [jnp.exp(v - vals[0]) for v in vals]
    tot = (es[0] + es[1]) + (es[2] + es[3])
    idx_out = jnp.zeros(logits.shape, F32)
    wt_out = jnp.zeros(logits.shape, F32)
    for k in range(TOP_K):
        idx_out = jnp.where(lane == float(k), idxs[k], idx_out)
        wt_out = jnp.where(lane == float(k), es[k] / tot, wt_out)
    idx_ref[...] = idx_out.astype(jnp.int32)
    wt_ref[...] = wt_out

    @pl.when(pl.program_id(0) == 0)
    def _():
        cnt_scr[...] = jnp.zeros(cnt_scr.shape, F32)

    tm = logits.shape[0]
    hits = [lane == idxs[k] for k in range(TOP_K)]
    sel = jnp.zeros(logits.shape, F32)
    for k in range(TOP_K):
        sel = sel + jnp.where(hits[k], 1.0, 0.0)
    rr = lax.broadcasted_iota(jnp.int32, (tm, tm), 0)
    cc = lax.broadcasted_iota(jnp.int32, (tm, tm), 1)
    before = jnp.where(rr > cc, 1.0, 0.0).astype(BF16)
    rank_all = cnt_scr[0:1, :] + _dot(before, sel.astype(BF16))
    rank_out = jnp.zeros(logits.shape, F32)
    for k in range(TOP_K):
        rk = jnp.sum(jnp.where(hits[k], rank_all, 0.0), axis=-1, keepdims=True)
        rank_out = jnp.where(lane == float(k), rk, rank_out)
    rank_ref[...] = rank_out.astype(jnp.int32)
    total = cnt_scr[...] + jnp.sum(sel, axis=0, keepdims=True)
    cnt_scr[...] = total
    cnt_ref[...] = total.astype(jnp.int32)


def _out_proj(layer, n_ctx, n_tok, lat_seq, ys, w_out, x_cur, mods, gain, rw_pad, rb_pad):
    tm = BIG_TOKEN_TILE
    nb, nb_ctx, tpl = n_tok // tm, n_ctx // tm, lat_seq // tm
    D = D_MODEL
    row = lambda i: (i, 0)
    specs = []
    for ya, _ in ys:
        specs += [pl.BlockSpec((tm, ya.shape[1]), lambda i: (jnp.minimum(i, nb_ctx - 1), 0)),
                  pl.BlockSpec((tm, ya.shape[1]), lambda i: (jnp.maximum(i - nb_ctx, 0), 0))]
    specs += [_resident((D, D)), pl.BlockSpec((tm, D), row),
              pl.BlockSpec((1, 1, 6 * D), lambda i: (layer * 8 + _cond_row(i, nb_ctx, tpl), 0, 0)),
              _resident((1, D)), _resident((D, LANES)), _resident((1, LANES))]
    return pl.pallas_call(
        functools.partial(_out_proj_kernel, n_mix=len(ys), nb_ctx=nb_ctx),
        grid=(nb,),
        in_specs=specs,
        out_specs=[pl.BlockSpec((tm, D), row), pl.BlockSpec((tm * ROW_UNITS, LANES), row),
                   pl.BlockSpec((tm, LANES), row), pl.BlockSpec((tm, LANES), row),
                   pl.BlockSpec((tm, LANES), row), pl.BlockSpec((8, LANES), lambda i: (0, 0))],
        out_shape=[jax.ShapeDtypeStruct((n_tok, D), F32), jax.ShapeDtypeStruct((n_tok * ROW_UNITS, LANES), F32),
                   jax.ShapeDtypeStruct((n_tok, LANES), jnp.int32), jax.ShapeDtypeStruct((n_tok, LANES), F32),
                   jax.ShapeDtypeStruct((n_tok, LANES), jnp.int32), jax.ShapeDtypeStruct((8, LANES), jnp.int32)],
        scratch_shapes=[pltpu.VMEM((8, LANES), F32)],
        compiler_params=_cparams(("arbitrary",)),
        name=f"out_proj_l{layer}",
    )(*[y for pair in ys for y in pair], w_out, x_cur, mods, gain.reshape(1, D), rw_pad, rb_pad)


W1_SPLIT = 4
W2_SPLIT = 2


def _moe_kernel(seg_ref, nt_ref, code_hbm, x_hbm, *refs, n_tok):
    w1_refs = refs[:W1_SPLIT]
    b1_ref = refs[W1_SPLIT]
    w2_refs = refs[W1_SPLIT + 1:W1_SPLIT + 1 + W2_SPLIT]
    (b2_ref, y_hbm, w1_scr, b1_scr, w2_scr, act_scr, xbuf, obuf, code_smem, csem, gsem,
     ssem) = refs[W1_SPLIT + 1 + W2_SPLIT:]
    tm = MOE_TILE
    U = ROW_UNITS
    n_code_tiles = code_hbm.shape[0] // tm
    e = pl.program_id(0)
    nt = nt_ref[e]
    g_first = seg_ref[e] // tm

    GATHER, SCATTER = 0, 1

    def code_copy(kind, t, p):
        first = pl.multiple_of(jnp.clip(t, 0, n_code_tiles - 1) * tm, tm)
        return pltpu.make_async_copy(code_hbm.at[pl.ds(first, tm)], code_smem.at[kind, p], csem.at[kind, p])

    def gather_start(b, p):
        for r in range(tm):
            tok = jnp.bitwise_and(code_smem[GATHER, p, r], n_tok - 1)
            pltpu.make_async_copy(x_hbm.at[pl.ds(pl.multiple_of(tok * U, U), U)],
                                  xbuf.at[b, pl.ds(r * U, U)], gsem.at[b]).start()

    def gather_wait(b):
        pltpu.make_async_copy(x_hbm.at[pl.ds(0, tm * U)], xbuf.at[b], gsem.at[b]).wait()

    def scatter_start(b, p):
        for r in range(tm):
            dst = code_smem[SCATTER, p, r]
            pltpu.make_async_copy(obuf.at[b, pl.ds(r * U, U)],
                                  y_hbm.at[pl.ds(pl.multiple_of(dst * U, U), U)], ssem.at[b]).start()

    def scatter_wait(b):
        pltpu.make_async_copy(obuf.at[b], y_hbm.at[pl.ds(0, tm * U)], ssem.at[b]).wait()

    @pl.when(e == 0)
    def _():
        obuf[...] = jnp.zeros(obuf.shape, obuf.dtype)
        for h in range(2):
            cp = pltpu.make_async_copy(obuf.at[0], y_hbm.at[pl.ds((TOP_K * n_tok + h * tm) * U, tm * U)], ssem.at[0])
            cp.start()
            cp.wait()
        code_copy(GATHER, 0, 0).start()
        code_copy(GATHER, 1, 1).start()
        code_copy(SCATTER, n_code_tiles - 1, 1).start()
        code_copy(SCATTER, 0, 0).start()
        code_copy(GATHER, 0, 0).wait()
        gather_start(0, 0)

    @pl.when(nt > 0)
    def _():
        grp = 2 * LANES
        rr = lax.broadcasted_iota(jnp.int32, (grp, grp), 0)
        cc = lax.broadcasted_iota(jnp.int32, (grp, grp), 1)
        src = jnp.where(cc < LANES, 2 * cc, 2 * (cc - LANES) + 1)
        perm = jnp.where(rr == src, 1.0, 0.0).astype(BF16)
        cw = 2 * D_FF // W1_SPLIT
        for c, w_ref in enumerate(w1_refs):
            for j in range(cw // grp):
                cols = slice(j * grp, (j + 1) * grp)
                w1_scr[:, c * cw + j * grp:c * cw + (j + 1) * grp] = _dot(w_ref[0, 0, :, cols].astype(BF16),
                                                                         perm).astype(BF16)
        bias = jnp.broadcast_to(b1_ref[0, 0], (8, 2 * D_FF))
        for j in range(2 * D_FF // grp):
            cols = slice(j * grp, (j + 1) * grp)
            rest = bias[:, cols]
            acc = jnp.zeros((8, grp), F32)
            for _ in range(3):
                term = rest.astype(BF16)
                acc = acc + _dot(term, perm)
                rest = rest - term.astype(F32)
            b1_scr[:, cols] = acc
        rh = D_FF // W2_SPLIT
        for c, w_ref in enumerate(w2_refs):
            w2_scr[c * rh:(c + 1) * rh, :] = w_ref[0, 0].astype(BF16)
        blk = 4 * LANES

        def tile(g, b):
            o = 1 - b

            @pl.when(g >= 0)
            def _():
                code_copy(GATHER, g + 1, o).wait()
                code_copy(SCATTER, g - 1, o).wait()
                gather_start(o, o)
                scatter_start(o, o)
                code_copy(GATHER, g + 2, b).start()
                code_copy(SCATTER, g + 1, o).start()

            gather_wait(b)
            x = _rows_from_units(xbuf.at[b], tm).astype(BF16)
            for c in range(2 * D_FF // blk):
                cols = slice(c * blk, (c + 1) * blk)
                hid = _dot(x, w1_scr[:, cols]) + b1_scr[0:1, cols]
                acts = []
                for j in range(blk // grp):
                    glu = jnp.minimum(hid[:, j * grp:j * grp + LANES], SWIGLU_LIMIT)
                    lin = jnp.clip(hid[:, j * grp + LANES:(j + 1) * grp], -SWIGLU_LIMIT, SWIGLU_LIMIT)
                    acts.append(glu * _sigmoid(SWIGLU_ALPHA * glu) * (lin + 1.0))
                act_scr[:, c * (blk // 2):(c + 1) * (blk // 2)] = jnp.concatenate(acts, axis=1).astype(BF16)
            y = _dot(act_scr[...], w2_scr[...]) + b2_ref[0, 0]

            @pl.when(g >= 1)
            def _():
                scatter_wait(b)

            _rows_to_units(obuf.at[b], y)

        g_end = g_first + nt

        def tile_pair(m, carry):
            for b in range(2):
                g = 2 * m + b

                @pl.when(jnp.logical_and(g >= g_first, g < g_end))
                def _():
                    tile(g, b)

            return carry

        lax.fori_loop(g_first // 2, (g_end + 1) // 2, tile_pair, 0)

    @pl.when(e == N_EXPERTS - 1)
    def _():
        g_end = g_first + nt
        last = lax.rem(g_end + 1, 2)
        code_copy(SCATTER, g_end - 1, last).wait()
        scatter_start(last, last)
        scatter_wait(last)
        scatter_wait(1 - last)
        gather_wait(1 - last)
        code_copy(GATHER, g_end + 1, last).wait()
        code_copy(SCATTER, g_end, 1 - last).wait()


def _moe_experts(layer, n_tok, seg_start, tiles_per_e, code, x_units, w1, b1, w2, b2):
    tm = MOE_TILE
    D, F2, U = D_MODEL, 2 * D_FF, ROW_UNITS
    cw, rh = F2 // W1_SPLIT, D_FF // W2_SPLIT
    w1_specs = [pl.BlockSpec((1, 1, D, cw), (lambda c: (lambda e, sg, nt: (layer, e, 0, c)))(c))
                for c in range(W1_SPLIT)]
    w2_specs = [pl.BlockSpec((1, 1, rh, D), (lambda c: (lambda e, sg, nt: (layer, e, c, 0)))(c))
                for c in range(W2_SPLIT)]
    grid_spec = pltpu.PrefetchScalarGridSpec(
        num_scalar_prefetch=2,
        grid=(N_EXPERTS,),
        in_specs=[pl.BlockSpec(memory_space=pl.ANY), pl.BlockSpec(memory_space=pl.ANY)]
        + w1_specs + [pl.BlockSpec((1, 1, 1, F2), lambda e, sg, nt: (layer, e, 0, 0))]
        + w2_specs + [pl.BlockSpec((1, 1, 1, D), lambda e, sg, nt: (layer, e, 0, 0))],
        out_specs=pl.BlockSpec(memory_space=pl.ANY),
        scratch_shapes=[pltpu.VMEM((D, F2), BF16), pltpu.VMEM((8, F2), F32), pltpu.VMEM((D_FF, D), BF16),
                        pltpu.VMEM((tm, D_FF), BF16),
                        pltpu.VMEM((2, tm * U, LANES), F32), pltpu.VMEM((2, tm * U, LANES), F32),
                        pltpu.SMEM((2, 2, tm), jnp.int32),
                        pltpu.SemaphoreType.DMA((2, 2)), pltpu.SemaphoreType.DMA((2,)), pltpu.SemaphoreType.DMA((2,))],
    )
    n_out_rows = TOP_K * n_tok + 2 * tm
    return pl.pallas_call(
        functools.partial(_moe_kernel, n_tok=n_tok),
        grid_spec=grid_spec,
        out_shape=jax.ShapeDtypeStruct((n_out_rows * U, LANES), F32),
        compiler_params=_cparams(("arbitrary",)),
        name="moe_experts",
    )(seg_start, tiles_per_e, code, x_units, *([w1] * W1_SPLIT), b1.reshape(DEPTH, N_EXPERTS, 1, F2),
      *([w2] * W2_SPLIT), b2.reshape(DEPTH, N_EXPERTS, 1, D))


def _pos_kernel(idx_ref, rank_ref, seg_ref, pos_ref):
    lane = lax.broadcasted_iota(jnp.int32, idx_ref.shape, 1)
    idx, rank, seg = idx_ref[...], rank_ref[...], seg_ref[...]
    out = jnp.zeros(idx.shape, jnp.int32)
    for k in range(TOP_K):
        hit = lane == idx[:, k:k + 1]
        seg_k = jnp.sum(jnp.where(hit, seg, 0.0), axis=-1, keepdims=True).astype(jnp.int32)
        out = jnp.where(lane == k, seg_k + rank[:, k:k + 1], out)
    pos_ref[...] = out


def _pair_slots(idx, rank, seg_start):
    n_tok = idx.shape[0]
    tm = TOKEN_TILE
    row = lambda i: (i, 0)
    return pl.pallas_call(
        _pos_kernel,
        grid=(n_tok // tm,),
        in_specs=[pl.BlockSpec((tm, LANES), row), pl.BlockSpec((tm, LANES), row),
                  pl.BlockSpec((1, LANES), lambda i: (0, 0))],
        out_specs=pl.BlockSpec((tm, LANES), row),
        out_shape=jax.ShapeDtypeStruct((n_tok, LANES), jnp.int32),
        compiler_params=_cparams(("arbitrary",)),
        name="pair_slots",
    )(idx, rank, seg_start)


def _slot_code_kernel(pos_ref, unused_hbm, code_ref, sem, *, n_tok):
    i = pl.program_id(0)
    tm = pos_ref.shape[0] // TOP_K

    @pl.when(i == 0)
    def _():
        cp = pltpu.make_async_copy(unused_hbm, code_ref, sem)
        cp.start()
        cp.wait()

    base = i * tm
    for t in range(tm):
        for k in range(TOP_K):
            code_ref[pos_ref[t * TOP_K + k]] = base + (k * n_tok + t)


def _slot_codes(pos, unused):
    n_tok = pos.shape[0]
    tm = TOKEN_TILE
    return pl.pallas_call(
        functools.partial(_slot_code_kernel, n_tok=n_tok),
        grid=(n_tok // tm,),
        in_specs=[pl.BlockSpec((tm * TOP_K,), lambda i: (i,), memory_space=pltpu.SMEM),
                  pl.BlockSpec(memory_space=pl.ANY)],
        out_specs=pl.BlockSpec(memory_space=pltpu.SMEM),
        out_shape=jax.ShapeDtypeStruct(unused.shape, jnp.int32),
        scratch_shapes=[pltpu.SemaphoreType.DMA(())],
        compiler_params=_cparams(("arbitrary",)),
        name="slot_codes",
    )(pos.reshape(-1), unused)


def _moe_layer(layer, hn_units, idx, rank, counts, w1, b1, w2, b2):
    n_tok = hn_units.shape[0] // ROW_UNITS
    assert n_tok & (n_tok - 1) == 0
    tm = MOE_TILE
    n_pairs = n_tok * TOP_K
    n_slots = ((n_pairs + N_EXPERTS * (tm - 1)) // tm + 1) * tm
    tiles_per_e = (counts + tm - 1) // tm
    seg_start = (jnp.cumsum(tiles_per_e) - tiles_per_e) * tm
    seg_f32 = jnp.pad(seg_start.astype(F32), (0, LANES - N_EXPERTS)).reshape(1, LANES)
    pos = _pair_slots(idx, rank, seg_f32)[:, :TOP_K]
    unused = n_pairs + jnp.arange(n_slots, dtype=jnp.int32) % (2 * tm)
    code = _slot_codes(pos, unused)
    return _moe_experts(layer, n_tok, seg_start, tiles_per_e, code, hn_units, w1, b1, w2, b2)


def _final_kernel(*refs, nb_ctx):
    x_ref = refs[0]
    y_refs = refs[1:1 + TOP_K]
    wt_ref, mod_ref, gain_ref, oa_ref, ob_ref = refs[1 + TOP_K:]
    D = D_MODEL
    i = pl.program_id(0)
    x = x_ref[...] + mod_ref[0][:, 5 * D:6 * D] * _moe_combine(y_refs, wt_ref)
    out = _rms(x, gain_ref[...])

    @pl.when(i < nb_ctx)
    def _():
        oa_ref[...] = out

    @pl.when(i >= nb_ctx)
    def _():
        ob_ref[...] = out


def _final(n_ctx, n_tok, lat_seq, x_new, y4, wts, mods, gain):
    tm = TOKEN_TILE
    nb, nb_ctx, tpl = n_tok // tm, n_ctx // tm, lat_seq // tm
    D = D_MODEL
    specs = [pl.BlockSpec((tm, D), lambda i: (i, 0))]
    specs += [pl.BlockSpec((tm * ROW_UNITS, LANES), (lambda k: (lambda i: (k * nb + i, 0)))(k)) for k in range(TOP_K)]
    specs += [pl.BlockSpec((tm, LANES), lambda i: (i, 0)),
              pl.BlockSpec((1, 1, 6 * D), lambda i: ((DEPTH - 1) * 8 + _cond_row(i, nb_ctx, tpl), 0, 0)),
              _resident((1, D))]
    return pl.pallas_call(
        functools.partial(_final_kernel, nb_ctx=nb_ctx),
        grid=(nb,),
        in_specs=specs,
        out_specs=[pl.BlockSpec((tm, D), lambda i: (jnp.minimum(i, nb_ctx - 1), 0)),
                   pl.BlockSpec((tm, D), lambda i: (jnp.maximum(i - nb_ctx, 0), 0))],
        out_shape=[jax.ShapeDtypeStruct((n_ctx, D), F32), jax.ShapeDtypeStruct((n_tok - n_ctx, D), F32)],
        compiler_params=_cparams(("arbitrary",)),
        name="final_norm",
    )(x_new, *([y4] * TOP_K), wts, mods, gain.reshape(1, D))


def _rope_tables(n_tokens):
    rows = n_tokens // GRID_W
    r = jnp.repeat(jnp.arange(rows, dtype=F32), GRID_W)
    col = jnp.tile(jnp.arange(GRID_W, dtype=F32), rows)
    nf = RET_DK // 4
    inv = ROPE_BASE ** (-jnp.arange(nf, dtype=F32) / nf)
    ang = jnp.concatenate([r[:, None] * inv, col[:, None] * inv], axis=-1)
    cos, sin = jnp.cos(ang), jnp.sin(ang)
    return jnp.concatenate([cos, cos], axis=-1), jnp.concatenate([-sin, sin], axis=-1)


def kernel(x_prompt, x_sample, c, state_gla, state_ret, state_hgrn, c_ctx, norm_mix, norm_ffn, ada_w, ada_b,
           w_in_even, w_out_even, gla_gk_w, gla_gk_b, gla_gain, ret_decay_exp, w_in_odd, w_out_odd,
           hgrn_lb_logits, hgrn_gain, router_w, router_b, moe_w1, moe_b1, moe_w2, moe_b2, final_norm):
    D = D_MODEL
    B_ctx, T_ctx, _ = x_prompt.shape
    B_lat, T_lat, _ = x_sample.shape
    n_ctx, n_lat = B_ctx * T_ctx, B_lat * T_lat
    n_tok = n_ctx + n_lat
    assert n_ctx % BIG_TOKEN_TILE == 0 and T_lat % BIG_TOKEN_TILE == 0 and B_lat + 1 <= 8
    assert T_ctx % SCAN_CHUNK == 0 and T_lat % SCAN_CHUNK == 0 and n_ctx % T_lat == 0

    cond8 = jnp.concatenate([c_ctx[None, :], c, jnp.zeros((8 - 1 - B_lat, D), F32)], axis=0)
    mods = _ada_mods(cond8, ada_w, ada_b).reshape(DEPTH * 8, 1, 6 * D)

    rw_pad = jnp.pad(router_w, ((0, 0), (0, 0), (0, LANES - N_EXPERTS)))
    rb_pad = jnp.pad(router_b, ((0, 0), (0, LANES - N_EXPERTS)), constant_values=-1e30)

    w_even = w_in_even[0]
    lr0 = GLA_HEADS * (2 * GLA_DK + GLA_DV)
    lr1 = lr0 + 2 * GLA_LOW_RANK
    w_main = jnp.concatenate([w_even[:, :lr0], w_even[:, lr1:]], axis=1)
    w_lr = jnp.pad(w_even[:, lr0:lr1], ((0, 0), (0, LANES - 2 * GLA_LOW_RANK)))
    nqk = GLA_HEADS * GLA_DK
    gkw = jnp.zeros((LANES, 2 * nqk), F32)
    gkw = gkw.at[0:GLA_LOW_RANK, 0:nqk].set(gla_gk_w[0, 0])
    gkw = gkw.at[GLA_LOW_RANK:2 * GLA_LOW_RANK, nqk:].set(gla_gk_w[0, 1])
    gkb = gla_gk_b[0].reshape(1, 2 * nqk)
    proj, gdec, x_cur = _in_proj(0, n_ctx, n_tok, T_lat, mods, norm_mix[0], w_main,
                                 x_parts=(x_prompt.reshape(n_ctx, D), x_sample.reshape(n_lat, D)),
                                 even_extra=(w_lr, gkw, gkb))

    y_gla_c, fin_gla = _gla_scan(proj, gdec, gla_gain[0], None, 0, B_ctx, T_ctx, True)
    (y_gla_l,) = _gla_scan(proj, gdec, gla_gain[0], state_gla[:, 0], n_ctx, B_lat, T_lat, False)
    dexp = jnp.broadcast_to(ret_decay_exp[0].T[:, :, None], (RET_HEADS, 2, LANES))
    y_ret_c, fin_ret = _ret_scan(proj, dexp, None, None, 0, B_ctx, T_ctx, True)
    (y_ret_l,) = _ret_scan(proj, dexp, _rope_tables(T_lat), state_ret[:, 0], n_ctx, B_lat, T_lat, False)

    x_new, hn, idx, wts, rank, cnt = _out_proj(0, n_ctx, n_tok, T_lat, [(y_gla_c, y_gla_l), (y_ret_c, y_ret_l)],
                                               w_out_even[0], x_cur, mods, norm_ffn[0], rw_pad[0], rb_pad[0:1])
    y4 = _moe_layer(0, hn, idx, rank, cnt[0, :N_EXPERTS], moe_w1, moe_b1, moe_w2, moe_b2)

    (proj, x_cur) = _in_proj(1, n_ctx, n_tok, T_lat, mods, norm_mix[1], w_in_odd[0], x_prev=x_new, y4=(y4, wts))
    y_h_c, fin_h = _hgrn_scan(proj, hgrn_lb_logits, hgrn_gain[0], None, 0, B_ctx, T_ctx, True)
    (y_h_l,) = _hgrn_scan(proj, hgrn_lb_logits, hgrn_gain[0], state_hgrn[:, 0], n_ctx, B_lat, T_lat, False)
    x_new, hn, idx, wts, rank, cnt = _out_proj(1, n_ctx, n_tok, T_lat, [(y_h_c, y_h_l)], w_out_odd[0], x_cur, mods,
                                               norm_ffn[1], rw_pad[1], rb_pad[1:2])
    y4 = _moe_layer(1, hn, idx, rank, cnt[0, :N_EXPERTS], moe_w1, moe_b1, moe_w2, moe_b2)

    y_ctx, y_lat = _final(n_ctx, n_tok, T_lat, x_new, y4, wts, mods, final_norm)

    new_state_gla = fin_gla.reshape(B_ctx, 1, 2, GLA_HEADS, GLA_DK, GLA_DV)
    new_state_ret = fin_ret.reshape(B_ctx, 1, 2, RET_HEADS, RET_DK, RET_DV)
    new_state_hgrn = fin_h.reshape(B_ctx, 1, 2, HG_HEADS, HG_DK, HG_DV)
    return (y_ctx.reshape(B_ctx, T_ctx, D), y_lat.reshape(B_lat, T_lat, D), new_state_gla, new_state_ret,
            new_state_hgrn)
```

```python
import functools

import jax
import jax.numpy as jnp
from jax import lax
from jax.experimental import pallas as pl
from jax.experimental.pallas import tpu as pltpu

F32 = jnp.float32
BF16 = jnp.bfloat16

D_MODEL = 1024
DEPTH = 2
GRID_W = 64
GLA_HEADS, GLA_DK, GLA_DV, GLA_LOW_RANK = 4, 64, 128, 16
GLA_NORMALIZER = 16.0
RET_HEADS, RET_DK, RET_DV = 4, 128, 128
ROPE_BASE = 10000.0
HG_HEADS, HG_DK, HG_DV = 8, 128, 128
N_EXPERTS, TOP_K, D_FF = 32, 4, 1024
SWIGLU_ALPHA, SWIGLU_LIMIT = 1.702, 7.0
EPS = 1e-6

LANES = 128
SCAN_CHUNK = 128
SCAN_ROWS = 1024
TOKEN_TILE = 256
BIG_TOKEN_TILE = 512
MOE_TILE = 384
VMEM_LIMIT = 56 * 1024 * 1024

_GQ, _GK, _GV, _GG, _RQ, _RK, _RV, _RG, _EVEN_MAIN = 0, 256, 512, 1024, 1536, 2048, 2560, 3072, 3584
_HQ, _HFF, _HFB, _HI, _HG, _ODD_MAIN = 0, 1024, 2048, 3072, 4096, 5120


def _dot(a, b):
    return jnp.dot(a, b, preferred_element_type=F32)


def _dot_nt(a, b):
    return lax.dot_general(a, b, (((1,), (1,)), ((), ())), preferred_element_type=F32)


def _rms(x, gain=None):
    y = x * lax.rsqrt(jnp.mean(x * x, axis=-1, keepdims=True) + EPS)
    if gain is not None:
        y = y * gain
    return y


def _sigmoid(x):
    return 1.0 / (1.0 + jnp.exp(-x))


def _silu(x):
    return x * _sigmoid(x)


def _cparams(sem, vmem=VMEM_LIMIT):
    return pltpu.CompilerParams(dimension_semantics=sem, vmem_limit_bytes=vmem)


def _resident(shape):
    nd = len(shape)
    return pl.BlockSpec(shape, lambda *_: (0,) * nd, pipeline_mode=pl.Buffered(1))


def _ada_kernel(c_ref, w_ref, b_ref, o_ref):
    o_ref[0] = _dot(_silu(c_ref[...]), w_ref[0]) + b_ref[0]


def _ada_mods(cond8, ada_w, ada_b):
    tn = 1536
    return pl.pallas_call(
        _ada_kernel,
        grid=(DEPTH, 6 * D_MODEL // tn),
        in_specs=[
            pl.BlockSpec((8, D_MODEL), lambda l, j: (0, 0)),
            pl.BlockSpec((1, D_MODEL, tn), lambda l, j: (l, 0, j)),
            pl.BlockSpec((1, 1, tn), lambda l, j: (l, 0, j)),
        ],
        out_specs=pl.BlockSpec((1, 8, tn), lambda l, j: (l, 0, j)),
        out_shape=jax.ShapeDtypeStruct((DEPTH, 8, 6 * D_MODEL), F32),
        compiler_params=_cparams(("arbitrary", "arbitrary")),
        name="ada_mods",
    )(cond8, ada_w, ada_b.reshape(DEPTH, 1, 6 * D_MODEL))


ROW_UNITS = D_MODEL // LANES


def _rows_from_units(ref, n_rows):
    return jnp.concatenate([ref[pl.ds(c, n_rows, stride=ROW_UNITS), :] for c in range(ROW_UNITS)], axis=1)


def _rows_to_units(ref, val):
    n_rows = val.shape[0]
    for c in range(ROW_UNITS):
        ref[pl.ds(c, n_rows, stride=ROW_UNITS), :] = val[:, c * LANES:(c + 1) * LANES]


def _moe_combine(y_refs, wt_ref):
    w = wt_ref[...]
    terms = [w[:, k:k + 1] * _rows_from_units(y_refs[k], w.shape[0]) for k in range(TOP_K)]
    return (terms[0] + terms[1]) + (terms[2] + terms[3])


def _in_proj_kernel(*refs, first, even, nb_ctx):
    it = iter(refs)
    if first:
        xa_ref, xb_ref = next(it), next(it)
    else:
        xp_ref = next(it)
        y_refs = [next(it) for _ in range(TOP_K)]
        wt_ref, modp_ref = next(it), next(it)
    gain_ref, mod_ref, w_ref = next(it), next(it), next(it)
    if even:
        wlr_ref, gkw_ref, gkb_ref = next(it), next(it), next(it)
    proj_ref = next(it)
    if even:
        g_ref = next(it)
    xcur_ref = next(it)

    D = D_MODEL
    i = pl.program_id(0)
    if first:
        x = jnp.where(i < nb_ctx, xa_ref[...], xb_ref[...])
    else:
        x = xp_ref[...] + modp_ref[0][:, 5 * D:6 * D] * _moe_combine(y_refs, wt_ref)
    xcur_ref[...] = x
    m = mod_ref[0]
    hn = _rms(x, gain_ref[...]) * (1.0 + m[:, D:2 * D]) + m[:, 0:D]
    proj_ref[...] = _dot(hn, w_ref[...])
    if even:
        z = _dot(_dot(hn, wlr_ref[...]), gkw_ref[...]) + gkb_ref[...]
        g_ref[...] = (jnp.minimum(z, 0.0) - jnp.log(1.0 + jnp.exp(-jnp.abs(z)))) * (1.0 / GLA_NORMALIZER)


def _cond_row(i, nb_ctx, tiles_per_lat_seq):
    return jnp.where(i < nb_ctx, 0, 1 + (i - nb_ctx) // tiles_per_lat_seq)


def _in_proj(layer, n_ctx, n_tok, lat_seq, mods, gain, w_main, *, x_parts=None, x_prev=None, y4=None, even_extra=None):
    first = x_parts is not None
    even = even_extra is not None
    tm = BIG_TOKEN_TILE if first else TOKEN_TILE
    nb = n_tok // tm
    nb_ctx = n_ctx // tm
    tpl = lat_seq // tm
    np_cols = w_main.shape[1]
    D = D_MODEL

    def mod_map(l):
        return lambda i: (l * 8 + _cond_row(i, nb_ctx, tpl), 0, 0)

    row = lambda i: (i, 0)
    args, specs = [], []
    if first:
        xa, xb = x_parts
        args += [xa, xb]
        specs += [pl.BlockSpec((tm, D), lambda i: (jnp.minimum(i, nb_ctx - 1), 0)),
                  pl.BlockSpec((tm, D), lambda i: (jnp.maximum(i - nb_ctx, 0), 0))]
    else:
        y4, wts = y4
        args += [x_prev] + [y4] * TOP_K + [wts, mods]
        specs += [pl.BlockSpec((tm, D), row)]
        specs += [pl.BlockSpec((tm * ROW_UNITS, LANES), (lambda k: (lambda i: (k * nb + i, 0)))(k))
                  for k in range(TOP_K)]
        specs += [pl.BlockSpec((tm, LANES), row), pl.BlockSpec((1, 1, 6 * D), mod_map(layer - 1))]
    args += [gain.reshape(1, D), mods, w_main]
    specs += [_resident((1, D)), pl.BlockSpec((1, 1, 6 * D), mod_map(layer)), _resident((D, np_cols))]
    out_shapes = [jax.ShapeDtypeStruct((n_tok, np_cols), F32)]
    out_specs = [pl.BlockSpec((tm, np_cols), row)]
    if even:
        w_lr, gkw, gkb = even_extra
        args += [w_lr, gkw, gkb]
        specs += [_resident(w_lr.shape), _resident(gkw.shape), _resident(gkb.shape)]
        out_shapes.append(jax.ShapeDtypeStruct((n_tok, 2 * GLA_HEADS * GLA_DK), F32))
        out_specs.append(pl.BlockSpec((tm, 2 * GLA_HEADS * GLA_DK), row))
    out_shapes.append(jax.ShapeDtypeStruct((n_tok, D), F32))
    out_specs.append(pl.BlockSpec((tm, D), row))
    return pl.pallas_call(
        functools.partial(_in_proj_kernel, first=first, even=even, nb_ctx=nb_ctx),
        grid=(nb,),
        in_specs=specs,
        out_specs=out_specs,
        out_shape=out_shapes,
        compiler_params=_cparams(("arbitrary",)),
        name=f"in_proj_l{layer}",
    )(*args)


def _tri(c, lower):
    r = lax.broadcasted_iota(jnp.int32, (c, c), 0)
    s = lax.broadcasted_iota(jnp.int32, (c, c), 1)
    return (r >= s) if lower else (r <= s)


def _cumsum_mm(tri_bf16, g):
    g_hi = g.astype(BF16)
    g_lo = (g - g_hi.astype(F32)).astype(BF16)
    r = _dot(tri_bf16, jnp.concatenate([g_hi, g_lo], axis=1))
    w = g.shape[1]
    return r[:, :w] + r[:, w:]


def _gated_scan_kernel(*refs, variant, T, nseq, has_s0, want_final, hpb):
    C = SCAN_CHUNK
    ncs = T // C
    nc = nseq * ncs
    it = iter(refs)
    if variant == "gla":
        q_ref, k_ref, v_ref, og_ref, gf_ref, gb_ref, gain_ref = (next(it) for _ in range(7))
    else:
        q_ref, ff_ref, fb_ref, v_ref, og_ref, lbl_ref, gain_ref = (next(it) for _ in range(7))
    s0_ref = next(it) if has_s0 else None
    y_ref = next(it)
    sfin_ref = next(it) if want_final else None
    qi_scr, o_scr, u_scr, dec_scr, sin_scr = (next(it) for _ in range(5))

    lane = lax.broadcasted_iota(jnp.int32, (1, LANES), 1)
    if hpb == 2:
        masks = [lane < GLA_DK, lane >= GLA_DK]
    else:
        masks = [None]
    tri_l, tri_u = _tri(C, True), _tri(C, False)
    tri_l16, tri_u16 = tri_l.astype(F32).astype(BF16), tri_u.astype(F32).astype(BF16)
    mid_f, mid_b = C // 2 - 1, C // 2

    if variant == "hgrn":
        lgs = [lbl_ref[l] for l in range(DEPTH)]
        mx = functools.reduce(jnp.maximum, lgs)
        es = [jnp.exp(l - mx) for l in lgs]
        tot = functools.reduce(lambda a, b: a + b, es)
        ps = [e / tot for e in es]
        layer = DEPTH - 1
        lb = functools.reduce(lambda a, b: a + b, ps[:layer + 1]) - ps[0]
        lb_f, lb_b = lb[0:1], lb[1:2]

    for n in range(nc):
        rows = pl.ds(n * C, C)
        if variant == "gla":
            q = q_ref[rows, :] * (GLA_DK ** -0.5)
            kf = kb = k_ref[rows, :]
            gf, gb = gf_ref[rows, :], gb_ref[rows, :]
        else:
            q = _silu(q_ref[rows, :])
            f_f = lb_f + (1.0 - lb_f) * _sigmoid(ff_ref[rows, :])
            f_b = lb_b + (1.0 - lb_b) * _sigmoid(fb_ref[rows, :])
            kf, kb = 1.0 - f_f, 1.0 - f_b
            gf, gb = jnp.log(f_f), jnp.log(f_b)
        bf = _cumsum_mm(tri_l16, gf)
        bb = _cumsum_mm(tri_u16, gb)
        bf_mid, bf_end = bf[mid_f:mid_f + 1], bf[C - 1:C]
        bb_mid, bb_end = bb[mid_b:mid_b + 1], bb[0:1]
        qd_f, kd_f = q * jnp.exp(bf - bf_mid), kf * jnp.exp(bf_mid - bf)
        qd_b, kd_b = q * jnp.exp(bb - bb_mid), kb * jnp.exp(bb_mid - bb)
        kend = jnp.concatenate([kf * jnp.exp(bf_end - bf), kb * jnp.exp(bb_end - bb)], axis=1)
        qi_scr[rows, :] = jnp.concatenate([q * jnp.exp(bf), q * jnp.exp(bb)], axis=1)
        dec_scr[n] = jnp.broadcast_to(jnp.concatenate([jnp.exp(bf_end), jnp.exp(bb_end)], axis=1), (8, 2 * LANES))
        for h in range(hpb):
            v = v_ref[rows, h * LANES:(h + 1) * LANES]
            if masks[h] is None:
                qf_h, qb_h, kend_h = qd_f, qd_b, kend
            else:
                qf_h, qb_h = jnp.where(masks[h], qd_f, 0.0), jnp.where(masks[h], qd_b, 0.0)
                kend_h = jnp.where(jnp.concatenate([masks[h], masks[h]], axis=1), kend, 0.0)
            s = jnp.where(tri_l, _dot_nt(qf_h, kd_f), 0.0) + jnp.where(tri_u, _dot_nt(qb_h, kd_b), 0.0)
            o_scr[rows, h * LANES:(h + 1) * LANES] = _dot(s, v)
            u_scr[h, n] = _dot(v.T, kend_h)

    for sq in range(nseq):
        for h in range(hpb):
            if has_s0:
                s_f, s_b = s0_ref[sq, 0, 0].T, s0_ref[sq, 1, 0].T
                if masks[h] is not None:
                    s_f, s_b = jnp.where(masks[h], s_f, 0.0), jnp.where(masks[h], s_b, 0.0)
            else:
                s_f = s_b = jnp.zeros((LANES, LANES), F32)
            for n in range(sq * ncs, (sq + 1) * ncs):
                sin_scr[h, n, :, 0:LANES] = s_f
                s_f = s_f * dec_scr[n, 0:1, 0:LANES] + u_scr[h, n, :, 0:LANES]
            for n in reversed(range(sq * ncs, (sq + 1) * ncs)):
                sin_scr[h, n, :, LANES:2 * LANES] = s_b
                s_b = s_b * dec_scr[n, 0:1, LANES:2 * LANES] + u_scr[h, n, :, LANES:2 * LANES]
            if want_final:
                if h == 0:
                    fin_f, fin_b = s_f, s_b
                else:
                    fin_f, fin_b = fin_f + s_f, fin_b + s_b
        if want_final:
            sfin_ref[sq, 0, 0] = fin_f.T
            sfin_ref[sq, 1, 0] = fin_b.T

    gain = gain_ref[...]
    for n in range(nc):
        rows = pl.ds(n * C, C)
        qi = qi_scr[rows, :]
        for h in range(hpb):
            cols = slice(h * LANES, (h + 1) * LANES)
            o = o_scr[rows, cols] + _dot_nt(qi, sin_scr[h, n])
            y_ref[rows, cols] = (_rms(o, gain) * _silu(og_ref[rows, cols])).astype(y_ref.dtype)


def _scan_scratch(T, hpb):
    nc = T // SCAN_CHUNK
    return [
        pltpu.VMEM((T, 2 * LANES), F32),
        pltpu.VMEM((T, hpb * LANES), F32),
        pltpu.VMEM((hpb, nc, LANES, 2 * LANES), F32),
        pltpu.VMEM((nc, 8, 2 * LANES), F32),
        pltpu.VMEM((hpb, nc, LANES, 2 * LANES), F32),
    ]


def _seqs_per_step(B, T):
    nseq = max(1, min(B, SCAN_ROWS // T))
    assert B % nseq == 0
    return nseq


def _gla_scan(proj, gdec, gain, s0, row0, B, T, want_final):
    nseq = _seqs_per_step(B, T)
    R = nseq * T
    rb0 = row0 // R
    assert row0 % R == 0
    has_s0 = s0 is not None

    def col(base, width=LANES):
        return lambda b, j: (rb0 + b, base // width + j)

    args = [proj, proj, proj, proj, gdec, gdec, gain.reshape(1, GLA_DV)]
    specs = [pl.BlockSpec((R, LANES), col(_GQ)), pl.BlockSpec((R, LANES), col(_GK)),
             pl.BlockSpec((R, 2 * LANES), col(_GV, 2 * LANES)), pl.BlockSpec((R, 2 * LANES), col(_GG, 2 * LANES)),
             pl.BlockSpec((R, LANES), col(0)), pl.BlockSpec((R, LANES), col(GLA_HEADS * GLA_DK)),
             pl.BlockSpec((1, GLA_DV), lambda b, j: (0, 0))]
    st_spec = pl.BlockSpec((nseq, 2, 1, LANES, LANES), lambda b, j: (b, 0, j, 0, 0))
    if has_s0:
        args.append(s0.reshape(B, 2, GLA_HEADS // 2, 2 * GLA_DK, GLA_DV))
        specs.append(st_spec)
    out_shapes = [jax.ShapeDtypeStruct((B * T, GLA_HEADS * GLA_DV), BF16)]
    out_specs = [pl.BlockSpec((R, 2 * LANES), lambda b, j: (b, j))]
    if want_final:
        out_shapes.append(jax.ShapeDtypeStruct((B, 2, GLA_HEADS // 2, 2 * GLA_DK, GLA_DV), F32))
        out_specs.append(st_spec)
    kern = functools.partial(_gated_scan_kernel, variant="gla", T=T, nseq=nseq, has_s0=has_s0,
                             want_final=want_final, hpb=2)
    return pl.pallas_call(
        kern, grid=(B // nseq, GLA_HEADS // 2), in_specs=specs, out_specs=out_specs, out_shape=out_shapes,
        scratch_shapes=_scan_scratch(R, 2),
        compiler_params=_cparams(("arbitrary", "arbitrary")), name=f"gla_scan_T{T}",
    )(*args)


def _hgrn_scan(proj, lb_logits, gain, s0, row0, B, T, want_final):
    nseq = _seqs_per_step(B, T)
    R = nseq * T
    rb0 = row0 // R
    assert row0 % R == 0
    has_s0 = s0 is not None

    def col(base):
        return lambda b, j: (rb0 + b, base // LANES + j)

    args = [proj, proj, proj, proj, proj, lb_logits, gain.reshape(1, HG_DV)]
    specs = [pl.BlockSpec((R, LANES), col(_HQ)), pl.BlockSpec((R, LANES), col(_HFF)),
             pl.BlockSpec((R, LANES), col(_HFB)), pl.BlockSpec((R, LANES), col(_HI)),
             pl.BlockSpec((R, LANES), col(_HG)),
             pl.BlockSpec((DEPTH, 2, LANES), lambda b, j: (0, 0, j)),
             pl.BlockSpec((1, HG_DV), lambda b, j: (0, 0))]
    st_spec = pl.BlockSpec((nseq, 2, 1, LANES, LANES), lambda b, j: (b, 0, j, 0, 0))
    if has_s0:
        args.append(s0.reshape(B, 2, HG_HEADS, HG_DK, HG_DV))
        specs.append(st_spec)
    out_shapes = [jax.ShapeDtypeStruct((B * T, HG_HEADS * HG_DV), BF16)]
    out_specs = [pl.BlockSpec((R, LANES), lambda b, j: (b, j))]
    if want_final:
        out_shapes.append(jax.ShapeDtypeStruct((B, 2, HG_HEADS, HG_DK, HG_DV), F32))
        out_specs.append(st_spec)
    kern = functools.partial(_gated_scan_kernel, variant="hgrn", T=T, nseq=nseq, has_s0=has_s0,
                             want_final=want_final, hpb=1)
    return pl.pallas_call(
        kern, grid=(B // nseq, HG_HEADS), in_specs=specs, out_specs=out_specs, out_shape=out_shapes,
        scratch_shapes=_scan_scratch(R, 1),
        compiler_params=_cparams(("arbitrary", "arbitrary")), name=f"hgrn_scan_T{T}",
    )(*args)


def _ret_scan_kernel(*refs, T, nseq, has_s0, want_final, rope):
    C = SCAN_CHUNK
    ncs = T // C
    nc = nseq * ncs
    it = iter(refs)
    q_ref, k_ref, v_ref, og_ref, dexp_ref = (next(it) for _ in range(5))
    if rope:
        cos_ref, sin_ref = next(it), next(it)
    s0_ref = next(it) if has_s0 else None
    y_ref = next(it)
    sfin_ref = next(it) if want_final else None
    qi_scr, o_scr, u_scr, sin_scr = (next(it) for _ in range(4))

    lg = jnp.log1p(-jnp.exp2(-dexp_ref[0]))
    lg_f, lg_b = lg[0:1], lg[1:2]
    r = lax.broadcasted_iota(jnp.int32, (C, C), 0)
    s = lax.broadcasted_iota(jnp.int32, (C, C), 1)
    dist = (r - s).astype(F32)
    dmask = (jnp.where(r >= s, jnp.exp(jnp.maximum(dist, 0.0) * lg_f[:, 0:1]), 0.0)
             + jnp.where(r <= s, jnp.exp(jnp.maximum(-dist, 0.0) * lg_b[:, 0:1]), 0.0))
    pos = lax.broadcasted_iota(jnp.int32, (C, LANES), 0).astype(F32)
    xi = jnp.concatenate([jnp.exp((pos + 1.0) * lg_f), jnp.exp((C - pos) * lg_b)], axis=1)
    zeta = jnp.concatenate([jnp.exp((C - 1.0 - pos) * lg_f), jnp.exp(pos * lg_b)], axis=1)
    d_f, d_b = jnp.exp(C * lg_f), jnp.exp(C * lg_b)

    def rot(x, seq_rows):
        if not rope:
            return x
        return x * cos_ref[seq_rows, :] + pltpu.roll(x, RET_DK // 2, axis=1) * sin_ref[seq_rows, :]

    for n in range(nc):
        rows = pl.ds(n * C, C)
        seq_rows = pl.ds((n % ncs) * C, C)
        q = rot(q_ref[rows, :], seq_rows)
        k = rot(k_ref[rows, :] * (RET_DK ** -0.5), seq_rows)
        v = v_ref[rows, :]
        o_scr[rows, :] = _dot(_dot_nt(q, k) * dmask, v)
        qi_scr[rows, :] = jnp.concatenate([q, q], axis=1) * xi
        u_scr[n] = _dot(v.T, jnp.concatenate([k, k], axis=1) * zeta)

    for sq in range(nseq):
        if has_s0:
            s_f, s_b = s0_ref[sq, 0, 0].T, s0_ref[sq, 1, 0].T
        else:
            s_f = s_b = jnp.zeros((LANES, LANES), F32)
        for n in range(sq * ncs, (sq + 1) * ncs):
            sin_scr[n, :, 0:LANES] = s_f
            s_f = s_f * d_f + u_scr[n, :, 0:LANES]
        for n in reversed(range(sq * ncs, (sq + 1) * ncs)):
            sin_scr[n, :, LANES:2 * LANES] = s_b
            s_b = s_b * d_b + u_scr[n, :, LANES:2 * LANES]
        if want_final:
            sfin_ref[sq, 0, 0] = s_f.T
            sfin_ref[sq, 1, 0] = s_b.T

    for n in range(nc):
        rows = pl.ds(n * C, C)
        o = o_scr[rows, :] + _dot_nt(qi_scr[rows, :], sin_scr[n])
        y_ref[rows, :] = (_rms(o) * _silu(og_ref[rows, :])).astype(y_ref.dtype)


def _ret_scan(proj, dexp, rope_tabs, s0, row0, B, T, want_final):
    nseq = _seqs_per_step(B, T)
    R = nseq * T
    rb0 = row0 // R
    assert row0 % R == 0
    nc = R // SCAN_CHUNK
    has_s0 = s0 is not None
    rope = rope_tabs is not None

    def col(base):
        return lambda b, j: (rb0 + b, base // LANES + j)

    args = [proj, proj, proj, proj, dexp]
    specs = [pl.BlockSpec((R, LANES), col(_RQ)), pl.BlockSpec((R, LANES), col(_RK)),
             pl.BlockSpec((R, LANES), col(_RV)), pl.BlockSpec((R, LANES), col(_RG)),
             pl.BlockSpec((1, 2, LANES), lambda b, j: (j, 0, 0))]
    if rope:
        args += list(rope_tabs)
        specs += [pl.BlockSpec((T, LANES), lambda b, j: (0, 0))] * 2
    st_spec = pl.BlockSpec((nseq, 2, 1, LANES, LANES), lambda b, j: (b, 0, j, 0, 0))
    if has_s0:
        args.append(s0.reshape(B, 2, RET_HEADS, RET_DK, RET_DV))
        specs.append(st_spec)
    out_shapes = [jax.ShapeDtypeStruct((B * T, RET_HEADS * RET_DV), BF16)]
    out_specs = [pl.BlockSpec((R, LANES), lambda b, j: (b, j))]
    if want_final:
        out_shapes.append(jax.ShapeDtypeStruct((B, 2, RET_HEADS, RET_DK, RET_DV), F32))
        out_specs.append(st_spec)
    kern = functools.partial(_ret_scan_kernel, T=T, nseq=nseq, has_s0=has_s0, want_final=want_final, rope=rope)
    scratch = [pltpu.VMEM((R, 2 * LANES), F32), pltpu.VMEM((R, LANES), F32),
               pltpu.VMEM((nc, LANES, 2 * LANES), F32), pltpu.VMEM((nc, LANES, 2 * LANES), F32)]
    return pl.pallas_call(
        kern, grid=(B // nseq, RET_HEADS), in_specs=specs, out_specs=out_specs, out_shape=out_shapes,
        scratch_shapes=scratch,
        compiler_params=_cparams(("arbitrary", "arbitrary")), name=f"ret_scan_T{T}",
    )(*args)


def _out_proj_kernel(*refs, n_mix, nb_ctx):
    it = iter(refs)
    y_refs = [(next(it), next(it)) for _ in range(n_mix)]
    wo_ref, x_ref, mod_ref, gain_ref, rw_ref, rb_ref = (next(it) for _ in range(6))
    xnew_ref, hn_ref, idx_ref, wt_ref, rank_ref, cnt_ref = (next(it) for _ in range(6))
    cnt_scr = next(it)
    D = D_MODEL
    is_ctx = pl.program_id(0) < nb_ctx
    mix = None
    r0 = 0
    for ya_ref, yb_ref in y_refs:
        w = ya_ref.shape[1]
        part = _dot(jnp.where(is_ctx, ya_ref[...], yb_ref[...]), wo_ref[r0:r0 + w, :])
        mix = part if mix is None else mix + part
        r0 += w
    m = mod_ref[0]
    xn = x_ref[...] + m[:, 2 * D:3 * D] * mix
    xnew_ref[...] = xn
    hn = _rms(xn, gain_ref[...]) * (1.0 + m[:, 4 * D:5 * D]) + m[:, 3 * D:4 * D]
    hn_hi = hn.astype(BF16)
    _rows_to_units(hn_ref, hn)
    hn_lo = (hn - hn_hi.astype(F32)).astype(BF16)
    rw = rw_ref[...]
    rw_hi = rw.astype(BF16)
    rw_lo = (rw - rw_hi.astype(F32)).astype(BF16)
    logits = (_dot(hn_hi, rw_hi) + (_dot(hn_lo, rw_hi) + _dot(hn_hi, rw_lo))) + rb_ref[...]
    lane = lax.broadcasted_iota(jnp.int32, logits.shape, 1).astype(F32)
    vals, idxs = [], []
    cur = logits
    for _ in range(TOP_K):
        mx = jnp.max(cur, axis=-1, keepdims=True)
        ik = jnp.min(jnp.where(cur == mx, lane, float(LANES)), axis=-1, keepdims=True)
        vals.append(mx)
        idxs.append(ik)
        cur = jnp.where(lane == ik, -jnp.inf, cur)
    es = [jnp.exp(v - vals[0]) for v in vals]
    tot = (es[0] + es[1]) + (es[2] + es[3])
    idx_out = jnp.zeros(logits.shape, F32)
    wt_out = jnp.zeros(logits.shape, F32)
    for k in range(TOP_K):
        idx_out = jnp.where(lane == float(k), idxs[k], idx_out)
        wt_out = jnp.where(lane == float(k), es[k] / tot, wt_out)
    idx_ref[...] = idx_out.astype(jnp.int32)
    wt_ref[...] = wt_out

    @pl.when(pl.program_id(0) == 0)
    def _():
        cnt_scr[...] = jnp.zeros(cnt_scr.shape, F32)

    tm = logits.shape[0]
    hits = [lane == idxs[k] for k in range(TOP_K)]
    sel = jnp.zeros(logits.shape, F32)
    for k in range(TOP_K):
        sel = sel + jnp.where(hits[k], 1.0, 0.0)
    rr = lax.broadcasted_iota(jnp.int32, (tm, tm), 0)
    cc = lax.broadcasted_iota(jnp.int32, (tm, tm), 1)
    before = jnp.where(rr > cc, 1.0, 0.0).astype(BF16)
    rank_all = cnt_scr[0:1, :] + _dot(before, sel.astype(BF16))
    rank_out = jnp.zeros(logits.shape, F32)
    for k in range(TOP_K):
        rk = jnp.sum(jnp.where(hits[k], rank_all, 0.0), axis=-1, keepdims=True)
        rank_out = jnp.where(lane == float(k), rk, rank_out)
    rank_ref[...] = rank_out.astype(jnp.int32)
    total = cnt_scr[...] + jnp.sum(sel, axis=0, keepdims=True)
    cnt_scr[...] = total
    cnt_ref[...] = total.astype(jnp.int32)


def _out_proj(layer, n_ctx, n_tok, lat_seq, ys, w_out, x_cur, mods, gain, rw_pad, rb_pad):
    tm = BIG_TOKEN_TILE
    nb, nb_ctx, tpl = n_tok // tm, n_ctx // tm, lat_seq // tm
    D = D_MODEL
    row = lambda i: (i, 0)
    specs = []
    for ya, _ in ys:
        specs += [pl.BlockSpec((tm, ya.shape[1]), lambda i: (jnp.minimum(i, nb_ctx - 1), 0)),
                  pl.BlockSpec((tm, ya.shape[1]), lambda i: (jnp.maximum(i - nb_ctx, 0), 0))]
    specs += [_resident((D, D)), pl.BlockSpec((tm, D), row),
              pl.BlockSpec((1, 1, 6 * D), lambda i: (layer * 8 + _cond_row(i, nb_ctx, tpl), 0, 0)),
              _resident((1, D)), _resident((D, LANES)), _resident((1, LANES))]
    return pl.pallas_call(
        functools.partial(_out_proj_kernel, n_mix=len(ys), nb_ctx=nb_ctx),
        grid=(nb,),
        in_specs=specs,
        out_specs=[pl.BlockSpec((tm, D), row), pl.BlockSpec((tm * ROW_UNITS, LANES), row),
                   pl.BlockSpec((tm, LANES), row), pl.BlockSpec((tm, LANES), row),
                   pl.BlockSpec((tm, LANES), row), pl.BlockSpec((8, LANES), lambda i: (0, 0))],
        out_shape=[jax.ShapeDtypeStruct((n_tok, D), F32), jax.ShapeDtypeStruct((n_tok * ROW_UNITS, LANES), F32),
                   jax.ShapeDtypeStruct((n_tok, LANES), jnp.int32), jax.ShapeDtypeStruct((n_tok, LANES), F32),
                   jax.ShapeDtypeStruct((n_tok, LANES), jnp.int32), jax.ShapeDtypeStruct((8, LANES), jnp.int32)],
        scratch_shapes=[pltpu.VMEM((8, LANES), F32)],
        compiler_params=_cparams(("arbitrary",)),
        name=f"out_proj_l{layer}",
    )(*[y for pair in ys for y in pair], w_out, x_cur, mods, gain.reshape(1, D), rw_pad, rb_pad)


W1_SPLIT = 4
W2_SPLIT = 2


def _moe_kernel(seg_ref, nt_ref, code_hbm, x_hbm, w1_hbm, b1_ref, w2_hbm, b2_ref, y_hbm, w1_scr, b1_scr, w2_scr,
                act_scr, xbuf, obuf, w1buf, w2buf, code_smem, csem, gsem, ssem, wsem, *, n_tok, layer):
    tm = MOE_TILE
    U = ROW_UNITS
    n_code_tiles = code_hbm.shape[0] // tm
    e = pl.program_id(0)
    nt = nt_ref[e]
    g_first = seg_ref[e] // tm
    wslot = lax.rem(e, 2)

    def weight_copies(ex, ws):
        cw, rh = 2 * D_FF // W1_SPLIT, D_FF // W2_SPLIT
        cps = [pltpu.make_async_copy(w1_hbm.at[layer, ex, :, pl.ds(c * cw, cw)],
                                     w1buf.at[ws, :, pl.ds(c * cw, cw)], wsem.at[ws]) for c in range(W1_SPLIT)]
        cps += [pltpu.make_async_copy(w2_hbm.at[layer, ex, pl.ds(c * rh, rh), :],
                                      w2buf.at[ws, pl.ds(c * rh, rh), :], wsem.at[ws]) for c in range(W2_SPLIT)]
        return cps

    @pl.when(e == 0)
    def _():
        for cp in weight_copies(0, 0):
            cp.start(priority=1)

    @pl.when(e + 1 < N_EXPERTS)
    def _():
        for cp in weight_copies(e + 1, 1 - wslot):
            cp.start(priority=1)

    for cp in weight_copies(e, wslot):
        cp.wait()

    GATHER, SCATTER = 0, 1

    def code_copy(kind, t, p):
        first = pl.multiple_of(jnp.clip(t, 0, n_code_tiles - 1) * tm, tm)
        return pltpu.make_async_copy(code_hbm.at[pl.ds(first, tm)], code_smem.at[kind, p], csem.at[kind, p])

    def gather_start(b, p):
        for r in range(tm):
            tok = jnp.bitwise_and(code_smem[GATHER, p, r], n_tok - 1)
            pltpu.make_async_copy(x_hbm.at[pl.ds(pl.multiple_of(tok * U, U), U)],
                                  xbuf.at[b, pl.ds(r * U, U)], gsem.at[b]).start()

    def gather_wait(b):
        pltpu.make_async_copy(x_hbm.at[pl.ds(0, tm * U)], xbuf.at[b], gsem.at[b]).wait()

    def scatter_start(b, p):
        for r in range(tm):
            dst = code_smem[SCATTER, p, r]
            pltpu.make_async_copy(obuf.at[b, pl.ds(r * U, U)],
                                  y_hbm.at[pl.ds(pl.multiple_of(dst * U, U), U)], ssem.at[b]).start()

    def scatter_wait(b):
        pltpu.make_async_copy(obuf.at[b], y_hbm.at[pl.ds(0, tm * U)], ssem.at[b]).wait()

    @pl.when(e == 0)
    def _():
        obuf[...] = jnp.zeros(obuf.shape, obuf.dtype)
        for h in range(2):
            cp = pltpu.make_async_copy(obuf.at[0], y_hbm.at[pl.ds((TOP_K * n_tok + h * tm) * U, tm * U)], ssem.at[0])
            cp.start()
            cp.wait()
        code_copy(GATHER, 0, 0).start()
        code_copy(GATHER, 1, 1).start()
        code_copy(SCATTER, n_code_tiles - 1, 1).start()
        code_copy(SCATTER, 0, 0).start()
        code_copy(GATHER, 0, 0).wait()
        gather_start(0, 0)

    @pl.when(nt > 0)
    def _():
        grp = 2 * LANES
        rr = lax.broadcasted_iota(jnp.int32, (grp, grp), 0)
        cc = lax.broadcasted_iota(jnp.int32, (grp, grp), 1)
        src = jnp.where(cc < LANES, 2 * cc, 2 * (cc - LANES) + 1)
        perm = jnp.where(rr == src, 1.0, 0.0).astype(BF16)
        for j in range(2 * D_FF // grp):
            cols = slice(j * grp, (j + 1) * grp)
            w1_scr[:, cols] = _dot(w1buf[wslot, :, cols].astype(BF16), perm).astype(BF16)
        bias = jnp.broadcast_to(b1_ref[0, 0], (8, 2 * D_FF))
        for j in range(2 * D_FF // grp):
            cols = slice(j * grp, (j + 1) * grp)
            rest = bias[:, cols]
            acc = jnp.zeros((8, grp), F32)
            for _ in range(3):
                term = rest.astype(BF16)
                acc = acc + _dot(term, perm)
                rest = rest - term.astype(F32)
            b1_scr[:, cols] = acc
        w2_scr[...] = w2buf[wslot].astype(BF16)
        blk = 4 * LANES

        def tile(g, b):
            o = 1 - b

            @pl.when(g >= 0)
            def _():
                code_copy(GATHER, g + 1, o).wait()
                code_copy(SCATTER, g - 1, o).wait()
                gather_start(o, o)
                scatter_start(o, o)
                code_copy(GATHER, g + 2, b).start()
                code_copy(SCATTER, g + 1, o).start()

            gather_wait(b)
            x = _rows_from_units(xbuf.at[b], tm).astype(BF16)
            for c in range(2 * D_FF // blk):
                cols = slice(c * blk, (c + 1) * blk)
                hid = _dot(x, w1_scr[:, cols]) + b1_scr[0:1, cols]
                acts = []
                for j in range(blk // grp):
                    glu = jnp.minimum(hid[:, j * grp:j * grp + LANES], SWIGLU_LIMIT)
                    lin = jnp.clip(hid[:, j * grp + LANES:(j + 1) * grp], -SWIGLU_LIMIT, SWIGLU_LIMIT)
                    acts.append(glu * _sigmoid(SWIGLU_ALPHA * glu) * (lin + 1.0))
                act_scr[:, c * (blk // 2):(c + 1) * (blk // 2)] = jnp.concatenate(acts, axis=1).astype(BF16)
            y = _dot(act_scr[...], w2_scr[...]) + b2_ref[0, 0]

            @pl.when(g >= 1)
            def _():
                scatter_wait(b)

            _rows_to_units(obuf.at[b], y)

        g_end = g_first + nt

        def tile_pair(m, carry):
            for b in range(2):
                g = 2 * m + b

                @pl.when(jnp.logical_and(g >= g_first, g < g_end))
                def _():
                    tile(g, b)

            return carry

        lax.fori_loop(g_first // 2, (g_end + 1) // 2, tile_pair, 0)

    @pl.when(e == N_EXPERTS - 1)
    def _():
        g_end = g_first + nt
        last = lax.rem(g_end + 1, 2)
        code_copy(SCATTER, g_end - 1, last).wait()
        scatter_start(last, last)
        scatter_wait(last)
        scatter_wait(1 - last)
        gather_wait(1 - last)
        code_copy(GATHER, g_end + 1, last).wait()
        code_copy(SCATTER, g_end, 1 - last).wait()


def _moe_experts(layer, n_tok, seg_start, tiles_per_e, code, x_units, w1, b1, w2, b2):
    tm = MOE_TILE
    D, F2, U = D_MODEL, 2 * D_FF, ROW_UNITS
    any_space = pl.BlockSpec(memory_space=pl.ANY)
    grid_spec = pltpu.PrefetchScalarGridSpec(
        num_scalar_prefetch=2,
        grid=(N_EXPERTS,),
        in_specs=[any_space, any_space,
                  any_space, pl.BlockSpec((1, 1, 1, F2), lambda e, sg, nt: (layer, e, 0, 0)),
                  any_space, pl.BlockSpec((1, 1, 1, D), lambda e, sg, nt: (layer, e, 0, 0))],
        out_specs=any_space,
        scratch_shapes=[pltpu.VMEM((D, F2), BF16), pltpu.VMEM((8, F2), F32), pltpu.VMEM((D_FF, D), BF16),
                        pltpu.VMEM((tm, D_FF), BF16),
                        pltpu.VMEM((2, tm * U, LANES), F32), pltpu.VMEM((2, tm * U, LANES), F32),
                        pltpu.VMEM((2, D, F2), F32), pltpu.VMEM((2, D_FF, D), F32),
                        pltpu.SMEM((2, 2, tm), jnp.int32),
                        pltpu.SemaphoreType.DMA((2, 2)), pltpu.SemaphoreType.DMA((2,)), pltpu.SemaphoreType.DMA((2,)),
                        pltpu.SemaphoreType.DMA((2,))],
    )
    n_out_rows = TOP_K * n_tok + 2 * tm
    return pl.pallas_call(
        functools.partial(_moe_kernel, n_tok=n_tok, layer=layer),
        grid_spec=grid_spec,
        out_shape=jax.ShapeDtypeStruct((n_out_rows * U, LANES), F32),
        compiler_params=_cparams(("arbitrary",)),
        name="moe_experts",
    )(seg_start, tiles_per_e, code, x_units, w1, b1.reshape(DEPTH, N_EXPERTS, 1, F2),
      w2, b2.reshape(DEPTH, N_EXPERTS, 1, D))


def _pos_kernel(idx_ref, rank_ref, seg_ref, pos_ref):
    lane = lax.broadcasted_iota(jnp.int32, idx_ref.shape, 1)
    idx, rank, seg = idx_ref[...], rank_ref[...], seg_ref[...]
    out = jnp.zeros(idx.shape, jnp.int32)
    for k in range(TOP_K):
        hit = lane == idx[:, k:k + 1]
        seg_k = jnp.sum(jnp.where(hit, seg, 0.0), axis=-1, keepdims=True).astype(jnp.int32)
        out = jnp.where(lane == k, seg_k + rank[:, k:k + 1], out)
    pos_ref[...] = out


def _pair_slots(idx, rank, seg_start):
    n_tok = idx.shape[0]
    tm = TOKEN_TILE
    row = lambda i: (i, 0)
    return pl.pallas_call(
        _pos_kernel,
        grid=(n_tok // tm,),
        in_specs=[pl.BlockSpec((tm, LANES), row), pl.BlockSpec((tm, LANES), row),
                  pl.BlockSpec((1, LANES), lambda i: (0, 0))],
        out_specs=pl.BlockSpec((tm, LANES), row),
        out_shape=jax.ShapeDtypeStruct((n_tok, LANES), jnp.int32),
        compiler_params=_cparams(("arbitrary",)),
        name="pair_slots",
    )(idx, rank, seg_start)


def _slot_code_kernel(pos_ref, unused_hbm, code_ref, sem, *, n_tok):
    i = pl.program_id(0)
    tm = pos_ref.shape[0] // TOP_K

    @pl.when(i == 0)
    def _():
        cp = pltpu.make_async_copy(unused_hbm, code_ref, sem)
        cp.start()
        cp.wait()

    base = i * tm
    for t in range(tm):
        for k in range(TOP_K):
            code_ref[pos_ref[t * TOP_K + k]] = base + (k * n_tok + t)


def _slot_codes(pos, unused):
    n_tok = pos.shape[0]
    tm = TOKEN_TILE
    return pl.pallas_call(
        functools.partial(_slot_code_kernel, n_tok=n_tok),
        grid=(n_tok // tm,),
        in_specs=[pl.BlockSpec((tm * TOP_K,), lambda i: (i,), memory_space=pltpu.SMEM),
                  pl.BlockSpec(memory_space=pl.ANY)],
        out_specs=pl.BlockSpec(memory_space=pltpu.SMEM),
        out_shape=jax.ShapeDtypeStruct(unused.shape, jnp.int32),
        scratch_shapes=[pltpu.SemaphoreType.DMA(())],
        compiler_params=_cparams(("arbitrary",)),
        name="slot_codes",
    )(pos.reshape(-1), unused)


def _moe_layer(layer, hn_units, idx, rank, counts, w1, b1, w2, b2):
    n_tok = hn_units.shape[0] // ROW_UNITS
    assert n_tok & (n_tok - 1) == 0
    tm = MOE_TILE
    n_pairs = n_tok * TOP_K
    n_slots = ((n_pairs + N_EXPERTS * (tm - 1)) // tm + 1) * tm
    tiles_per_e = (counts + tm - 1) // tm
    seg_start = (jnp.cumsum(tiles_per_e) - tiles_per_e) * tm
    seg_f32 = jnp.pad(seg_start.astype(F32), (0, LANES - N_EXPERTS)).reshape(1, LANES)
    pos = _pair_slots(idx, rank, seg_f32)[:, :TOP_K]
    unused = n_pairs + jnp.arange(n_slots, dtype=jnp.int32) % (2 * tm)
    code = _slot_codes(pos, unused)
    return _moe_experts(layer, n_tok, seg_start, tiles_per_e, code, hn_units, w1, b1, w2, b2)


def _final_kernel(*refs, nb_ctx):
    x_ref = refs[0]
    y_refs = refs[1:1 + TOP_K]
    wt_ref, mod_ref, gain_ref, oa_ref, ob_ref = refs[1 + TOP_K:]
    D = D_MODEL
    i = pl.program_id(0)
    x = x_ref[...] + mod_ref[0][:, 5 * D:6 * D] * _moe_combine(y_refs, wt_ref)
    out = _rms(x, gain_ref[...])

    @pl.when(i < nb_ctx)
    def _():
        oa_ref[...] = out

    @pl.when(i >= nb_ctx)
    def _():
        ob_ref[...] = out


def _final(n_ctx, n_tok, lat_seq, x_new, y4, wts, mods, gain):
    tm = TOKEN_TILE
    nb, nb_ctx, tpl = n_tok // tm, n_ctx // tm, lat_seq // tm
    D = D_MODEL
    specs = [pl.BlockSpec((tm, D), lambda i: (i, 0))]
    specs += [pl.BlockSpec((tm * ROW_UNITS, LANES), (lambda k: (lambda i: (k * nb + i, 0)))(k)) for k in range(TOP_K)]
    specs += [pl.BlockSpec((tm, LANES), lambda i: (i, 0)),
              pl.BlockSpec((1, 1, 6 * D), lambda i: ((DEPTH - 1) * 8 + _cond_row(i, nb_ctx, tpl), 0, 0)),
              _resident((1, D))]
    return pl.pallas_call(
        functools.partial(_final_kernel, nb_ctx=nb_ctx),
        grid=(nb,),
        in_specs=specs,
        out_specs=[pl.BlockSpec((tm, D), lambda i: (jnp.minimum(i, nb_ctx - 1), 0)),
                   pl.BlockSpec((tm, D), lambda i: (jnp.maximum(i - nb_ctx, 0), 0))],
        out_shape=[jax.ShapeDtypeStruct((n_ctx, D), F32), jax.ShapeDtypeStruct((n_tok - n_ctx, D), F32)],
        compiler_params=_cparams(("arbitrary",)),
        name="final_norm",
    )(x_new, *([y4] * TOP_K), wts, mods, gain.reshape(1, D))


def _rope_tables(n_tokens):
    rows = n_tokens // GRID_W
    r = jnp.repeat(jnp.arange(rows, dtype=F32), GRID_W)
    col = jnp.tile(jnp.arange(GRID_W, dtype=F32), rows)
    nf = RET_DK // 4
    inv = ROPE_BASE ** (-jnp.arange(nf, dtype=F32) / nf)
    ang = jnp.concatenate([r[:, None] * inv, col[:, None] * inv], axis=-1)
    cos, sin = jnp.cos(ang), jnp.sin(ang)
    return jnp.concatenate([cos, cos], axis=-1), jnp.concatenate([-sin, sin], axis=-1)


def kernel(x_prompt, x_sample, c, state_gla, state_ret, state_hgrn, c_ctx, norm_mix, norm_ffn, ada_w, ada_b,
           w_in_even, w_out_even, gla_gk_w, gla_gk_b, gla_gain, ret_decay_exp, w_in_odd, w_out_odd,
           hgrn_lb_logits, hgrn_gain, router_w, router_b, moe_w1, moe_b1, moe_w2, moe_b2, final_norm):
    D = D_MODEL
    B_ctx, T_ctx, _ = x_prompt.shape
    B_lat, T_lat, _ = x_sample.shape
    n_ctx, n_lat = B_ctx * T_ctx, B_lat * T_lat
    n_tok = n_ctx + n_lat
    assert n_ctx % BIG_TOKEN_TILE == 0 and T_lat % BIG_TOKEN_TILE == 0 and B_lat + 1 <= 8
    assert T_ctx % SCAN_CHUNK == 0 and T_lat % SCAN_CHUNK == 0 and n_ctx % T_lat == 0

    cond8 = jnp.concatenate([c_ctx[None, :], c, jnp.zeros((8 - 1 - B_lat, D), F32)], axis=0)
    mods = _ada_mods(cond8, ada_w, ada_b).reshape(DEPTH * 8, 1, 6 * D)

    rw_pad = jnp.pad(router_w, ((0, 0), (0, 0), (0, LANES - N_EXPERTS)))
    rb_pad = jnp.pad(router_b, ((0, 0), (0, LANES - N_EXPERTS)), constant_values=-1e30)

    w_even = w_in_even[0]
    lr0 = GLA_HEADS * (2 * GLA_DK + GLA_DV)
    lr1 = lr0 + 2 * GLA_LOW_RANK
    w_main = jnp.concatenate([w_even[:, :lr0], w_even[:, lr1:]], axis=1)
    w_lr = jnp.pad(w_even[:, lr0:lr1], ((0, 0), (0, LANES - 2 * GLA_LOW_RANK)))
    nqk = GLA_HEADS * GLA_DK
    gkw = jnp.zeros((LANES, 2 * nqk), F32)
    gkw = gkw.at[0:GLA_LOW_RANK, 0:nqk].set(gla_gk_w[0, 0])
    gkw = gkw.at[GLA_LOW_RANK:2 * GLA_LOW_RANK, nqk:].set(gla_gk_w[0, 1])
    gkb = gla_gk_b[0].reshape(1, 2 * nqk)
    proj, gdec, x_cur = _in_proj(0, n_ctx, n_tok, T_lat, mods, norm_mix[0], w_main,
                                 x_parts=(x_prompt.reshape(n_ctx, D), x_sample.reshape(n_lat, D)),
                                 even_extra=(w_lr, gkw, gkb))

    y_gla_c, fin_gla = _gla_scan(proj, gdec, gla_gain[0], None, 0, B_ctx, T_ctx, True)
    (y_gla_l,) = _gla_scan(proj, gdec, gla_gain[0], state_gla[:, 0], n_ctx, B_lat, T_lat, False)
    dexp = jnp.broadcast_to(ret_decay_exp[0].T[:, :, None], (RET_HEADS, 2, LANES))
    y_ret_c, fin_ret = _ret_scan(proj, dexp, None, None, 0, B_ctx, T_ctx, True)
    (y_ret_l,) = _ret_scan(proj, dexp, _rope_tables(T_lat), state_ret[:, 0], n_ctx, B_lat, T_lat, False)

    x_new, hn, idx, wts, rank, cnt = _out_proj(0, n_ctx, n_tok, T_lat, [(y_gla_c, y_gla_l), (y_ret_c, y_ret_l)],
                                               w_out_even[0], x_cur, mods, norm_ffn[0], rw_pad[0], rb_pad[0:1])
    y4 = _moe_layer(0, hn, idx, rank, cnt[0, :N_EXPERTS], moe_w1, moe_b1, moe_w2, moe_b2)

    (proj, x_cur) = _in_proj(1, n_ctx, n_tok, T_lat, mods, norm_mix[1], w_in_odd[0], x_prev=x_new, y4=(y4, wts))
    y_h_c, fin_h = _hgrn_scan(proj, hgrn_lb_logits, hgrn_gain[0], None, 0, B_ctx, T_ctx, True)
    (y_h_l,) = _hgrn_scan(proj, hgrn_lb_logits, hgrn_gain[0], state_hgrn[:, 0], n_ctx, B_lat, T_lat, False)
    x_new, hn, idx, wts, rank, cnt = _out_proj(1, n_ctx, n_tok, T_lat, [(y_h_c, y_h_l)], w_out_odd[0], x_cur, mods,
                                               norm_ffn[1], rw_pad[1], rb_pad[1:2])
    y4 = _moe_layer(1, hn, idx, rank, cnt[0, :N_EXPERTS], moe_w1, moe_b1, moe_w2, moe_b2)

    y_ctx, y_lat = _final(n_ctx, n_tok, T_lat, x_new, y4, wts, mods, final_norm)

    new_state_gla = fin_gla.reshape(B_ctx, 1, 2, GLA_HEADS, GLA_DK, GLA_DV)
    new_state_ret = fin_ret.reshape(B_ctx, 1, 2, RET_HEADS, RET_DK, RET_DV)
    new_state_hgrn = fin_h.reshape(B_ctx, 1, 2, HG_HEADS, HG_DK, HG_DV)
    return (y_ctx.reshape(B_ctx, T_ctx, D), y_lat.reshape(B_lat, T_lat, D), new_state_gla, new_state_ret,
            new_state_hgrn)
```

```python
import functools

import jax
import jax.numpy as jnp
from jax import lax
from jax.experimental import pallas as pl
from jax.experimental.pallas import tpu as pltpu

F32 = jnp.float32
BF16 = jnp.bfloat16

D_MODEL = 1024
DEPTH = 2
GRID_W = 64
GLA_HEADS, GLA_DK, GLA_DV, GLA_LOW_RANK = 4, 64, 128, 16
GLA_NORMALIZER = 16.0
RET_HEADS, RET_DK, RET_DV = 4, 128, 128
ROPE_BASE = 10000.0
HG_HEADS, HG_DK, HG_DV = 8, 128, 128
N_EXPERTS, TOP_K, D_FF = 32, 4, 1024
SWIGLU_ALPHA, SWIGLU_LIMIT = 1.702, 7.0
EPS = 1e-6

LANES = 128
SCAN_CHUNK = 128
SCAN_ROWS = 1024
TOKEN_TILE = 256
BIG_TOKEN_TILE = 512
MOE_TILE = 384
VMEM_LIMIT = 56 * 1024 * 1024

_GQ, _GK, _GV, _GG, _RQ, _RK, _RV, _RG, _EVEN_MAIN = 0, 256, 512, 1024, 1536, 2048, 2560, 3072, 3584
_HQ, _HFF, _HFB, _HI, _HG, _ODD_MAIN = 0, 1024, 2048, 3072, 4096, 5120


def _dot(a, b):
    return jnp.dot(a, b, preferred_element_type=F32)


def _dot_nt(a, b):
    return lax.dot_general(a, b, (((1,), (1,)), ((), ())), preferred_element_type=F32)


def _rms(x, gain=None):
    y = x * lax.rsqrt(jnp.mean(x * x, axis=-1, keepdims=True) + EPS)
    if gain is not None:
        y = y * gain
    return y


def _sigmoid(x):
    return 0.5 * jnp.tanh(0.5 * x) + 0.5


def _silu(x):
    return x * _sigmoid(x)


def _cparams(sem, vmem=VMEM_LIMIT):
    return pltpu.CompilerParams(dimension_semantics=sem, vmem_limit_bytes=vmem)


def _resident(shape):
    nd = len(shape)
    return pl.BlockSpec(shape, lambda *_: (0,) * nd, pipeline_mode=pl.Buffered(1))


def _ada_kernel(c_ref, w_ref, b_ref, o_ref):
    o_ref[0] = _dot(_silu(c_ref[...]), w_ref[0]) + b_ref[0]


def _ada_mods(cond8, ada_w, ada_b):
    tn = 1536
    return pl.pallas_call(
        _ada_kernel,
        grid=(DEPTH, 6 * D_MODEL // tn),
        in_specs=[
            pl.BlockSpec((8, D_MODEL), lambda l, j: (0, 0)),
            pl.BlockSpec((1, D_MODEL, tn), lambda l, j: (l, 0, j)),
            pl.BlockSpec((1, 1, tn), lambda l, j: (l, 0, j)),
        ],
        out_specs=pl.BlockSpec((1, 8, tn), lambda l, j: (l, 0, j)),
        out_shape=jax.ShapeDtypeStruct((DEPTH, 8, 6 * D_MODEL), F32),
        compiler_params=_cparams(("arbitrary", "arbitrary")),
        name="ada_mods",
    )(cond8, ada_w, ada_b.reshape(DEPTH, 1, 6 * D_MODEL))


ROW_UNITS = D_MODEL // LANES


def _rows_from_units(ref, n_rows):
    return jnp.concatenate([ref[pl.ds(c, n_rows, stride=ROW_UNITS), :] for c in range(ROW_UNITS)], axis=1)


def _rows_to_units(ref, val):
    n_rows = val.shape[0]
    for c in range(ROW_UNITS):
        ref[pl.ds(c, n_rows, stride=ROW_UNITS), :] = val[:, c * LANES:(c + 1) * LANES]


def _moe_combine(y_refs, wt_ref):
    w = wt_ref[...]
    terms = [w[:, k:k + 1] * _rows_from_units(y_refs[k], w.shape[0]) for k in range(TOP_K)]
    return (terms[0] + terms[1]) + (terms[2] + terms[3])


def _in_proj_kernel(*refs, first, even, nb_ctx):
    it = iter(refs)
    if first:
        xa_ref, xb_ref = next(it), next(it)
    else:
        xp_ref = next(it)
        y_refs = [next(it) for _ in range(TOP_K)]
        wt_ref, modp_ref = next(it), next(it)
    gain_ref, mod_ref, w_ref = next(it), next(it), next(it)
    if even:
        wlr_ref, gkw_ref, gkb_ref = next(it), next(it), next(it)
    proj_ref = next(it)
    if even:
        g_ref = next(it)
    xcur_ref = next(it)

    D = D_MODEL
    i = pl.program_id(0)
    if first:
        x = jnp.where(i < nb_ctx, xa_ref[...], xb_ref[...])
    else:
        x = xp_ref[...] + modp_ref[0][:, 5 * D:6 * D] * _moe_combine(y_refs, wt_ref)
    xcur_ref[...] = x
    m = mod_ref[0]
    hn = _rms(x, gain_ref[...]) * (1.0 + m[:, D:2 * D]) + m[:, 0:D]
    proj_ref[...] = _dot(hn, w_ref[...])
    if even:
        z = _dot(_dot(hn, wlr_ref[...]), gkw_ref[...]) + gkb_ref[...]
        g_ref[...] = (jnp.minimum(z, 0.0) - jnp.log(1.0 + jnp.exp(-jnp.abs(z)))) * (1.0 / GLA_NORMALIZER)


def _cond_row(i, nb_ctx, tiles_per_lat_seq):
    return jnp.where(i < nb_ctx, 0, 1 + (i - nb_ctx) // tiles_per_lat_seq)


def _in_proj(layer, n_ctx, n_tok, lat_seq, mods, gain, w_main, *, x_parts=None, x_prev=None, y4=None, even_extra=None):
    first = x_parts is not None
    even = even_extra is not None
    tm = BIG_TOKEN_TILE if first else TOKEN_TILE
    nb = n_tok // tm
    nb_ctx = n_ctx // tm
    tpl = lat_seq // tm
    np_cols = w_main.shape[1]
    D = D_MODEL

    def mod_map(l):
        return lambda i: (l * 8 + _cond_row(i, nb_ctx, tpl), 0, 0)

    row = lambda i: (i, 0)
    args, specs = [], []
    if first:
        xa, xb = x_parts
        args += [xa, xb]
        specs += [pl.BlockSpec((tm, D), lambda i: (jnp.minimum(i, nb_ctx - 1), 0)),
                  pl.BlockSpec((tm, D), lambda i: (jnp.maximum(i - nb_ctx, 0), 0))]
    else:
        y4, wts = y4
        args += [x_prev] + [y4] * TOP_K + [wts, mods]
        specs += [pl.BlockSpec((tm, D), row)]
        specs += [pl.BlockSpec((tm * ROW_UNITS, LANES), (lambda k: (lambda i: (k * nb + i, 0)))(k))
                  for k in range(TOP_K)]
        specs += [pl.BlockSpec((tm, LANES), row), pl.BlockSpec((1, 1, 6 * D), mod_map(layer - 1))]
    args += [gain.reshape(1, D), mods, w_main]
    specs += [_resident((1, D)), pl.BlockSpec((1, 1, 6 * D), mod_map(layer)), _resident((D, np_cols))]
    out_shapes = [jax.ShapeDtypeStruct((n_tok, np_cols), F32)]
    out_specs = [pl.BlockSpec((tm, np_cols), row)]
    if even:
        w_lr, gkw, gkb = even_extra
        args += [w_lr, gkw, gkb]
        specs += [_resident(w_lr.shape), _resident(gkw.shape), _resident(gkb.shape)]
        out_shapes.append(jax.ShapeDtypeStruct((n_tok, 2 * GLA_HEADS * GLA_DK), F32))
        out_specs.append(pl.BlockSpec((tm, 2 * GLA_HEADS * GLA_DK), row))
    out_shapes.append(jax.ShapeDtypeStruct((n_tok, D), F32))
    out_specs.append(pl.BlockSpec((tm, D), row))
    return pl.pallas_call(
        functools.partial(_in_proj_kernel, first=first, even=even, nb_ctx=nb_ctx),
        grid=(nb,),
        in_specs=specs,
        out_specs=out_specs,
        out_shape=out_shapes,
        compiler_params=_cparams(("arbitrary",)),
        name=f"in_proj_l{layer}",
    )(*args)


def _tri(c, lower):
    r = lax.broadcasted_iota(jnp.int32, (c, c), 0)
    s = lax.broadcasted_iota(jnp.int32, (c, c), 1)
    return (r >= s) if lower else (r <= s)


def _cumsum_mm(tri_bf16, g):
    g_hi = g.astype(BF16)
    g_lo = (g - g_hi.astype(F32)).astype(BF16)
    r = _dot(tri_bf16, jnp.concatenate([g_hi, g_lo], axis=1))
    w = g.shape[1]
    return r[:, :w] + r[:, w:]


def _gated_scan_kernel(*refs, variant, T, nseq, has_s0, want_final, hpb):
    C = SCAN_CHUNK
    ncs = T // C
    nc = nseq * ncs
    it = iter(refs)
    if variant == "gla":
        q_ref, k_ref, v_ref, og_ref, gf_ref, gb_ref, gain_ref = (next(it) for _ in range(7))
    else:
        q_ref, ff_ref, fb_ref, v_ref, og_ref, lbl_ref, gain_ref = (next(it) for _ in range(7))
    s0_ref = next(it) if has_s0 else None
    y_ref = next(it)
    sfin_ref = next(it) if want_final else None
    qi_scr, o_scr, u_scr, dec_scr, sin_scr = (next(it) for _ in range(5))

    lane = lax.broadcasted_iota(jnp.int32, (1, LANES), 1)
    if hpb == 2:
        masks = [lane < GLA_DK, lane >= GLA_DK]
    else:
        masks = [None]
    tri_l, tri_u = _tri(C, True), _tri(C, False)
    tri_l16, tri_u16 = tri_l.astype(F32).astype(BF16), tri_u.astype(F32).astype(BF16)
    mid_f, mid_b = C // 2 - 1, C // 2

    if variant == "hgrn":
        lgs = [lbl_ref[l] for l in range(DEPTH)]
        mx = functools.reduce(jnp.maximum, lgs)
        es = [jnp.exp(l - mx) for l in lgs]
        tot = functools.reduce(lambda a, b: a + b, es)
        ps = [e / tot for e in es]
        layer = DEPTH - 1
        lb = functools.reduce(lambda a, b: a + b, ps[:layer + 1]) - ps[0]
        lb_f, lb_b = lb[0:1], lb[1:2]

    for n in range(nc):
        rows = pl.ds(n * C, C)
        if variant == "gla":
            q = q_ref[rows, :] * (GLA_DK ** -0.5)
            kf = kb = k_ref[rows, :]
            gf, gb = gf_ref[rows, :], gb_ref[rows, :]
        else:
            q = _silu(q_ref[rows, :])
            f_f = lb_f + (1.0 - lb_f) * _sigmoid(ff_ref[rows, :])
            f_b = lb_b + (1.0 - lb_b) * _sigmoid(fb_ref[rows, :])
            kf, kb = 1.0 - f_f, 1.0 - f_b
            gf, gb = jnp.log(f_f), jnp.log(f_b)
        bf = _cumsum_mm(tri_l16, gf)
        bb = _cumsum_mm(tri_u16, gb)
        bf_mid, bf_end = bf[mid_f:mid_f + 1], bf[C - 1:C]
        bb_mid, bb_end = bb[mid_b:mid_b + 1], bb[0:1]
        qd_f, kd_f = q * jnp.exp(bf - bf_mid), kf * jnp.exp(bf_mid - bf)
        qd_b, kd_b = q * jnp.exp(bb - bb_mid), kb * jnp.exp(bb_mid - bb)
        kend = jnp.concatenate([kf * jnp.exp(bf_end - bf), kb * jnp.exp(bb_end - bb)], axis=1)
        qi_scr[rows, :] = jnp.concatenate([q * jnp.exp(bf), q * jnp.exp(bb)], axis=1)
        dec_scr[n] = jnp.broadcast_to(jnp.concatenate([jnp.exp(bf_end), jnp.exp(bb_end)], axis=1), (8, 2 * LANES))
        for h in range(hpb):
            v = v_ref[rows, h * LANES:(h + 1) * LANES]
            if masks[h] is None:
                qf_h, qb_h, kend_h = qd_f, qd_b, kend
            else:
                qf_h, qb_h = jnp.where(masks[h], qd_f, 0.0), jnp.where(masks[h], qd_b, 0.0)
                kend_h = jnp.where(jnp.concatenate([masks[h], masks[h]], axis=1), kend, 0.0)
            s = jnp.where(tri_l, _dot_nt(qf_h, kd_f), 0.0) + jnp.where(tri_u, _dot_nt(qb_h, kd_b), 0.0)
            o_scr[rows, h * LANES:(h + 1) * LANES] = _dot(s, v)
            u_scr[h, n] = _dot(v.T, kend_h)

    for sq in range(nseq):
        for h in range(hpb):
            if has_s0:
                s_f, s_b = s0_ref[sq, 0, 0].T, s0_ref[sq, 1, 0].T
                if masks[h] is not None:
                    s_f, s_b = jnp.where(masks[h], s_f, 0.0), jnp.where(masks[h], s_b, 0.0)
            else:
                s_f = s_b = jnp.zeros((LANES, LANES), F32)
            for n in range(sq * ncs, (sq + 1) * ncs):
                sin_scr[h, n, :, 0:LANES] = s_f
                s_f = s_f * dec_scr[n, 0:1, 0:LANES] + u_scr[h, n, :, 0:LANES]
            for n in reversed(range(sq * ncs, (sq + 1) * ncs)):
                sin_scr[h, n, :, LANES:2 * LANES] = s_b
                s_b = s_b * dec_scr[n, 0:1, LANES:2 * LANES] + u_scr[h, n, :, LANES:2 * LANES]
            if want_final:
                if h == 0:
                    fin_f, fin_b = s_f, s_b
                else:
                    fin_f, fin_b = fin_f + s_f, fin_b + s_b
        if want_final:
            sfin_ref[sq, 0, 0] = fin_f.T
            sfin_ref[sq, 1, 0] = fin_b.T

    gain = gain_ref[...]
    for n in range(nc):
        rows = pl.ds(n * C, C)
        qi = qi_scr[rows, :]
        for h in range(hpb):
            cols = slice(h * LANES, (h + 1) * LANES)
            o = o_scr[rows, cols] + _dot_nt(qi, sin_scr[h, n])
            y_ref[rows, cols] = (_rms(o, gain) * _silu(og_ref[rows, cols])).astype(y_ref.dtype)


def _scan_scratch(T, hpb):
    nc = T // SCAN_CHUNK
    return [
        pltpu.VMEM((T, 2 * LANES), F32),
        pltpu.VMEM((T, hpb * LANES), F32),
        pltpu.VMEM((hpb, nc, LANES, 2 * LANES), F32),
        pltpu.VMEM((nc, 8, 2 * LANES), F32),
        pltpu.VMEM((hpb, nc, LANES, 2 * LANES), F32),
    ]


def _seqs_per_step(B, T):
    nseq = max(1, min(B, SCAN_ROWS // T))
    assert B % nseq == 0
    return nseq


def _gla_scan(proj, gdec, gain, s0, row0, B, T, want_final):
    nseq = _seqs_per_step(B, T)
    R = nseq * T
    rb0 = row0 // R
    assert row0 % R == 0
    has_s0 = s0 is not None

    def col(base, width=LANES):
        return lambda b, j: (rb0 + b, base // width + j)

    args = [proj, proj, proj, proj, gdec, gdec, gain.reshape(1, GLA_DV)]
    specs = [pl.BlockSpec((R, LANES), col(_GQ)), pl.BlockSpec((R, LANES), col(_GK)),
             pl.BlockSpec((R, 2 * LANES), col(_GV, 2 * LANES)), pl.BlockSpec((R, 2 * LANES), col(_GG, 2 * LANES)),
             pl.BlockSpec((R, LANES), col(0)), pl.BlockSpec((R, LANES), col(GLA_HEADS * GLA_DK)),
             pl.BlockSpec((1, GLA_DV), lambda b, j: (0, 0))]
    st_spec = pl.BlockSpec((nseq, 2, 1, LANES, LANES), lambda b, j: (b, 0, j, 0, 0))
    if has_s0:
        args.append(s0.reshape(B, 2, GLA_HEADS // 2, 2 * GLA_DK, GLA_DV))
        specs.append(st_spec)
    out_shapes = [jax.ShapeDtypeStruct((B * T, GLA_HEADS * GLA_DV), BF16)]
    out_specs = [pl.BlockSpec((R, 2 * LANES), lambda b, j: (b, j))]
    if want_final:
        out_shapes.append(jax.ShapeDtypeStruct((B, 2, GLA_HEADS // 2, 2 * GLA_DK, GLA_DV), F32))
        out_specs.append(st_spec)
    kern = functools.partial(_gated_scan_kernel, variant="gla", T=T, nseq=nseq, has_s0=has_s0,
                             want_final=want_final, hpb=2)
    return pl.pallas_call(
        kern, grid=(B // nseq, GLA_HEADS // 2), in_specs=specs, out_specs=out_specs, out_shape=out_shapes,
        scratch_shapes=_scan_scratch(R, 2),
        compiler_params=_cparams(("arbitrary", "arbitrary")), name=f"gla_scan_T{T}",
    )(*args)


def _hgrn_scan(proj, lb_logits, gain, s0, row0, B, T, want_final):
    nseq = _seqs_per_step(B, T)
    R = nseq * T
    rb0 = row0 // R
    assert row0 % R == 0
    has_s0 = s0 is not None

    def col(base):
        return lambda b, j: (rb0 + b, base // LANES + j)

    args = [proj, proj, proj, proj, proj, lb_logits, gain.reshape(1, HG_DV)]
    specs = [pl.BlockSpec((R, LANES), col(_HQ)), pl.BlockSpec((R, LANES), col(_HFF)),
             pl.BlockSpec((R, LANES), col(_HFB)), pl.BlockSpec((R, LANES), col(_HI)),
             pl.BlockSpec((R, LANES), col(_HG)),
             pl.BlockSpec((DEPTH, 2, LANES), lambda b, j: (0, 0, j)),
             pl.BlockSpec((1, HG_DV), lambda b, j: (0, 0))]
    st_spec = pl.BlockSpec((nseq, 2, 1, LANES, LANES), lambda b, j: (b, 0, j, 0, 0))
    if has_s0:
        args.append(s0.reshape(B, 2, HG_HEADS, HG_DK, HG_DV))
        specs.append(st_spec)
    out_shapes = [jax.ShapeDtypeStruct((B * T, HG_HEADS * HG_DV), BF16)]
    out_specs = [pl.BlockSpec((R, LANES), lambda b, j: (b, j))]
    if want_final:
        out_shapes.append(jax.ShapeDtypeStruct((B, 2, HG_HEADS, HG_DK, HG_DV), F32))
        out_specs.append(st_spec)
    kern = functools.partial(_gated_scan_kernel, variant="hgrn", T=T, nseq=nseq, has_s0=has_s0,
                             want_final=want_final, hpb=1)
    return pl.pallas_call(
        kern, grid=(B // nseq, HG_HEADS), in_specs=specs, out_specs=out_specs, out_shape=out_shapes,
        scratch_shapes=_scan_scratch(R, 1),
        compiler_params=_cparams(("arbitrary", "arbitrary")), name=f"hgrn_scan_T{T}",
    )(*args)


def _ret_scan_kernel(*refs, T, nseq, has_s0, want_final, rope):
    C = SCAN_CHUNK
    ncs = T // C
    nc = nseq * ncs
    it = iter(refs)
    q_ref, k_ref, v_ref, og_ref, dexp_ref = (next(it) for _ in range(5))
    if rope:
        cos_ref, sin_ref = next(it), next(it)
    s0_ref = next(it) if has_s0 else None
    y_ref = next(it)
    sfin_ref = next(it) if want_final else None
    qi_scr, o_scr, u_scr, sin_scr = (next(it) for _ in range(4))

    lg = jnp.log1p(-jnp.exp2(-dexp_ref[0]))
    lg_f, lg_b = lg[0:1], lg[1:2]
    r = lax.broadcasted_iota(jnp.int32, (C, C), 0)
    s = lax.broadcasted_iota(jnp.int32, (C, C), 1)
    dist = (r - s).astype(F32)
    dmask = (jnp.where(r >= s, jnp.exp(jnp.maximum(dist, 0.0) * lg_f[:, 0:1]), 0.0)
             + jnp.where(r <= s, jnp.exp(jnp.maximum(-dist, 0.0) * lg_b[:, 0:1]), 0.0))
    pos = lax.broadcasted_iota(jnp.int32, (C, LANES), 0).astype(F32)
    xi = jnp.concatenate([jnp.exp((pos + 1.0) * lg_f), jnp.exp((C - pos) * lg_b)], axis=1)
    zeta = jnp.concatenate([jnp.exp((C - 1.0 - pos) * lg_f), jnp.exp(pos * lg_b)], axis=1)
    d_f, d_b = jnp.exp(C * lg_f), jnp.exp(C * lg_b)

    def rot(x, seq_rows):
        if not rope:
            return x
        return x * cos_ref[seq_rows, :] + pltpu.roll(x, RET_DK // 2, axis=1) * sin_ref[seq_rows, :]

    for n in range(nc):
        rows = pl.ds(n * C, C)
        seq_rows = pl.ds((n % ncs) * C, C)
        q = rot(q_ref[rows, :], seq_rows)
        k = rot(k_ref[rows, :] * (RET_DK ** -0.5), seq_rows)
        v = v_ref[rows, :]
        o_scr[rows, :] = _dot(_dot_nt(q, k) * dmask, v)
        qi_scr[rows, :] = jnp.concatenate([q, q], axis=1) * xi
        u_scr[n] = _dot(v.T, jnp.concatenate([k, k], axis=1) * zeta)

    for sq in range(nseq):
        if has_s0:
            s_f, s_b = s0_ref[sq, 0, 0].T, s0_ref[sq, 1, 0].T
        else:
            s_f = s_b = jnp.zeros((LANES, LANES), F32)
        for n in range(sq * ncs, (sq + 1) * ncs):
            sin_scr[n, :, 0:LANES] = s_f
            s_f = s_f * d_f + u_scr[n, :, 0:LANES]
        for n in reversed(range(sq * ncs, (sq + 1) * ncs)):
            sin_scr[n, :, LANES:2 * LANES] = s_b
            s_b = s_b * d_b + u_scr[n, :, LANES:2 * LANES]
        if want_final:
            sfin_ref[sq, 0, 0] = s_f.T
            sfin_ref[sq, 1, 0] = s_b.T

    for n in range(nc):
        rows = pl.ds(n * C, C)
        o = o_scr[rows, :] + _dot_nt(qi_scr[rows, :], sin_scr[n])
        y_ref[rows, :] = (_rms(o) * _silu(og_ref[rows, :])).astype(y_ref.dtype)


def _ret_scan(proj, dexp, rope_tabs, s0, row0, B, T, want_final):
    nseq = _seqs_per_step(B, T)
    R = nseq * T
    rb0 = row0 // R
    assert row0 % R == 0
    nc = R // SCAN_CHUNK
    has_s0 = s0 is not None
    rope = rope_tabs is not None

    def col(base):
        return lambda b, j: (rb0 + b, base // LANES + j)

    args = [proj, proj, proj, proj, dexp]
    specs = [pl.BlockSpec((R, LANES), col(_RQ)), pl.BlockSpec((R, LANES), col(_RK)),
             pl.BlockSpec((R, LANES), col(_RV)), pl.BlockSpec((R, LANES), col(_RG)),
             pl.BlockSpec((1, 2, LANES), lambda b, j: (j, 0, 0))]
    if rope:
        args += list(rope_tabs)
        specs += [pl.BlockSpec((T, LANES), lambda b, j: (0, 0))] * 2
    st_spec = pl.BlockSpec((nseq, 2, 1, LANES, LANES), lambda b, j: (b, 0, j, 0, 0))
    if has_s0:
        args.append(s0.reshape(B, 2, RET_HEADS, RET_DK, RET_DV))
        specs.append(st_spec)
    out_shapes = [jax.ShapeDtypeStruct((B * T, RET_HEADS * RET_DV), BF16)]
    out_specs = [pl.BlockSpec((R, LANES), lambda b, j: (b, j))]
    if want_final:
        out_shapes.append(jax.ShapeDtypeStruct((B, 2, RET_HEADS, RET_DK, RET_DV), F32))
        out_specs.append(st_spec)
    kern = functools.partial(_ret_scan_kernel, T=T, nseq=nseq, has_s0=has_s0, want_final=want_final, rope=rope)
    scratch = [pltpu.VMEM((R, 2 * LANES), F32), pltpu.VMEM((R, LANES), F32),
               pltpu.VMEM((nc, LANES, 2 * LANES), F32), pltpu.VMEM((nc, LANES, 2 * LANES), F32)]
    return pl.pallas_call(
        kern, grid=(B // nseq, RET_HEADS), in_specs=specs, out_specs=out_specs, out_shape=out_shapes,
        scratch_shapes=scratch,
        compiler_params=_cparams(("arbitrary", "arbitrary")), name=f"ret_scan_T{T}",
    )(*args)


def _out_proj_kernel(*refs, n_mix, nb_ctx):
    it = iter(refs)
    y_refs = [(next(it), next(it)) for _ in range(n_mix)]
    wo_ref, x_ref, mod_ref, gain_ref, rw_ref, rb_ref = (next(it) for _ in range(6))
    xnew_ref, hn_ref, idx_ref, wt_ref, rank_ref, cnt_ref = (next(it) for _ in range(6))
    cnt_scr = next(it)
    D = D_MODEL
    is_ctx = pl.program_id(0) < nb_ctx
    mix = None
    r0 = 0
    for ya_ref, yb_ref in y_refs:
        w = ya_ref.shape[1]
        part = _dot(jnp.where(is_ctx, ya_ref[...], yb_ref[...]), wo_ref[r0:r0 + w, :])
        mix = part if mix is None else mix + part
        r0 += w
    m = mod_ref[0]
    xn = x_ref[...] + m[:, 2 * D:3 * D] * mix
    xnew_ref[...] = xn
    hn = _rms(xn, gain_ref[...]) * (1.0 + m[:, 4 * D:5 * D]) + m[:, 3 * D:4 * D]
    hn_hi = hn.astype(BF16)
    _rows_to_units(hn_ref, hn)
    hn_lo = (hn - hn_hi.astype(F32)).astype(BF16)
    rw = rw_ref[...]
    rw_hi = rw.astype(BF16)
    rw_lo = (rw - rw_hi.astype(F32)).astype(BF16)
    logits = (_dot(hn_hi, rw_hi) + (_dot(hn_lo, rw_hi) + _dot(hn_hi, rw_lo))) + rb_ref[...]
    lane = lax.broadcasted_iota(jnp.int32, logits.shape, 1).astype(F32)
    vals, idxs = [], []
    cur = logits
    for _ in range(TOP_K):
        mx = jnp.max(cur, axis=-1, keepdims=True)
        ik = jnp.min(jnp.where(cur == mx, lane, float(LANES)), axis=-1, keepdims=True)
        vals.append(mx)
        idxs.append(ik)
        cur = jnp.where(lane == ik, -jnp.inf, cur)
    es = [jnp.exp(v - vals[0]) for v in vals]
    tot = (es[0] + es[1]) + (es[2] + es[3])
    idx_out = jnp.zeros(logits.shape, F32)
    wt_out = jnp.zeros(logits.shape, F32)
    for k in range(TOP_K):
        idx_out = jnp.where(lane == float(k), idxs[k], idx_out)
        wt_out = jnp.where(lane == float(k), es[k] / tot, wt_out)
    idx_ref[...] = idx_out.astype(jnp.int32)
    wt_ref[...] = wt_out

    @pl.when(pl.program_id(0) == 0)
    def _():
        cnt_scr[...] = jnp.zeros(cnt_scr.shape, F32)

    tm = logits.shape[0]
    hits = [lane == idxs[k] for k in range(TOP_K)]
    sel = jnp.zeros(logits.shape, F32)
    for k in range(TOP_K):
        sel = sel + jnp.where(hits[k], 1.0, 0.0)
    rr = lax.broadcasted_iota(jnp.int32, (tm, tm), 0)
    cc = lax.broadcasted_iota(jnp.int32, (tm, tm), 1)
    before = jnp.where(rr > cc, 1.0, 0.0).astype(BF16)
    rank_all = cnt_scr[0:1, :] + _dot(before, sel.astype(BF16))
    rank_out = jnp.zeros(logits.shape, F32)
    for k in range(TOP_K):
        rk = jnp.sum(jnp.where(hits[k], rank_all, 0.0), axis=-1, keepdims=True)
        rank_out = jnp.where(lane == float(k), rk, rank_out)
    rank_ref[...] = rank_out.astype(jnp.int32)
    total = cnt_scr[...] + jnp.sum(sel, axis=0, keepdims=True)
    cnt_scr[...] = total
    cnt_ref[...] = total.astype(jnp.int32)


def _out_proj(layer, n_ctx, n_tok, lat_seq, ys, w_out, x_cur, mods, gain, rw_pad, rb_pad):
    tm = BIG_TOKEN_TILE
    nb, nb_ctx, tpl = n_tok // tm, n_ctx // tm, lat_seq // tm
    D = D_MODEL
    row = lambda i: (i, 0)
    specs = []
    for ya, _ in ys:
        specs += [pl.BlockSpec((tm, ya.shape[1]), lambda i: (jnp.minimum(i, nb_ctx - 1), 0)),
                  pl.BlockSpec((tm, ya.shape[1]), lambda i: (jnp.maximum(i - nb_ctx, 0), 0))]
    specs += [_resident((D, D)), pl.BlockSpec((tm, D), row),
              pl.BlockSpec((1, 1, 6 * D), lambda i: (layer * 8 + _cond_row(i, nb_ctx, tpl), 0, 0)),
              _resident((1, D)), _resident((D, LANES)), _resident((1, LANES))]
    return pl.pallas_call(
        functools.partial(_out_proj_kernel, n_mix=len(ys), nb_ctx=nb_ctx),
        grid=(nb,),
        in_specs=specs,
        out_specs=[pl.BlockSpec((tm, D), row), pl.BlockSpec((tm * ROW_UNITS, LANES), row),
                   pl.BlockSpec((tm, LANES), row), pl.BlockSpec((tm, LANES), row),
                   pl.BlockSpec((tm, LANES), row), pl.BlockSpec((8, LANES), lambda i: (0, 0))],
        out_shape=[jax.ShapeDtypeStruct((n_tok, D), F32), jax.ShapeDtypeStruct((n_tok * ROW_UNITS, LANES), F32),
                   jax.ShapeDtypeStruct((n_tok, LANES), jnp.int32), jax.ShapeDtypeStruct((n_tok, LANES), F32),
                   jax.ShapeDtypeStruct((n_tok, LANES), jnp.int32), jax.ShapeDtypeStruct((8, LANES), jnp.int32)],
        scratch_shapes=[pltpu.VMEM((8, LANES), F32)],
        compiler_params=_cparams(("arbitrary",)),
        name=f"out_proj_l{layer}",
    )(*[y for pair in ys for y in pair], w_out, x_cur, mods, gain.reshape(1, D), rw_pad, rb_pad)


W1_SPLIT = 4
W2_SPLIT = 2


def _moe_kernel(seg_ref, nt_ref, code_hbm, x_hbm, w1_hbm, b1_ref, w2_hbm, b2_ref, y_hbm, w1_scr, b1_scr, w2_scr,
                act_scr, xbuf, obuf, w1buf, w2buf, code_smem, csem, gsem, ssem, wsem, *, n_tok, layer):
    tm = MOE_TILE
    U = ROW_UNITS
    n_code_tiles = code_hbm.shape[0] // tm
    e = pl.program_id(0)
    nt = nt_ref[e]
    g_first = seg_ref[e] // tm
    wslot = lax.rem(e, 2)

    def weight_copies(ex, ws):
        cw, rh = 2 * D_FF // W1_SPLIT, D_FF // W2_SPLIT
        cps = [pltpu.make_async_copy(w1_hbm.at[layer, ex, :, pl.ds(c * cw, cw)],
                                     w1buf.at[ws, :, pl.ds(c * cw, cw)], wsem.at[ws]) for c in range(W1_SPLIT)]
        cps += [pltpu.make_async_copy(w2_hbm.at[layer, ex, pl.ds(c * rh, rh), :],
                                      w2buf.at[ws, pl.ds(c * rh, rh), :], wsem.at[ws]) for c in range(W2_SPLIT)]
        return cps

    @pl.when(e == 0)
    def _():
        for cp in weight_copies(0, 0):
            cp.start(priority=1)

    @pl.when(e + 1 < N_EXPERTS)
    def _():
        for cp in weight_copies(e + 1, 1 - wslot):
            cp.start(priority=1)

    for cp in weight_copies(e, wslot):
        cp.wait()

    GATHER, SCATTER = 0, 1

    def code_copy(kind, t, p):
        first = pl.multiple_of(jnp.clip(t, 0, n_code_tiles - 1) * tm, tm)
        return pltpu.make_async_copy(code_hbm.at[pl.ds(first, tm)], code_smem.at[kind, p], csem.at[kind, p])

    def gather_start(b, p):
        for r in range(tm):
            tok = jnp.bitwise_and(code_smem[GATHER, p, r], n_tok - 1)
            pltpu.make_async_copy(x_hbm.at[pl.ds(pl.multiple_of(tok * U, U), U)],
                                  xbuf.at[b, pl.ds(r * U, U)], gsem.at[b]).start()

    def gather_wait(b):
        pltpu.make_async_copy(x_hbm.at[pl.ds(0, tm * U)], xbuf.at[b], gsem.at[b]).wait()

    def scatter_start(b, p):
        for r in range(tm):
            dst = code_smem[SCATTER, p, r]
            pltpu.make_async_copy(obuf.at[b, pl.ds(r * U, U)],
                                  y_hbm.at[pl.ds(pl.multiple_of(dst * U, U), U)], ssem.at[b]).start(priority=1)

    def scatter_wait(b):
        pltpu.make_async_copy(obuf.at[b], y_hbm.at[pl.ds(0, tm * U)], ssem.at[b]).wait()

    @pl.when(e == 0)
    def _():
        obuf[...] = jnp.zeros(obuf.shape, obuf.dtype)
        for h in range(2):
            cp = pltpu.make_async_copy(obuf.at[0], y_hbm.at[pl.ds((TOP_K * n_tok + h * tm) * U, tm * U)], ssem.at[0])
            cp.start()
            cp.wait()
        code_copy(GATHER, 0, 0).start()
        code_copy(GATHER, 1, 1).start()
        code_copy(SCATTER, n_code_tiles - 1, 1).start()
        code_copy(SCATTER, 0, 0).start()
        code_copy(GATHER, 0, 0).wait()
        gather_start(0, 0)

    @pl.when(nt > 0)
    def _():
        grp = 2 * LANES
        rr = lax.broadcasted_iota(jnp.int32, (grp, grp), 0)
        cc = lax.broadcasted_iota(jnp.int32, (grp, grp), 1)
        src = jnp.where(cc < LANES, 2 * cc, 2 * (cc - LANES) + 1)
        perm = jnp.where(rr == src, 1.0, 0.0).astype(BF16)
        for j in range(2 * D_FF // grp):
            cols = slice(j * grp, (j + 1) * grp)
            w1_scr[:, cols] = _dot(w1buf[wslot, :, cols].astype(BF16), perm).astype(BF16)
        bias = jnp.broadcast_to(b1_ref[0, 0], (8, 2 * D_FF))
        for j in range(2 * D_FF // grp):
            cols = slice(j * grp, (j + 1) * grp)
            rest = bias[:, cols]
            acc = jnp.zeros((8, grp), F32)
            for _ in range(3):
                term = rest.astype(BF16)
                acc = acc + _dot(term, perm)
                rest = rest - term.astype(F32)
            b1_scr[:, cols] = acc
        w2_scr[...] = w2buf[wslot].astype(BF16)
        blk = 4 * LANES

        def tile(g, b):
            o = 1 - b

            @pl.when(g >= 0)
            def _():
                code_copy(GATHER, g + 1, o).wait()
                code_copy(SCATTER, g - 1, o).wait()
                gather_start(o, o)
                scatter_start(o, o)
                code_copy(GATHER, g + 2, b).start()
                code_copy(SCATTER, g + 1, o).start()

            gather_wait(b)
            x = _rows_from_units(xbuf.at[b], tm).astype(BF16)
            for c in range(2 * D_FF // blk):
                cols = slice(c * blk, (c + 1) * blk)
                hid = _dot(x, w1_scr[:, cols]) + b1_scr[0:1, cols]
                acts = []
                for j in range(blk // grp):
                    glu = jnp.minimum(hid[:, j * grp:j * grp + LANES], SWIGLU_LIMIT)
                    lin = jnp.clip(hid[:, j * grp + LANES:(j + 1) * grp], -SWIGLU_LIMIT, SWIGLU_LIMIT)
                    acts.append(glu * _sigmoid(SWIGLU_ALPHA * glu) * (lin + 1.0))
                act_scr[:, c * (blk // 2):(c + 1) * (blk // 2)] = jnp.concatenate(acts, axis=1).astype(BF16)
            y = _dot(act_scr[...], w2_scr[...]) + b2_ref[0, 0]

            @pl.when(g >= 1)
            def _():
                scatter_wait(b)

            _rows_to_units(obuf.at[b], y)

        g_end = g_first + nt

        def tile_pair(m, carry):
            for b in range(2):
                g = 2 * m + b

                @pl.when(jnp.logical_and(g >= g_first, g < g_end))
                def _():
                    tile(g, b)

            return carry

        lax.fori_loop(g_first // 2, (g_end + 1) // 2, tile_pair, 0)

    @pl.when(e == N_EXPERTS - 1)
    def _():
        g_end = g_first + nt
        last = lax.rem(g_end + 1, 2)
        code_copy(SCATTER, g_end - 1, last).wait()
        scatter_start(last, last)
        scatter_wait(last)
        scatter_wait(1 - last)
        gather_wait(1 - last)
        code_copy(GATHER, g_end + 1, last).wait()
        code_copy(SCATTER, g_end, 1 - last).wait()


def _moe_experts(layer, n_tok, seg_start, tiles_per_e, code, x_units, w1, b1, w2, b2):
    tm = MOE_TILE
    D, F2, U = D_MODEL, 2 * D_FF, ROW_UNITS
    any_space = pl.BlockSpec(memory_space=pl.ANY)
    grid_spec = pltpu.PrefetchScalarGridSpec(
        num_scalar_prefetch=2,
        grid=(N_EXPERTS,),
        in_specs=[any_space, any_space,
                  any_space, pl.BlockSpec((1, 1, 1, F2), lambda e, sg, nt: (layer, e, 0, 0)),
                  any_space, pl.BlockSpec((1, 1, 1, D), lambda e, sg, nt: (layer, e, 0, 0))],
        out_specs=any_space,
        scratch_shapes=[pltpu.VMEM((D, F2), BF16), pltpu.VMEM((8, F2), F32), pltpu.VMEM((D_FF, D), BF16),
                        pltpu.VMEM((tm, D_FF), BF16),
                        pltpu.VMEM((2, tm * U, LANES), F32), pltpu.VMEM((2, tm * U, LANES), F32),
                        pltpu.VMEM((2, D, F2), F32), pltpu.VMEM((2, D_FF, D), F32),
                        pltpu.SMEM((2, 2, tm), jnp.int32),
                        pltpu.SemaphoreType.DMA((2, 2)), pltpu.SemaphoreType.DMA((2,)), pltpu.SemaphoreType.DMA((2,)),
                        pltpu.SemaphoreType.DMA((2,))],
    )
    n_out_rows = TOP_K * n_tok + 2 * tm
    return pl.pallas_call(
        functools.partial(_moe_kernel, n_tok=n_tok, layer=layer),
        grid_spec=grid_spec,
        out_shape=jax.ShapeDtypeStruct((n_out_rows * U, LANES), F32),
        compiler_params=_cparams(("arbitrary",)),
        name="moe_experts",
    )(seg_start, tiles_per_e, code, x_units, w1, b1.reshape(DEPTH, N_EXPERTS, 1, F2),
      w2, b2.reshape(DEPTH, N_EXPERTS, 1, D))


def _slot_code_kernel(seg_ref, idx_ref, rank_ref, unused_hbm, code_ref, sem, *, n_tok):
    i = pl.program_id(0)
    tm = idx_ref.shape[0] // TOP_K

    @pl.when(i == 0)
    def _():
        cp = pltpu.make_async_copy(unused_hbm, code_ref, sem)
        cp.start()
        cp.wait()

    base = i * tm
    for t in range(tm):
        for k in range(TOP_K):
            p = t * TOP_K + k
            code_ref[seg_ref[idx_ref[p]] + rank_ref[p]] = base + (k * n_tok + t)


def _slot_codes(idx, rank, seg_start, unused):
    n_tok = idx.shape[0]
    tm = TOKEN_TILE
    pairs = lambda a: a[:, :TOP_K].reshape(-1)
    flat = pl.BlockSpec((tm * TOP_K,), lambda i, sg: (i,), memory_space=pltpu.SMEM)
    grid_spec = pltpu.PrefetchScalarGridSpec(
        num_scalar_prefetch=1,
        grid=(n_tok // tm,),
        in_specs=[flat, flat, pl.BlockSpec(memory_space=pl.ANY)],
        out_specs=pl.BlockSpec(memory_space=pltpu.SMEM),
        scratch_shapes=[pltpu.SemaphoreType.DMA(())],
    )
    return pl.pallas_call(
        functools.partial(_slot_code_kernel, n_tok=n_tok),
        grid_spec=grid_spec,
        out_shape=jax.ShapeDtypeStruct(unused.shape, jnp.int32),
        compiler_params=_cparams(("arbitrary",)),
        name="slot_codes",
    )(seg_start, pairs(idx), pairs(rank), unused)


def _moe_layer(layer, hn_units, idx, rank, counts, w1, b1, w2, b2):
    n_tok = hn_units.shape[0] // ROW_UNITS
    assert n_tok & (n_tok - 1) == 0
    tm = MOE_TILE
    n_pairs = n_tok * TOP_K
    n_slots = ((n_pairs + N_EXPERTS * (tm - 1)) // tm + 1) * tm
    tiles_per_e = (counts + tm - 1) // tm
    seg_start = (jnp.cumsum(tiles_per_e) - tiles_per_e) * tm
    unused = n_pairs + jnp.arange(n_slots, dtype=jnp.int32) % (2 * tm)
    code = _slot_codes(idx, rank, seg_start, unused)
    return _moe_experts(layer, n_tok, seg_start, tiles_per_e, code, hn_units, w1, b1, w2, b2)


def _final_kernel(*refs, nb_ctx):
    x_ref = refs[0]
    y_refs = refs[1:1 + TOP_K]
    wt_ref, mod_ref, gain_ref, oa_ref, ob_ref = refs[1 + TOP_K:]
    D = D_MODEL
    i = pl.program_id(0)
    x = x_ref[...] + mod_ref[0][:, 5 * D:6 * D] * _moe_combine(y_refs, wt_ref)
    out = _rms(x, gain_ref[...])

    @pl.when(i < nb_ctx)
    def _():
        oa_ref[...] = out

    @pl.when(i >= nb_ctx)
    def _():
        ob_ref[...] = out


def _final(n_ctx, n_tok, lat_seq, x_new, y4, wts, mods, gain):
    tm = TOKEN_TILE
    nb, nb_ctx, tpl = n_tok // tm, n_ctx // tm, lat_seq // tm
    D = D_MODEL
    specs = [pl.BlockSpec((tm, D), lambda i: (i, 0))]
    specs += [pl.BlockSpec((tm * ROW_UNITS, LANES), (lambda k: (lambda i: (k * nb + i, 0)))(k)) for k in range(TOP_K)]
    specs += [pl.BlockSpec((tm, LANES), lambda i: (i, 0)),
              pl.BlockSpec((1, 1, 6 * D), lambda i: ((DEPTH - 1) * 8 + _cond_row(i, nb_ctx, tpl), 0, 0)),
              _resident((1, D))]
    return pl.pallas_call(
        functools.partial(_final_kernel, nb_ctx=nb_ctx),
        grid=(nb,),
        in_specs=specs,
        out_specs=[pl.BlockSpec((tm, D), lambda i: (jnp.minimum(i, nb_ctx - 1), 0)),
                   pl.BlockSpec((tm, D), lambda i: (jnp.maximum(i - nb_ctx, 0), 0))],
        out_shape=[jax.ShapeDtypeStruct((n_ctx, D), F32), jax.ShapeDtypeStruct((n_tok - n_ctx, D), F32)],
        compiler_params=_cparams(("arbitrary",)),
        name="final_norm",
    )(x_new, *([y4] * TOP_K), wts, mods, gain.reshape(1, D))


def _rope_tables(n_tokens):
    rows = n_tokens // GRID_W
    r = jnp.repeat(jnp.arange(rows, dtype=F32), GRID_W)
    col = jnp.tile(jnp.arange(GRID_W, dtype=F32), rows)
    nf = RET_DK // 4
    inv = ROPE_BASE ** (-jnp.arange(nf, dtype=F32) / nf)
    ang = jnp.concatenate([r[:, None] * inv, col[:, None] * inv], axis=-1)
    cos, sin = jnp.cos(ang), jnp.sin(ang)
    return jnp.concatenate([cos, cos], axis=-1), jnp.concatenate([-sin, sin], axis=-1)


def kernel(x_prompt, x_sample, c, state_gla, state_ret, state_hgrn, c_ctx, norm_mix, norm_ffn, ada_w, ada_b,
           w_in_even, w_out_even, gla_gk_w, gla_gk_b, gla_gain, ret_decay_exp, w_in_odd, w_out_odd,
           hgrn_lb_logits, hgrn_gain, router_w, router_b, moe_w1, moe_b1, moe_w2, moe_b2, final_norm):
    D = D_MODEL
    B_ctx, T_ctx, _ = x_prompt.shape
    B_lat, T_lat, _ = x_sample.shape
    n_ctx, n_lat = B_ctx * T_ctx, B_lat * T_lat
    n_tok = n_ctx + n_lat
    assert n_ctx % BIG_TOKEN_TILE == 0 and T_lat % BIG_TOKEN_TILE == 0 and B_lat + 1 <= 8
    assert T_ctx % SCAN_CHUNK == 0 and T_lat % SCAN_CHUNK == 0 and n_ctx % T_lat == 0

    cond8 = jnp.concatenate([c_ctx[None, :], c, jnp.zeros((8 - 1 - B_lat, D), F32)], axis=0)
    mods = _ada_mods(cond8, ada_w, ada_b).reshape(DEPTH * 8, 1, 6 * D)

    rw_pad = jnp.pad(router_w, ((0, 0), (0, 0), (0, LANES - N_EXPERTS)))
    rb_pad = jnp.pad(router_b, ((0, 0), (0, LANES - N_EXPERTS)), constant_values=-1e30)

    w_even = w_in_even[0]
    lr0 = GLA_HEADS * (2 * GLA_DK + GLA_DV)
    lr1 = lr0 + 2 * GLA_LOW_RANK
    w_main = jnp.concatenate([w_even[:, :lr0], w_even[:, lr1:]], axis=1)
    w_lr = jnp.pad(w_even[:, lr0:lr1], ((0, 0), (0, LANES - 2 * GLA_LOW_RANK)))
    nqk = GLA_HEADS * GLA_DK
    gkw = jnp.zeros((LANES, 2 * nqk), F32)
    gkw = gkw.at[0:GLA_LOW_RANK, 0:nqk].set(gla_gk_w[0, 0])
    gkw = gkw.at[GLA_LOW_RANK:2 * GLA_LOW_RANK, nqk:].set(gla_gk_w[0, 1])
    gkb = gla_gk_b[0].reshape(1, 2 * nqk)
    proj, gdec, x_cur = _in_proj(0, n_ctx, n_tok, T_lat, mods, norm_mix[0], w_main,
                                 x_parts=(x_prompt.reshape(n_ctx, D), x_sample.reshape(n_lat, D)),
                                 even_extra=(w_lr, gkw, gkb))

    y_gla_c, fin_gla = _gla_scan(proj, gdec, gla_gain[0], None, 0, B_ctx, T_ctx, True)
    (y_gla_l,) = _gla_scan(proj, gdec, gla_gain[0], state_gla[:, 0], n_ctx, B_lat, T_lat, False)
    dexp = jnp.broadcast_to(ret_decay_exp[0].T[:, :, None], (RET_HEADS, 2, LANES))
    y_ret_c, fin_ret = _ret_scan(proj, dexp, None, None, 0, B_ctx, T_ctx, True)
    (y_ret_l,) = _ret_scan(proj, dexp, _rope_tables(T_lat), state_ret[:, 0], n_ctx, B_lat, T_lat, False)

    x_new, hn, idx, wts, rank, cnt = _out_proj(0, n_ctx, n_tok, T_lat, [(y_gla_c, y_gla_l), (y_ret_c, y_ret_l)],
                                               w_out_even[0], x_cur, mods, norm_ffn[0], rw_pad[0], rb_pad[0:1])
    y4 = _moe_layer(0, hn, idx, rank, cnt[0, :N_EXPERTS], moe_w1, moe_b1, moe_w2, moe_b2)

    (proj, x_cur) = _in_proj(1, n_ctx, n_tok, T_lat, mods, norm_mix[1], w_in_odd[0], x_prev=x_new, y4=(y4, wts))
    y_h_c, fin_h = _hgrn_scan(proj, hgrn_lb_logits, hgrn_gain[0], None, 0, B_ctx, T_ctx, True)
    (y_h_l,) = _hgrn_scan(proj, hgrn_lb_logits, hgrn_gain[0], state_hgrn[:, 0], n_ctx, B_lat, T_lat, False)
    x_new, hn, idx, wts, rank, cnt = _out_proj(1, n_ctx, n_tok, T_lat, [(y_h_c, y_h_l)], w_out_odd[0], x_cur, mods,
                                               norm_ffn[1], rw_pad[1], rb_pad[1:2])
    y4 = _moe_layer(1, hn, idx, rank, cnt[0, :N_EXPERTS], moe_w1, moe_b1, moe_w2, moe_b2)

    y_ctx, y_lat = _final(n_ctx, n_tok, T_lat, x_new, y4, wts, mods, final_norm)

    new_state_gla = fin_gla.reshape(B_ctx, 1, 2, GLA_HEADS, GLA_DK, GLA_DV)
    new_state_ret = fin_ret.reshape(B_ctx, 1, 2, RET_HEADS, RET_DK, RET_DV)
    new_state_hgrn = fin_h.reshape(B_ctx, 1, 2, HG_HEADS, HG_DK, HG_DV)
    return (y_ctx.reshape(B_ctx, T_ctx, D), y_lat.reshape(B_lat, T_lat, D), new_state_gla, new_state_ret,
            new_state_hgrn)
```

```python
import functools

import jax
import jax.numpy as jnp
from jax import lax
from jax.experimental import pallas as pl
from jax.experimental.pallas import tpu as pltpu

F32 = jnp.float32
BF16 = jnp.bfloat16

D_MODEL = 1024
DEPTH = 2
GRID_W = 64
GLA_HEADS, GLA_DK, GLA_DV, GLA_LOW_RANK = 4, 64, 128, 16
GLA_NORMALIZER = 16.0
RET_HEADS, RET_DK, RET_DV = 4, 128, 128
ROPE_BASE = 10000.0
HG_HEADS, HG_DK, HG_DV = 8, 128, 128
N_EXPERTS, TOP_K, D_FF = 32, 4, 1024
SWIGLU_ALPHA, SWIGLU_LIMIT = 1.702, 7.0
EPS = 1e-6

LANES = 128
SCAN_CHUNK = 128
SCAN_ROWS = 1024
TOKEN_TILE = 256
BIG_TOKEN_TILE = 512
MOE_TILE = 384
VMEM_LIMIT = 56 * 1024 * 1024

_GQ, _GK, _GV, _GG, _RQ, _RK, _RV, _RG, _EVEN_MAIN = 0, 256, 512, 1024, 1536, 2048, 2560, 3072, 3584
_HQ, _HFF, _HFB, _HI, _HG, _ODD_MAIN = 0, 1024, 2048, 3072, 4096, 5120


def _dot(a, b):
    return jnp.dot(a, b, preferred_element_type=F32)


def _dot_nt(a, b):
    return lax.dot_general(a, b, (((1,), (1,)), ((), ())), preferred_element_type=F32)


def _rms(x, gain=None):
    y = x * lax.rsqrt(jnp.mean(x * x, axis=-1, keepdims=True) + EPS)
    if gain is not None:
        y = y * gain
    return y


def _sigmoid(x):
    return 0.5 * jnp.tanh(0.5 * x) + 0.5


def _silu(x):
    return x * _sigmoid(x)


def _cparams(sem, vmem=VMEM_LIMIT):
    return pltpu.CompilerParams(dimension_semantics=sem, vmem_limit_bytes=vmem)


def _resident(shape):
    nd = len(shape)
    return pl.BlockSpec(shape, lambda *_: (0,) * nd, pipeline_mode=pl.Buffered(1))


def _ada_kernel(c_ref, w_ref, b_ref, o_ref):
    o_ref[0] = _dot(_silu(c_ref[...]), w_ref[0]) + b_ref[0]


def _ada_mods(cond8, ada_w, ada_b):
    tn = 1536
    return pl.pallas_call(
        _ada_kernel,
        grid=(DEPTH, 6 * D_MODEL // tn),
        in_specs=[
            pl.BlockSpec((8, D_MODEL), lambda l, j: (0, 0)),
            pl.BlockSpec((1, D_MODEL, tn), lambda l, j: (l, 0, j)),
            pl.BlockSpec((1, 1, tn), lambda l, j: (l, 0, j)),
        ],
        out_specs=pl.BlockSpec((1, 8, tn), lambda l, j: (l, 0, j)),
        out_shape=jax.ShapeDtypeStruct((DEPTH, 8, 6 * D_MODEL), F32),
        compiler_params=_cparams(("arbitrary", "arbitrary")),
        name="ada_mods",
    )(cond8, ada_w, ada_b.reshape(DEPTH, 1, 6 * D_MODEL))


ROW_UNITS = D_MODEL // LANES


def _rows_from_units(ref, n_rows):
    return jnp.concatenate([ref[pl.ds(c, n_rows, stride=ROW_UNITS), :] for c in range(ROW_UNITS)], axis=1)


def _rows_to_units(ref, val):
    n_rows = val.shape[0]
    for c in range(ROW_UNITS):
        ref[pl.ds(c, n_rows, stride=ROW_UNITS), :] = val[:, c * LANES:(c + 1) * LANES]


def _moe_combine(y_refs, wt_ref):
    w = wt_ref[...]
    terms = [w[:, k:k + 1] * _rows_from_units(y_refs[k], w.shape[0]) for k in range(TOP_K)]
    return (terms[0] + terms[1]) + (terms[2] + terms[3])


def _in_proj_kernel(*refs, first, even, nb_ctx):
    it = iter(refs)
    if first:
        xa_ref, xb_ref = next(it), next(it)
    else:
        xp_ref = next(it)
        y_refs = [next(it) for _ in range(TOP_K)]
        wt_ref, modp_ref = next(it), next(it)
    gain_ref, mod_ref, w_ref = next(it), next(it), next(it)
    if even:
        wlr_ref, gkw_ref, gkb_ref = next(it), next(it), next(it)
    proj_ref = next(it)
    if even:
        g_ref = next(it)
    xcur_ref = next(it)

    D = D_MODEL
    i = pl.program_id(0)
    if first:
        x = jnp.where(i < nb_ctx, xa_ref[...], xb_ref[...])
    else:
        x = xp_ref[...] + modp_ref[0][:, 5 * D:6 * D] * _moe_combine(y_refs, wt_ref)
    xcur_ref[...] = x
    m = mod_ref[0]
    hn = _rms(x, gain_ref[...]) * (1.0 + m[:, D:2 * D]) + m[:, 0:D]
    proj_ref[...] = _dot(hn, w_ref[...])
    if even:
        z = _dot(_dot(hn, wlr_ref[...]), gkw_ref[...]) + gkb_ref[...]
        g_ref[...] = (jnp.minimum(z, 0.0) - jnp.log(1.0 + jnp.exp(-jnp.abs(z)))) * (1.0 / GLA_NORMALIZER)


def _cond_row(i, nb_ctx, tiles_per_lat_seq):
    return jnp.where(i < nb_ctx, 0, 1 + (i - nb_ctx) // tiles_per_lat_seq)


def _in_proj(layer, n_ctx, n_tok, lat_seq, mods, gain, w_main, *, x_parts=None, x_prev=None, y4=None, even_extra=None):
    first = x_parts is not None
    even = even_extra is not None
    tm = BIG_TOKEN_TILE if first else TOKEN_TILE
    nb = n_tok // tm
    nb_ctx = n_ctx // tm
    tpl = lat_seq // tm
    np_cols = w_main.shape[1]
    D = D_MODEL

    def mod_map(l):
        return lambda i: (l * 8 + _cond_row(i, nb_ctx, tpl), 0, 0)

    row = lambda i: (i, 0)
    args, specs = [], []
    if first:
        xa, xb = x_parts
        args += [xa, xb]
        specs += [pl.BlockSpec((tm, D), lambda i: (jnp.minimum(i, nb_ctx - 1), 0)),
                  pl.BlockSpec((tm, D), lambda i: (jnp.maximum(i - nb_ctx, 0), 0))]
    else:
        y4, wts = y4
        args += [x_prev] + [y4] * TOP_K + [wts, mods]
        specs += [pl.BlockSpec((tm, D), row)]
        specs += [pl.BlockSpec((tm * ROW_UNITS, LANES), (lambda k: (lambda i: (k * nb + i, 0)))(k))
                  for k in range(TOP_K)]
        specs += [pl.BlockSpec((tm, LANES), row), pl.BlockSpec((1, 1, 6 * D), mod_map(layer - 1))]
    args += [gain.reshape(1, D), mods, w_main]
    specs += [_resident((1, D)), pl.BlockSpec((1, 1, 6 * D), mod_map(layer)), _resident((D, np_cols))]
    out_shapes = [jax.ShapeDtypeStruct((n_tok, np_cols), F32)]
    out_specs = [pl.BlockSpec((tm, np_cols), row)]
    if even:
        w_lr, gkw, gkb = even_extra
        args += [w_lr, gkw, gkb]
        specs += [_resident(w_lr.shape), _resident(gkw.shape), _resident(gkb.shape)]
        out_shapes.append(jax.ShapeDtypeStruct((n_tok, 2 * GLA_HEADS * GLA_DK), F32))
        out_specs.append(pl.BlockSpec((tm, 2 * GLA_HEADS * GLA_DK), row))
    out_shapes.append(jax.ShapeDtypeStruct((n_tok, D), F32))
    out_specs.append(pl.BlockSpec((tm, D), row))
    return pl.pallas_call(
        functools.partial(_in_proj_kernel, first=first, even=even, nb_ctx=nb_ctx),
        grid=(nb,),
        in_specs=specs,
        out_specs=out_specs,
        out_shape=out_shapes,
        compiler_params=_cparams(("arbitrary",)),
        name=f"in_proj_l{layer}",
    )(*args)


def _tri(c, lower):
    r = lax.broadcasted_iota(jnp.int32, (c, c), 0)
    s = lax.broadcasted_iota(jnp.int32, (c, c), 1)
    return (r >= s) if lower else (r <= s)


def _cumsum_mm(tri_bf16, g):
    g_hi = g.astype(BF16)
    g_lo = (g - g_hi.astype(F32)).astype(BF16)
    r = _dot(tri_bf16, jnp.concatenate([g_hi, g_lo], axis=1))
    w = g.shape[1]
    return r[:, :w] + r[:, w:]


def _gated_scan_kernel(*refs, variant, T, nseq, has_s0, want_final, hpb):
    C = SCAN_CHUNK
    ncs = T // C
    nc = nseq * ncs
    it = iter(refs)
    if variant == "gla":
        q_ref, k_ref, v_ref, og_ref, gf_ref, gb_ref, gain_ref = (next(it) for _ in range(7))
    else:
        q_ref, ff_ref, fb_ref, v_ref, og_ref, lbl_ref, gain_ref = (next(it) for _ in range(7))
    s0_ref = next(it) if has_s0 else None
    y_ref = next(it)
    sfin_ref = next(it) if want_final else None
    qi_scr, o_scr, u_scr, dec_scr, sin_scr = (next(it) for _ in range(5))

    lane = lax.broadcasted_iota(jnp.int32, (1, LANES), 1)
    if hpb == 2:
        masks = [lane < GLA_DK, lane >= GLA_DK]
    else:
        masks = [None]
    tri_l, tri_u = _tri(C, True), _tri(C, False)
    tri_l16, tri_u16 = tri_l.astype(F32).astype(BF16), tri_u.astype(F32).astype(BF16)
    H = C // 2
    upper_rows = lax.broadcasted_iota(jnp.int32, (C, 1), 0) < H

    if variant == "hgrn":
        lgs = [lbl_ref[l] for l in range(DEPTH)]
        mx = functools.reduce(jnp.maximum, lgs)
        es = [jnp.exp(l - mx) for l in lgs]
        tot = functools.reduce(lambda a, b: a + b, es)
        ps = [e / tot for e in es]
        layer = DEPTH - 1
        lb = functools.reduce(lambda a, b: a + b, ps[:layer + 1]) - ps[0]
        lb_f, lb_b = lb[0:1], lb[1:2]

    for n in range(nc):
        rows = pl.ds(n * C, C)
        if variant == "gla":
            q = q_ref[rows, :] * (GLA_DK ** -0.5)
            kf = kb = k_ref[rows, :]
            gf, gb = gf_ref[rows, :], gb_ref[rows, :]
        else:
            q = _silu(q_ref[rows, :])
            f_f = lb_f + (1.0 - lb_f) * _sigmoid(ff_ref[rows, :])
            f_b = lb_b + (1.0 - lb_b) * _sigmoid(fb_ref[rows, :])
            kf, kb = 1.0 - f_f, 1.0 - f_b
            gf, gb = jnp.log(f_f), jnp.log(f_b)
        bf = _cumsum_mm(tri_l16, gf)
        bb = _cumsum_mm(tri_u16, gb)
        bf_end, bb_end = bf[C - 1:C], bb[0:1]
        ref_f = jnp.where(upper_rows, bf[H // 2 - 1:H // 2], bf[H + H // 2 - 1:H + H // 2])
        ref_b = jnp.where(upper_rows, bb[H // 2:H // 2 + 1], bb[H + H // 2:H + H // 2 + 1])
        qd_f, qd_b = q * jnp.exp(bf - ref_f), q * jnp.exp(bb - ref_b)
        kd_f = [jnp.where(upper_rows, kf * jnp.exp(bf[H // 2 - 1:H // 2] - bf), 0.0),
                kf * jnp.exp(bf[H + H // 2 - 1:H + H // 2] - bf)]
        kd_b = [kb * jnp.exp(bb[H // 2:H // 2 + 1] - bb),
                jnp.where(upper_rows, 0.0, kb * jnp.exp(bb[H + H // 2:H + H // 2 + 1] - bb))]
        kend = jnp.concatenate([kf * jnp.exp(bf_end - bf), kb * jnp.exp(bb_end - bb)], axis=1)
        qi_scr[rows, :] = jnp.concatenate([q * jnp.exp(bf), q * jnp.exp(bb)], axis=1)
        dec_scr[n] = jnp.broadcast_to(jnp.concatenate([jnp.exp(bf_end), jnp.exp(bb_end)], axis=1), (8, 2 * LANES))
        for h in range(hpb):
            v = v_ref[rows, h * LANES:(h + 1) * LANES]
            if masks[h] is None:
                qf_h, qb_h, kend_h = qd_f, qd_b, kend
            else:
                qf_h, qb_h = jnp.where(masks[h], qd_f, 0.0), jnp.where(masks[h], qd_b, 0.0)
                kend_h = jnp.where(jnp.concatenate([masks[h], masks[h]], axis=1), kend, 0.0)
            s_f = jnp.concatenate([_dot_nt(qf_h[:H], kd_f[0]), _dot_nt(qf_h[H:], kd_f[1])], axis=0)
            s_b = jnp.concatenate([_dot_nt(qb_h[:H], kd_b[0]), _dot_nt(qb_h[H:], kd_b[1])], axis=0)
            s = jnp.where(tri_l, s_f, 0.0) + jnp.where(tri_u, s_b, 0.0)
            o_scr[rows, h * LANES:(h + 1) * LANES] = _dot(s, v)
            u_scr[h, n] = _dot(v.T, kend_h)

    for sq in range(nseq):
        for h in range(hpb):
            if has_s0:
                s_f, s_b = s0_ref[sq, 0, 0].T, s0_ref[sq, 1, 0].T
                if masks[h] is not None:
                    s_f, s_b = jnp.where(masks[h], s_f, 0.0), jnp.where(masks[h], s_b, 0.0)
            else:
                s_f = s_b = jnp.zeros((LANES, LANES), F32)
            for n in range(sq * ncs, (sq + 1) * ncs):
                sin_scr[h, n, :, 0:LANES] = s_f
                s_f = s_f * dec_scr[n, 0:1, 0:LANES] + u_scr[h, n, :, 0:LANES]
            for n in reversed(range(sq * ncs, (sq + 1) * ncs)):
                sin_scr[h, n, :, LANES:2 * LANES] = s_b
                s_b = s_b * dec_scr[n, 0:1, LANES:2 * LANES] + u_scr[h, n, :, LANES:2 * LANES]
            if want_final:
                if h == 0:
                    fin_f, fin_b = s_f, s_b
                else:
                    fin_f, fin_b = fin_f + s_f, fin_b + s_b
        if want_final:
            sfin_ref[sq, 0, 0] = fin_f.T
            sfin_ref[sq, 1, 0] = fin_b.T

    gain = gain_ref[...]
    for n in range(nc):
        rows = pl.ds(n * C, C)
        qi = qi_scr[rows, :]
        for h in range(hpb):
            cols = slice(h * LANES, (h + 1) * LANES)
            o = o_scr[rows, cols] + _dot_nt(qi, sin_scr[h, n])
            y_ref[rows, cols] = (_rms(o, gain) * _silu(og_ref[rows, cols])).astype(y_ref.dtype)


def _scan_scratch(T, hpb):
    nc = T // SCAN_CHUNK
    return [
        pltpu.VMEM((T, 2 * LANES), F32),
        pltpu.VMEM((T, hpb * LANES), F32),
        pltpu.VMEM((hpb, nc, LANES, 2 * LANES), F32),
        pltpu.VMEM((nc, 8, 2 * LANES), F32),
        pltpu.VMEM((hpb, nc, LANES, 2 * LANES), F32),
    ]


def _seqs_per_step(B, T):
    nseq = max(1, min(B, SCAN_ROWS // T))
    assert B % nseq == 0
    return nseq


def _gla_scan(proj, gdec, gain, s0, row0, B, T, want_final):
    nseq = _seqs_per_step(B, T)
    R = nseq * T
    rb0 = row0 // R
    assert row0 % R == 0
    has_s0 = s0 is not None

    def col(base, width=LANES):
        return lambda b, j: (rb0 + b, base // width + j)

    args = [proj, proj, proj, proj, gdec, gdec, gain.reshape(1, GLA_DV)]
    specs = [pl.BlockSpec((R, LANES), col(_GQ)), pl.BlockSpec((R, LANES), col(_GK)),
             pl.BlockSpec((R, 2 * LANES), col(_GV, 2 * LANES)), pl.BlockSpec((R, 2 * LANES), col(_GG, 2 * LANES)),
             pl.BlockSpec((R, LANES), col(0)), pl.BlockSpec((R, LANES), col(GLA_HEADS * GLA_DK)),
             pl.BlockSpec((1, GLA_DV), lambda b, j: (0, 0))]
    st_spec = pl.BlockSpec((nseq, 2, 1, LANES, LANES), lambda b, j: (b, 0, j, 0, 0))
    if has_s0:
        args.append(s0.reshape(B, 2, GLA_HEADS // 2, 2 * GLA_DK, GLA_DV))
        specs.append(st_spec)
    out_shapes = [jax.ShapeDtypeStruct((B * T, GLA_HEADS * GLA_DV), BF16)]
    out_specs = [pl.BlockSpec((R, 2 * LANES), lambda b, j: (b, j))]
    if want_final:
        out_shapes.append(jax.ShapeDtypeStruct((B, 2, GLA_HEADS // 2, 2 * GLA_DK, GLA_DV), F32))
        out_specs.append(st_spec)
    kern = functools.partial(_gated_scan_kernel, variant="gla", T=T, nseq=nseq, has_s0=has_s0,
                             want_final=want_final, hpb=2)
    return pl.pallas_call(
        kern, grid=(B // nseq, GLA_HEADS // 2), in_specs=specs, out_specs=out_specs, out_shape=out_shapes,
        scratch_shapes=_scan_scratch(R, 2),
        compiler_params=_cparams(("arbitrary", "arbitrary")), name=f"gla_scan_T{T}",
    )(*args)


def _hgrn_scan(proj, lb_logits, gain, s0, row0, B, T, want_final):
    nseq = _seqs_per_step(B, T)
    R = nseq * T
    rb0 = row0 // R
    assert row0 % R == 0
    has_s0 = s0 is not None

    def col(base):
        return lambda b, j: (rb0 + b, base // LANES + j)

    args = [proj, proj, proj, proj, proj, lb_logits, gain.reshape(1, HG_DV)]
    specs = [pl.BlockSpec((R, LANES), col(_HQ)), pl.BlockSpec((R, LANES), col(_HFF)),
             pl.BlockSpec((R, LANES), col(_HFB)), pl.BlockSpec((R, LANES), col(_HI)),
             pl.BlockSpec((R, LANES), col(_HG)),
             pl.BlockSpec((DEPTH, 2, LANES), lambda b, j: (0, 0, j)),
             pl.BlockSpec((1, HG_DV), lambda b, j: (0, 0))]
    st_spec = pl.BlockSpec((nseq, 2, 1, LANES, LANES), lambda b, j: (b, 0, j, 0, 0))
    if has_s0:
        args.append(s0.reshape(B, 2, HG_HEADS, HG_DK, HG_DV))
        specs.append(st_spec)
    out_shapes = [jax.ShapeDtypeStruct((B * T, HG_HEADS * HG_DV), BF16)]
    out_specs = [pl.BlockSpec((R, LANES), lambda b, j: (b, j))]
    if want_final:
        out_shapes.append(jax.ShapeDtypeStruct((B, 2, HG_HEADS, HG_DK, HG_DV), F32))
        out_specs.append(st_spec)
    kern = functools.partial(_gated_scan_kernel, variant="hgrn", T=T, nseq=nseq, has_s0=has_s0,
                             want_final=want_final, hpb=1)
    return pl.pallas_call(
        kern, grid=(B // nseq, HG_HEADS), in_specs=specs, out_specs=out_specs, out_shape=out_shapes,
        scratch_shapes=_scan_scratch(R, 1),
        compiler_params=_cparams(("arbitrary", "arbitrary")), name=f"hgrn_scan_T{T}",
    )(*args)


def _ret_scan_kernel(*refs, T, nseq, has_s0, want_final, rope):
    C = SCAN_CHUNK
    ncs = T // C
    nc = nseq * ncs
    it = iter(refs)
    q_ref, k_ref, v_ref, og_ref, dexp_ref = (next(it) for _ in range(5))
    if rope:
        cos_ref, sin_ref = next(it), next(it)
    s0_ref = next(it) if has_s0 else None
    y_ref = next(it)
    sfin_ref = next(it) if want_final else None
    qi_scr, o_scr, u_scr, sin_scr = (next(it) for _ in range(4))

    lg = jnp.log1p(-jnp.exp2(-dexp_ref[0]))
    lg_f, lg_b = lg[0:1], lg[1:2]
    r = lax.broadcasted_iota(jnp.int32, (C, C), 0)
    s = lax.broadcasted_iota(jnp.int32, (C, C), 1)
    dist = (r - s).astype(F32)
    dmask = (jnp.where(r >= s, jnp.exp(jnp.maximum(dist, 0.0) * lg_f[:, 0:1]), 0.0)
             + jnp.where(r <= s, jnp.exp(jnp.maximum(-dist, 0.0) * lg_b[:, 0:1]), 0.0))
    pos = lax.broadcasted_iota(jnp.int32, (C, LANES), 0).astype(F32)
    xi = jnp.concatenate([jnp.exp((pos + 1.0) * lg_f), jnp.exp((C - pos) * lg_b)], axis=1)
    zeta = jnp.concatenate([jnp.exp((C - 1.0 - pos) * lg_f), jnp.exp(pos * lg_b)], axis=1)
    d_f, d_b = jnp.exp(C * lg_f), jnp.exp(C * lg_b)

    def rot(x, seq_rows):
        if not rope:
            return x
        return x * cos_ref[seq_rows, :] + pltpu.roll(x, RET_DK // 2, axis=1) * sin_ref[seq_rows, :]

    for n in range(nc):
        rows = pl.ds(n * C, C)
        seq_rows = pl.ds((n % ncs) * C, C)
        q = rot(q_ref[rows, :], seq_rows)
        k = rot(k_ref[rows, :] * (RET_DK ** -0.5), seq_rows)
        v = v_ref[rows, :]
        o_scr[rows, :] = _dot(_dot_nt(q, k) * dmask, v)
        qi_scr[rows, :] = jnp.concatenate([q, q], axis=1) * xi
        u_scr[n] = _dot(v.T, jnp.concatenate([k, k], axis=1) * zeta)

    for sq in range(nseq):
        if has_s0:
            s_f, s_b = s0_ref[sq, 0, 0].T, s0_ref[sq, 1, 0].T
        else:
            s_f = s_b = jnp.zeros((LANES, LANES), F32)
        for n in range(sq * ncs, (sq + 1) * ncs):
            sin_scr[n, :, 0:LANES] = s_f
            s_f = s_f * d_f + u_scr[n, :, 0:LANES]
        for n in reversed(range(sq * ncs, (sq + 1) * ncs)):
            sin_scr[n, :, LANES:2 * LANES] = s_b
            s_b = s_b * d_b + u_scr[n, :, LANES:2 * LANES]
        if want_final:
            sfin_ref[sq, 0, 0] = s_f.T
            sfin_ref[sq, 1, 0] = s_b.T

    for n in range(nc):
        rows = pl.ds(n * C, C)
        o = o_scr[rows, :] + _dot_nt(qi_scr[rows, :], sin_scr[n])
        y_ref[rows, :] = (_rms(o) * _silu(og_ref[rows, :])).astype(y_ref.dtype)


def _ret_scan(proj, dexp, rope_tabs, s0, row0, B, T, want_final):
    nseq = _seqs_per_step(B, T)
    R = nseq * T
    rb0 = row0 // R
    assert row0 % R == 0
    nc = R // SCAN_CHUNK
    has_s0 = s0 is not None
    rope = rope_tabs is not None

    def col(base):
        return lambda b, j: (rb0 + b, base // LANES + j)

    args = [proj, proj, proj, proj, dexp]
    specs = [pl.BlockSpec((R, LANES), col(_RQ)), pl.BlockSpec((R, LANES), col(_RK)),
             pl.BlockSpec((R, LANES), col(_RV)), pl.BlockSpec((R, LANES), col(_RG)),
             pl.BlockSpec((1, 2, LANES), lambda b, j: (j, 0, 0))]
    if rope:
        args += list(rope_tabs)
        specs += [pl.BlockSpec((T, LANES), lambda b, j: (0, 0))] * 2
    st_spec = pl.BlockSpec((nseq, 2, 1, LANES, LANES), lambda b, j: (b, 0, j, 0, 0))
    if has_s0:
        args.append(s0.reshape(B, 2, RET_HEADS, RET_DK, RET_DV))
        specs.append(st_spec)
    out_shapes = [jax.ShapeDtypeStruct((B * T, RET_HEADS * RET_DV), BF16)]
    out_specs = [pl.BlockSpec((R, LANES), lambda b, j: (b, j))]
    if want_final:
        out_shapes.append(jax.ShapeDtypeStruct((B, 2, RET_HEADS, RET_DK, RET_DV), F32))
        out_specs.append(st_spec)
    kern = functools.partial(_ret_scan_kernel, T=T, nseq=nseq, has_s0=has_s0, want_final=want_final, rope=rope)
    scratch = [pltpu.VMEM((R, 2 * LANES), F32), pltpu.VMEM((R, LANES), F32),
               pltpu.VMEM((nc, LANES, 2 * LANES), F32), pltpu.VMEM((nc, LANES, 2 * LANES), F32)]
    return pl.pallas_call(
        kern, grid=(B // nseq, RET_HEADS), in_specs=specs, out_specs=out_specs, out_shape=out_shapes,
        scratch_shapes=scratch,
        compiler_params=_cparams(("arbitrary", "arbitrary")), name=f"ret_scan_T{T}",
    )(*args)


def _out_proj_kernel(*refs, n_mix, nb_ctx):
    it = iter(refs)
    y_refs = [(next(it), next(it)) for _ in range(n_mix)]
    wo_ref, x_ref, mod_ref, gain_ref, rw_ref, rb_ref = (next(it) for _ in range(6))
    xnew_ref, hn_ref, idx_ref, wt_ref, rank_ref, cnt_ref = (next(it) for _ in range(6))
    cnt_scr = next(it)
    D = D_MODEL
    is_ctx = pl.program_id(0) < nb_ctx
    mix = None
    r0 = 0
    for ya_ref, yb_ref in y_refs:
        w = ya_ref.shape[1]
        part = _dot(jnp.where(is_ctx, ya_ref[...], yb_ref[...]), wo_ref[r0:r0 + w, :])
        mix = part if mix is None else mix + part
        r0 += w
    m = mod_ref[0]
    xn = x_ref[...] + m[:, 2 * D:3 * D] * mix
    xnew_ref[...] = xn
    hn = _rms(xn, gain_ref[...]) * (1.0 + m[:, 4 * D:5 * D]) + m[:, 3 * D:4 * D]
    hn_hi = hn.astype(BF16)
    _rows_to_units(hn_ref, hn)
    hn_lo = (hn - hn_hi.astype(F32)).astype(BF16)
    rw = rw_ref[...]
    rw_hi = rw.astype(BF16)
    rw_lo = (rw - rw_hi.astype(F32)).astype(BF16)
    logits = (_dot(hn_hi, rw_hi) + (_dot(hn_lo, rw_hi) + _dot(hn_hi, rw_lo))) + rb_ref[...]
    lane = lax.broadcasted_iota(jnp.int32, logits.shape, 1).astype(F32)
    vals, idxs = [], []
    cur = logits
    for _ in range(TOP_K):
        mx = jnp.max(cur, axis=-1, keepdims=True)
        ik = jnp.min(jnp.where(cur == mx, lane, float(LANES)), axis=-1, keepdims=True)
        vals.append(mx)
        idxs.append(ik)
        cur = jnp.where(lane == ik, -jnp.inf, cur)
    es = [jnp.exp(v - vals[0]) for v in vals]
    tot = (es[0] + es[1]) + (es[2] + es[3])
    idx_out = jnp.zeros(logits.shape, F32)
    wt_out = jnp.zeros(logits.shape, F32)
    for k in range(TOP_K):
        idx_out = jnp.where(lane == float(k), idxs[k], idx_out)
        wt_out = jnp.where(lane == float(k), es[k] / tot, wt_out)
    idx_ref[...] = idx_out.astype(jnp.int32)
    wt_ref[...] = wt_out

    @pl.when(pl.program_id(0) == 0)
    def _():
        cnt_scr[...] = jnp.zeros(cnt_scr.shape, F32)

    tm = logits.shape[0]
    hits = [lane == idxs[k] for k in range(TOP_K)]
    sel = jnp.zeros(logits.shape, F32)
    for k in range(TOP_K):
        sel = sel + jnp.where(hits[k], 1.0, 0.0)
    rr = lax.broadcasted_iota(jnp.int32, (tm, tm), 0)
    cc = lax.broadcasted_iota(jnp.int32, (tm, tm), 1)
    before = jnp.where(rr > cc, 1.0, 0.0).astype(BF16)
    rank_all = cnt_scr[0:1, :] + _dot(before, sel.astype(BF16))
    rank_out = jnp.zeros(logits.shape, F32)
    for k in range(TOP_K):
        rk = jnp.sum(jnp.where(hits[k], rank_all, 0.0), axis=-1, keepdims=True)
        rank_out = jnp.where(lane == float(k), rk, rank_out)
    rank_ref[...] = rank_out.astype(jnp.int32)
    total = cnt_scr[...] + jnp.sum(sel, axis=0, keepdims=True)
    cnt_scr[...] = total
    cnt_ref[...] = total.astype(jnp.int32)


def _out_proj(layer, n_ctx, n_tok, lat_seq, ys, w_out, x_cur, mods, gain, rw_pad, rb_pad):
    tm = BIG_TOKEN_TILE
    nb, nb_ctx, tpl = n_tok // tm, n_ctx // tm, lat_seq // tm
    D = D_MODEL
    row = lambda i: (i, 0)
    specs = []
    for ya, _ in ys:
        specs += [pl.BlockSpec((tm, ya.shape[1]), lambda i: (jnp.minimum(i, nb_ctx - 1), 0)),
                  pl.BlockSpec((tm, ya.shape[1]), lambda i: (jnp.maximum(i - nb_ctx, 0), 0))]
    specs += [_resident((D, D)), pl.BlockSpec((tm, D), row),
              pl.BlockSpec((1, 1, 6 * D), lambda i: (layer * 8 + _cond_row(i, nb_ctx, tpl), 0, 0)),
              _resident((1, D)), _resident((D, LANES)), _resident((1, LANES))]
    return pl.pallas_call(
        functools.partial(_out_proj_kernel, n_mix=len(ys), nb_ctx=nb_ctx),
        grid=(nb,),
        in_specs=specs,
        out_specs=[pl.BlockSpec((tm, D), row), pl.BlockSpec((tm * ROW_UNITS, LANES), row),
                   pl.BlockSpec((tm, LANES), row), pl.BlockSpec((tm, LANES), row),
                   pl.BlockSpec((tm, LANES), row), pl.BlockSpec((8, LANES), lambda i: (0, 0))],
        out_shape=[jax.ShapeDtypeStruct((n_tok, D), F32), jax.ShapeDtypeStruct((n_tok * ROW_UNITS, LANES), F32),
                   jax.ShapeDtypeStruct((n_tok, LANES), jnp.int32), jax.ShapeDtypeStruct((n_tok, LANES), F32),
                   jax.ShapeDtypeStruct((n_tok, LANES), jnp.int32), jax.ShapeDtypeStruct((8, LANES), jnp.int32)],
        scratch_shapes=[pltpu.VMEM((8, LANES), F32)],
        compiler_params=_cparams(("arbitrary",)),
        name=f"out_proj_l{layer}",
    )(*[y for pair in ys for y in pair], w_out, x_cur, mods, gain.reshape(1, D), rw_pad, rb_pad)


W1_SPLIT = 4
W2_SPLIT = 2


def _moe_kernel(seg_ref, nt_ref, code_hbm, x_hbm, w1_hbm, b1_ref, w2_hbm, b2_ref, y_hbm, w1_scr, b1_scr, w2_scr,
                act_scr, xbuf, obuf, w1buf, w2buf, code_smem, csem, gsem, ssem, wsem, *, n_tok, layer):
    tm = MOE_TILE
    U = ROW_UNITS
    n_code_tiles = code_hbm.shape[0] // tm
    e = pl.program_id(0)
    nt = nt_ref[e]
    g_first = seg_ref[e] // tm
    wslot = lax.rem(e, 2)

    def weight_copies(ex, ws):
        cw, rh = 2 * D_FF // W1_SPLIT, D_FF // W2_SPLIT
        cps = [pltpu.make_async_copy(w1_hbm.at[layer, ex, :, pl.ds(c * cw, cw)],
                                     w1buf.at[ws, :, pl.ds(c * cw, cw)], wsem.at[ws]) for c in range(W1_SPLIT)]
        cps += [pltpu.make_async_copy(w2_hbm.at[layer, ex, pl.ds(c * rh, rh), :],
                                      w2buf.at[ws, pl.ds(c * rh, rh), :], wsem.at[ws]) for c in range(W2_SPLIT)]
        return cps

    @pl.when(e == 0)
    def _():
        for cp in weight_copies(0, 0):
            cp.start(priority=1)

    @pl.when(e + 1 < N_EXPERTS)
    def _():
        for cp in weight_copies(e + 1, 1 - wslot):
            cp.start(priority=1)

    for cp in weight_copies(e, wslot):
        cp.wait()

    GATHER, SCATTER = 0, 1

    def code_copy(kind, t, p):
        first = pl.multiple_of(jnp.clip(t, 0, n_code_tiles - 1) * tm, tm)
        return pltpu.make_async_copy(code_hbm.at[pl.ds(first, tm)], code_smem.at[kind, p], csem.at[kind, p])

    def gather_start(b, p):
        for r in range(tm):
            tok = jnp.bitwise_and(code_smem[GATHER, p, r], n_tok - 1)
            pltpu.make_async_copy(x_hbm.at[pl.ds(pl.multiple_of(tok * U, U), U)],
                                  xbuf.at[b, pl.ds(r * U, U)], gsem.at[b]).start()

    def gather_wait(b):
        pltpu.make_async_copy(x_hbm.at[pl.ds(0, tm * U)], xbuf.at[b], gsem.at[b]).wait()

    def scatter_start(b, p):
        for r in range(tm):
            dst = code_smem[SCATTER, p, r]
            pltpu.make_async_copy(obuf.at[b, pl.ds(r * U, U)],
                                  y_hbm.at[pl.ds(pl.multiple_of(dst * U, U), U)], ssem.at[b]).start(priority=1)

    def scatter_wait(b):
        pltpu.make_async_copy(obuf.at[b], y_hbm.at[pl.ds(0, tm * U)], ssem.at[b]).wait()

    @pl.when(e == 0)
    def _():
        obuf[...] = jnp.zeros(obuf.shape, obuf.dtype)
        for h in range(2):
            cp = pltpu.make_async_copy(obuf.at[0], y_hbm.at[pl.ds((TOP_K * n_tok + h * tm) * U, tm * U)], ssem.at[0])
            cp.start()
            cp.wait()
        code_copy(GATHER, 0, 0).start()
        code_copy(GATHER, 1, 1).start()
        code_copy(SCATTER, n_code_tiles - 1, 1).start()
        code_copy(SCATTER, 0, 0).start()
        code_copy(GATHER, 0, 0).wait()
        gather_start(0, 0)

    @pl.when(nt > 0)
    def _():
        grp = 2 * LANES
        rr = lax.broadcasted_iota(jnp.int32, (grp, grp), 0)
        cc = lax.broadcasted_iota(jnp.int32, (grp, grp), 1)
        src = jnp.where(cc < LANES, 2 * cc, 2 * (cc - LANES) + 1)
        perm = jnp.where(rr == src, 1.0, 0.0).astype(BF16)
        for j in range(2 * D_FF // grp):
            cols = slice(j * grp, (j + 1) * grp)
            w1_scr[:, cols] = _dot(w1buf[wslot, :, cols].astype(BF16), perm).astype(BF16)
        bias = jnp.broadcast_to(b1_ref[0, 0], (8, 2 * D_FF))
        for j in range(2 * D_FF // grp):
            cols = slice(j * grp, (j + 1) * grp)
            rest = bias[:, cols]
            acc = jnp.zeros((8, grp), F32)
            for _ in range(3):
                term = rest.astype(BF16)
                acc = acc + _dot(term, perm)
                rest = rest - term.astype(F32)
            b1_scr[:, cols] = acc
        w2_scr[...] = w2buf[wslot].astype(BF16)
        blk = 4 * LANES

        def tile(g, b):
            o = 1 - b

            @pl.when(g >= 0)
            def _():
                code_copy(GATHER, g + 1, o).wait()
                code_copy(SCATTER, g - 1, o).wait()
                gather_start(o, o)
                scatter_start(o, o)
                code_copy(GATHER, g + 2, b).start()
                code_copy(SCATTER, g + 1, o).start()

            gather_wait(b)
            x = _rows_from_units(xbuf.at[b], tm).astype(BF16)
            for c in range(2 * D_FF // blk):
                cols = slice(c * blk, (c + 1) * blk)
                hid = _dot(x, w1_scr[:, cols]) + b1_scr[0:1, cols]
                acts = []
                for j in range(blk // grp):
                    glu = jnp.minimum(hid[:, j * grp:j * grp + LANES], SWIGLU_LIMIT)
                    lin = jnp.clip(hid[:, j * grp + LANES:(j + 1) * grp], -SWIGLU_LIMIT, SWIGLU_LIMIT)
                    acts.append(glu * _sigmoid(SWIGLU_ALPHA * glu) * (lin + 1.0))
                act_scr[:, c * (blk // 2):(c + 1) * (blk // 2)] = jnp.concatenate(acts, axis=1).astype(BF16)
            y = _dot(act_scr[...], w2_scr[...]) + b2_ref[0, 0]

            @pl.when(g >= 1)
            def _():
                scatter_wait(b)

            _rows_to_units(obuf.at[b], y)

        g_end = g_first + nt

        def tile_pair(m, carry):
            for b in range(2):
                g = 2 * m + b

                @pl.when(jnp.logical_and(g >= g_first, g < g_end))
                def _():
                    tile(g, b)

            return carry

        lax.fori_loop(g_first // 2, (g_end + 1) // 2, tile_pair, 0)

    @pl.when(e == N_EXPERTS - 1)
    def _():
        g_end = g_first + nt
        last = lax.rem(g_end + 1, 2)
        code_copy(SCATTER, g_end - 1, last).wait()
        scatter_start(last, last)
        scatter_wait(last)
        scatter_wait(1 - last)
        gather_wait(1 - last)
        code_copy(GATHER, g_end + 1, last).wait()
        code_copy(SCATTER, g_end, 1 - last).wait()


def _moe_experts(layer, n_tok, seg_start, tiles_per_e, code, x_units, w1, b1, w2, b2):
    tm = MOE_TILE
    D, F2, U = D_MODEL, 2 * D_FF, ROW_UNITS
    any_space = pl.BlockSpec(memory_space=pl.ANY)
    grid_spec = pltpu.PrefetchScalarGridSpec(
        num_scalar_prefetch=2,
        grid=(N_EXPERTS,),
        in_specs=[any_space, any_space,
                  any_space, pl.BlockSpec((1, 1, 1, F2), lambda e, sg, nt: (layer, e, 0, 0)),
                  any_space, pl.BlockSpec((1, 1, 1, D), lambda e, sg, nt: (layer, e, 0, 0))],
        out_specs=any_space,
        scratch_shapes=[pltpu.VMEM((D, F2), BF16), pltpu.VMEM((8, F2), F32), pltpu.VMEM((D_FF, D), BF16),
                        pltpu.VMEM((tm, D_FF), BF16),
                        pltpu.VMEM((2, tm * U, LANES), F32), pltpu.VMEM((2, tm * U, LANES), F32),
                        pltpu.VMEM((2, D, F2), F32), pltpu.VMEM((2, D_FF, D), F32),
                        pltpu.SMEM((2, 2, tm), jnp.int32),
                        pltpu.SemaphoreType.DMA((2, 2)), pltpu.SemaphoreType.DMA((2,)), pltpu.SemaphoreType.DMA((2,)),
                        pltpu.SemaphoreType.DMA((2,))],
    )
    n_out_rows = TOP_K * n_tok + 2 * tm
    return pl.pallas_call(
        functools.partial(_moe_kernel, n_tok=n_tok, layer=layer),
        grid_spec=grid_spec,
        out_shape=jax.ShapeDtypeStruct((n_out_rows * U, LANES), F32),
        compiler_params=_cparams(("arbitrary",)),
        name="moe_experts",
    )(seg_start, tiles_per_e, code, x_units, w1, b1.reshape(DEPTH, N_EXPERTS, 1, F2),
      w2, b2.reshape(DEPTH, N_EXPERTS, 1, D))


def _slot_code_kernel(seg_ref, idx_ref, rank_ref, unused_hbm, code_ref, sem, *, n_tok):
    i = pl.program_id(0)
    tm = idx_ref.shape[0] // TOP_K

    @pl.when(i == 0)
    def _():
        cp = pltpu.make_async_copy(unused_hbm, code_ref, sem)
        cp.start()
        cp.wait()

    base = i * tm
    for t in range(tm):
        for k in range(TOP_K):
            p = t * TOP_K + k
            code_ref[seg_ref[idx_ref[p]] + rank_ref[p]] = base + (k * n_tok + t)


def _slot_codes(idx, rank, seg_start, unused):
    n_tok = idx.shape[0]
    tm = TOKEN_TILE
    pairs = lambda a: a[:, :TOP_K].reshape(-1)
    flat = pl.BlockSpec((tm * TOP_K,), lambda i, sg: (i,), memory_space=pltpu.SMEM)
    grid_spec = pltpu.PrefetchScalarGridSpec(
        num_scalar_prefetch=1,
        grid=(n_tok // tm,),
        in_specs=[flat, flat, pl.BlockSpec(memory_space=pl.ANY)],
        out_specs=pl.BlockSpec(memory_space=pltpu.SMEM),
        scratch_shapes=[pltpu.SemaphoreType.DMA(())],
    )
    return pl.pallas_call(
        functools.partial(_slot_code_kernel, n_tok=n_tok),
        grid_spec=grid_spec,
        out_shape=jax.ShapeDtypeStruct(unused.shape, jnp.int32),
        compiler_params=_cparams(("arbitrary",)),
        name="slot_codes",
    )(seg_start, pairs(idx), pairs(rank), unused)


def _moe_layer(layer, hn_units, idx, rank, counts, w1, b1, w2, b2):
    n_tok = hn_units.shape[0] // ROW_UNITS
    assert n_tok & (n_tok - 1) == 0
    tm = MOE_TILE
    n_pairs = n_tok * TOP_K
    n_slots = ((n_pairs + N_EXPERTS * (tm - 1)) // tm + 1) * tm
    tiles_per_e = (counts + tm - 1) // tm
    seg_start = (jnp.cumsum(tiles_per_e) - tiles_per_e) * tm
    unused = n_pairs + jnp.arange(n_slots, dtype=jnp.int32) % (2 * tm)
    code = _slot_codes(idx, rank, seg_start, unused)
    return _moe_experts(layer, n_tok, seg_start, tiles_per_e, code, hn_units, w1, b1, w2, b2)


def _final_kernel(*refs, nb_ctx):
    x_ref = refs[0]
    y_refs = refs[1:1 + TOP_K]
    wt_ref, mod_ref, gain_ref, oa_ref, ob_ref = refs[1 + TOP_K:]
    D = D_MODEL
    i = pl.program_id(0)
    x = x_ref[...] + mod_ref[0][:, 5 * D:6 * D] * _moe_combine(y_refs, wt_ref)
    out = _rms(x, gain_ref[...])

    @pl.when(i < nb_ctx)
    def _():
        oa_ref[...] = out

    @pl.when(i >= nb_ctx)
    def _():
        ob_ref[...] = out


def _final(n_ctx, n_tok, lat_seq, x_new, y4, wts, mods, gain):
    tm = TOKEN_TILE
    nb, nb_ctx, tpl = n_tok // tm, n_ctx // tm, lat_seq // tm
    D = D_MODEL
    specs = [pl.BlockSpec((tm, D), lambda i: (i, 0))]
    specs += [pl.BlockSpec((tm * ROW_UNITS, LANES), (lambda k: (lambda i: (k * nb + i, 0)))(k)) for k in range(TOP_K)]
    specs += [pl.BlockSpec((tm, LANES), lambda i: (i, 0)),
              pl.BlockSpec((1, 1, 6 * D), lambda i: ((DEPTH - 1) * 8 + _cond_row(i, nb_ctx, tpl), 0, 0)),
              _resident((1, D))]
    return pl.pallas_call(
        functools.partial(_final_kernel, nb_ctx=nb_ctx),
        grid=(nb,),
        in_specs=specs,
        out_specs=[pl.BlockSpec((tm, D), lambda i: (jnp.minimum(i, nb_ctx - 1), 0)),
                   pl.BlockSpec((tm, D), lambda i: (jnp.maximum(i - nb_ctx, 0), 0))],
        out_shape=[jax.ShapeDtypeStruct((n_ctx, D), F32), jax.ShapeDtypeStruct((n_tok - n_ctx, D), F32)],
        compiler_params=_cparams(("arbitrary",)),
        name="final_norm",
    )(x_new, *([y4] * TOP_K), wts, mods, gain.reshape(1, D))


def _rope_tables(n_tokens):
    rows = n_tokens // GRID_W
    r = jnp.repeat(jnp.arange(rows, dtype=F32), GRID_W)
    col = jnp.tile(jnp.arange(GRID_W, dtype=F32), rows)
    nf = RET_DK // 4
    inv = ROPE_BASE ** (-jnp.arange(nf, dtype=F32) / nf)
    ang = jnp.concatenate([r[:, None] * inv, col[:, None] * inv], axis=-1)
    cos, sin = jnp.cos(ang), jnp.sin(ang)
    return jnp.concatenate([cos, cos], axis=-1), jnp.concatenate([-sin, sin], axis=-1)


def kernel(x_prompt, x_sample, c, state_gla, state_ret, state_hgrn, c_ctx, norm_mix, norm_ffn, ada_w, ada_b,
           w_in_even, w_out_even, gla_gk_w, gla_gk_b, gla_gain, ret_decay_exp, w_in_odd, w_out_odd,
           hgrn_lb_logits, hgrn_gain, router_w, router_b, moe_w1, moe_b1, moe_w2, moe_b2, final_norm):
    D = D_MODEL
    B_ctx, T_ctx, _ = x_prompt.shape
    B_lat, T_lat, _ = x_sample.shape
    n_ctx, n_lat = B_ctx * T_ctx, B_lat * T_lat
    n_tok = n_ctx + n_lat
    assert n_ctx % BIG_TOKEN_TILE == 0 and T_lat % BIG_TOKEN_TILE == 0 and B_lat + 1 <= 8
    assert T_ctx % SCAN_CHUNK == 0 and T_lat % SCAN_CHUNK == 0 and n_ctx % T_lat == 0

    cond8 = jnp.concatenate([c_ctx[None, :], c, jnp.zeros((8 - 1 - B_lat, D), F32)], axis=0)
    mods = _ada_mods(cond8, ada_w, ada_b).reshape(DEPTH * 8, 1, 6 * D)

    rw_pad = jnp.pad(router_w, ((0, 0), (0, 0), (0, LANES - N_EXPERTS)))
    rb_pad = jnp.pad(router_b, ((0, 0), (0, LANES - N_EXPERTS)), constant_values=-1e30)

    w_even = w_in_even[0]
    lr0 = GLA_HEADS * (2 * GLA_DK + GLA_DV)
    lr1 = lr0 + 2 * GLA_LOW_RANK
    w_main = jnp.concatenate([w_even[:, :lr0], w_even[:, lr1:]], axis=1)
    w_lr = jnp.pad(w_even[:, lr0:lr1], ((0, 0), (0, LANES - 2 * GLA_LOW_RANK)))
    nqk = GLA_HEADS * GLA_DK
    gkw = jnp.zeros((LANES, 2 * nqk), F32)
    gkw = gkw.at[0:GLA_LOW_RANK, 0:nqk].set(gla_gk_w[0, 0])
    gkw = gkw.at[GLA_LOW_RANK:2 * GLA_LOW_RANK, nqk:].set(gla_gk_w[0, 1])
    gkb = gla_gk_b[0].reshape(1, 2 * nqk)
    proj, gdec, x_cur = _in_proj(0, n_ctx, n_tok, T_lat, mods, norm_mix[0], w_main,
                                 x_parts=(x_prompt.reshape(n_ctx, D), x_sample.reshape(n_lat, D)),
                                 even_extra=(w_lr, gkw, gkb))

    y_gla_c, fin_gla = _gla_scan(proj, gdec, gla_gain[0], None, 0, B_ctx, T_ctx, True)
    (y_gla_l,) = _gla_scan(proj, gdec, gla_gain[0], state_gla[:, 0], n_ctx, B_lat, T_lat, False)
    dexp = jnp.broadcast_to(ret_decay_exp[0].T[:, :, None], (RET_HEADS, 2, LANES))
    y_ret_c, fin_ret = _ret_scan(proj, dexp, None, None, 0, B_ctx, T_ctx, True)
    (y_ret_l,) = _ret_scan(proj, dexp, _rope_tables(T_lat), state_ret[:, 0], n_ctx, B_lat, T_lat, False)

    x_new, hn, idx, wts, rank, cnt = _out_proj(0, n_ctx, n_tok, T_lat, [(y_gla_c, y_gla_l), (y_ret_c, y_ret_l)],
                                               w_out_even[0], x_cur, mods, norm_ffn[0], rw_pad[0], rb_pad[0:1])
    y4 = _moe_layer(0, hn, idx, rank, cnt[0, :N_EXPERTS], moe_w1, moe_b1, moe_w2, moe_b2)

    (proj, x_cur) = _in_proj(1, n_ctx, n_tok, T_lat, mods, norm_mix[1], w_in_odd[0], x_prev=x_new, y4=(y4, wts))
    y_h_c, fin_h = _hgrn_scan(proj, hgrn_lb_logits, hgrn_gain[0], None, 0, B_ctx, T_ctx, True)
    (y_h_l,) = _hgrn_scan(proj, hgrn_lb_logits, hgrn_gain[0], state_hgrn[:, 0], n_ctx, B_lat, T_lat, False)
    x_new, hn, idx, wts, rank, cnt = _out_proj(1, n_ctx, n_tok, T_lat, [(y_h_c, y_h_l)], w_out_odd[0], x_cur, mods,
                                               norm_ffn[1], rw_pad[1], rb_pad[1:2])
    y4 = _moe_layer(1, hn, idx, rank, cnt[0, :N_EXPERTS], moe_w1, moe_b1, moe_w2, moe_b2)

    y_ctx, y_lat = _final(n_ctx, n_tok, T_lat, x_new, y4, wts, mods, final_norm)

    new_state_gla = fin_gla.reshape(B_ctx, 1, 2, GLA_HEADS, GLA_DK, GLA_DV)
    new_state_ret = fin_ret.reshape(B_ctx, 1, 2, RET_HEADS, RET_DK, RET_DV)
    new_state_hgrn = fin_h.reshape(B_ctx, 1, 2, HG_HEADS, HG_DK, HG_DV)
    return (y_ctx.reshape(B_ctx, T_ctx, D), y_lat.reshape(B_lat, T_lat, D), new_state_gla, new_state_ret,
            new_state_hgrn)
```

```python
import functools

import jax
import jax.numpy as jnp
from jax import lax
from jax.experimental import pallas as pl
from jax.experimental.pallas import tpu as pltpu

F32 = jnp.float32
BF16 = jnp.bfloat16

D_MODEL = 1024
DEPTH = 2
GRID_W = 64
GLA_HEADS, GLA_DK, GLA_DV, GLA_LOW_RANK = 4, 64, 128, 16
GLA_NORMALIZER = 16.0
RET_HEADS, RET_DK, RET_DV = 4, 128, 128
ROPE_BASE = 10000.0
HG_HEADS, HG_DK, HG_DV = 8, 128, 128
N_EXPERTS, TOP_K, D_FF = 32, 4, 1024
SWIGLU_ALPHA, SWIGLU_LIMIT = 1.702, 7.0
EPS = 1e-6

LANES = 128
SCAN_CHUNK = 128
SCAN_ROWS = 1024
TOKEN_TILE = 256
BIG_TOKEN_TILE = 512
MOE_TILE = 384
VMEM_LIMIT = 56 * 1024 * 1024

_GQ, _GK, _GV, _GG, _RQ, _RK, _RV, _RG, _EVEN_MAIN = 0, 256, 512, 1024, 1536, 2048, 2560, 3072, 3584
_HQ, _HFF, _HFB, _HI, _HG, _ODD_MAIN = 0, 1024, 2048, 3072, 4096, 5120


def _dot(a, b):
    return jnp.dot(a, b, preferred_element_type=F32)


def _dot_nt(a, b):
    return lax.dot_general(a, b, (((1,), (1,)), ((), ())), preferred_element_type=F32)


def _rms(x, gain=None):
    y = x * lax.rsqrt(jnp.mean(x * x, axis=-1, keepdims=True) + EPS)
    if gain is not None:
        y = y * gain
    return y


def _sigmoid(x):
    return 0.5 * jnp.tanh(0.5 * x) + 0.5


def _silu(x):
    return x * _sigmoid(x)


def _cparams(sem, vmem=VMEM_LIMIT):
    return pltpu.CompilerParams(dimension_semantics=sem, vmem_limit_bytes=vmem)


def _resident(shape):
    nd = len(shape)
    return pl.BlockSpec(shape, lambda *_: (0,) * nd, pipeline_mode=pl.Buffered(1))


def _ada_kernel(c_ref, w_ref, b_ref, o_ref):
    o_ref[0] = _dot(_silu(c_ref[...]), w_ref[0]) + b_ref[0]


def _ada_mods(cond8, ada_w, ada_b):
    tn = 1536
    return pl.pallas_call(
        _ada_kernel,
        grid=(DEPTH, 6 * D_MODEL // tn),
        in_specs=[
            pl.BlockSpec((8, D_MODEL), lambda l, j: (0, 0)),
            pl.BlockSpec((1, D_MODEL, tn), lambda l, j: (l, 0, j)),
            pl.BlockSpec((1, 1, tn), lambda l, j: (l, 0, j)),
        ],
        out_specs=pl.BlockSpec((1, 8, tn), lambda l, j: (l, 0, j)),
        out_shape=jax.ShapeDtypeStruct((DEPTH, 8, 6 * D_MODEL), F32),
        compiler_params=_cparams(("arbitrary", "arbitrary")),
        name="ada_mods",
    )(cond8, ada_w, ada_b.reshape(DEPTH, 1, 6 * D_MODEL))


ROW_UNITS = D_MODEL // LANES


def _rows_from_units(ref, n_rows):
    return jnp.concatenate([ref[pl.ds(c, n_rows, stride=ROW_UNITS), :] for c in range(ROW_UNITS)], axis=1)


def _rows_to_units(ref, val):
    n_rows = val.shape[0]
    for c in range(ROW_UNITS):
        ref[pl.ds(c, n_rows, stride=ROW_UNITS), :] = val[:, c * LANES:(c + 1) * LANES]


def _moe_combine(y_refs, wt_ref):
    w = wt_ref[...]
    terms = [w[:, k:k + 1] * _rows_from_units(y_refs[k], w.shape[0]) for k in range(TOP_K)]
    return (terms[0] + terms[1]) + (terms[2] + terms[3])


def _in_proj_kernel(*refs, first, even, nb_ctx):
    it = iter(refs)
    if first:
        xa_ref, xb_ref = next(it), next(it)
    else:
        xp_ref = next(it)
        y_refs = [next(it) for _ in range(TOP_K)]
        wt_ref, modp_ref = next(it), next(it)
    gain_ref, mod_ref, w_ref = next(it), next(it), next(it)
    if even:
        wlr_ref, gkw_ref, gkb_ref = next(it), next(it), next(it)
    proj_ref = next(it)
    if even:
        g_ref = next(it)
    xcur_ref = next(it)

    D = D_MODEL
    i = pl.program_id(0)
    if first:
        x = jnp.where(i < nb_ctx, xa_ref[...], xb_ref[...])
    else:
        x = xp_ref[...] + modp_ref[0][:, 5 * D:6 * D] * _moe_combine(y_refs, wt_ref)
    xcur_ref[...] = x
    m = mod_ref[0]
    hn = _rms(x, gain_ref[...]) * (1.0 + m[:, D:2 * D]) + m[:, 0:D]
    proj_ref[...] = _dot(hn, w_ref[...])
    if even:
        z = _dot(_dot(hn, wlr_ref[...]), gkw_ref[...]) + gkb_ref[...]
        g_ref[...] = (jnp.minimum(z, 0.0) - jnp.log(1.0 + jnp.exp(-jnp.abs(z)))) * (1.0 / GLA_NORMALIZER)


def _cond_row(i, nb_ctx, tiles_per_lat_seq):
    return jnp.where(i < nb_ctx, 0, 1 + (i - nb_ctx) // tiles_per_lat_seq)


def _in_proj(layer, n_ctx, n_tok, lat_seq, mods, gain, w_main, *, x_parts=None, x_prev=None, y4=None, even_extra=None):
    first = x_parts is not None
    even = even_extra is not None
    tm = BIG_TOKEN_TILE if first else TOKEN_TILE
    nb = n_tok // tm
    nb_ctx = n_ctx // tm
    tpl = lat_seq // tm
    np_cols = w_main.shape[1]
    D = D_MODEL

    def mod_map(l):
        return lambda i: (l * 8 + _cond_row(i, nb_ctx, tpl), 0, 0)

    row = lambda i: (i, 0)
    args, specs = [], []
    if first:
        xa, xb = x_parts
        args += [xa, xb]
        specs += [pl.BlockSpec((tm, D), lambda i: (jnp.minimum(i, nb_ctx - 1), 0)),
                  pl.BlockSpec((tm, D), lambda i: (jnp.maximum(i - nb_ctx, 0), 0))]
    else:
        y4, wts = y4
        args += [x_prev] + [y4] * TOP_K + [wts, mods]
        specs += [pl.BlockSpec((tm, D), row)]
        specs += [pl.BlockSpec((tm * ROW_UNITS, LANES), (lambda k: (lambda i: (k * nb + i, 0)))(k))
                  for k in range(TOP_K)]
        specs += [pl.BlockSpec((tm, LANES), row), pl.BlockSpec((1, 1, 6 * D), mod_map(layer - 1))]
    args += [gain.reshape(1, D), mods, w_main]
    specs += [_resident((1, D)), pl.BlockSpec((1, 1, 6 * D), mod_map(layer)), _resident((D, np_cols))]
    out_shapes = [jax.ShapeDtypeStruct((n_tok, np_cols), F32)]
    out_specs = [pl.BlockSpec((tm, np_cols), row)]
    if even:
        w_lr, gkw, gkb = even_extra
        args += [w_lr, gkw, gkb]
        specs += [_resident(w_lr.shape), _resident(gkw.shape), _resident(gkb.shape)]
        out_shapes.append(jax.ShapeDtypeStruct((n_tok, 2 * GLA_HEADS * GLA_DK), F32))
        out_specs.append(pl.BlockSpec((tm, 2 * GLA_HEADS * GLA_DK), row))
    out_shapes.append(jax.ShapeDtypeStruct((n_tok, D), F32))
    out_specs.append(pl.BlockSpec((tm, D), row))
    return pl.pallas_call(
        functools.partial(_in_proj_kernel, first=first, even=even, nb_ctx=nb_ctx),
        grid=(nb,),
        in_specs=specs,
        out_specs=out_specs,
        out_shape=out_shapes,
        compiler_params=_cparams(("arbitrary",)),
        name=f"in_proj_l{layer}",
    )(*args)


def _tri(c, lower):
    r = lax.broadcasted_iota(jnp.int32, (c, c), 0)
    s = lax.broadcasted_iota(jnp.int32, (c, c), 1)
    return (r >= s) if lower else (r <= s)


def _cumsum_mm(tri_bf16, g):
    g_hi = g.astype(BF16)
    g_lo = (g - g_hi.astype(F32)).astype(BF16)
    r = _dot(tri_bf16, jnp.concatenate([g_hi, g_lo], axis=1))
    w = g.shape[1]
    return r[:, :w] + r[:, w:]


def _gated_scan_kernel(*refs, variant, T, nseq, has_s0, want_final, hpb):
    C = SCAN_CHUNK
    ncs = T // C
    nc = nseq * ncs
    it = iter(refs)
    if variant == "gla":
        q_ref, k_ref, v_ref, og_ref, gf_ref, gb_ref, gain_ref = (next(it) for _ in range(7))
    else:
        q_ref, ff_ref, fb_ref, v_ref, og_ref, lbl_ref, gain_ref = (next(it) for _ in range(7))
    s0_ref = next(it) if has_s0 else None
    y_ref = next(it)
    sfin_ref = next(it) if want_final else None
    qi_scr, o_scr, u_scr, dec_scr, sin_scr = (next(it) for _ in range(5))

    lane = lax.broadcasted_iota(jnp.int32, (1, LANES), 1)
    if hpb == 2:
        masks = [lane < GLA_DK, lane >= GLA_DK]
    else:
        masks = [None]
    tri_l, tri_u = _tri(C, True), _tri(C, False)
    tri_l16, tri_u16 = tri_l.astype(F32).astype(BF16), tri_u.astype(F32).astype(BF16)
    H = C // 2
    upper_rows = lax.broadcasted_iota(jnp.int32, (C, 1), 0) < H

    if variant == "hgrn":
        lgs = [lbl_ref[l] for l in range(DEPTH)]
        mx = functools.reduce(jnp.maximum, lgs)
        es = [jnp.exp(l - mx) for l in lgs]
        tot = functools.reduce(lambda a, b: a + b, es)
        ps = [e / tot for e in es]
        layer = DEPTH - 1
        lb = functools.reduce(lambda a, b: a + b, ps[:layer + 1]) - ps[0]
        lb_f, lb_b = lb[0:1], lb[1:2]

    for n in range(nc):
        rows = pl.ds(n * C, C)
        if variant == "gla":
            q = q_ref[rows, :] * (GLA_DK ** -0.5)
            kf = kb = k_ref[rows, :]
            gf, gb = gf_ref[rows, :], gb_ref[rows, :]
        else:
            q = _silu(q_ref[rows, :])
            f_f = lb_f + (1.0 - lb_f) * _sigmoid(ff_ref[rows, :])
            f_b = lb_b + (1.0 - lb_b) * _sigmoid(fb_ref[rows, :])
            kf, kb = 1.0 - f_f, 1.0 - f_b
            gf, gb = jnp.log(f_f), jnp.log(f_b)
        bf = _cumsum_mm(tri_l16, gf)
        bb = _cumsum_mm(tri_u16, gb)
        bf_end, bb_end = bf[C - 1:C], bb[0:1]
        ref_f = jnp.where(upper_rows, bf[H // 2 - 1:H // 2], bf[H + H // 2 - 1:H + H // 2])
        ref_b = jnp.where(upper_rows, bb[H // 2:H // 2 + 1], bb[H + H // 2:H + H // 2 + 1])
        qd_f, qd_b = q * jnp.exp(bf - ref_f), q * jnp.exp(bb - ref_b)
        kd_f = [jnp.where(upper_rows, kf * jnp.exp(bf[H // 2 - 1:H // 2] - bf), 0.0),
                kf * jnp.exp(bf[H + H // 2 - 1:H + H // 2] - bf)]
        kd_b = [kb * jnp.exp(bb[H // 2:H // 2 + 1] - bb),
                jnp.where(upper_rows, 0.0, kb * jnp.exp(bb[H + H // 2:H + H // 2 + 1] - bb))]
        kend = jnp.concatenate([kf * jnp.exp(bf_end - bf), kb * jnp.exp(bb_end - bb)], axis=1)
        qi_scr[rows, :] = jnp.concatenate([q * jnp.exp(bf), q * jnp.exp(bb)], axis=1)
        dec_scr[n] = jnp.broadcast_to(jnp.concatenate([jnp.exp(bf_end), jnp.exp(bb_end)], axis=1), (8, 2 * LANES))
        for h in range(hpb):
            v = v_ref[rows, h * LANES:(h + 1) * LANES]
            if masks[h] is None:
                qf_h, qb_h, kend_h = qd_f, qd_b, kend
            else:
                qf_h, qb_h = jnp.where(masks[h], qd_f, 0.0), jnp.where(masks[h], qd_b, 0.0)
                kend_h = jnp.where(jnp.concatenate([masks[h], masks[h]], axis=1), kend, 0.0)
            s_f = jnp.concatenate([_dot_nt(qf_h[:H], kd_f[0]), _dot_nt(qf_h[H:], kd_f[1])], axis=0)
            s_b = jnp.concatenate([_dot_nt(qb_h[:H], kd_b[0]), _dot_nt(qb_h[H:], kd_b[1])], axis=0)
            s = jnp.where(tri_l, s_f, 0.0) + jnp.where(tri_u, s_b, 0.0)
            o_scr[rows, h * LANES:(h + 1) * LANES] = _dot(s, v)
            u_scr[h, n] = _dot(v.T, kend_h)

    for sq in range(nseq):
        for h in range(hpb):
            if has_s0:
                s_f, s_b = s0_ref[sq, 0, 0].T, s0_ref[sq, 1, 0].T
                if masks[h] is not None:
                    s_f, s_b = jnp.where(masks[h], s_f, 0.0), jnp.where(masks[h], s_b, 0.0)
            else:
                s_f = s_b = jnp.zeros((LANES, LANES), F32)
            for n in range(sq * ncs, (sq + 1) * ncs):
                sin_scr[h, n, :, 0:LANES] = s_f
                s_f = s_f * dec_scr[n, 0:1, 0:LANES] + u_scr[h, n, :, 0:LANES]
            for n in reversed(range(sq * ncs, (sq + 1) * ncs)):
                sin_scr[h, n, :, LANES:2 * LANES] = s_b
                s_b = s_b * dec_scr[n, 0:1, LANES:2 * LANES] + u_scr[h, n, :, LANES:2 * LANES]
            if want_final:
                if h == 0:
                    fin_f, fin_b = s_f, s_b
                else:
                    fin_f, fin_b = fin_f + s_f, fin_b + s_b
        if want_final:
            sfin_ref[sq, 0, 0] = fin_f.T
            sfin_ref[sq, 1, 0] = fin_b.T

    gain = gain_ref[...]
    for n in range(nc):
        rows = pl.ds(n * C, C)
        qi = qi_scr[rows, :]
        for h in range(hpb):
            cols = slice(h * LANES, (h + 1) * LANES)
            o = o_scr[rows, cols] + _dot_nt(qi, sin_scr[h, n])
            y_ref[rows, cols] = (_rms(o, gain) * _silu(og_ref[rows, cols])).astype(y_ref.dtype)


def _scan_scratch(T, hpb):
    nc = T // SCAN_CHUNK
    return [
        pltpu.VMEM((T, 2 * LANES), F32),
        pltpu.VMEM((T, hpb * LANES), F32),
        pltpu.VMEM((hpb, nc, LANES, 2 * LANES), F32),
        pltpu.VMEM((nc, 8, 2 * LANES), F32),
        pltpu.VMEM((hpb, nc, LANES, 2 * LANES), F32),
    ]


def _seqs_per_step(B, T):
    nseq = max(1, min(B, SCAN_ROWS // T))
    assert B % nseq == 0
    return nseq


def _gla_scan(proj, gdec, gain, s0, row0, B, T, want_final):
    nseq = _seqs_per_step(B, T)
    R = nseq * T
    rb0 = row0 // R
    assert row0 % R == 0
    has_s0 = s0 is not None

    def col(base, width=LANES):
        return lambda b, j: (rb0 + b, base // width + j)

    args = [proj, proj, proj, proj, gdec, gdec, gain.reshape(1, GLA_DV)]
    specs = [pl.BlockSpec((R, LANES), col(_GQ)), pl.BlockSpec((R, LANES), col(_GK)),
             pl.BlockSpec((R, 2 * LANES), col(_GV, 2 * LANES)), pl.BlockSpec((R, 2 * LANES), col(_GG, 2 * LANES)),
             pl.BlockSpec((R, LANES), col(0)), pl.BlockSpec((R, LANES), col(GLA_HEADS * GLA_DK)),
             pl.BlockSpec((1, GLA_DV), lambda b, j: (0, 0))]
    st_spec = pl.BlockSpec((nseq, 2, 1, LANES, LANES), lambda b, j: (b, 0, j, 0, 0))
    if has_s0:
        args.append(s0.reshape(B, 2, GLA_HEADS // 2, 2 * GLA_DK, GLA_DV))
        specs.append(st_spec)
    out_shapes = [jax.ShapeDtypeStruct((B * T, GLA_HEADS * GLA_DV), BF16)]
    out_specs = [pl.BlockSpec((R, 2 * LANES), lambda b, j: (b, j))]
    if want_final:
        out_shapes.append(jax.ShapeDtypeStruct((B, 2, GLA_HEADS // 2, 2 * GLA_DK, GLA_DV), F32))
        out_specs.append(st_spec)
    kern = functools.partial(_gated_scan_kernel, variant="gla", T=T, nseq=nseq, has_s0=has_s0,
                             want_final=want_final, hpb=2)
    return pl.pallas_call(
        kern, grid=(B // nseq, GLA_HEADS // 2), in_specs=specs, out_specs=out_specs, out_shape=out_shapes,
        scratch_shapes=_scan_scratch(R, 2),
        compiler_params=_cparams(("arbitrary", "arbitrary")), name=f"gla_scan_T{T}",
    )(*args)


def _hgrn_scan(proj, lb_logits, gain, s0, row0, B, T, want_final):
    nseq = _seqs_per_step(B, T)
    R = nseq * T
    rb0 = row0 // R
    assert row0 % R == 0
    has_s0 = s0 is not None

    def col(base):
        return lambda b, j: (rb0 + b, base // LANES + j)

    args = [proj, proj, proj, proj, proj, lb_logits, gain.reshape(1, HG_DV)]
    specs = [pl.BlockSpec((R, LANES), col(_HQ)), pl.BlockSpec((R, LANES), col(_HFF)),
             pl.BlockSpec((R, LANES), col(_HFB)), pl.BlockSpec((R, LANES), col(_HI)),
             pl.BlockSpec((R, LANES), col(_HG)),
             pl.BlockSpec((DEPTH, 2, LANES), lambda b, j: (0, 0, j)),
             pl.BlockSpec((1, HG_DV), lambda b, j: (0, 0))]
    st_spec = pl.BlockSpec((nseq, 2, 1, LANES, LANES), lambda b, j: (b, 0, j, 0, 0))
    if has_s0:
        args.append(s0.reshape(B, 2, HG_HEADS, HG_DK, HG_DV))
        specs.append(st_spec)
    out_shapes = [jax.ShapeDtypeStruct((B * T, HG_HEADS * HG_DV), BF16)]
    out_specs = [pl.BlockSpec((R, LANES), lambda b, j: (b, j))]
    if want_final:
        out_shapes.append(jax.ShapeDtypeStruct((B, 2, HG_HEADS, HG_DK, HG_DV), F32))
        out_specs.append(st_spec)
    kern = functools.partial(_gated_scan_kernel, variant="hgrn", T=T, nseq=nseq, has_s0=has_s0,
                             want_final=want_final, hpb=1)
    return pl.pallas_call(
        kern, grid=(B // nseq, HG_HEADS), in_specs=specs, out_specs=out_specs, out_shape=out_shapes,
        scratch_shapes=_scan_scratch(R, 1),
        compiler_params=_cparams(("arbitrary", "arbitrary")), name=f"hgrn_scan_T{T}",
    )(*args)


def _ret_scan_kernel(*refs, T, nseq, has_s0, want_final, rope):
    C = SCAN_CHUNK
    ncs = T // C
    nc = nseq * ncs
    it = iter(refs)
    q_ref, k_ref, v_ref, og_ref, dexp_ref = (next(it) for _ in range(5))
    if rope:
        cos_ref, sin_ref = next(it), next(it)
    s0_ref = next(it) if has_s0 else None
    y_ref = next(it)
    sfin_ref = next(it) if want_final else None
    qi_scr, o_scr, u_scr, sin_scr = (next(it) for _ in range(4))

    lg = jnp.log1p(-jnp.exp2(-dexp_ref[0]))
    lg_f, lg_b = lg[0:1], lg[1:2]
    r = lax.broadcasted_iota(jnp.int32, (C, C), 0)
    s = lax.broadcasted_iota(jnp.int32, (C, C), 1)
    dist = (r - s).astype(F32)
    dmask = (jnp.where(r >= s, jnp.exp(jnp.maximum(dist, 0.0) * lg_f[:, 0:1]), 0.0)
             + jnp.where(r <= s, jnp.exp(jnp.maximum(-dist, 0.0) * lg_b[:, 0:1]), 0.0))
    pos = lax.broadcasted_iota(jnp.int32, (C, LANES), 0).astype(F32)
    xi = jnp.concatenate([jnp.exp((pos + 1.0) * lg_f), jnp.exp((C - pos) * lg_b)], axis=1)
    zeta = jnp.concatenate([jnp.exp((C - 1.0 - pos) * lg_f), jnp.exp(pos * lg_b)], axis=1)
    d_f, d_b = jnp.exp(C * lg_f), jnp.exp(C * lg_b)

    def rot(x, seq_rows):
        if not rope:
            return x
        return x * cos_ref[seq_rows, :] + pltpu.roll(x, RET_DK // 2, axis=1) * sin_ref[seq_rows, :]

    for n in range(nc):
        rows = pl.ds(n * C, C)
        seq_rows = pl.ds((n % ncs) * C, C)
        q = rot(q_ref[rows, :], seq_rows)
        k = rot(k_ref[rows, :] * (RET_DK ** -0.5), seq_rows)
        v = v_ref[rows, :]
        o_scr[rows, :] = _dot(_dot_nt(q, k) * dmask, v)
        qi_scr[rows, :] = jnp.concatenate([q, q], axis=1) * xi
        u_scr[n] = _dot(v.T, jnp.concatenate([k, k], axis=1) * zeta)

    for sq in range(nseq):
        if has_s0:
            s_f, s_b = s0_ref[sq, 0, 0].T, s0_ref[sq, 1, 0].T
        else:
            s_f = s_b = jnp.zeros((LANES, LANES), F32)
        for n in range(sq * ncs, (sq + 1) * ncs):
            sin_scr[n, :, 0:LANES] = s_f
            s_f = s_f * d_f + u_scr[n, :, 0:LANES]
        for n in reversed(range(sq * ncs, (sq + 1) * ncs)):
            sin_scr[n, :, LANES:2 * LANES] = s_b
            s_b = s_b * d_b + u_scr[n, :, LANES:2 * LANES]
        if want_final:
            sfin_ref[sq, 0, 0] = s_f.T
            sfin_ref[sq, 1, 0] = s_b.T

    for n in range(nc):
        rows = pl.ds(n * C, C)
        o = o_scr[rows, :] + _dot_nt(qi_scr[rows, :], sin_scr[n])
        y_ref[rows, :] = (_rms(o) * _silu(og_ref[rows, :])).astype(y_ref.dtype)


def _ret_scan(proj, dexp, rope_tabs, s0, row0, B, T, want_final):
    nseq = _seqs_per_step(B, T)
    R = nseq * T
    rb0 = row0 // R
    assert row0 % R == 0
    nc = R // SCAN_CHUNK
    has_s0 = s0 is not None
    rope = rope_tabs is not None

    def col(base):
        return lambda b, j: (rb0 + b, base // LANES + j)

    args = [proj, proj, proj, proj, dexp]
    specs = [pl.BlockSpec((R, LANES), col(_RQ)), pl.BlockSpec((R, LANES), col(_RK)),
             pl.BlockSpec((R, LANES), col(_RV)), pl.BlockSpec((R, LANES), col(_RG)),
             pl.BlockSpec((1, 2, LANES), lambda b, j: (j, 0, 0))]
    if rope:
        args += list(rope_tabs)
        specs += [pl.BlockSpec((T, LANES), lambda b, j: (0, 0))] * 2
    st_spec = pl.BlockSpec((nseq, 2, 1, LANES, LANES), lambda b, j: (b, 0, j, 0, 0))
    if has_s0:
        args.append(s0.reshape(B, 2, RET_HEADS, RET_DK, RET_DV))
        specs.append(st_spec)
    out_shapes = [jax.ShapeDtypeStruct((B * T, RET_HEADS * RET_DV), BF16)]
    out_specs = [pl.BlockSpec((R, LANES), lambda b, j: (b, j))]
    if want_final:
        out_shapes.append(jax.ShapeDtypeStruct((B, 2, RET_HEADS, RET_DK, RET_DV), F32))
        out_specs.append(st_spec)
    kern = functools.partial(_ret_scan_kernel, T=T, nseq=nseq, has_s0=has_s0, want_final=want_final, rope=rope)
    scratch = [pltpu.VMEM((R, 2 * LANES), F32), pltpu.VMEM((R, LANES), F32),
               pltpu.VMEM((nc, LANES, 2 * LANES), F32), pltpu.VMEM((nc, LANES, 2 * LANES), F32)]
    return pl.pallas_call(
        kern, grid=(B // nseq, RET_HEADS), in_specs=specs, out_specs=out_specs, out_shape=out_shapes,
        scratch_shapes=scratch,
        compiler_params=_cparams(("arbitrary", "arbitrary")), name=f"ret_scan_T{T}",
    )(*args)


def _out_proj_kernel(*refs, n_mix, nb_ctx):
    it = iter(refs)
    y_refs = [(next(it), next(it)) for _ in range(n_mix)]
    wo_ref, x_ref, mod_ref, gain_ref, rw_ref, rb_ref = (next(it) for _ in range(6))
    xnew_ref, hn_ref, idx_ref, wt_ref, rank_ref, cnt_ref = (next(it) for _ in range(6))
    cnt_scr = next(it)
    D = D_MODEL
    is_ctx = pl.program_id(0) < nb_ctx
    mix = None
    r0 = 0
    for ya_ref, yb_ref in y_refs:
        w = ya_ref.shape[1]
        part = _dot(jnp.where(is_ctx, ya_ref[...], yb_ref[...]), wo_ref[r0:r0 + w, :])
        mix = part if mix is None else mix + part
        r0 += w
    m = mod_ref[0]
    xn = x_ref[...] + m[:, 2 * D:3 * D] * mix
    xnew_ref[...] = xn
    hn = _rms(xn, gain_ref[...]) * (1.0 + m[:, 4 * D:5 * D]) + m[:, 3 * D:4 * D]
    hn_hi = hn.astype(BF16)
    _rows_to_units(hn_ref, hn)
    hn_lo = (hn - hn_hi.astype(F32)).astype(BF16)
    rw = rw_ref[...]
    rw_hi = rw.astype(BF16)
    rw_lo = (rw - rw_hi.astype(F32)).astype(BF16)
    logits = (_dot(hn_hi, rw_hi) + (_dot(hn_lo, rw_hi) + _dot(hn_hi, rw_lo))) + rb_ref[...]
    lane = lax.broadcasted_iota(jnp.int32, logits.shape, 1).astype(F32)
    vals, idxs = [], []
    cur = logits
    for _ in range(TOP_K):
        mx = jnp.max(cur, axis=-1, keepdims=True)
        ik = jnp.min(jnp.where(cur == mx, lane, float(LANES)), axis=-1, keepdims=True)
        vals.append(mx)
        idxs.append(ik)
        cur = jnp.where(lane == ik, -jnp.inf, cur)
    es = [jnp.exp(v - vals[0]) for v in vals]
    tot = (es[0] + es[1]) + (es[2] + es[3])
    idx_out = jnp.zeros(logits.shape, F32)
    wt_out = jnp.zeros(logits.shape, F32)
    for k in range(TOP_K):
        idx_out = jnp.where(lane == float(k), idxs[k], idx_out)
        wt_out = jnp.where(lane == float(k), es[k] / tot, wt_out)
    idx_ref[...] = idx_out.astype(jnp.int32)
    wt_ref[...] = wt_out

    @pl.when(pl.program_id(0) == 0)
    def _():
        cnt_scr[...] = jnp.zeros(cnt_scr.shape, F32)

    tm = logits.shape[0]
    hits = [lane == idxs[k] for k in range(TOP_K)]
    sel = jnp.zeros(logits.shape, F32)
    for k in range(TOP_K):
        sel = sel + jnp.where(hits[k], 1.0, 0.0)
    rr = lax.broadcasted_iota(jnp.int32, (tm, tm), 0)
    cc = lax.broadcasted_iota(jnp.int32, (tm, tm), 1)
    before = jnp.where(rr > cc, 1.0, 0.0).astype(BF16)
    rank_all = cnt_scr[0:1, :] + _dot(before, sel.astype(BF16))
    rank_out = jnp.zeros(logits.shape, F32)
    for k in range(TOP_K):
        rk = jnp.sum(jnp.where(hits[k], rank_all, 0.0), axis=-1, keepdims=True)
        rank_out = jnp.where(lane == float(k), rk, rank_out)
    rank_ref[...] = rank_out.astype(jnp.int32)
    total = cnt_scr[...] + jnp.sum(sel, axis=0, keepdims=True)
    cnt_scr[...] = total
    cnt_ref[...] = total.astype(jnp.int32)


def _out_proj(layer, n_ctx, n_tok, lat_seq, ys, w_out, x_cur, mods, gain, rw_pad, rb_pad):
    tm = BIG_TOKEN_TILE
    nb, nb_ctx, tpl = n_tok // tm, n_ctx // tm, lat_seq // tm
    D = D_MODEL
    row = lambda i: (i, 0)
    specs = []
    for ya, _ in ys:
        specs += [pl.BlockSpec((tm, ya.shape[1]), lambda i: (jnp.minimum(i, nb_ctx - 1), 0)),
                  pl.BlockSpec((tm, ya.shape[1]), lambda i: (jnp.maximum(i - nb_ctx, 0), 0))]
    specs += [_resident((D, D)), pl.BlockSpec((tm, D), row),
              pl.BlockSpec((1, 1, 6 * D), lambda i: (layer * 8 + _cond_row(i, nb_ctx, tpl), 0, 0)),
              _resident((1, D)), _resident((D, LANES)), _resident((1, LANES))]
    return pl.pallas_call(
        functools.partial(_out_proj_kernel, n_mix=len(ys), nb_ctx=nb_ctx),
        grid=(nb,),
        in_specs=specs,
        out_specs=[pl.BlockSpec((tm, D), row), pl.BlockSpec((tm * ROW_UNITS, LANES), row),
                   pl.BlockSpec((tm, LANES), row), pl.BlockSpec((tm, LANES), row),
                   pl.BlockSpec((tm, LANES), row), pl.BlockSpec((8, LANES), lambda i: (0, 0))],
        out_shape=[jax.ShapeDtypeStruct((n_tok, D), F32), jax.ShapeDtypeStruct((n_tok * ROW_UNITS, LANES), F32),
                   jax.ShapeDtypeStruct((n_tok, LANES), jnp.int32), jax.ShapeDtypeStruct((n_tok, LANES), F32),
                   jax.ShapeDtypeStruct((n_tok, LANES), jnp.int32), jax.ShapeDtypeStruct((8, LANES), jnp.int32)],
        scratch_shapes=[pltpu.VMEM((8, LANES), F32)],
        compiler_params=_cparams(("arbitrary",)),
        name=f"out_proj_l{layer}",
    )(*[y for pair in ys for y in pair], w_out, x_cur, mods, gain.reshape(1, D), rw_pad, rb_pad)


W1_SPLIT = 4
W2_SPLIT = 2
SPARE_TILES = 2
ROUTER_PAD_BIAS = -1e30


def _moe_kernel(seg_ref, nt_ref, code_hbm, x_hbm, *refs, n_tok):
    w1_refs = refs[:W1_SPLIT]
    b1_ref = refs[W1_SPLIT]
    w2_refs = refs[W1_SPLIT + 1:W1_SPLIT + 1 + W2_SPLIT]
    (b2_ref, y_hbm, w1_scr, b1_scr, w2_scr, act_scr, xbuf, obuf, code_smem, csem, gsem,
     ssem) = refs[W1_SPLIT + 1 + W2_SPLIT:]
    tm = MOE_TILE
    U = ROW_UNITS
    n_code_tiles = code_hbm.shape[0] // tm
    e = pl.program_id(0)
    nt = nt_ref[e]
    g_first = seg_ref[e] // tm

    GATHER, SCATTER = 0, 1

    def code_copy(kind, t, p):
        first = pl.multiple_of(jnp.clip(t, 0, n_code_tiles - 1) * tm, tm)
        return pltpu.make_async_copy(code_hbm.at[pl.ds(first, tm)], code_smem.at[kind, p], csem.at[kind, p])

    def gather_start(b, p):
        for r in range(tm):
            tok = jnp.bitwise_and(code_smem[GATHER, p, r], n_tok - 1)
            pltpu.make_async_copy(x_hbm.at[pl.ds(pl.multiple_of(tok * U, U), U)],
                                  xbuf.at[b, pl.ds(r * U, U)], gsem.at[b]).start()

    def gather_wait(b):
        pltpu.make_async_copy(x_hbm.at[pl.ds(0, tm * U)], xbuf.at[b], gsem.at[b]).wait()

    def scatter_start(b, p):
        for r in range(tm):
            dst = code_smem[SCATTER, p, r]
            pltpu.make_async_copy(obuf.at[b, pl.ds(r * U, U)],
                                  y_hbm.at[pl.ds(pl.multiple_of(dst * U, U), U)], ssem.at[b]).start()

    def scatter_wait(b):
        pltpu.make_async_copy(obuf.at[b], y_hbm.at[pl.ds(0, tm * U)], ssem.at[b]).wait()

    @pl.when(e == 0)
    def _():
        obuf[...] = jnp.zeros(obuf.shape, obuf.dtype)
        for h in range(SPARE_TILES):
            cp = pltpu.make_async_copy(obuf.at[0], y_hbm.at[pl.ds((TOP_K * n_tok + h * tm) * U, tm * U)], ssem.at[0])
            cp.start()
            cp.wait()
        code_copy(GATHER, 0, 0).start()
        code_copy(GATHER, 1, 1).start()
        code_copy(SCATTER, n_code_tiles - 1, 1).start()
        code_copy(SCATTER, 0, 0).start()
        code_copy(GATHER, 0, 0).wait()
        gather_start(0, 0)

    @pl.when(nt > 0)
    def _():
        grp = 2 * LANES
        rr = lax.broadcasted_iota(jnp.int32, (grp, grp), 0)
        cc = lax.broadcasted_iota(jnp.int32, (grp, grp), 1)
        src = jnp.where(cc < LANES, 2 * cc, 2 * (cc - LANES) + 1)
        perm = jnp.where(rr == src, 1.0, 0.0).astype(BF16)
        cw = 2 * D_FF // W1_SPLIT
        for c, w_ref in enumerate(w1_refs):
            for j in range(cw // grp):
                cols = slice(j * grp, (j + 1) * grp)
                w1_scr[:, c * cw + j * grp:c * cw + (j + 1) * grp] = _dot(w_ref[0, 0, :, cols].astype(BF16),
                                                                         perm).astype(BF16)
        bias = jnp.broadcast_to(b1_ref[0, 0], (8, 2 * D_FF))
        for j in range(2 * D_FF // grp):
            cols = slice(j * grp, (j + 1) * grp)
            rest = bias[:, cols]
            acc = jnp.zeros((8, grp), F32)
            for _ in range(3):
                term = rest.astype(BF16)
                acc = acc + _dot(term, perm)
                rest = rest - term.astype(F32)
            b1_scr[:, cols] = acc
        rh = D_FF // W2_SPLIT
        for c, w_ref in enumerate(w2_refs):
            w2_scr[c * rh:(c + 1) * rh, :] = w_ref[0, 0].astype(BF16)
        blk = 4 * LANES

        def tile(g, b):
            o = 1 - b

            @pl.when(g >= 0)
            def _():
                code_copy(GATHER, g + 1, o).wait()
                gather_start(o, o)
                code_copy(GATHER, g + 2, b).start()

            gather_wait(b)
            x = _rows_from_units(xbuf.at[b], tm).astype(BF16)
            n_blk = 2 * D_FF // blk
            for c in range(n_blk):
                if c == n_blk // 2:
                    @pl.when(g >= 0)
                    def _():
                        code_copy(SCATTER, g - 1, o).wait()
                        scatter_start(o, o)
                        code_copy(SCATTER, g + 1, o).start()

                cols = slice(c * blk, (c + 1) * blk)
                hid = _dot(x, w1_scr[:, cols]) + b1_scr[0:1, cols]
                acts = []
                for j in range(blk // grp):
                    glu = jnp.minimum(hid[:, j * grp:j * grp + LANES], SWIGLU_LIMIT)
                    lin = jnp.clip(hid[:, j * grp + LANES:(j + 1) * grp], -SWIGLU_LIMIT, SWIGLU_LIMIT)
                    acts.append(glu * _sigmoid(SWIGLU_ALPHA * glu) * (lin + 1.0))
                act_scr[:, c * (blk // 2):(c + 1) * (blk // 2)] = jnp.concatenate(acts, axis=1).astype(BF16)
            y = _dot(act_scr[...], w2_scr[...]) + b2_ref[0, 0]

            @pl.when(g >= 1)
            def _():
                scatter_wait(b)

            _rows_to_units(obuf.at[b], y)

        g_end = g_first + nt

        def tile_pair(m, carry):
            for b in range(2):
                g = 2 * m + b

                @pl.when(jnp.logical_and(g >= g_first, g < g_end))
                def _():
                    tile(g, b)

            return carry

        lax.fori_loop(g_first // 2, (g_end + 1) // 2, tile_pair, 0)

    @pl.when(e == N_EXPERTS - 1)
    def _():
        g_end = g_first + nt
        last = lax.rem(g_end + 1, 2)
        code_copy(SCATTER, g_end - 1, last).wait()
        scatter_start(last, last)
        scatter_wait(last)
        scatter_wait(1 - last)
        gather_wait(1 - last)
        code_copy(GATHER, g_end + 1, last).wait()
        code_copy(SCATTER, g_end, 1 - last).wait()


def _moe_experts(layer, n_tok, seg_start, tiles_per_e, code, x_units, w1, b1, w2, b2):
    tm = MOE_TILE
    D, F2, U = D_MODEL, 2 * D_FF, ROW_UNITS
    cw, rh = F2 // W1_SPLIT, D_FF // W2_SPLIT
    any_space = pl.BlockSpec(memory_space=pl.ANY)
    w1_specs = [pl.BlockSpec((1, 1, D, cw), (lambda c: (lambda e, sg, nt: (layer, e, 0, c)))(c))
                for c in range(W1_SPLIT)]
    w2_specs = [pl.BlockSpec((1, 1, rh, D), (lambda c: (lambda e, sg, nt: (layer, e, c, 0)))(c))
                for c in range(W2_SPLIT)]
    grid_spec = pltpu.PrefetchScalarGridSpec(
        num_scalar_prefetch=2,
        grid=(N_EXPERTS,),
        in_specs=[any_space, any_space]
        + w1_specs + [pl.BlockSpec((1, 1, 1, F2), lambda e, sg, nt: (layer, e, 0, 0))]
        + w2_specs + [pl.BlockSpec((1, 1, 1, D), lambda e, sg, nt: (layer, e, 0, 0))],
        out_specs=any_space,
        scratch_shapes=[pltpu.VMEM((D, F2), BF16), pltpu.VMEM((8, F2), F32), pltpu.VMEM((D_FF, D), BF16),
                        pltpu.VMEM((tm, D_FF), BF16),
                        pltpu.VMEM((2, tm * U, LANES), F32), pltpu.VMEM((2, tm * U, LANES), F32),
                        pltpu.SMEM((2, 2, tm), jnp.int32),
                        pltpu.SemaphoreType.DMA((2, 2)), pltpu.SemaphoreType.DMA((2,)), pltpu.SemaphoreType.DMA((2,))],
    )
    n_out_rows = TOP_K * n_tok + SPARE_TILES * tm
    return pl.pallas_call(
        functools.partial(_moe_kernel, n_tok=n_tok),
        grid_spec=grid_spec,
        out_shape=jax.ShapeDtypeStruct((n_out_rows * U, LANES), F32),
        compiler_params=_cparams(("arbitrary",)),
        name="moe_experts",
    )(seg_start, tiles_per_e, code, x_units, *([w1] * W1_SPLIT), b1.reshape(DEPTH, N_EXPERTS, 1, F2),
      *([w2] * W2_SPLIT), b2.reshape(DEPTH, N_EXPERTS, 1, D))


def _slot_code_kernel(seg_ref, idx_ref, rank_ref, unused_hbm, code_ref, sem, *, n_tok):
    i = pl.program_id(0)
    tm = idx_ref.shape[0] // TOP_K

    @pl.when(i == 0)
    def _():
        cp = pltpu.make_async_copy(unused_hbm, code_ref, sem)
        cp.start()
        cp.wait()

    base = i * tm
    for t in range(tm):
        for k in range(TOP_K):
            p = t * TOP_K + k
            code_ref[seg_ref[idx_ref[p]] + rank_ref[p]] = base + (k * n_tok + t)


def _slot_codes(idx, rank, seg_start, unused):
    n_tok = idx.shape[0]
    tm = TOKEN_TILE
    pairs = lambda a: a[:, :TOP_K].reshape(-1)
    flat = pl.BlockSpec((tm * TOP_K,), lambda i, sg: (i,), memory_space=pltpu.SMEM)
    grid_spec = pltpu.PrefetchScalarGridSpec(
        num_scalar_prefetch=1,
        grid=(n_tok // tm,),
        in_specs=[flat, flat, pl.BlockSpec(memory_space=pl.ANY)],
        out_specs=pl.BlockSpec(memory_space=pltpu.SMEM),
        scratch_shapes=[pltpu.SemaphoreType.DMA(())],
    )
    return pl.pallas_call(
        functools.partial(_slot_code_kernel, n_tok=n_tok),
        grid_spec=grid_spec,
        out_shape=jax.ShapeDtypeStruct(unused.shape, jnp.int32),
        compiler_params=_cparams(("arbitrary",)),
        name="slot_codes",
    )(seg_start, pairs(idx), pairs(rank), unused)


def _moe_layer(layer, hn_units, idx, rank, counts, w1, b1, w2, b2):
    n_tok = hn_units.shape[0] // ROW_UNITS
    assert n_tok & (n_tok - 1) == 0
    tm = MOE_TILE
    n_pairs = n_tok * TOP_K
    n_slots = ((n_pairs + N_EXPERTS * (tm - 1)) // tm + 1) * tm
    tiles_per_e = (counts + tm - 1) // tm
    seg_start = (jnp.cumsum(tiles_per_e) - tiles_per_e) * tm
    unused = n_pairs + jnp.arange(n_slots, dtype=jnp.int32) % (SPARE_TILES * tm)
    code = _slot_codes(idx, rank, seg_start, unused)
    return _moe_experts(layer, n_tok, seg_start, tiles_per_e, code, hn_units, w1, b1, w2, b2)


def _final_kernel(*refs, nb_ctx):
    x_ref = refs[0]
    y_refs = refs[1:1 + TOP_K]
    wt_ref, mod_ref, gain_ref, oa_ref, ob_ref = refs[1 + TOP_K:]
    D = D_MODEL
    i = pl.program_id(0)
    x = x_ref[...] + mod_ref[0][:, 5 * D:6 * D] * _moe_combine(y_refs, wt_ref)
    out = _rms(x, gain_ref[...])

    @pl.when(i < nb_ctx)
    def _():
        oa_ref[...] = out

    @pl.when(i >= nb_ctx)
    def _():
        ob_ref[...] = out


def _final(n_ctx, n_tok, lat_seq, x_new, y4, wts, mods, gain):
    tm = TOKEN_TILE
    nb, nb_ctx, tpl = n_tok // tm, n_ctx // tm, lat_seq // tm
    D = D_MODEL
    specs = [pl.BlockSpec((tm, D), lambda i: (i, 0))]
    specs += [pl.BlockSpec((tm * ROW_UNITS, LANES), (lambda k: (lambda i: (k * nb + i, 0)))(k)) for k in range(TOP_K)]
    specs += [pl.BlockSpec((tm, LANES), lambda i: (i, 0)),
              pl.BlockSpec((1, 1, 6 * D), lambda i: ((DEPTH - 1) * 8 + _cond_row(i, nb_ctx, tpl), 0, 0)),
              _resident((1, D))]
    return pl.pallas_call(
        functools.partial(_final_kernel, nb_ctx=nb_ctx),
        grid=(nb,),
        in_specs=specs,
        out_specs=[pl.BlockSpec((tm, D), lambda i: (jnp.minimum(i, nb_ctx - 1), 0)),
                   pl.BlockSpec((tm, D), lambda i: (jnp.maximum(i - nb_ctx, 0), 0))],
        out_shape=[jax.ShapeDtypeStruct((n_ctx, D), F32), jax.ShapeDtypeStruct((n_tok - n_ctx, D), F32)],
        compiler_params=_cparams(("arbitrary",)),
        name="final_norm",
    )(x_new, *([y4] * TOP_K), wts, mods, gain.reshape(1, D))


def _rope_tables(n_tokens):
    rows = n_tokens // GRID_W
    r = jnp.repeat(jnp.arange(rows, dtype=F32), GRID_W)
    col = jnp.tile(jnp.arange(GRID_W, dtype=F32), rows)
    nf = RET_DK // 4
    inv = ROPE_BASE ** (-jnp.arange(nf, dtype=F32) / nf)
    ang = jnp.concatenate([r[:, None] * inv, col[:, None] * inv], axis=-1)
    cos, sin = jnp.cos(ang), jnp.sin(ang)
    return jnp.concatenate([cos, cos], axis=-1), jnp.concatenate([-sin, sin], axis=-1)


def kernel(x_prompt, x_sample, c, state_gla, state_ret, state_hgrn, c_ctx, norm_mix, norm_ffn, ada_w, ada_b,
           w_in_even, w_out_even, gla_gk_w, gla_gk_b, gla_gain, ret_decay_exp, w_in_odd, w_out_odd,
           hgrn_lb_logits, hgrn_gain, router_w, router_b, moe_w1, moe_b1, moe_w2, moe_b2, final_norm):
    D = D_MODEL
    B_ctx, T_ctx, _ = x_prompt.shape
    B_lat, T_lat, _ = x_sample.shape
    n_ctx, n_lat = B_ctx * T_ctx, B_lat * T_lat
    n_tok = n_ctx + n_lat
    assert n_ctx % BIG_TOKEN_TILE == 0 and T_lat % BIG_TOKEN_TILE == 0 and B_lat + 1 <= 8
    assert T_ctx % SCAN_CHUNK == 0 and T_lat % SCAN_CHUNK == 0 and n_ctx % T_lat == 0

    cond8 = jnp.concatenate([c_ctx[None, :], c, jnp.zeros((8 - 1 - B_lat, D), F32)], axis=0)
    mods = _ada_mods(cond8, ada_w, ada_b).reshape(DEPTH * 8, 1, 6 * D)

    rw_pad = jnp.pad(router_w, ((0, 0), (0, 0), (0, LANES - N_EXPERTS)))
    rb_pad = jnp.pad(router_b, ((0, 0), (0, LANES - N_EXPERTS)), constant_values=ROUTER_PAD_BIAS)

    w_even = w_in_even[0]
    lr0 = GLA_HEADS * (2 * GLA_DK + GLA_DV)
    lr1 = lr0 + 2 * GLA_LOW_RANK
    w_main = jnp.concatenate([w_even[:, :lr0], w_even[:, lr1:]], axis=1)
    w_lr = jnp.pad(w_even[:, lr0:lr1], ((0, 0), (0, LANES - 2 * GLA_LOW_RANK)))
    nqk = GLA_HEADS * GLA_DK
    gkw = jnp.zeros((LANES, 2 * nqk), F32)
    gkw = gkw.at[0:GLA_LOW_RANK, 0:nqk].set(gla_gk_w[0, 0])
    gkw = gkw.at[GLA_LOW_RANK:2 * GLA_LOW_RANK, nqk:].set(gla_gk_w[0, 1])
    gkb = gla_gk_b[0].reshape(1, 2 * nqk)
    proj, gdec, x_cur = _in_proj(0, n_ctx, n_tok, T_lat, mods, norm_mix[0], w_main,
                                 x_parts=(x_prompt.reshape(n_ctx, D), x_sample.reshape(n_lat, D)),
                                 even_extra=(w_lr, gkw, gkb))

    y_gla_c, fin_gla = _gla_scan(proj, gdec, gla_gain[0], None, 0, B_ctx, T_ctx, True)
    (y_gla_l,) = _gla_scan(proj, gdec, gla_gain[0], state_gla[:, 0], n_ctx, B_lat, T_lat, False)
    dexp = jnp.broadcast_to(ret_decay_exp[0].T[:, :, None], (RET_HEADS, 2, LANES))
    y_ret_c, fin_ret = _ret_scan(proj, dexp, None, None, 0, B_ctx, T_ctx, True)
    (y_ret_l,) = _ret_scan(proj, dexp, _rope_tables(T_lat), state_ret[:, 0], n_ctx, B_lat, T_lat, False)

    x_new, hn, idx, wts, rank, cnt = _out_proj(0, n_ctx, n_tok, T_lat, [(y_gla_c, y_gla_l), (y_ret_c, y_ret_l)],
                                               w_out_even[0], x_cur, mods, norm_ffn[0], rw_pad[0], rb_pad[0:1])
    y4 = _moe_layer(0, hn, idx, rank, cnt[0, :N_EXPERTS], moe_w1, moe_b1, moe_w2, moe_b2)

    (proj, x_cur) = _in_proj(1, n_ctx, n_tok, T_lat, mods, norm_mix[1], w_in_odd[0], x_prev=x_new, y4=(y4, wts))
    y_h_c, fin_h = _hgrn_scan(proj, hgrn_lb_logits, hgrn_gain[0], None, 0, B_ctx, T_ctx, True)
    (y_h_l,) = _hgrn_scan(proj, hgrn_lb_logits, hgrn_gain[0], state_hgrn[:, 0], n_ctx, B_lat, T_lat, False)
    x_new, hn, idx, wts, rank, cnt = _out_proj(1, n_ctx, n_tok, T_lat, [(y_h_c, y_h_l)], w_out_odd[0], x_cur, mods,
                                               norm_ffn[1], rw_pad[1], rb_pad[1:2])
    y4 = _moe_layer(1, hn, idx, rank, cnt[0, :N_EXPERTS], moe_w1, moe_b1, moe_w2, moe_b2)

    y_ctx, y_lat = _final(n_ctx, n_tok, T_lat, x_new, y4, wts, mods, final_norm)

    new_state_gla = fin_gla.reshape(B_ctx, 1, 2, GLA_HEADS, GLA_DK, GLA_DV)
    new_state_ret = fin_ret.reshape(B_ctx, 1, 2, RET_HEADS, RET_DK, RET_DV)
    new_state_hgrn = fin_h.reshape(B_ctx, 1, 2, HG_HEADS, HG_DK, HG_DV)
    return (y_ctx.reshape(B_ctx, T_ctx, D), y_lat.reshape(B_lat, T_lat, D), new_state_gla, new_state_ret,
            new_state_hgrn)
```

```python
import functools

import jax
import jax.numpy as jnp
from jax import lax
from jax.experimental import pallas as pl
from jax.experimental.pallas import tpu as pltpu

F32 = jnp.float32
BF16 = jnp.bfloat16

D_MODEL = 1024
DEPTH = 2
GRID_W = 64
GLA_HEADS, GLA_DK, GLA_DV, GLA_LOW_RANK = 4, 64, 128, 16
GLA_NORMALIZER = 16.0
RET_HEADS, RET_DK, RET_DV = 4, 128, 128
ROPE_BASE = 10000.0
HG_HEADS, HG_DK, HG_DV = 8, 128, 128
N_EXPERTS, TOP_K, D_FF = 32, 4, 1024
SWIGLU_ALPHA, SWIGLU_LIMIT = 1.702, 7.0
EPS = 1e-6

LANES = 128
SCAN_CHUNK = 128
SCAN_ROWS = 2048
TOKEN_TILE = 256
BIG_TOKEN_TILE = 512
MOE_TILE = 384
VMEM_LIMIT = 56 * 1024 * 1024

_GQ, _GK, _GV, _GG, _RQ, _RK, _RV, _RG, _EVEN_MAIN = 0, 256, 512, 1024, 1536, 2048, 2560, 3072, 3584
_HQ, _HFF, _HFB, _HI, _HG, _ODD_MAIN = 0, 1024, 2048, 3072, 4096, 5120


def _dot(a, b):
    return jnp.dot(a, b, preferred_element_type=F32)


def _dot_nt(a, b):
    return lax.dot_general(a, b, (((1,), (1,)), ((), ())), preferred_element_type=F32)


def _rms(x, gain=None):
    y = x * lax.rsqrt(jnp.mean(x * x, axis=-1, keepdims=True) + EPS)
    if gain is not None:
        y = y * gain
    return y


def _sigmoid(x):
    return 0.5 * jnp.tanh(0.5 * x) + 0.5


def _silu(x):
    return x * _sigmoid(x)


def _cparams(sem, vmem=VMEM_LIMIT):
    return pltpu.CompilerParams(dimension_semantics=sem, vmem_limit_bytes=vmem)


def _resident(shape):
    nd = len(shape)
    return pl.BlockSpec(shape, lambda *_: (0,) * nd, pipeline_mode=pl.Buffered(1))


def _ada_kernel(c_ref, w_ref, b_ref, o_ref):
    o_ref[0] = _dot(_silu(c_ref[...]), w_ref[0]) + b_ref[0]


def _ada_mods(cond8, ada_w, ada_b):
    tn = 1536
    return pl.pallas_call(
        _ada_kernel,
        grid=(DEPTH, 6 * D_MODEL // tn),
        in_specs=[
            pl.BlockSpec((8, D_MODEL), lambda l, j: (0, 0)),
            pl.BlockSpec((1, D_MODEL, tn), lambda l, j: (l, 0, j)),
            pl.BlockSpec((1, 1, tn), lambda l, j: (l, 0, j)),
        ],
        out_specs=pl.BlockSpec((1, 8, tn), lambda l, j: (l, 0, j)),
        out_shape=jax.ShapeDtypeStruct((DEPTH, 8, 6 * D_MODEL), F32),
        compiler_params=_cparams(("arbitrary", "arbitrary")),
        name="ada_mods",
    )(cond8, ada_w, ada_b.reshape(DEPTH, 1, 6 * D_MODEL))


ROW_UNITS = D_MODEL // LANES


def _rows_from_units(ref, n_rows):
    return jnp.concatenate([ref[pl.ds(c, n_rows, stride=ROW_UNITS), :] for c in range(ROW_UNITS)], axis=1)


def _rows_to_units(ref, val):
    n_rows = val.shape[0]
    for c in range(ROW_UNITS):
        ref[pl.ds(c, n_rows, stride=ROW_UNITS), :] = val[:, c * LANES:(c + 1) * LANES]


def _moe_combine(y_refs, wt_ref):
    w = wt_ref[...]
    terms = [w[:, k:k + 1] * _rows_from_units(y_refs[k], w.shape[0]) for k in range(TOP_K)]
    return (terms[0] + terms[1]) + (terms[2] + terms[3])


def _in_proj_kernel(*refs, first, even, nb_ctx):
    it = iter(refs)
    if first:
        xa_ref, xb_ref = next(it), next(it)
    else:
        xp_ref = next(it)
        y_refs = [next(it) for _ in range(TOP_K)]
        wt_ref, modp_ref = next(it), next(it)
    gain_ref, mod_ref, w_ref = next(it), next(it), next(it)
    if even:
        wlr_ref, gkw_ref, gkb_ref = next(it), next(it), next(it)
    proj_ref = next(it)
    if even:
        g_ref = next(it)
    xcur_ref = next(it)

    D = D_MODEL
    i = pl.program_id(0)
    if first:
        x = jnp.where(i < nb_ctx, xa_ref[...], xb_ref[...])
    else:
        x = xp_ref[...] + modp_ref[0][:, 5 * D:6 * D] * _moe_combine(y_refs, wt_ref)
    xcur_ref[...] = x
    m = mod_ref[0]
    hn = _rms(x, gain_ref[...]) * (1.0 + m[:, D:2 * D]) + m[:, 0:D]
    proj_ref[...] = _dot(hn, w_ref[...])
    if even:
        z = _dot(_dot(hn, wlr_ref[...]), gkw_ref[...]) + gkb_ref[...]
        g_ref[...] = (jnp.minimum(z, 0.0) - jnp.log(1.0 + jnp.exp(-jnp.abs(z)))) * (1.0 / GLA_NORMALIZER)


def _cond_row(i, nb_ctx, tiles_per_lat_seq):
    return jnp.where(i < nb_ctx, 0, 1 + (i - nb_ctx) // tiles_per_lat_seq)


def _in_proj(layer, n_ctx, n_tok, lat_seq, mods, gain, w_main, *, x_parts=None, x_prev=None, y4=None, even_extra=None):
    first = x_parts is not None
    even = even_extra is not None
    tm = BIG_TOKEN_TILE if first else TOKEN_TILE
    nb = n_tok // tm
    nb_ctx = n_ctx // tm
    tpl = lat_seq // tm
    np_cols = w_main.shape[1]
    D = D_MODEL

    def mod_map(l):
        return lambda i: (l * 8 + _cond_row(i, nb_ctx, tpl), 0, 0)

    row = lambda i: (i, 0)
    args, specs = [], []
    if first:
        xa, xb = x_parts
        args += [xa, xb]
        specs += [pl.BlockSpec((tm, D), lambda i: (jnp.minimum(i, nb_ctx - 1), 0)),
                  pl.BlockSpec((tm, D), lambda i: (jnp.maximum(i - nb_ctx, 0), 0))]
    else:
        y4, wts = y4
        args += [x_prev] + [y4] * TOP_K + [wts, mods]
        specs += [pl.BlockSpec((tm, D), row)]
        specs += [pl.BlockSpec((tm * ROW_UNITS, LANES), (lambda k: (lambda i: (k * nb + i, 0)))(k))
                  for k in range(TOP_K)]
        specs += [pl.BlockSpec((tm, LANES), row), pl.BlockSpec((1, 1, 6 * D), mod_map(layer - 1))]
    args += [gain.reshape(1, D), mods, w_main]
    specs += [_resident((1, D)), pl.BlockSpec((1, 1, 6 * D), mod_map(layer)), _resident((D, np_cols))]
    out_shapes = [jax.ShapeDtypeStruct((n_tok, np_cols), F32)]
    out_specs = [pl.BlockSpec((tm, np_cols), row)]
    if even:
        w_lr, gkw, gkb = even_extra
        args += [w_lr, gkw, gkb]
        specs += [_resident(w_lr.shape), _resident(gkw.shape), _resident(gkb.shape)]
        out_shapes.append(jax.ShapeDtypeStruct((n_tok, 2 * GLA_HEADS * GLA_DK), F32))
        out_specs.append(pl.BlockSpec((tm, 2 * GLA_HEADS * GLA_DK), row))
    out_shapes.append(jax.ShapeDtypeStruct((n_tok, D), F32))
    out_specs.append(pl.BlockSpec((tm, D), row))
    return pl.pallas_call(
        functools.partial(_in_proj_kernel, first=first, even=even, nb_ctx=nb_ctx),
        grid=(nb,),
        in_specs=specs,
        out_specs=out_specs,
        out_shape=out_shapes,
        compiler_params=_cparams(("arbitrary",)),
        name=f"in_proj_l{layer}",
    )(*args)


def _tri(c, lower):
    r = lax.broadcasted_iota(jnp.int32, (c, c), 0)
    s = lax.broadcasted_iota(jnp.int32, (c, c), 1)
    return (r >= s) if lower else (r <= s)


def _cumsum_mm(tri_bf16, g):
    g_hi = g.astype(BF16)
    g_lo = (g - g_hi.astype(F32)).astype(BF16)
    r = _dot(tri_bf16, jnp.concatenate([g_hi, g_lo], axis=1))
    w = g.shape[1]
    return r[:, :w] + r[:, w:]


def _gated_scan_kernel(*refs, variant, T, nseq, has_s0, want_final, hpb):
    C = SCAN_CHUNK
    ncs = T // C
    nc = nseq * ncs
    it = iter(refs)
    if variant == "gla":
        q_ref, k_ref, v_ref, og_ref, gf_ref, gb_ref, gain_ref = (next(it) for _ in range(7))
    else:
        q_ref, ff_ref, fb_ref, v_ref, og_ref, lbl_ref, gain_ref = (next(it) for _ in range(7))
    s0_ref = next(it) if has_s0 else None
    y_ref = next(it)
    sfin_ref = next(it) if want_final else None
    qi_scr, o_scr, u_scr, dec_scr, sin_scr = (next(it) for _ in range(5))

    lane = lax.broadcasted_iota(jnp.int32, (1, LANES), 1)
    if hpb == 2:
        masks = [lane < GLA_DK, lane >= GLA_DK]
    else:
        masks = [None]
    tri_l, tri_u = _tri(C, True), _tri(C, False)
    tri_l16, tri_u16 = tri_l.astype(F32).astype(BF16), tri_u.astype(F32).astype(BF16)
    H = C // 2
    upper_rows = lax.broadcasted_iota(jnp.int32, (C, 1), 0) < H

    if variant == "hgrn":
        lgs = [lbl_ref[l] for l in range(DEPTH)]
        mx = functools.reduce(jnp.maximum, lgs)
        es = [jnp.exp(l - mx) for l in lgs]
        tot = functools.reduce(lambda a, b: a + b, es)
        ps = [e / tot for e in es]
        layer = DEPTH - 1
        lb = functools.reduce(lambda a, b: a + b, ps[:layer + 1]) - ps[0]
        lb_f, lb_b = lb[0:1], lb[1:2]

    for n in range(nc):
        rows = pl.ds(n * C, C)
        if variant == "gla":
            q = q_ref[rows, :] * (GLA_DK ** -0.5)
            kf = kb = k_ref[rows, :]
            gf, gb = gf_ref[rows, :], gb_ref[rows, :]
        else:
            q = _silu(q_ref[rows, :])
            f_f = lb_f + (1.0 - lb_f) * _sigmoid(ff_ref[rows, :])
            f_b = lb_b + (1.0 - lb_b) * _sigmoid(fb_ref[rows, :])
            kf, kb = 1.0 - f_f, 1.0 - f_b
            gf, gb = jnp.log(f_f), jnp.log(f_b)
        bf = _cumsum_mm(tri_l16, gf)
        bb = _cumsum_mm(tri_u16, gb)
        bf_end, bb_end = bf[C - 1:C], bb[0:1]
        ref_f = jnp.where(upper_rows, bf[H // 2 - 1:H // 2], bf[H + H // 2 - 1:H + H // 2])
        ref_b = jnp.where(upper_rows, bb[H // 2:H // 2 + 1], bb[H + H // 2:H + H // 2 + 1])
        qd_f, qd_b = q * jnp.exp(bf - ref_f), q * jnp.exp(bb - ref_b)
        kd_f = [jnp.where(upper_rows, kf * jnp.exp(bf[H // 2 - 1:H // 2] - bf), 0.0),
                kf * jnp.exp(bf[H + H // 2 - 1:H + H // 2] - bf)]
        kd_b = [kb * jnp.exp(bb[H // 2:H // 2 + 1] - bb),
                jnp.where(upper_rows, 0.0, kb * jnp.exp(bb[H + H // 2:H + H // 2 + 1] - bb))]
        kend = jnp.concatenate([kf * jnp.exp(bf_end - bf), kb * jnp.exp(bb_end - bb)], axis=1)
        qi_scr[rows, :] = jnp.concatenate([q * jnp.exp(bf), q * jnp.exp(bb)], axis=1)
        dec_scr[n] = jnp.broadcast_to(jnp.concatenate([jnp.exp(bf_end), jnp.exp(bb_end)], axis=1), (8, 2 * LANES))
        for h in range(hpb):
            v = v_ref[rows, h * LANES:(h + 1) * LANES]
            if masks[h] is None:
                qf_h, qb_h, kend_h = qd_f, qd_b, kend
            else:
                qf_h, qb_h = jnp.where(masks[h], qd_f, 0.0), jnp.where(masks[h], qd_b, 0.0)
                kend_h = jnp.where(jnp.concatenate([masks[h], masks[h]], axis=1), kend, 0.0)
            s_f = jnp.concatenate([_dot_nt(qf_h[:H], kd_f[0]), _dot_nt(qf_h[H:], kd_f[1])], axis=0)
            s_b = jnp.concatenate([_dot_nt(qb_h[:H], kd_b[0]), _dot_nt(qb_h[H:], kd_b[1])], axis=0)
            s = jnp.where(tri_l, s_f, 0.0) + jnp.where(tri_u, s_b, 0.0)
            o_scr[rows, h * LANES:(h + 1) * LANES] = _dot(s, v)
            u_scr[h, n] = _dot(v.T, kend_h)

    for sq in range(nseq):
        for h in range(hpb):
            if has_s0:
                s_f, s_b = s0_ref[sq, 0, 0].T, s0_ref[sq, 1, 0].T
                if masks[h] is not None:
                    s_f, s_b = jnp.where(masks[h], s_f, 0.0), jnp.where(masks[h], s_b, 0.0)
            else:
                s_f = s_b = jnp.zeros((LANES, LANES), F32)
            for n in range(sq * ncs, (sq + 1) * ncs):
                sin_scr[h, n, :, 0:LANES] = s_f
                s_f = s_f * dec_scr[n, 0:1, 0:LANES] + u_scr[h, n, :, 0:LANES]
            for n in reversed(range(sq * ncs, (sq + 1) * ncs)):
                sin_scr[h, n, :, LANES:2 * LANES] = s_b
                s_b = s_b * dec_scr[n, 0:1, LANES:2 * LANES] + u_scr[h, n, :, LANES:2 * LANES]
            if want_final:
                if h == 0:
                    fin_f, fin_b = s_f, s_b
                else:
                    fin_f, fin_b = fin_f + s_f, fin_b + s_b
        if want_final:
            sfin_ref[sq, 0, 0] = fin_f.T
            sfin_ref[sq, 1, 0] = fin_b.T

    gain = gain_ref[...]
    for n in range(nc):
        rows = pl.ds(n * C, C)
        qi = qi_scr[rows, :]
        for h in range(hpb):
            cols = slice(h * LANES, (h + 1) * LANES)
            o = o_scr[rows, cols] + _dot_nt(qi, sin_scr[h, n])
            y_ref[rows, cols] = (_rms(o, gain) * _silu(og_ref[rows, cols])).astype(y_ref.dtype)


def _scan_scratch(T, hpb):
    nc = T // SCAN_CHUNK
    return [
        pltpu.VMEM((T, 2 * LANES), F32),
        pltpu.VMEM((T, hpb * LANES), F32),
        pltpu.VMEM((hpb, nc, LANES, 2 * LANES), F32),
        pltpu.VMEM((nc, 8, 2 * LANES), F32),
        pltpu.VMEM((hpb, nc, LANES, 2 * LANES), F32),
    ]


def _seqs_per_step(B, T):
    nseq = max(1, min(B, SCAN_ROWS // T))
    assert B % nseq == 0
    return nseq


def _gla_scan(proj, gdec, gain, s0, row0, B, T, want_final):
    nseq = _seqs_per_step(B, T)
    R = nseq * T
    rb0 = row0 // R
    assert row0 % R == 0
    has_s0 = s0 is not None

    def col(base, width=LANES):
        return lambda b, j: (rb0 + b, base // width + j)

    args = [proj, proj, proj, proj, gdec, gdec, gain.reshape(1, GLA_DV)]
    specs = [pl.BlockSpec((R, LANES), col(_GQ)), pl.BlockSpec((R, LANES), col(_GK)),
             pl.BlockSpec((R, 2 * LANES), col(_GV, 2 * LANES)), pl.BlockSpec((R, 2 * LANES), col(_GG, 2 * LANES)),
             pl.BlockSpec((R, LANES), col(0)), pl.BlockSpec((R, LANES), col(GLA_HEADS * GLA_DK)),
             pl.BlockSpec((1, GLA_DV), lambda b, j: (0, 0))]
    st_spec = pl.BlockSpec((nseq, 2, 1, LANES, LANES), lambda b, j: (b, 0, j, 0, 0))
    if has_s0:
        args.append(s0.reshape(B, 2, GLA_HEADS // 2, 2 * GLA_DK, GLA_DV))
        specs.append(st_spec)
    out_shapes = [jax.ShapeDtypeStruct((B * T, GLA_HEADS * GLA_DV), BF16)]
    out_specs = [pl.BlockSpec((R, 2 * LANES), lambda b, j: (b, j))]
    if want_final:
        out_shapes.append(jax.ShapeDtypeStruct((B, 2, GLA_HEADS // 2, 2 * GLA_DK, GLA_DV), F32))
        out_specs.append(st_spec)
    kern = functools.partial(_gated_scan_kernel, variant="gla", T=T, nseq=nseq, has_s0=has_s0,
                             want_final=want_final, hpb=2)
    return pl.pallas_call(
        kern, grid=(B // nseq, GLA_HEADS // 2), in_specs=specs, out_specs=out_specs, out_shape=out_shapes,
        scratch_shapes=_scan_scratch(R, 2),
        compiler_params=_cparams(("arbitrary", "arbitrary")), name=f"gla_scan_T{T}",
    )(*args)


def _hgrn_scan(proj, lb_logits, gain, s0, row0, B, T, want_final):
    nseq = _seqs_per_step(B, T)
    R = nseq * T
    rb0 = row0 // R
    assert row0 % R == 0
    has_s0 = s0 is not None

    def col(base):
        return lambda b, j: (rb0 + b, base // LANES + j)

    args = [proj, proj, proj, proj, proj, lb_logits, gain.reshape(1, HG_DV)]
    specs = [pl.BlockSpec((R, LANES), col(_HQ)), pl.BlockSpec((R, LANES), col(_HFF)),
             pl.BlockSpec((R, LANES), col(_HFB)), pl.BlockSpec((R, LANES), col(_HI)),
             pl.BlockSpec((R, LANES), col(_HG)),
             pl.BlockSpec((DEPTH, 2, LANES), lambda b, j: (0, 0, j)),
             pl.BlockSpec((1, HG_DV), lambda b, j: (0, 0))]
    st_spec = pl.BlockSpec((nseq, 2, 1, LANES, LANES), lambda b, j: (b, 0, j, 0, 0))
    if has_s0:
        args.append(s0.reshape(B, 2, HG_HEADS, HG_DK, HG_DV))
        specs.append(st_spec)
    out_shapes = [jax.ShapeDtypeStruct((B * T, HG_HEADS * HG_DV), BF16)]
    out_specs = [pl.BlockSpec((R, LANES), lambda b, j: (b, j))]
    if want_final:
        out_shapes.append(jax.ShapeDtypeStruct((B, 2, HG_HEADS, HG_DK, HG_DV), F32))
        out_specs.append(st_spec)
    kern = functools.partial(_gated_scan_kernel, variant="hgrn", T=T, nseq=nseq, has_s0=has_s0,
                             want_final=want_final, hpb=1)
    return pl.pallas_call(
        kern, grid=(B // nseq, HG_HEADS), in_specs=specs, out_specs=out_specs, out_shape=out_shapes,
        scratch_shapes=_scan_scratch(R, 1),
        compiler_params=_cparams(("arbitrary", "arbitrary")), name=f"hgrn_scan_T{T}",
    )(*args)


def _ret_scan_kernel(*refs, T, nseq, has_s0, want_final, rope):
    C = SCAN_CHUNK
    ncs = T // C
    nc = nseq * ncs
    it = iter(refs)
    q_ref, k_ref, v_ref, og_ref, dexp_ref = (next(it) for _ in range(5))
    if rope:
        cos_ref, sin_ref = next(it), next(it)
    s0_ref = next(it) if has_s0 else None
    y_ref = next(it)
    sfin_ref = next(it) if want_final else None
    qi_scr, o_scr, u_scr, sin_scr = (next(it) for _ in range(4))

    lg = jnp.log1p(-jnp.exp2(-dexp_ref[0]))
    lg_f, lg_b = lg[0:1], lg[1:2]
    r = lax.broadcasted_iota(jnp.int32, (C, C), 0)
    s = lax.broadcasted_iota(jnp.int32, (C, C), 1)
    dist = (r - s).astype(F32)
    dmask = (jnp.where(r >= s, jnp.exp(jnp.maximum(dist, 0.0) * lg_f[:, 0:1]), 0.0)
             + jnp.where(r <= s, jnp.exp(jnp.maximum(-dist, 0.0) * lg_b[:, 0:1]), 0.0))
    pos = lax.broadcasted_iota(jnp.int32, (C, LANES), 0).astype(F32)
    xi = jnp.concatenate([jnp.exp((pos + 1.0) * lg_f), jnp.exp((C - pos) * lg_b)], axis=1)
    zeta = jnp.concatenate([jnp.exp((C - 1.0 - pos) * lg_f), jnp.exp(pos * lg_b)], axis=1)
    d_f, d_b = jnp.exp(C * lg_f), jnp.exp(C * lg_b)

    def rot(x, seq_rows):
        if not rope:
            return x
        return x * cos_ref[seq_rows, :] + pltpu.roll(x, RET_DK // 2, axis=1) * sin_ref[seq_rows, :]

    for n in range(nc):
        rows = pl.ds(n * C, C)
        seq_rows = pl.ds((n % ncs) * C, C)
        q = rot(q_ref[rows, :], seq_rows)
        k = rot(k_ref[rows, :] * (RET_DK ** -0.5), seq_rows)
        v = v_ref[rows, :]
        o_scr[rows, :] = _dot(_dot_nt(q, k) * dmask, v)
        qi_scr[rows, :] = jnp.concatenate([q, q], axis=1) * xi
        u_scr[n] = _dot(v.T, jnp.concatenate([k, k], axis=1) * zeta)

    for sq in range(nseq):
        if has_s0:
            s_f, s_b = s0_ref[sq, 0, 0].T, s0_ref[sq, 1, 0].T
        else:
            s_f = s_b = jnp.zeros((LANES, LANES), F32)
        for n in range(sq * ncs, (sq + 1) * ncs):
            sin_scr[n, :, 0:LANES] = s_f
            s_f = s_f * d_f + u_scr[n, :, 0:LANES]
        for n in reversed(range(sq * ncs, (sq + 1) * ncs)):
            sin_scr[n, :, LANES:2 * LANES] = s_b
            s_b = s_b * d_b + u_scr[n, :, LANES:2 * LANES]
        if want_final:
            sfin_ref[sq, 0, 0] = s_f.T
            sfin_ref[sq, 1, 0] = s_b.T

    for n in range(nc):
        rows = pl.ds(n * C, C)
        o = o_scr[rows, :] + _dot_nt(qi_scr[rows, :], sin_scr[n])
        y_ref[rows, :] = (_rms(o) * _silu(og_ref[rows, :])).astype(y_ref.dtype)


def _ret_scan(proj, dexp, rope_tabs, s0, row0, B, T, want_final):
    nseq = _seqs_per_step(B, T)
    R = nseq * T
    rb0 = row0 // R
    assert row0 % R == 0
    nc = R // SCAN_CHUNK
    has_s0 = s0 is not None
    rope = rope_tabs is not None

    def col(base):
        return lambda b, j: (rb0 + b, base // LANES + j)

    args = [proj, proj, proj, proj, dexp]
    specs = [pl.BlockSpec((R, LANES), col(_RQ)), pl.BlockSpec((R, LANES), col(_RK)),
             pl.BlockSpec((R, LANES), col(_RV)), pl.BlockSpec((R, LANES), col(_RG)),
             pl.BlockSpec((1, 2, LANES), lambda b, j: (j, 0, 0))]
    if rope:
        args += list(rope_tabs)
        specs += [pl.BlockSpec((T, LANES), lambda b, j: (0, 0))] * 2
    st_spec = pl.BlockSpec((nseq, 2, 1, LANES, LANES), lambda b, j: (b, 0, j, 0, 0))
    if has_s0:
        args.append(s0.reshape(B, 2, RET_HEADS, RET_DK, RET_DV))
        specs.append(st_spec)
    out_shapes = [jax.ShapeDtypeStruct((B * T, RET_HEADS * RET_DV), BF16)]
    out_specs = [pl.BlockSpec((R, LANES), lambda b, j: (b, j))]
    if want_final:
        out_shapes.append(jax.ShapeDtypeStruct((B, 2, RET_HEADS, RET_DK, RET_DV), F32))
        out_specs.append(st_spec)
    kern = functools.partial(_ret_scan_kernel, T=T, nseq=nseq, has_s0=has_s0, want_final=want_final, rope=rope)
    scratch = [pltpu.VMEM((R, 2 * LANES), F32), pltpu.VMEM((R, LANES), F32),
               pltpu.VMEM((nc, LANES, 2 * LANES), F32), pltpu.VMEM((nc, LANES, 2 * LANES), F32)]
    return pl.pallas_call(
        kern, grid=(B // nseq, RET_HEADS), in_specs=specs, out_specs=out_specs, out_shape=out_shapes,
        scratch_shapes=scratch,
        compiler_params=_cparams(("arbitrary", "arbitrary")), name=f"ret_scan_T{T}",
    )(*args)


def _out_proj_kernel(*refs, n_mix, nb_ctx):
    it = iter(refs)
    y_refs = [(next(it), next(it)) for _ in range(n_mix)]
    wo_ref, x_ref, mod_ref, gain_ref, rw_ref, rb_ref = (next(it) for _ in range(6))
    xnew_ref, hn_ref, idx_ref, wt_ref, rank_ref, cnt_ref = (next(it) for _ in range(6))
    cnt_scr = next(it)
    D = D_MODEL
    is_ctx = pl.program_id(0) < nb_ctx
    mix = None
    r0 = 0
    for ya_ref, yb_ref in y_refs:
        w = ya_ref.shape[1]
        part = _dot(jnp.where(is_ctx, ya_ref[...], yb_ref[...]), wo_ref[r0:r0 + w, :])
        mix = part if mix is None else mix + part
        r0 += w
    m = mod_ref[0]
    xn = x_ref[...] + m[:, 2 * D:3 * D] * mix
    xnew_ref[...] = xn
    hn = _rms(xn, gain_ref[...]) * (1.0 + m[:, 4 * D:5 * D]) + m[:, 3 * D:4 * D]
    hn_hi = hn.astype(BF16)
    _rows_to_units(hn_ref, hn)
    hn_lo = (hn - hn_hi.astype(F32)).astype(BF16)
    rw = rw_ref[...]
    rw_hi = rw.astype(BF16)
    rw_lo = (rw - rw_hi.astype(F32)).astype(BF16)
    logits = (_dot(hn_hi, rw_hi) + (_dot(hn_lo, rw_hi) + _dot(hn_hi, rw_lo))) + rb_ref[...]
    lane = lax.broadcasted_iota(jnp.int32, logits.shape, 1).astype(F32)
    vals, idxs = [], []
    cur = logits
    for _ in range(TOP_K):
        mx = jnp.max(cur, axis=-1, keepdims=True)
        ik = jnp.min(jnp.where(cur == mx, lane, float(LANES)), axis=-1, keepdims=True)
        vals.append(mx)
        idxs.append(ik)
        cur = jnp.where(lane == ik, -jnp.inf, cur)
    es = [jnp.exp(v - vals[0]) for v in vals]
    tot = (es[0] + es[1]) + (es[2] + es[3])
    idx_out = jnp.zeros(logits.shape, F32)
    wt_out = jnp.zeros(logits.shape, F32)
    for k in range(TOP_K):
        idx_out = jnp.where(lane == float(k), idxs[k], idx_out)
        wt_out = jnp.where(lane == float(k), es[k] / tot, wt_out)
    idx_ref[...] = idx_out.astype(jnp.int32)
    wt_ref[...] = wt_out

    @pl.when(pl.program_id(0) == 0)
    def _():
        cnt_scr[...] = jnp.zeros(cnt_scr.shape, F32)

    tm = logits.shape[0]
    hits = [lane == idxs[k] for k in range(TOP_K)]
    sel = jnp.zeros(logits.shape, F32)
    for k in range(TOP_K):
        sel = sel + jnp.where(hits[k], 1.0, 0.0)
    rr = lax.broadcasted_iota(jnp.int32, (tm, tm), 0)
    cc = lax.broadcasted_iota(jnp.int32, (tm, tm), 1)
    before = jnp.where(rr > cc, 1.0, 0.0).astype(BF16)
    rank_all = cnt_scr[0:1, :] + _dot(before, sel.astype(BF16))
    rank_out = jnp.zeros(logits.shape, F32)
    for k in range(TOP_K):
        rk = jnp.sum(jnp.where(hits[k], rank_all, 0.0), axis=-1, keepdims=True)
        rank_out = jnp.where(lane == float(k), rk, rank_out)
    rank_ref[...] = rank_out.astype(jnp.int32)
    total = cnt_scr[...] + jnp.sum(sel, axis=0, keepdims=True)
    cnt_scr[...] = total
    cnt_ref[...] = total.astype(jnp.int32)


def _out_proj(layer, n_ctx, n_tok, lat_seq, ys, w_out, x_cur, mods, gain, rw_pad, rb_pad):
    tm = BIG_TOKEN_TILE
    nb, nb_ctx, tpl = n_tok // tm, n_ctx // tm, lat_seq // tm
    D = D_MODEL
    row = lambda i: (i, 0)
    specs = []
    for ya, _ in ys:
        specs += [pl.BlockSpec((tm, ya.shape[1]), lambda i: (jnp.minimum(i, nb_ctx - 1), 0)),
                  pl.BlockSpec((tm, ya.shape[1]), lambda i: (jnp.maximum(i - nb_ctx, 0), 0))]
    specs += [_resident((D, D)), pl.BlockSpec((tm, D), row),
              pl.BlockSpec((1, 1, 6 * D), lambda i: (layer * 8 + _cond_row(i, nb_ctx, tpl), 0, 0)),
              _resident((1, D)), _resident((D, LANES)), _resident((1, LANES))]
    return pl.pallas_call(
        functools.partial(_out_proj_kernel, n_mix=len(ys), nb_ctx=nb_ctx),
        grid=(nb,),
        in_specs=specs,
        out_specs=[pl.BlockSpec((tm, D), row), pl.BlockSpec((tm * ROW_UNITS, LANES), row),
                   pl.BlockSpec((tm, LANES), row), pl.BlockSpec((tm, LANES), row),
                   pl.BlockSpec((tm, LANES), row), pl.BlockSpec((8, LANES), lambda i: (0, 0))],
        out_shape=[jax.ShapeDtypeStruct((n_tok, D), F32), jax.ShapeDtypeStruct((n_tok * ROW_UNITS, LANES), F32),
                   jax.ShapeDtypeStruct((n_tok, LANES), jnp.int32), jax.ShapeDtypeStruct((n_tok, LANES), F32),
                   jax.ShapeDtypeStruct((n_tok, LANES), jnp.int32), jax.ShapeDtypeStruct((8, LANES), jnp.int32)],
        scratch_shapes=[pltpu.VMEM((8, LANES), F32)],
        compiler_params=_cparams(("arbitrary",)),
        name=f"out_proj_l{layer}",
    )(*[y for pair in ys for y in pair], w_out, x_cur, mods, gain.reshape(1, D), rw_pad, rb_pad)


W1_SPLIT = 4
W2_SPLIT = 2
SPARE_TILES = 2
ROUTER_PAD_BIAS = -1e30


def _moe_kernel(seg_ref, nt_ref, code_hbm, x_hbm, *refs, n_tok):
    w1_refs = refs[:W1_SPLIT]
    b1_ref = refs[W1_SPLIT]
    w2_refs = refs[W1_SPLIT + 1:W1_SPLIT + 1 + W2_SPLIT]
    (b2_ref, y_hbm, w1_scr, b1_scr, w2_scr, act_scr, xbuf, obuf, code_smem, csem, gsem,
     ssem) = refs[W1_SPLIT + 1 + W2_SPLIT:]
    tm = MOE_TILE
    U = ROW_UNITS
    n_code_tiles = code_hbm.shape[0] // tm
    e = pl.program_id(0)
    nt = nt_ref[e]
    g_first = seg_ref[e] // tm

    GATHER, SCATTER = 0, 1

    def code_copy(kind, t, p):
        first = pl.multiple_of(jnp.clip(t, 0, n_code_tiles - 1) * tm, tm)
        return pltpu.make_async_copy(code_hbm.at[pl.ds(first, tm)], code_smem.at[kind, p], csem.at[kind, p])

    def gather_start(b, p):
        for r in range(tm):
            tok = jnp.bitwise_and(code_smem[GATHER, p, r], n_tok - 1)
            pltpu.make_async_copy(x_hbm.at[pl.ds(pl.multiple_of(tok * U, U), U)],
                                  xbuf.at[b, pl.ds(r * U, U)], gsem.at[b]).start()

    def gather_wait(b):
        pltpu.make_async_copy(x_hbm.at[pl.ds(0, tm * U)], xbuf.at[b], gsem.at[b]).wait()

    def scatter_start(b, p):
        for r in range(tm):
            dst = code_smem[SCATTER, p, r]
            pltpu.make_async_copy(obuf.at[b, pl.ds(r * U, U)],
                                  y_hbm.at[pl.ds(pl.multiple_of(dst * U, U), U)], ssem.at[b]).start()

    def scatter_wait(b):
        pltpu.make_async_copy(obuf.at[b], y_hbm.at[pl.ds(0, tm * U)], ssem.at[b]).wait()

    @pl.when(e == 0)
    def _():
        obuf[...] = jnp.zeros(obuf.shape, obuf.dtype)
        for h in range(SPARE_TILES):
            cp = pltpu.make_async_copy(obuf.at[0], y_hbm.at[pl.ds((TOP_K * n_tok + h * tm) * U, tm * U)], ssem.at[0])
            cp.start()
            cp.wait()
        code_copy(GATHER, 0, 0).start()
        code_copy(GATHER, 1, 1).start()
        code_copy(SCATTER, n_code_tiles - 1, 1).start()
        code_copy(SCATTER, 0, 0).start()
        code_copy(GATHER, 0, 0).wait()
        gather_start(0, 0)

    @pl.when(nt > 0)
    def _():
        grp = 2 * LANES
        rr = lax.broadcasted_iota(jnp.int32, (grp, grp), 0)
        cc = lax.broadcasted_iota(jnp.int32, (grp, grp), 1)
        src = jnp.where(cc < LANES, 2 * cc, 2 * (cc - LANES) + 1)
        perm = jnp.where(rr == src, 1.0, 0.0).astype(BF16)
        cw = 2 * D_FF // W1_SPLIT
        for c, w_ref in enumerate(w1_refs):
            for j in range(cw // grp):
                cols = slice(j * grp, (j + 1) * grp)
                w1_scr[:, c * cw + j * grp:c * cw + (j + 1) * grp] = _dot(w_ref[0, 0, :, cols].astype(BF16),
                                                                         perm).astype(BF16)
        bias = jnp.broadcast_to(b1_ref[0, 0], (8, 2 * D_FF))
        for j in range(2 * D_FF // grp):
            cols = slice(j * grp, (j + 1) * grp)
            rest = bias[:, cols]
            acc = jnp.zeros((8, grp), F32)
            for _ in range(3):
                term = rest.astype(BF16)
                acc = acc + _dot(term, perm)
                rest = rest - term.astype(F32)
            b1_scr[:, cols] = acc
        rh = D_FF // W2_SPLIT
        for c, w_ref in enumerate(w2_refs):
            w2_scr[c * rh:(c + 1) * rh, :] = w_ref[0, 0].astype(BF16)
        blk = 4 * LANES

        def tile(g, b):
            o = 1 - b

            @pl.when(g >= 0)
            def _():
                code_copy(GATHER, g + 1, o).wait()
                code_copy(SCATTER, g - 1, o).wait()
                gather_start(o, o)
                scatter_start(o, o)
                code_copy(GATHER, g + 2, b).start()
                code_copy(SCATTER, g + 1, o).start()

            gather_wait(b)
            x = _rows_from_units(xbuf.at[b], tm).astype(BF16)
            for c in range(2 * D_FF // blk):
                cols = slice(c * blk, (c + 1) * blk)
                hid = _dot(x, w1_scr[:, cols]) + b1_scr[0:1, cols]
                acts = []
                for j in range(blk // grp):
                    glu = jnp.minimum(hid[:, j * grp:j * grp + LANES], SWIGLU_LIMIT)
                    lin = jnp.clip(hid[:, j * grp + LANES:(j + 1) * grp], -SWIGLU_LIMIT, SWIGLU_LIMIT)
                    acts.append(glu * _sigmoid(SWIGLU_ALPHA * glu) * (lin + 1.0))
                act_scr[:, c * (blk // 2):(c + 1) * (blk // 2)] = jnp.concatenate(acts, axis=1).astype(BF16)
            y = _dot(act_scr[...], w2_scr[...]) + b2_ref[0, 0]

            @pl.when(g >= 1)
            def _():
                scatter_wait(b)

            _rows_to_units(obuf.at[b], y)

        g_end = g_first + nt

        def tile_pair(m, carry):
            for b in range(2):
                g = 2 * m + b

                @pl.when(jnp.logical_and(g >= g_first, g < g_end))
                def _():
                    tile(g, b)

            return carry

        lax.fori_loop(g_first // 2, (g_end + 1) // 2, tile_pair, 0)

    @pl.when(e == N_EXPERTS - 1)
    def _():
        g_end = g_first + nt
        last = lax.rem(g_end + 1, 2)
        code_copy(SCATTER, g_end - 1, last).wait()
        scatter_start(last, last)
        scatter_wait(last)
        scatter_wait(1 - last)
        gather_wait(1 - last)
        code_copy(GATHER, g_end + 1, last).wait()
        code_copy(SCATTER, g_end, 1 - last).wait()


def _moe_experts(layer, n_tok, seg_start, tiles_per_e, code, x_units, w1, b1, w2, b2):
    tm = MOE_TILE
    D, F2, U = D_MODEL, 2 * D_FF, ROW_UNITS
    cw, rh = F2 // W1_SPLIT, D_FF // W2_SPLIT
    any_space = pl.BlockSpec(memory_space=pl.ANY)
    w1_specs = [pl.BlockSpec((1, 1, D, cw), (lambda c: (lambda e, sg, nt: (layer, e, 0, c)))(c))
                for c in range(W1_SPLIT)]
    w2_specs = [pl.BlockSpec((1, 1, rh, D), (lambda c: (lambda e, sg, nt: (layer, e, c, 0)))(c))
                for c in range(W2_SPLIT)]
    grid_spec = pltpu.PrefetchScalarGridSpec(
        num_scalar_prefetch=2,
        grid=(N_EXPERTS,),
        in_specs=[any_space, any_space]
        + w1_specs + [pl.BlockSpec((1, 1, 1, F2), lambda e, sg, nt: (layer, e, 0, 0))]
        + w2_specs + [pl.BlockSpec((1, 1, 1, D), lambda e, sg, nt: (layer, e, 0, 0))],
        out_specs=any_space,
        scratch_shapes=[pltpu.VMEM((D, F2), BF16), pltpu.VMEM((8, F2), F32), pltpu.VMEM((D_FF, D), BF16),
                        pltpu.VMEM((tm, D_FF), BF16),
                        pltpu.VMEM((2, tm * U, LANES), F32), pltpu.VMEM((2, tm * U, LANES), F32),
                        pltpu.SMEM((2, 2, tm), jnp.int32),
                        pltpu.SemaphoreType.DMA((2, 2)), pltpu.SemaphoreType.DMA((2,)), pltpu.SemaphoreType.DMA((2,))],
    )
    n_out_rows = TOP_K * n_tok + SPARE_TILES * tm
    return pl.pallas_call(
        functools.partial(_moe_kernel, n_tok=n_tok),
        grid_spec=grid_spec,
        out_shape=jax.ShapeDtypeStruct((n_out_rows * U, LANES), F32),
        compiler_params=_cparams(("arbitrary",)),
        name="moe_experts",
    )(seg_start, tiles_per_e, code, x_units, *([w1] * W1_SPLIT), b1.reshape(DEPTH, N_EXPERTS, 1, F2),
      *([w2] * W2_SPLIT), b2.reshape(DEPTH, N_EXPERTS, 1, D))


def _slot_code_kernel(seg_ref, idx_ref, rank_ref, unused_hbm, code_ref, sem, *, n_tok):
    i = pl.program_id(0)
    tm = idx_ref.shape[0] // TOP_K

    @pl.when(i == 0)
    def _():
        cp = pltpu.make_async_copy(unused_hbm, code_ref, sem)
        cp.start()
        cp.wait()

    base = i * tm
    for t in range(tm):
        for k in range(TOP_K):
            p = t * TOP_K + k
            code_ref[seg_ref[idx_ref[p]] + rank_ref[p]] = base + (k * n_tok + t)


def _slot_codes(idx, rank, seg_start, unused):
    n_tok = idx.shape[0]
    tm = TOKEN_TILE
    pairs = lambda a: a[:, :TOP_K].reshape(-1)
    flat = pl.BlockSpec((tm * TOP_K,), lambda i, sg: (i,), memory_space=pltpu.SMEM)
    grid_spec = pltpu.PrefetchScalarGridSpec(
        num_scalar_prefetch=1,
        grid=(n_tok // tm,),
        in_specs=[flat, flat, pl.BlockSpec(memory_space=pl.ANY)],
        out_specs=pl.BlockSpec(memory_space=pltpu.SMEM),
        scratch_shapes=[pltpu.SemaphoreType.DMA(())],
    )
    return pl.pallas_call(
        functools.partial(_slot_code_kernel, n_tok=n_tok),
        grid_spec=grid_spec,
        out_shape=jax.ShapeDtypeStruct(unused.shape, jnp.int32),
        compiler_params=_cparams(("arbitrary",)),
        name="slot_codes",
    )(seg_start, pairs(idx), pairs(rank), unused)


def _moe_layer(layer, hn_units, idx, rank, counts, w1, b1, w2, b2):
    n_tok = hn_units.shape[0] // ROW_UNITS
    assert n_tok & (n_tok - 1) == 0
    tm = MOE_TILE
    n_pairs = n_tok * TOP_K
    n_slots = ((n_pairs + N_EXPERTS * (tm - 1)) // tm + 1) * tm
    tiles_per_e = (counts + tm - 1) // tm
    seg_start = (jnp.cumsum(tiles_per_e) - tiles_per_e) * tm
    unused = n_pairs + jnp.arange(n_slots, dtype=jnp.int32) % (SPARE_TILES * tm)
    code = _slot_codes(idx, rank, seg_start, unused)
    return _moe_experts(layer, n_tok, seg_start, tiles_per_e, code, hn_units, w1, b1, w2, b2)


def _final_kernel(*refs, nb_ctx):
    x_ref = refs[0]
    y_refs = refs[1:1 + TOP_K]
    wt_ref, mod_ref, gain_ref, oa_ref, ob_ref = refs[1 + TOP_K:]
    D = D_MODEL
    i = pl.program_id(0)
    x = x_ref[...] + mod_ref[0][:, 5 * D:6 * D] * _moe_combine(y_refs, wt_ref)
    out = _rms(x, gain_ref[...])

    @pl.when(i < nb_ctx)
    def _():
        oa_ref[...] = out

    @pl.when(i >= nb_ctx)
    def _():
        ob_ref[...] = out


def _final(n_ctx, n_tok, lat_seq, x_new, y4, wts, mods, gain):
    tm = TOKEN_TILE
    nb, nb_ctx, tpl = n_tok // tm, n_ctx // tm, lat_seq // tm
    D = D_MODEL
    specs = [pl.BlockSpec((tm, D), lambda i: (i, 0))]
    specs += [pl.BlockSpec((tm * ROW_UNITS, LANES), (lambda k: (lambda i: (k * nb + i, 0)))(k)) for k in range(TOP_K)]
    specs += [pl.BlockSpec((tm, LANES), lambda i: (i, 0)),
              pl.BlockSpec((1, 1, 6 * D), lambda i: ((DEPTH - 1) * 8 + _cond_row(i, nb_ctx, tpl), 0, 0)),
              _resident((1, D))]
    return pl.pallas_call(
        functools.partial(_final_kernel, nb_ctx=nb_ctx),
        grid=(nb,),
        in_specs=specs,
        out_specs=[pl.BlockSpec((tm, D), lambda i: (jnp.minimum(i, nb_ctx - 1), 0)),
                   pl.BlockSpec((tm, D), lambda i: (jnp.maximum(i - nb_ctx, 0), 0))],
        out_shape=[jax.ShapeDtypeStruct((n_ctx, D), F32), jax.ShapeDtypeStruct((n_tok - n_ctx, D), F32)],
        compiler_params=_cparams(("arbitrary",)),
        name="final_norm",
    )(x_new, *([y4] * TOP_K), wts, mods, gain.reshape(1, D))


def _rope_tables(n_tokens):
    rows = n_tokens // GRID_W
    r = jnp.repeat(jnp.arange(rows, dtype=F32), GRID_W)
    col = jnp.tile(jnp.arange(GRID_W, dtype=F32), rows)
    nf = RET_DK // 4
    inv = ROPE_BASE ** (-jnp.arange(nf, dtype=F32) / nf)
    ang = jnp.concatenate([r[:, None] * inv, col[:, None] * inv], axis=-1)
    cos, sin = jnp.cos(ang), jnp.sin(ang)
    return jnp.concatenate([cos, cos], axis=-1), jnp.concatenate([-sin, sin], axis=-1)


def kernel(x_prompt, x_sample, c, state_gla, state_ret, state_hgrn, c_ctx, norm_mix, norm_ffn, ada_w, ada_b,
           w_in_even, w_out_even, gla_gk_w, gla_gk_b, gla_gain, ret_decay_exp, w_in_odd, w_out_odd,
           hgrn_lb_logits, hgrn_gain, router_w, router_b, moe_w1, moe_b1, moe_w2, moe_b2, final_norm):
    D = D_MODEL
    B_ctx, T_ctx, _ = x_prompt.shape
    B_lat, T_lat, _ = x_sample.shape
    n_ctx, n_lat = B_ctx * T_ctx, B_lat * T_lat
    n_tok = n_ctx + n_lat
    assert n_ctx % BIG_TOKEN_TILE == 0 and T_lat % BIG_TOKEN_TILE == 0 and B_lat + 1 <= 8
    assert T_ctx % SCAN_CHUNK == 0 and T_lat % SCAN_CHUNK == 0 and n_ctx % T_lat == 0

    cond8 = jnp.concatenate([c_ctx[None, :], c, jnp.zeros((8 - 1 - B_lat, D), F32)], axis=0)
    mods = _ada_mods(cond8, ada_w, ada_b).reshape(DEPTH * 8, 1, 6 * D)

    rw_pad = jnp.pad(router_w, ((0, 0), (0, 0), (0, LANES - N_EXPERTS)))
    rb_pad = jnp.pad(router_b, ((0, 0), (0, LANES - N_EXPERTS)), constant_values=ROUTER_PAD_BIAS)

    w_even = w_in_even[0]
    lr0 = GLA_HEADS * (2 * GLA_DK + GLA_DV)
    lr1 = lr0 + 2 * GLA_LOW_RANK
    w_main = jnp.concatenate([w_even[:, :lr0], w_even[:, lr1:]], axis=1)
    w_lr = jnp.pad(w_even[:, lr0:lr1], ((0, 0), (0, LANES - 2 * GLA_LOW_RANK)))
    nqk = GLA_HEADS * GLA_DK
    gkw = jnp.zeros((LANES, 2 * nqk), F32)
    gkw = gkw.at[0:GLA_LOW_RANK, 0:nqk].set(gla_gk_w[0, 0])
    gkw = gkw.at[GLA_LOW_RANK:2 * GLA_LOW_RANK, nqk:].set(gla_gk_w[0, 1])
    gkb = gla_gk_b[0].reshape(1, 2 * nqk)
    proj, gdec, x_cur = _in_proj(0, n_ctx, n_tok, T_lat, mods, norm_mix[0], w_main,
                                 x_parts=(x_prompt.reshape(n_ctx, D), x_sample.reshape(n_lat, D)),
                                 even_extra=(w_lr, gkw, gkb))

    y_gla_c, fin_gla = _gla_scan(proj, gdec, gla_gain[0], None, 0, B_ctx, T_ctx, True)
    (y_gla_l,) = _gla_scan(proj, gdec, gla_gain[0], state_gla[:, 0], n_ctx, B_lat, T_lat, False)
    dexp = jnp.broadcast_to(ret_decay_exp[0].T[:, :, None], (RET_HEADS, 2, LANES))
    y_ret_c, fin_ret = _ret_scan(proj, dexp, None, None, 0, B_ctx, T_ctx, True)
    (y_ret_l,) = _ret_scan(proj, dexp, _rope_tables(T_lat), state_ret[:, 0], n_ctx, B_lat, T_lat, False)

    x_new, hn, idx, wts, rank, cnt = _out_proj(0, n_ctx, n_tok, T_lat, [(y_gla_c, y_gla_l), (y_ret_c, y_ret_l)],
                                               w_out_even[0], x_cur, mods, norm_ffn[0], rw_pad[0], rb_pad[0:1])
    y4 = _moe_layer(0, hn, idx, rank, cnt[0, :N_EXPERTS], moe_w1, moe_b1, moe_w2, moe_b2)

    (proj, x_cur) = _in_proj(1, n_ctx, n_tok, T_lat, mods, norm_mix[1], w_in_odd[0], x_prev=x_new, y4=(y4, wts))
    y_h_c, fin_h = _hgrn_scan(proj, hgrn_lb_logits, hgrn_gain[0], None, 0, B_ctx, T_ctx, True)
    (y_h_l,) = _hgrn_scan(proj, hgrn_lb_logits, hgrn_gain[0], state_hgrn[:, 0], n_ctx, B_lat, T_lat, False)
    x_new, hn, idx, wts, rank, cnt = _out_proj(1, n_ctx, n_tok, T_lat, [(y_h_c, y_h_l)], w_out_odd[0], x_cur, mods,
                                               norm_ffn[1], rw_pad[1], rb_pad[1:2])
    y4 = _moe_layer(1, hn, idx, rank, cnt[0, :N_EXPERTS], moe_w1, moe_b1, moe_w2, moe_b2)

    y_ctx, y_lat = _final(n_ctx, n_tok, T_lat, x_new, y4, wts, mods, final_norm)

    new_state_gla = fin_gla.reshape(B_ctx, 1, 2, GLA_HEADS, GLA_DK, GLA_DV)
    new_state_ret = fin_ret.reshape(B_ctx, 1, 2, RET_HEADS, RET_DK, RET_DV)
    new_state_hgrn = fin_h.reshape(B_ctx, 1, 2, HG_HEADS, HG_DK, HG_DV)
    return (y_ctx.reshape(B_ctx, T_ctx, D), y_lat.reshape(B_lat, T_lat, D), new_state_gla, new_state_ret,
            new_state_hgrn)
```

```python
import functools

import jax
import jax.numpy as jnp
from jax import lax
from jax.experimental import pallas as pl
from jax.experimental.pallas import tpu as pltpu

F32 = jnp.float32
BF16 = jnp.bfloat16

D_MODEL = 1024
DEPTH = 2
GRID_W = 64
GLA_HEADS, GLA_DK, GLA_DV, GLA_LOW_RANK = 4, 64, 128, 16
GLA_NORMALIZER = 16.0
RET_HEADS, RET_DK, RET_DV = 4, 128, 128
ROPE_BASE = 10000.0
HG_HEADS, HG_DK, HG_DV = 8, 128, 128
N_EXPERTS, TOP_K, D_FF = 32, 4, 1024
SWIGLU_ALPHA, SWIGLU_LIMIT = 1.702, 7.0
EPS = 1e-6

LANES = 128
SCAN_CHUNK = 128
SCAN_ROWS = 2048
TOKEN_TILE = 256
BIG_TOKEN_TILE = 512
OUT_PROJ_TILE = 1024
MOE_TILE = 384
VMEM_LIMIT = 56 * 1024 * 1024

_GQ, _GK, _GV, _GG, _RQ, _RK, _RV, _RG, _EVEN_MAIN = 0, 256, 512, 1024, 1536, 2048, 2560, 3072, 3584
_HQ, _HFF, _HFB, _HI, _HG, _ODD_MAIN = 0, 1024, 2048, 3072, 4096, 5120


def _dot(a, b):
    return jnp.dot(a, b, preferred_element_type=F32)


def _dot_nt(a, b):
    return lax.dot_general(a, b, (((1,), (1,)), ((), ())), preferred_element_type=F32)


def _rms(x, gain=None):
    y = x * lax.rsqrt(jnp.mean(x * x, axis=-1, keepdims=True) + EPS)
    if gain is not None:
        y = y * gain
    return y


def _sigmoid(x):
    return 0.5 * jnp.tanh(0.5 * x) + 0.5


def _silu(x):
    return x * _sigmoid(x)


def _cparams(sem, vmem=VMEM_LIMIT):
    return pltpu.CompilerParams(dimension_semantics=sem, vmem_limit_bytes=vmem)


def _resident(shape):
    nd = len(shape)
    return pl.BlockSpec(shape, lambda *_: (0,) * nd, pipeline_mode=pl.Buffered(1))


def _ada_kernel(c_ref, w_ref, b_ref, o_ref):
    o_ref[0] = _dot(_silu(c_ref[...]), w_ref[0]) + b_ref[0]


def _ada_mods(cond8, ada_w, ada_b):
    tn = 1536
    return pl.pallas_call(
        _ada_kernel,
        grid=(DEPTH, 6 * D_MODEL // tn),
        in_specs=[
            pl.BlockSpec((8, D_MODEL), lambda l, j: (0, 0)),
            pl.BlockSpec((1, D_MODEL, tn), lambda l, j: (l, 0, j)),
            pl.BlockSpec((1, 1, tn), lambda l, j: (l, 0, j)),
        ],
        out_specs=pl.BlockSpec((1, 8, tn), lambda l, j: (l, 0, j)),
        out_shape=jax.ShapeDtypeStruct((DEPTH, 8, 6 * D_MODEL), F32),
        compiler_params=_cparams(("arbitrary", "arbitrary")),
        name="ada_mods",
    )(cond8, ada_w, ada_b.reshape(DEPTH, 1, 6 * D_MODEL))


ROW_UNITS = D_MODEL // LANES


def _rows_from_units(ref, n_rows):
    return jnp.concatenate([ref[pl.ds(c, n_rows, stride=ROW_UNITS), :] for c in range(ROW_UNITS)], axis=1)


def _rows_to_units(ref, val):
    n_rows = val.shape[0]
    for c in range(ROW_UNITS):
        ref[pl.ds(c, n_rows, stride=ROW_UNITS), :] = val[:, c * LANES:(c + 1) * LANES]


def _moe_combine(y_refs, wt_ref):
    w = wt_ref[...]
    terms = [w[:, k:k + 1] * _rows_from_units(y_refs[k], w.shape[0]) for k in range(TOP_K)]
    return (terms[0] + terms[1]) + (terms[2] + terms[3])


def _in_proj_kernel(*refs, first, even, nb_ctx):
    it = iter(refs)
    if first:
        xa_ref, xb_ref = next(it), next(it)
    else:
        xp_ref = next(it)
        y_refs = [next(it) for _ in range(TOP_K)]
        wt_ref, modp_ref = next(it), next(it)
    gain_ref, mod_ref, w_ref = next(it), next(it), next(it)
    if even:
        wlr_ref, gkw_ref, gkb_ref = next(it), next(it), next(it)
    proj_ref = next(it)
    if even:
        g_ref = next(it)
    xcur_ref = next(it)

    D = D_MODEL
    i = pl.program_id(0)
    if first:
        x = jnp.where(i < nb_ctx, xa_ref[...], xb_ref[...])
    else:
        x = xp_ref[...] + modp_ref[0][:, 5 * D:6 * D] * _moe_combine(y_refs, wt_ref)
    xcur_ref[...] = x
    m = mod_ref[0]
    hn = _rms(x, gain_ref[...]) * (1.0 + m[:, D:2 * D]) + m[:, 0:D]
    proj_ref[...] = _dot(hn, w_ref[...])
    if even:
        z = _dot(_dot(hn, wlr_ref[...]), gkw_ref[...]) + gkb_ref[...]
        g_ref[...] = (jnp.minimum(z, 0.0) - jnp.log(1.0 + jnp.exp(-jnp.abs(z)))) * (1.0 / GLA_NORMALIZER)


def _cond_row(i, nb_ctx, tiles_per_lat_seq):
    return jnp.where(i < nb_ctx, 0, 1 + (i - nb_ctx) // tiles_per_lat_seq)


def _in_proj(layer, n_ctx, n_tok, lat_seq, mods, gain, w_main, *, x_parts=None, x_prev=None, y4=None, even_extra=None):
    first = x_parts is not None
    even = even_extra is not None
    tm = BIG_TOKEN_TILE if first else TOKEN_TILE
    nb = n_tok // tm
    nb_ctx = n_ctx // tm
    tpl = lat_seq // tm
    np_cols = w_main.shape[1]
    D = D_MODEL

    def mod_map(l):
        return lambda i: (l * 8 + _cond_row(i, nb_ctx, tpl), 0, 0)

    row = lambda i: (i, 0)
    args, specs = [], []
    if first:
        xa, xb = x_parts
        args += [xa, xb]
        specs += [pl.BlockSpec((tm, D), lambda i: (jnp.minimum(i, nb_ctx - 1), 0)),
                  pl.BlockSpec((tm, D), lambda i: (jnp.maximum(i - nb_ctx, 0), 0))]
    else:
        y4, wts = y4
        args += [x_prev] + [y4] * TOP_K + [wts, mods]
        specs += [pl.BlockSpec((tm, D), row)]
        specs += [pl.BlockSpec((tm * ROW_UNITS, LANES), (lambda k: (lambda i: (k * nb + i, 0)))(k))
                  for k in range(TOP_K)]
        specs += [pl.BlockSpec((tm, LANES), row), pl.BlockSpec((1, 1, 6 * D), mod_map(layer - 1))]
    args += [gain.reshape(1, D), mods, w_main]
    specs += [_resident((1, D)), pl.BlockSpec((1, 1, 6 * D), mod_map(layer)), _resident((D, np_cols))]
    out_shapes = [jax.ShapeDtypeStruct((n_tok, np_cols), F32)]
    out_specs = [pl.BlockSpec((tm, np_cols), row)]
    if even:
        w_lr, gkw, gkb = even_extra
        args += [w_lr, gkw, gkb]
        specs += [_resident(w_lr.shape), _resident(gkw.shape), _resident(gkb.shape)]
        out_shapes.append(jax.ShapeDtypeStruct((n_tok, 2 * GLA_HEADS * GLA_DK), F32))
        out_specs.append(pl.BlockSpec((tm, 2 * GLA_HEADS * GLA_DK), row))
    out_shapes.append(jax.ShapeDtypeStruct((n_tok, D), F32))
    out_specs.append(pl.BlockSpec((tm, D), row))
    return pl.pallas_call(
        functools.partial(_in_proj_kernel, first=first, even=even, nb_ctx=nb_ctx),
        grid=(nb,),
        in_specs=specs,
        out_specs=out_specs,
        out_shape=out_shapes,
        compiler_params=_cparams(("arbitrary",)),
        name=f"in_proj_l{layer}",
    )(*args)


def _tri(c, lower):
    r = lax.broadcasted_iota(jnp.int32, (c, c), 0)
    s = lax.broadcasted_iota(jnp.int32, (c, c), 1)
    return (r >= s) if lower else (r <= s)


def _cumsum_mm(tri_bf16, g):
    g_hi = g.astype(BF16)
    g_lo = (g - g_hi.astype(F32)).astype(BF16)
    r = _dot(tri_bf16, jnp.concatenate([g_hi, g_lo], axis=1))
    w = g.shape[1]
    return r[:, :w] + r[:, w:]


def _gated_scan_kernel(*refs, variant, T, nseq, has_s0, want_final, hpb):
    C = SCAN_CHUNK
    ncs = T // C
    nc = nseq * ncs
    it = iter(refs)
    if variant == "gla":
        q_ref, k_ref, v_ref, og_ref, gf_ref, gb_ref, gain_ref = (next(it) for _ in range(7))
    else:
        q_ref, ff_ref, fb_ref, v_ref, og_ref, lbl_ref, gain_ref = (next(it) for _ in range(7))
    s0_ref = next(it) if has_s0 else None
    y_ref = next(it)
    sfin_ref = next(it) if want_final else None
    qi_scr, o_scr, u_scr, dec_scr, sin_scr = (next(it) for _ in range(5))

    lane = lax.broadcasted_iota(jnp.int32, (1, LANES), 1)
    if hpb == 2:
        masks = [lane < GLA_DK, lane >= GLA_DK]
    else:
        masks = [None]
    tri_l, tri_u = _tri(C, True), _tri(C, False)
    tri_l16, tri_u16 = tri_l.astype(F32).astype(BF16), tri_u.astype(F32).astype(BF16)
    H = C // 2
    upper_rows = lax.broadcasted_iota(jnp.int32, (C, 1), 0) < H

    if variant == "hgrn":
        lgs = [lbl_ref[l] for l in range(DEPTH)]
        mx = functools.reduce(jnp.maximum, lgs)
        es = [jnp.exp(l - mx) for l in lgs]
        tot = functools.reduce(lambda a, b: a + b, es)
        ps = [e / tot for e in es]
        layer = DEPTH - 1
        lb = functools.reduce(lambda a, b: a + b, ps[:layer + 1]) - ps[0]
        lb_f, lb_b = lb[0:1], lb[1:2]

    for n in range(nc):
        rows = pl.ds(n * C, C)
        if variant == "gla":
            q = q_ref[rows, :] * (GLA_DK ** -0.5)
            kf = kb = k_ref[rows, :]
            gf, gb = gf_ref[rows, :], gb_ref[rows, :]
        else:
            q = _silu(q_ref[rows, :])
            f_f = lb_f + (1.0 - lb_f) * _sigmoid(ff_ref[rows, :])
            f_b = lb_b + (1.0 - lb_b) * _sigmoid(fb_ref[rows, :])
            kf, kb = 1.0 - f_f, 1.0 - f_b
            gf, gb = jnp.log(f_f), jnp.log(f_b)
        bf = _cumsum_mm(tri_l16, gf)
        bb = _cumsum_mm(tri_u16, gb)
        bf_end, bb_end = bf[C - 1:C], bb[0:1]
        ref_f = jnp.where(upper_rows, bf[H // 2 - 1:H // 2], bf[H + H // 2 - 1:H + H // 2])
        ref_b = jnp.where(upper_rows, bb[H // 2:H // 2 + 1], bb[H + H // 2:H + H // 2 + 1])
        qd_f, qd_b = q * jnp.exp(bf - ref_f), q * jnp.exp(bb - ref_b)
        kd_f = [jnp.where(upper_rows, kf * jnp.exp(bf[H // 2 - 1:H // 2] - bf), 0.0),
                kf * jnp.exp(bf[H + H // 2 - 1:H + H // 2] - bf)]
        kd_b = [kb * jnp.exp(bb[H // 2:H // 2 + 1] - bb),
                jnp.where(upper_rows, 0.0, kb * jnp.exp(bb[H + H // 2:H + H // 2 + 1] - bb))]
        kend = jnp.concatenate([kf * jnp.exp(bf_end - bf), kb * jnp.exp(bb_end - bb)], axis=1)
        qi_scr[rows, :] = jnp.concatenate([q * jnp.exp(bf), q * jnp.exp(bb)], axis=1)
        dec_scr[n] = jnp.broadcast_to(jnp.concatenate([jnp.exp(bf_end), jnp.exp(bb_end)], axis=1), (8, 2 * LANES))
        for h in range(hpb):
            v = v_ref[rows, h * LANES:(h + 1) * LANES]
            if masks[h] is None:
                qf_h, qb_h, kend_h = qd_f, qd_b, kend
            else:
                qf_h, qb_h = jnp.where(masks[h], qd_f, 0.0), jnp.where(masks[h], qd_b, 0.0)
                kend_h = jnp.where(jnp.concatenate([masks[h], masks[h]], axis=1), kend, 0.0)
            s_f = jnp.concatenate([_dot_nt(qf_h[:H], kd_f[0]), _dot_nt(qf_h[H:], kd_f[1])], axis=0)
            s_b = jnp.concatenate([_dot_nt(qb_h[:H], kd_b[0]), _dot_nt(qb_h[H:], kd_b[1])], axis=0)
            s = jnp.where(tri_l, s_f, 0.0) + jnp.where(tri_u, s_b, 0.0)
            o_scr[rows, h * LANES:(h + 1) * LANES] = _dot(s, v)
            u_scr[h, n] = _dot(v.T, kend_h)

    for sq in range(nseq):
        for h in range(hpb):
            if has_s0:
                s_f, s_b = s0_ref[sq, 0, 0].T, s0_ref[sq, 1, 0].T
                if masks[h] is not None:
                    s_f, s_b = jnp.where(masks[h], s_f, 0.0), jnp.where(masks[h], s_b, 0.0)
            else:
                s_f = s_b = jnp.zeros((LANES, LANES), F32)
            for n in range(sq * ncs, (sq + 1) * ncs):
                sin_scr[h, n, :, 0:LANES] = s_f
                s_f = s_f * dec_scr[n, 0:1, 0:LANES] + u_scr[h, n, :, 0:LANES]
            for n in reversed(range(sq * ncs, (sq + 1) * ncs)):
                sin_scr[h, n, :, LANES:2 * LANES] = s_b
                s_b = s_b * dec_scr[n, 0:1, LANES:2 * LANES] + u_scr[h, n, :, LANES:2 * LANES]
            if want_final:
                if h == 0:
                    fin_f, fin_b = s_f, s_b
                else:
                    fin_f, fin_b = fin_f + s_f, fin_b + s_b
        if want_final:
            sfin_ref[sq, 0, 0] = fin_f.T
            sfin_ref[sq, 1, 0] = fin_b.T

    gain = gain_ref[...]
    for n in range(nc):
        rows = pl.ds(n * C, C)
        qi = qi_scr[rows, :]
        for h in range(hpb):
            cols = slice(h * LANES, (h + 1) * LANES)
            o = o_scr[rows, cols] + _dot_nt(qi, sin_scr[h, n])
            y_ref[rows, cols] = (_rms(o, gain) * _silu(og_ref[rows, cols])).astype(y_ref.dtype)


def _scan_scratch(T, hpb):
    nc = T // SCAN_CHUNK
    return [
        pltpu.VMEM((T, 2 * LANES), F32),
        pltpu.VMEM((T, hpb * LANES), F32),
        pltpu.VMEM((hpb, nc, LANES, 2 * LANES), F32),
        pltpu.VMEM((nc, 8, 2 * LANES), F32),
        pltpu.VMEM((hpb, nc, LANES, 2 * LANES), F32),
    ]


def _seqs_per_step(B, T):
    nseq = max(1, min(B, SCAN_ROWS // T))
    assert B % nseq == 0
    return nseq


def _gla_scan(proj, gdec, gain, s0, row0, B, T, want_final):
    nseq = _seqs_per_step(B, T)
    R = nseq * T
    rb0 = row0 // R
    assert row0 % R == 0
    has_s0 = s0 is not None

    def col(base, width=LANES):
        return lambda b, j: (rb0 + b, base // width + j)

    args = [proj, proj, proj, proj, gdec, gdec, gain.reshape(1, GLA_DV)]
    specs = [pl.BlockSpec((R, LANES), col(_GQ)), pl.BlockSpec((R, LANES), col(_GK)),
             pl.BlockSpec((R, 2 * LANES), col(_GV, 2 * LANES)), pl.BlockSpec((R, 2 * LANES), col(_GG, 2 * LANES)),
             pl.BlockSpec((R, LANES), col(0)), pl.BlockSpec((R, LANES), col(GLA_HEADS * GLA_DK)),
             pl.BlockSpec((1, GLA_DV), lambda b, j: (0, 0))]
    st_spec = pl.BlockSpec((nseq, 2, 1, LANES, LANES), lambda b, j: (b, 0, j, 0, 0))
    if has_s0:
        args.append(s0.reshape(B, 2, GLA_HEADS // 2, 2 * GLA_DK, GLA_DV))
        specs.append(st_spec)
    out_shapes = [jax.ShapeDtypeStruct((B * T, GLA_HEADS * GLA_DV), BF16)]
    out_specs = [pl.BlockSpec((R, 2 * LANES), lambda b, j: (b, j))]
    if want_final:
        out_shapes.append(jax.ShapeDtypeStruct((B, 2, GLA_HEADS // 2, 2 * GLA_DK, GLA_DV), F32))
        out_specs.append(st_spec)
    kern = functools.partial(_gated_scan_kernel, variant="gla", T=T, nseq=nseq, has_s0=has_s0,
                             want_final=want_final, hpb=2)
    return pl.pallas_call(
        kern, grid=(B // nseq, GLA_HEADS // 2), in_specs=specs, out_specs=out_specs, out_shape=out_shapes,
        scratch_shapes=_scan_scratch(R, 2),
        compiler_params=_cparams(("arbitrary", "arbitrary")), name=f"gla_scan_T{T}",
    )(*args)


def _hgrn_scan(proj, lb_logits, gain, s0, row0, B, T, want_final):
    nseq = _seqs_per_step(B, T)
    R = nseq * T
    rb0 = row0 // R
    assert row0 % R == 0
    has_s0 = s0 is not None

    def col(base):
        return lambda b, j: (rb0 + b, base // LANES + j)

    args = [proj, proj, proj, proj, proj, lb_logits, gain.reshape(1, HG_DV)]
    specs = [pl.BlockSpec((R, LANES), col(_HQ)), pl.BlockSpec((R, LANES), col(_HFF)),
             pl.BlockSpec((R, LANES), col(_HFB)), pl.BlockSpec((R, LANES), col(_HI)),
             pl.BlockSpec((R, LANES), col(_HG)),
             pl.BlockSpec((DEPTH, 2, LANES), lambda b, j: (0, 0, j)),
             pl.BlockSpec((1, HG_DV), lambda b, j: (0, 0))]
    st_spec = pl.BlockSpec((nseq, 2, 1, LANES, LANES), lambda b, j: (b, 0, j, 0, 0))
    if has_s0:
        args.append(s0.reshape(B, 2, HG_HEADS, HG_DK, HG_DV))
        specs.append(st_spec)
    out_shapes = [jax.ShapeDtypeStruct((B * T, HG_HEADS * HG_DV), BF16)]
    out_specs = [pl.BlockSpec((R, LANES), lambda b, j: (b, j))]
    if want_final:
        out_shapes.append(jax.ShapeDtypeStruct((B, 2, HG_HEADS, HG_DK, HG_DV), F32))
        out_specs.append(st_spec)
    kern = functools.partial(_gated_scan_kernel, variant="hgrn", T=T, nseq=nseq, has_s0=has_s0,
                             want_final=want_final, hpb=1)
    return pl.pallas_call(
        kern, grid=(B // nseq, HG_HEADS), in_specs=specs, out_specs=out_specs, out_shape=out_shapes,
        scratch_shapes=_scan_scratch(R, 1),
        compiler_params=_cparams(("arbitrary", "arbitrary")), name=f"hgrn_scan_T{T}",
    )(*args)


def _ret_scan_kernel(*refs, T, nseq, has_s0, want_final, rope):
    C = SCAN_CHUNK
    ncs = T // C
    nc = nseq * ncs
    it = iter(refs)
    q_ref, k_ref, v_ref, og_ref, dexp_ref = (next(it) for _ in range(5))
    if rope:
        cos_ref, sin_ref = next(it), next(it)
    s0_ref = next(it) if has_s0 else None
    y_ref = next(it)
    sfin_ref = next(it) if want_final else None
    qi_scr, o_scr, u_scr, sin_scr = (next(it) for _ in range(4))

    lg = jnp.log1p(-jnp.exp2(-dexp_ref[0]))
    lg_f, lg_b = lg[0:1], lg[1:2]
    r = lax.broadcasted_iota(jnp.int32, (C, C), 0)
    s = lax.broadcasted_iota(jnp.int32, (C, C), 1)
    dist = (r - s).astype(F32)
    dmask = (jnp.where(r >= s, jnp.exp(jnp.maximum(dist, 0.0) * lg_f[:, 0:1]), 0.0)
             + jnp.where(r <= s, jnp.exp(jnp.maximum(-dist, 0.0) * lg_b[:, 0:1]), 0.0))
    pos = lax.broadcasted_iota(jnp.int32, (C, LANES), 0).astype(F32)
    xi = jnp.concatenate([jnp.exp((pos + 1.0) * lg_f), jnp.exp((C - pos) * lg_b)], axis=1)
    zeta = jnp.concatenate([jnp.exp((C - 1.0 - pos) * lg_f), jnp.exp(pos * lg_b)], axis=1)
    d_f, d_b = jnp.exp(C * lg_f), jnp.exp(C * lg_b)

    def rot(x, seq_rows):
        if not rope:
            return x
        return x * cos_ref[seq_rows, :] + pltpu.roll(x, RET_DK // 2, axis=1) * sin_ref[seq_rows, :]

    for n in range(nc):
        rows = pl.ds(n * C, C)
        seq_rows = pl.ds((n % ncs) * C, C)
        q = rot(q_ref[rows, :], seq_rows)
        k = rot(k_ref[rows, :] * (RET_DK ** -0.5), seq_rows)
        v = v_ref[rows, :]
        o_scr[rows, :] = _dot(_dot_nt(q, k) * dmask, v)
        qi_scr[rows, :] = jnp.concatenate([q, q], axis=1) * xi
        u_scr[n] = _dot(v.T, jnp.concatenate([k, k], axis=1) * zeta)

    for sq in range(nseq):
        if has_s0:
            s_f, s_b = s0_ref[sq, 0, 0].T, s0_ref[sq, 1, 0].T
        else:
            s_f = s_b = jnp.zeros((LANES, LANES), F32)
        for n in range(sq * ncs, (sq + 1) * ncs):
            sin_scr[n, :, 0:LANES] = s_f
            s_f = s_f * d_f + u_scr[n, :, 0:LANES]
        for n in reversed(range(sq * ncs, (sq + 1) * ncs)):
            sin_scr[n, :, LANES:2 * LANES] = s_b
            s_b = s_b * d_b + u_scr[n, :, LANES:2 * LANES]
        if want_final:
            sfin_ref[sq, 0, 0] = s_f.T
            sfin_ref[sq, 1, 0] = s_b.T

    for n in range(nc):
        rows = pl.ds(n * C, C)
        o = o_scr[rows, :] + _dot_nt(qi_scr[rows, :], sin_scr[n])
        y_ref[rows, :] = (_rms(o) * _silu(og_ref[rows, :])).astype(y_ref.dtype)


def _ret_scan(proj, dexp, rope_tabs, s0, row0, B, T, want_final):
    nseq = _seqs_per_step(B, T)
    R = nseq * T
    rb0 = row0 // R
    assert row0 % R == 0
    nc = R // SCAN_CHUNK
    has_s0 = s0 is not None
    rope = rope_tabs is not None

    def col(base):
        return lambda b, j: (rb0 + b, base // LANES + j)

    args = [proj, proj, proj, proj, dexp]
    specs = [pl.BlockSpec((R, LANES), col(_RQ)), pl.BlockSpec((R, LANES), col(_RK)),
             pl.BlockSpec((R, LANES), col(_RV)), pl.BlockSpec((R, LANES), col(_RG)),
             pl.BlockSpec((1, 2, LANES), lambda b, j: (j, 0, 0))]
    if rope:
        args += list(rope_tabs)
        specs += [pl.BlockSpec((T, LANES), lambda b, j: (0, 0))] * 2
    st_spec = pl.BlockSpec((nseq, 2, 1, LANES, LANES), lambda b, j: (b, 0, j, 0, 0))
    if has_s0:
        args.append(s0.reshape(B, 2, RET_HEADS, RET_DK, RET_DV))
        specs.append(st_spec)
    out_shapes = [jax.ShapeDtypeStruct((B * T, RET_HEADS * RET_DV), BF16)]
    out_specs = [pl.BlockSpec((R, LANES), lambda b, j: (b, j))]
    if want_final:
        out_shapes.append(jax.ShapeDtypeStruct((B, 2, RET_HEADS, RET_DK, RET_DV), F32))
        out_specs.append(st_spec)
    kern = functools.partial(_ret_scan_kernel, T=T, nseq=nseq, has_s0=has_s0, want_final=want_final, rope=rope)
    scratch = [pltpu.VMEM((R, 2 * LANES), F32), pltpu.VMEM((R, LANES), F32),
               pltpu.VMEM((nc, LANES, 2 * LANES), F32), pltpu.VMEM((nc, LANES, 2 * LANES), F32)]
    return pl.pallas_call(
        kern, grid=(B // nseq, RET_HEADS), in_specs=specs, out_specs=out_specs, out_shape=out_shapes,
        scratch_shapes=scratch,
        compiler_params=_cparams(("arbitrary", "arbitrary")), name=f"ret_scan_T{T}",
    )(*args)


def _out_proj_kernel(*refs, n_mix, nb_ctx):
    it = iter(refs)
    y_refs = [(next(it), next(it)) for _ in range(n_mix)]
    wo_ref, x_ref, mod_ref, gain_ref, rw_ref, rb_ref = (next(it) for _ in range(6))
    xnew_ref, hn_ref, idx_ref, wt_ref, rank_ref, cnt_ref = (next(it) for _ in range(6))
    cnt_scr = next(it)
    D = D_MODEL
    is_ctx = pl.program_id(0) < nb_ctx
    mix = None
    r0 = 0
    for ya_ref, yb_ref in y_refs:
        w = ya_ref.shape[1]
        part = _dot(jnp.where(is_ctx, ya_ref[...], yb_ref[...]), wo_ref[r0:r0 + w, :])
        mix = part if mix is None else mix + part
        r0 += w
    m = mod_ref[0]
    xn = x_ref[...] + m[:, 2 * D:3 * D] * mix
    xnew_ref[...] = xn
    hn = _rms(xn, gain_ref[...]) * (1.0 + m[:, 4 * D:5 * D]) + m[:, 3 * D:4 * D]
    hn_hi = hn.astype(BF16)
    _rows_to_units(hn_ref, hn)
    hn_lo = (hn - hn_hi.astype(F32)).astype(BF16)
    rw = rw_ref[...]
    rw_hi = rw.astype(BF16)
    rw_lo = (rw - rw_hi.astype(F32)).astype(BF16)
    logits = (_dot(hn_hi, rw_hi) + (_dot(hn_lo, rw_hi) + _dot(hn_hi, rw_lo))) + rb_ref[...]
    lane = lax.broadcasted_iota(jnp.int32, logits.shape, 1).astype(F32)
    vals, idxs = [], []
    cur = logits
    for _ in range(TOP_K):
        mx = jnp.max(cur, axis=-1, keepdims=True)
        ik = jnp.min(jnp.where(cur == mx, lane, float(LANES)), axis=-1, keepdims=True)
        vals.append(mx)
        idxs.append(ik)
        cur = jnp.where(lane == ik, -jnp.inf, cur)
    es = [jnp.exp(v - vals[0]) for v in vals]
    tot = (es[0] + es[1]) + (es[2] + es[3])
    idx_out = jnp.zeros(logits.shape, F32)
    wt_out = jnp.zeros(logits.shape, F32)
    for k in range(TOP_K):
        idx_out = jnp.where(lane == float(k), idxs[k], idx_out)
        wt_out = jnp.where(lane == float(k), es[k] / tot, wt_out)
    idx_ref[...] = idx_out.astype(jnp.int32)
    wt_ref[...] = wt_out

    @pl.when(pl.program_id(0) == 0)
    def _():
        cnt_scr[...] = jnp.zeros(cnt_scr.shape, F32)

    tm = logits.shape[0]
    hits = [lane == idxs[k] for k in range(TOP_K)]
    sel = jnp.zeros(logits.shape, F32)
    for k in range(TOP_K):
        sel = sel + jnp.where(hits[k], 1.0, 0.0)
    rr = lax.broadcasted_iota(jnp.int32, (tm, tm), 0)
    cc = lax.broadcasted_iota(jnp.int32, (tm, tm), 1)
    before = jnp.where(rr > cc, 1.0, 0.0).astype(BF16)
    rank_all = cnt_scr[0:1, :] + _dot(before, sel.astype(BF16))
    rank_out = jnp.zeros(logits.shape, F32)
    for k in range(TOP_K):
        rk = jnp.sum(jnp.where(hits[k], rank_all, 0.0), axis=-1, keepdims=True)
        rank_out = jnp.where(lane == float(k), rk, rank_out)
    rank_ref[...] = rank_out.astype(jnp.int32)
    total = cnt_scr[...] + jnp.sum(sel, axis=0, keepdims=True)
    cnt_scr[...] = total
    cnt_ref[...] = total.astype(jnp.int32)


def _out_proj(layer, n_ctx, n_tok, lat_seq, ys, w_out, x_cur, mods, gain, rw_pad, rb_pad):
    tm = OUT_PROJ_TILE
    nb, nb_ctx, tpl = n_tok // tm, n_ctx // tm, lat_seq // tm
    D = D_MODEL
    row = lambda i: (i, 0)
    specs = []
    for ya, _ in ys:
        specs += [pl.BlockSpec((tm, ya.shape[1]), lambda i: (jnp.minimum(i, nb_ctx - 1), 0)),
                  pl.BlockSpec((tm, ya.shape[1]), lambda i: (jnp.maximum(i - nb_ctx, 0), 0))]
    specs += [_resident((D, D)), pl.BlockSpec((tm, D), row),
              pl.BlockSpec((1, 1, 6 * D), lambda i: (layer * 8 + _cond_row(i, nb_ctx, tpl), 0, 0)),
              _resident((1, D)), _resident((D, LANES)), _resident((1, LANES))]
    return pl.pallas_call(
        functools.partial(_out_proj_kernel, n_mix=len(ys), nb_ctx=nb_ctx),
        grid=(nb,),
        in_specs=specs,
        out_specs=[pl.BlockSpec((tm, D), row), pl.BlockSpec((tm * ROW_UNITS, LANES), row),
                   pl.BlockSpec((tm, LANES), row), pl.BlockSpec((tm, LANES), row),
                   pl.BlockSpec((tm, LANES), row), pl.BlockSpec((8, LANES), lambda i: (0, 0))],
        out_shape=[jax.ShapeDtypeStruct((n_tok, D), F32), jax.ShapeDtypeStruct((n_tok * ROW_UNITS, LANES), F32),
                   jax.ShapeDtypeStruct((n_tok, LANES), jnp.int32), jax.ShapeDtypeStruct((n_tok, LANES), F32),
                   jax.ShapeDtypeStruct((n_tok, LANES), jnp.int32), jax.ShapeDtypeStruct((8, LANES), jnp.int32)],
        scratch_shapes=[pltpu.VMEM((8, LANES), F32)],
        compiler_params=_cparams(("arbitrary",)),
        name=f"out_proj_l{layer}",
    )(*[y for pair in ys for y in pair], w_out, x_cur, mods, gain.reshape(1, D), rw_pad, rb_pad)


W1_SPLIT = 4
W2_SPLIT = 2
SPARE_TILES = 2
ROUTER_PAD_BIAS = -1e30


def _moe_kernel(seg_ref, nt_ref, code_hbm, x_hbm, *refs, n_tok):
    w1_refs = refs[:W1_SPLIT]
    b1_ref = refs[W1_SPLIT]
    w2_refs = refs[W1_SPLIT + 1:W1_SPLIT + 1 + W2_SPLIT]
    (b2_ref, y_hbm, w1_scr, b1_scr, w2_scr, act_scr, xbuf, obuf, code_smem, csem, gsem,
     ssem) = refs[W1_SPLIT + 1 + W2_SPLIT:]
    tm = MOE_TILE
    U = ROW_UNITS
    n_code_tiles = code_hbm.shape[0] // tm
    e = pl.program_id(0)
    nt = nt_ref[e]
    g_first = seg_ref[e] // tm

    GATHER, SCATTER = 0, 1

    def code_copy(kind, t, p):
        first = pl.multiple_of(jnp.clip(t, 0, n_code_tiles - 1) * tm, tm)
        return pltpu.make_async_copy(code_hbm.at[pl.ds(first, tm)], code_smem.at[kind, p], csem.at[kind, p])

    def gather_start(b, p):
        for r in range(tm):
            tok = jnp.bitwise_and(code_smem[GATHER, p, r], n_tok - 1)
            pltpu.make_async_copy(x_hbm.at[pl.ds(pl.multiple_of(tok * U, U), U)],
                                  xbuf.at[b, pl.ds(r * U, U)], gsem.at[b]).start()

    def gather_wait(b):
        pltpu.make_async_copy(x_hbm.at[pl.ds(0, tm * U)], xbuf.at[b], gsem.at[b]).wait()

    def scatter_start(b, p):
        for r in range(tm):
            dst = code_smem[SCATTER, p, r]
            pltpu.make_async_copy(obuf.at[b, pl.ds(r * U, U)],
                                  y_hbm.at[pl.ds(pl.multiple_of(dst * U, U), U)], ssem.at[b]).start()

    def scatter_wait(b):
        pltpu.make_async_copy(obuf.at[b], y_hbm.at[pl.ds(0, tm * U)], ssem.at[b]).wait()

    @pl.when(e == 0)
    def _():
        obuf[...] = jnp.zeros(obuf.shape, obuf.dtype)
        for h in range(SPARE_TILES):
            cp = pltpu.make_async_copy(obuf.at[0], y_hbm.at[pl.ds((TOP_K * n_tok + h * tm) * U, tm * U)], ssem.at[0])
            cp.start()
            cp.wait()
        code_copy(GATHER, 0, 0).start()
        code_copy(GATHER, 1, 1).start()
        code_copy(SCATTER, n_code_tiles - 1, 1).start()
        code_copy(SCATTER, 0, 0).start()
        code_copy(GATHER, 0, 0).wait()
        gather_start(0, 0)

    @pl.when(nt > 0)
    def _():
        grp = 2 * LANES
        rr = lax.broadcasted_iota(jnp.int32, (grp, grp), 0)
        cc = lax.broadcasted_iota(jnp.int32, (grp, grp), 1)
        src = jnp.where(cc < LANES, 2 * cc, 2 * (cc - LANES) + 1)
        perm = jnp.where(rr == src, 1.0, 0.0).astype(BF16)
        cw = 2 * D_FF // W1_SPLIT
        for c, w_ref in enumerate(w1_refs):
            for j in range(cw // grp):
                cols = slice(j * grp, (j + 1) * grp)
                w1_scr[:, c * cw + j * grp:c * cw + (j + 1) * grp] = _dot(w_ref[0, 0, :, cols].astype(BF16),
                                                                         perm).astype(BF16)
        bias = jnp.broadcast_to(b1_ref[0, 0], (8, 2 * D_FF))
        for j in range(2 * D_FF // grp):
            cols = slice(j * grp, (j + 1) * grp)
            rest = bias[:, cols]
            acc = jnp.zeros((8, grp), F32)
            for _ in range(3):
                term = rest.astype(BF16)
                acc = acc + _dot(term, perm)
                rest = rest - term.astype(F32)
            b1_scr[:, cols] = acc
        rh = D_FF // W2_SPLIT
        for c, w_ref in enumerate(w2_refs):
            w2_scr[c * rh:(c + 1) * rh, :] = w_ref[0, 0].astype(BF16)
        blk = 4 * LANES

        def tile(g, b):
            o = 1 - b

            @pl.when(g >= 0)
            def _():
                code_copy(GATHER, g + 1, o).wait()
                code_copy(SCATTER, g - 1, o).wait()
                gather_start(o, o)
                scatter_start(o, o)
                code_copy(GATHER, g + 2, b).start()
                code_copy(SCATTER, g + 1, o).start()

            gather_wait(b)
            x = _rows_from_units(xbuf.at[b], tm).astype(BF16)
            for c in range(2 * D_FF // blk):
                cols = slice(c * blk, (c + 1) * blk)
                hid = _dot(x, w1_scr[:, cols]) + b1_scr[0:1, cols]
                acts = []
                for j in range(blk // grp):
                    glu = jnp.minimum(hid[:, j * grp:j * grp + LANES], SWIGLU_LIMIT)
                    lin = jnp.clip(hid[:, j * grp + LANES:(j + 1) * grp], -SWIGLU_LIMIT, SWIGLU_LIMIT)
                    acts.append(glu * _sigmoid(SWIGLU_ALPHA * glu) * (lin + 1.0))
                act_scr[:, c * (blk // 2):(c + 1) * (blk // 2)] = jnp.concatenate(acts, axis=1).astype(BF16)
            y = _dot(act_scr[...], w2_scr[...]) + b2_ref[0, 0]

            @pl.when(g >= 1)
            def _():
                scatter_wait(b)

            _rows_to_units(obuf.at[b], y)

        g_end = g_first + nt

        def tile_pair(m, carry):
            for b in range(2):
                g = 2 * m + b

                @pl.when(jnp.logical_and(g >= g_first, g < g_end))
                def _():
                    tile(g, b)

            return carry

        lax.fori_loop(g_first // 2, (g_end + 1) // 2, tile_pair, 0)

    @pl.when(e == N_EXPERTS - 1)
    def _():
        g_end = g_first + nt
        last = lax.rem(g_end + 1, 2)
        code_copy(SCATTER, g_end - 1, last).wait()
        scatter_start(last, last)
        scatter_wait(last)
        scatter_wait(1 - last)
        gather_wait(1 - last)
        code_copy(GATHER, g_end + 1, last).wait()
        code_copy(SCATTER, g_end, 1 - last).wait()


def _moe_experts(layer, n_tok, seg_start, tiles_per_e, code, x_units, w1, b1, w2, b2):
    tm = MOE_TILE
    D, F2, U = D_MODEL, 2 * D_FF, ROW_UNITS
    cw, rh = F2 // W1_SPLIT, D_FF // W2_SPLIT
    any_space = pl.BlockSpec(memory_space=pl.ANY)
    w1_specs = [pl.BlockSpec((1, 1, D, cw), (lambda c: (lambda e, sg, nt: (layer, e, 0, c)))(c))
                for c in range(W1_SPLIT)]
    w2_specs = [pl.BlockSpec((1, 1, rh, D), (lambda c: (lambda e, sg, nt: (layer, e, c, 0)))(c))
                for c in range(W2_SPLIT)]
    grid_spec = pltpu.PrefetchScalarGridSpec(
        num_scalar_prefetch=2,
        grid=(N_EXPERTS,),
        in_specs=[any_space, any_space]
        + w1_specs + [pl.BlockSpec((1, 1, 1, F2), lambda e, sg, nt: (layer, e, 0, 0))]
        + w2_specs + [pl.BlockSpec((1, 1, 1, D), lambda e, sg, nt: (layer, e, 0, 0))],
        out_specs=any_space,
        scratch_shapes=[pltpu.VMEM((D, F2), BF16), pltpu.VMEM((8, F2), F32), pltpu.VMEM((D_FF, D), BF16),
                        pltpu.VMEM((tm, D_FF), BF16),
                        pltpu.VMEM((2, tm * U, LANES), F32), pltpu.VMEM((2, tm * U, LANES), F32),
                        pltpu.SMEM((2, 2, tm), jnp.int32),
                        pltpu.SemaphoreType.DMA((2, 2)), pltpu.SemaphoreType.DMA((2,)), pltpu.SemaphoreType.DMA((2,))],
    )
    n_out_rows = TOP_K * n_tok + SPARE_TILES * tm
    return pl.pallas_call(
        functools.partial(_moe_kernel, n_tok=n_tok),
        grid_spec=grid_spec,
        out_shape=jax.ShapeDtypeStruct((n_out_rows * U, LANES), F32),
        compiler_params=_cparams(("arbitrary",)),
        name="moe_experts",
    )(seg_start, tiles_per_e, code, x_units, *([w1] * W1_SPLIT), b1.reshape(DEPTH, N_EXPERTS, 1, F2),
      *([w2] * W2_SPLIT), b2.reshape(DEPTH, N_EXPERTS, 1, D))


def _slot_code_kernel(seg_ref, idx_ref, rank_ref, unused_hbm, code_ref, sem, *, n_tok):
    i = pl.program_id(0)
    tm = idx_ref.shape[0] // TOP_K

    @pl.when(i == 0)
    def _():
        cp = pltpu.make_async_copy(unused_hbm, code_ref, sem)
        cp.start()
        cp.wait()

    base = i * tm
    for t in range(tm):
        for k in range(TOP_K):
            p = t * TOP_K + k
            code_ref[seg_ref[idx_ref[p]] + rank_ref[p]] = base + (k * n_tok + t)


def _slot_codes(idx, rank, seg_start, unused):
    n_tok = idx.shape[0]
    tm = TOKEN_TILE
    pairs = lambda a: a[:, :TOP_K].reshape(-1)
    flat = pl.BlockSpec((tm * TOP_K,), lambda i, sg: (i,), memory_space=pltpu.SMEM)
    grid_spec = pltpu.PrefetchScalarGridSpec(
        num_scalar_prefetch=1,
        grid=(n_tok // tm,),
        in_specs=[flat, flat, pl.BlockSpec(memory_space=pl.ANY)],
        out_specs=pl.BlockSpec(memory_space=pltpu.SMEM),
        scratch_shapes=[pltpu.SemaphoreType.DMA(())],
    )
    return pl.pallas_call(
        functools.partial(_slot_code_kernel, n_tok=n_tok),
        grid_spec=grid_spec,
        out_shape=jax.ShapeDtypeStruct(unused.shape, jnp.int32),
        compiler_params=_cparams(("arbitrary",)),
        name="slot_codes",
    )(seg_start, pairs(idx), pairs(rank), unused)


def _moe_layer(layer, hn_units, idx, rank, counts, w1, b1, w2, b2):
    n_tok = hn_units.shape[0] // ROW_UNITS
    assert n_tok & (n_tok - 1) == 0
    tm = MOE_TILE
    n_pairs = n_tok * TOP_K
    n_slots = ((n_pairs + N_EXPERTS * (tm - 1)) // tm + 1) * tm
    tiles_per_e = (counts + tm - 1) // tm
    seg_start = (jnp.cumsum(tiles_per_e) - tiles_per_e) * tm
    unused = n_pairs + jnp.arange(n_slots, dtype=jnp.int32) % (SPARE_TILES * tm)
    code = _slot_codes(idx, rank, seg_start, unused)
    return _moe_experts(layer, n_tok, seg_start, tiles_per_e, code, hn_units, w1, b1, w2, b2)


def _final_kernel(*refs, nb_ctx):
    x_ref = refs[0]
    y_refs = refs[1:1 + TOP_K]
    wt_ref, mod_ref, gain_ref, oa_ref, ob_ref = refs[1 + TOP_K:]
    D = D_MODEL
    i = pl.program_id(0)
    x = x_ref[...] + mod_ref[0][:, 5 * D:6 * D] * _moe_combine(y_refs, wt_ref)
    out = _rms(x, gain_ref[...])

    @pl.when(i < nb_ctx)
    def _():
        oa_ref[...] = out

    @pl.when(i >= nb_ctx)
    def _():
        ob_ref[...] = out


def _final(n_ctx, n_tok, lat_seq, x_new, y4, wts, mods, gain):
    tm = TOKEN_TILE
    nb, nb_ctx, tpl = n_tok // tm, n_ctx // tm, lat_seq // tm
    D = D_MODEL
    specs = [pl.BlockSpec((tm, D), lambda i: (i, 0))]
    specs += [pl.BlockSpec((tm * ROW_UNITS, LANES), (lambda k: (lambda i: (k * nb + i, 0)))(k)) for k in range(TOP_K)]
    specs += [pl.BlockSpec((tm, LANES), lambda i: (i, 0)),
              pl.BlockSpec((1, 1, 6 * D), lambda i: ((DEPTH - 1) * 8 + _cond_row(i, nb_ctx, tpl), 0, 0)),
              _resident((1, D))]
    return pl.pallas_call(
        functools.partial(_final_kernel, nb_ctx=nb_ctx),
        grid=(nb,),
        in_specs=specs,
        out_specs=[pl.BlockSpec((tm, D), lambda i: (jnp.minimum(i, nb_ctx - 1), 0)),
                   pl.BlockSpec((tm, D), lambda i: (jnp.maximum(i - nb_ctx, 0), 0))],
        out_shape=[jax.ShapeDtypeStruct((n_ctx, D), F32), jax.ShapeDtypeStruct((n_tok - n_ctx, D), F32)],
        compiler_params=_cparams(("arbitrary",)),
        name="final_norm",
    )(x_new, *([y4] * TOP_K), wts, mods, gain.reshape(1, D))


def _rope_tables(n_tokens):
    rows = n_tokens // GRID_W
    r = jnp.repeat(jnp.arange(rows, dtype=F32), GRID_W)
    col = jnp.tile(jnp.arange(GRID_W, dtype=F32), rows)
    nf = RET_DK // 4
    inv = ROPE_BASE ** (-jnp.arange(nf, dtype=F32) / nf)
    ang = jnp.concatenate([r[:, None] * inv, col[:, None] * inv], axis=-1)
    cos, sin = jnp.cos(ang), jnp.sin(ang)
    return jnp.concatenate([cos, cos], axis=-1), jnp.concatenate([-sin, sin], axis=-1)


def kernel(x_prompt, x_sample, c, state_gla, state_ret, state_hgrn, c_ctx, norm_mix, norm_ffn, ada_w, ada_b,
           w_in_even, w_out_even, gla_gk_w, gla_gk_b, gla_gain, ret_decay_exp, w_in_odd, w_out_odd,
           hgrn_lb_logits, hgrn_gain, router_w, router_b, moe_w1, moe_b1, moe_w2, moe_b2, final_norm):
    D = D_MODEL
    B_ctx, T_ctx, _ = x_prompt.shape
    B_lat, T_lat, _ = x_sample.shape
    n_ctx, n_lat = B_ctx * T_ctx, B_lat * T_lat
    n_tok = n_ctx + n_lat
    assert n_ctx % OUT_PROJ_TILE == 0 and T_lat % OUT_PROJ_TILE == 0 and OUT_PROJ_TILE % BIG_TOKEN_TILE == 0
    assert B_lat + 1 <= 8
    assert T_ctx % SCAN_CHUNK == 0 and T_lat % SCAN_CHUNK == 0 and n_ctx % T_lat == 0

    cond8 = jnp.concatenate([c_ctx[None, :], c, jnp.zeros((8 - 1 - B_lat, D), F32)], axis=0)
    mods = _ada_mods(cond8, ada_w, ada_b).reshape(DEPTH * 8, 1, 6 * D)

    rw_pad = jnp.pad(router_w, ((0, 0), (0, 0), (0, LANES - N_EXPERTS)))
    rb_pad = jnp.pad(router_b, ((0, 0), (0, LANES - N_EXPERTS)), constant_values=ROUTER_PAD_BIAS)

    w_even = w_in_even[0]
    lr0 = GLA_HEADS * (2 * GLA_DK + GLA_DV)
    lr1 = lr0 + 2 * GLA_LOW_RANK
    w_main = jnp.concatenate([w_even[:, :lr0], w_even[:, lr1:]], axis=1)
    w_lr = jnp.pad(w_even[:, lr0:lr1], ((0, 0), (0, LANES - 2 * GLA_LOW_RANK)))
    nqk = GLA_HEADS * GLA_DK
    gkw = jnp.zeros((LANES, 2 * nqk), F32)
    gkw = gkw.at[0:GLA_LOW_RANK, 0:nqk].set(gla_gk_w[0, 0])
    gkw = gkw.at[GLA_LOW_RANK:2 * GLA_LOW_RANK, nqk:].set(gla_gk_w[0, 1])
    gkb = gla_gk_b[0].reshape(1, 2 * nqk)
    proj, gdec, x_cur = _in_proj(0, n_ctx, n_tok, T_lat, mods, norm_mix[0], w_main,
                                 x_parts=(x_prompt.reshape(n_ctx, D), x_sample.reshape(n_lat, D)),
                                 even_extra=(w_lr, gkw, gkb))

    y_gla_c, fin_gla = _gla_scan(proj, gdec, gla_gain[0], None, 0, B_ctx, T_ctx, True)
    (y_gla_l,) = _gla_scan(proj, gdec, gla_gain[0], state_gla[:, 0], n_ctx, B_lat, T_lat, False)
    dexp = jnp.broadcast_to(ret_decay_exp[0].T[:, :, None], (RET_HEADS, 2, LANES))
    y_ret_c, fin_ret = _ret_scan(proj, dexp, None, None, 0, B_ctx, T_ctx, True)
    (y_ret_l,) = _ret_scan(proj, dexp, _rope_tables(T_lat), state_ret[:, 0], n_ctx, B_lat, T_lat, False)

    x_new, hn, idx, wts, rank, cnt = _out_proj(0, n_ctx, n_tok, T_lat, [(y_gla_c, y_gla_l), (y_ret_c, y_ret_l)],
                                               w_out_even[0], x_cur, mods, norm_ffn[0], rw_pad[0], rb_pad[0:1])
    y4 = _moe_layer(0, hn, idx, rank, cnt[0, :N_EXPERTS], moe_w1, moe_b1, moe_w2, moe_b2)

    (proj, x_cur) = _in_proj(1, n_ctx, n_tok, T_lat, mods, norm_mix[1], w_in_odd[0], x_prev=x_new, y4=(y4, wts))
    y_h_c, fin_h = _hgrn_scan(proj, hgrn_lb_logits, hgrn_gain[0], None, 0, B_ctx, T_ctx, True)
    (y_h_l,) = _hgrn_scan(proj, hgrn_lb_logits, hgrn_gain[0], state_hgrn[:, 0], n_ctx, B_lat, T_lat, False)
    x_new, hn, idx, wts, rank, cnt = _out_proj(1, n_ctx, n_tok, T_lat, [(y_h_c, y_h_l)], w_out_odd[0], x_cur, mods,
                                               norm_ffn[1], rw_pad[1], rb_pad[1:2])
    y4 = _moe_layer(1, hn, idx, rank, cnt[0, :N_EXPERTS], moe_w1, moe_b1, moe_w2, moe_b2)

    y_ctx, y_lat = _final(n_ctx, n_tok, T_lat, x_new, y4, wts, mods, final_norm)

    new_state_gla = fin_gla.reshape(B_ctx, 1, 2, GLA_HEADS, GLA_DK, GLA_DV)
    new_state_ret = fin_ret.reshape(B_ctx, 1, 2, RET_HEADS, RET_DK, RET_DV)
    new_state_hgrn = fin_h.reshape(B_ctx, 1, 2, HG_HEADS, HG_DK, HG_DV)
    return (y_ctx.reshape(B_ctx, T_ctx, D), y_lat.reshape(B_lat, T_lat, D), new_state_gla, new_state_ret,
            new_state_hgrn)
```

```python
import functools

import jax
import jax.numpy as jnp
from jax import lax
from jax.experimental import pallas as pl
from jax.experimental.pallas import tpu as pltpu

F32 = jnp.float32
BF16 = jnp.bfloat16

D_MODEL = 1024
DEPTH = 2
GRID_W = 64
GLA_HEADS, GLA_DK, GLA_DV, GLA_LOW_RANK = 4, 64, 128, 16
GLA_NORMALIZER = 16.0
RET_HEADS, RET_DK, RET_DV = 4, 128, 128
ROPE_BASE = 10000.0
HG_HEADS, HG_DK, HG_DV = 8, 128, 128
N_EXPERTS, TOP_K, D_FF = 32, 4, 1024
SWIGLU_ALPHA, SWIGLU_LIMIT = 1.702, 7.0
EPS = 1e-6

LANES = 128
SCAN_CHUNK = 128
SCAN_ROWS = 2048
TOKEN_TILE = 256
BIG_TOKEN_TILE = 512
MOE_TILE = 384
VMEM_LIMIT = 56 * 1024 * 1024

_GQ, _GK, _GV, _GG, _RQ, _RK, _RV, _RG, _EVEN_MAIN = 0, 256, 512, 1024, 1536, 2048, 2560, 3072, 3584
_HQ, _HFF, _HFB, _HI, _HG, _ODD_MAIN = 0, 1024, 2048, 3072, 4096, 5120


def _dot(a, b):
    return jnp.dot(a, b, preferred_element_type=F32)


def _dot_nt(a, b):
    return lax.dot_general(a, b, (((1,), (1,)), ((), ())), preferred_element_type=F32)


def _rms(x, gain=None):
    y = x * lax.rsqrt(jnp.mean(x * x, axis=-1, keepdims=True) + EPS)
    if gain is not None:
        y = y * gain
    return y


def _sigmoid(x):
    return 0.5 * jnp.tanh(0.5 * x) + 0.5


def _silu(x):
    return x * _sigmoid(x)


def _cparams(sem, vmem=VMEM_LIMIT):
    return pltpu.CompilerParams(dimension_semantics=sem, vmem_limit_bytes=vmem)


def _resident(shape):
    nd = len(shape)
    return pl.BlockSpec(shape, lambda *_: (0,) * nd, pipeline_mode=pl.Buffered(1))


def _ada_kernel(c_ref, w_ref, b_ref, o_ref):
    o_ref[0] = _dot(_silu(c_ref[...]), w_ref[0]) + b_ref[0]


def _ada_mods(cond8, ada_w, ada_b):
    tn = 1536
    return pl.pallas_call(
        _ada_kernel,
        grid=(DEPTH, 6 * D_MODEL // tn),
        in_specs=[
            pl.BlockSpec((8, D_MODEL), lambda l, j: (0, 0)),
            pl.BlockSpec((1, D_MODEL, tn), lambda l, j: (l, 0, j)),
            pl.BlockSpec((1, 1, tn), lambda l, j: (l, 0, j)),
        ],
        out_specs=pl.BlockSpec((1, 8, tn), lambda l, j: (l, 0, j)),
        out_shape=jax.ShapeDtypeStruct((DEPTH, 8, 6 * D_MODEL), F32),
        compiler_params=_cparams(("arbitrary", "arbitrary")),
        name="ada_mods",
    )(cond8, ada_w, ada_b.reshape(DEPTH, 1, 6 * D_MODEL))


ROW_UNITS = D_MODEL // LANES


def _rows_from_units(ref, n_rows):
    return jnp.concatenate([ref[pl.ds(c, n_rows, stride=ROW_UNITS), :] for c in range(ROW_UNITS)], axis=1)


def _rows_to_units(ref, val):
    n_rows = val.shape[0]
    for c in range(ROW_UNITS):
        ref[pl.ds(c, n_rows, stride=ROW_UNITS), :] = val[:, c * LANES:(c + 1) * LANES]


def _moe_combine(y_refs, wt_ref):
    w = wt_ref[...]
    terms = [w[:, k:k + 1] * _rows_from_units(y_refs[k], w.shape[0]) for k in range(TOP_K)]
    return (terms[0] + terms[1]) + (terms[2] + terms[3])


def _in_proj_kernel(*refs, first, even, nb_ctx):
    it = iter(refs)
    if first:
        xa_ref, xb_ref = next(it), next(it)
    else:
        xp_ref = next(it)
        y_refs = [next(it) for _ in range(TOP_K)]
        wt_ref, modp_ref = next(it), next(it)
    gain_ref, mod_ref, w_ref = next(it), next(it), next(it)
    if even:
        wlr_ref, gkw_ref, gkb_ref = next(it), next(it), next(it)
    proj_ref = next(it)
    if even:
        g_ref = next(it)
    xcur_ref = next(it)

    D = D_MODEL
    i = pl.program_id(0)
    if first:
        x = jnp.where(i < nb_ctx, xa_ref[...], xb_ref[...])
    else:
        x = xp_ref[...] + modp_ref[0][:, 5 * D:6 * D] * _moe_combine(y_refs, wt_ref)
    xcur_ref[...] = x
    m = mod_ref[0]
    hn = _rms(x, gain_ref[...]) * (1.0 + m[:, D:2 * D]) + m[:, 0:D]
    proj_ref[...] = _dot(hn, w_ref[...])
    if even:
        z = _dot(_dot(hn, wlr_ref[...]), gkw_ref[...]) + gkb_ref[...]
        g_ref[...] = (jnp.minimum(z, 0.0) - jnp.log(1.0 + jnp.exp(-jnp.abs(z)))) * (1.0 / GLA_NORMALIZER)


def _cond_row(i, nb_ctx, tiles_per_lat_seq):
    return jnp.where(i < nb_ctx, 0, 1 + (i - nb_ctx) // tiles_per_lat_seq)


def _in_proj(layer, n_ctx, n_tok, lat_seq, mods, gain, w_main, *, x_parts=None, x_prev=None, y4=None, even_extra=None):
    first = x_parts is not None
    even = even_extra is not None
    tm = BIG_TOKEN_TILE if first else TOKEN_TILE
    nb = n_tok // tm
    nb_ctx = n_ctx // tm
    tpl = lat_seq // tm
    np_cols = w_main.shape[1]
    D = D_MODEL

    def mod_map(l):
        return lambda i: (l * 8 + _cond_row(i, nb_ctx, tpl), 0, 0)

    row = lambda i: (i, 0)
    args, specs = [], []
    if first:
        xa, xb = x_parts
        args += [xa, xb]
        specs += [pl.BlockSpec((tm, D), lambda i: (jnp.minimum(i, nb_ctx - 1), 0)),
                  pl.BlockSpec((tm, D), lambda i: (jnp.maximum(i - nb_ctx, 0), 0))]
    else:
        y4, wts = y4
        args += [x_prev] + [y4] * TOP_K + [wts, mods]
        specs += [pl.BlockSpec((tm, D), row)]
        specs += [pl.BlockSpec((tm * ROW_UNITS, LANES), (lambda k: (lambda i: (k * nb + i, 0)))(k))
                  for k in range(TOP_K)]
        specs += [pl.BlockSpec((tm, LANES), row), pl.BlockSpec((1, 1, 6 * D), mod_map(layer - 1))]
    args += [gain.reshape(1, D), mods, w_main]
    specs += [_resident((1, D)), pl.BlockSpec((1, 1, 6 * D), mod_map(layer)), _resident((D, np_cols))]
    out_shapes = [jax.ShapeDtypeStruct((n_tok, np_cols), F32)]
    out_specs = [pl.BlockSpec((tm, np_cols), row)]
    if even:
        w_lr, gkw, gkb = even_extra
        args += [w_lr, gkw, gkb]
        specs += [_resident(w_lr.shape), _resident(gkw.shape), _resident(gkb.shape)]
        out_shapes.append(jax.ShapeDtypeStruct((n_tok, 2 * GLA_HEADS * GLA_DK), F32))
        out_specs.append(pl.BlockSpec((tm, 2 * GLA_HEADS * GLA_DK), row))
    out_shapes.append(jax.ShapeDtypeStruct((n_tok, D), F32))
    out_specs.append(pl.BlockSpec((tm, D), row))
    return pl.pallas_call(
        functools.partial(_in_proj_kernel, first=first, even=even, nb_ctx=nb_ctx),
        grid=(nb,),
        in_specs=specs,
        out_specs=out_specs,
        out_shape=out_shapes,
        compiler_params=_cparams(("arbitrary",)),
        name=f"in_proj_l{layer}",
    )(*args)


def _tri(c, lower):
    r = lax.broadcasted_iota(jnp.int32, (c, c), 0)
    s = lax.broadcasted_iota(jnp.int32, (c, c), 1)
    return (r >= s) if lower else (r <= s)


def _cumsum_mm(tri_bf16, g):
    g_hi = g.astype(BF16)
    g_lo = (g - g_hi.astype(F32)).astype(BF16)
    r = _dot(tri_bf16, jnp.concatenate([g_hi, g_lo], axis=1))
    w = g.shape[1]
    return r[:, :w] + r[:, w:]


def _gated_scan_kernel(*refs, variant, T, nseq, has_s0, want_final, hpb):
    C = SCAN_CHUNK
    ncs = T // C
    nc = nseq * ncs
    it = iter(refs)
    if variant == "gla":
        q_ref, k_ref, v_ref, og_ref, gf_ref, gb_ref, gain_ref = (next(it) for _ in range(7))
    else:
        q_ref, ff_ref, fb_ref, v_ref, og_ref, lbl_ref, gain_ref = (next(it) for _ in range(7))
    s0_ref = next(it) if has_s0 else None
    y_ref = next(it)
    sfin_ref = next(it) if want_final else None
    qi_scr, o_scr, u_scr, dec_scr, sin_scr = (next(it) for _ in range(5))

    lane = lax.broadcasted_iota(jnp.int32, (1, LANES), 1)
    if hpb == 2:
        masks = [lane < GLA_DK, lane >= GLA_DK]
    else:
        masks = [None]
    tri_l, tri_u = _tri(C, True), _tri(C, False)
    tri_l16, tri_u16 = tri_l.astype(F32).astype(BF16), tri_u.astype(F32).astype(BF16)
    H = C // 2
    upper_rows = lax.broadcasted_iota(jnp.int32, (C, 1), 0) < H

    if variant == "hgrn":
        lgs = [lbl_ref[l] for l in range(DEPTH)]
        mx = functools.reduce(jnp.maximum, lgs)
        es = [jnp.exp(l - mx) for l in lgs]
        tot = functools.reduce(lambda a, b: a + b, es)
        ps = [e / tot for e in es]
        layer = DEPTH - 1
        lb = functools.reduce(lambda a, b: a + b, ps[:layer + 1]) - ps[0]
        lb_f, lb_b = lb[0:1], lb[1:2]

    for n in range(nc):
        rows = pl.ds(n * C, C)
        if variant == "gla":
            q = q_ref[rows, :] * (GLA_DK ** -0.5)
            kf = kb = k_ref[rows, :]
            gf, gb = gf_ref[rows, :], gb_ref[rows, :]
        else:
            q = _silu(q_ref[rows, :])
            f_f = lb_f + (1.0 - lb_f) * _sigmoid(ff_ref[rows, :])
            f_b = lb_b + (1.0 - lb_b) * _sigmoid(fb_ref[rows, :])
            kf, kb = 1.0 - f_f, 1.0 - f_b
            gf, gb = jnp.log(f_f), jnp.log(f_b)
        bf = _cumsum_mm(tri_l16, gf)
        bb = _cumsum_mm(tri_u16, gb)
        bf_end, bb_end = bf[C - 1:C], bb[0:1]
        ref_f = jnp.where(upper_rows, bf[H // 2 - 1:H // 2], bf[H + H // 2 - 1:H + H // 2])
        ref_b = jnp.where(upper_rows, bb[H // 2:H // 2 + 1], bb[H + H // 2:H + H // 2 + 1])
        qd_f, qd_b = q * jnp.exp(bf - ref_f), q * jnp.exp(bb - ref_b)
        kd_f = [jnp.where(upper_rows, kf * jnp.exp(bf[H // 2 - 1:H // 2] - bf), 0.0),
                kf * jnp.exp(bf[H + H // 2 - 1:H + H // 2] - bf)]
        kd_b = [kb * jnp.exp(bb[H // 2:H // 2 + 1] - bb),
                jnp.where(upper_rows, 0.0, kb * jnp.exp(bb[H + H // 2:H + H // 2 + 1] - bb))]
        kend = jnp.concatenate([kf * jnp.exp(bf_end - bf), kb * jnp.exp(bb_end - bb)], axis=1)
        qi_scr[rows, :] = jnp.concatenate([q * jnp.exp(bf), q * jnp.exp(bb)], axis=1)
        dec_scr[n] = jnp.broadcast_to(jnp.concatenate([jnp.exp(bf_end), jnp.exp(bb_end)], axis=1), (8, 2 * LANES))
        for h in range(hpb):
            v = v_ref[rows, h * LANES:(h + 1) * LANES]
            if masks[h] is None:
                qf_h, qb_h, kend_h = qd_f, qd_b, kend
            else:
                qf_h, qb_h = jnp.where(masks[h], qd_f, 0.0), jnp.where(masks[h], qd_b, 0.0)
                kend_h = jnp.where(jnp.concatenate([masks[h], masks[h]], axis=1), kend, 0.0)
            s_f = jnp.concatenate([_dot_nt(qf_h[:H], kd_f[0]), _dot_nt(qf_h[H:], kd_f[1])], axis=0)
            s_b = jnp.concatenate([_dot_nt(qb_h[:H], kd_b[0]), _dot_nt(qb_h[H:], kd_b[1])], axis=0)
            s = jnp.where(tri_l, s_f, 0.0) + jnp.where(tri_u, s_b, 0.0)
            o_scr[rows, h * LANES:(h + 1) * LANES] = _dot(s, v)
            u_scr[h, n] = _dot(v.T, kend_h)

    for sq in range(nseq):
        for h in range(hpb):
            if has_s0:
                s_f, s_b = s0_ref[sq, 0, 0].T, s0_ref[sq, 1, 0].T
                if masks[h] is not None:
                    s_f, s_b = jnp.where(masks[h], s_f, 0.0), jnp.where(masks[h], s_b, 0.0)
            else:
                s_f = s_b = jnp.zeros((LANES, LANES), F32)
            for n in range(sq * ncs, (sq + 1) * ncs):
                sin_scr[h, n, :, 0:LANES] = s_f
                s_f = s_f * dec_scr[n, 0:1, 0:LANES] + u_scr[h, n, :, 0:LANES]
            for n in reversed(range(sq * ncs, (sq + 1) * ncs)):
                sin_scr[h, n, :, LANES:2 * LANES] = s_b
                s_b = s_b * dec_scr[n, 0:1, LANES:2 * LANES] + u_scr[h, n, :, LANES:2 * LANES]
            if want_final:
                if h == 0:
                    fin_f, fin_b = s_f, s_b
                else:
                    fin_f, fin_b = fin_f + s_f, fin_b + s_b
        if want_final:
            sfin_ref[sq, 0, 0] = fin_f.T
            sfin_ref[sq, 1, 0] = fin_b.T

    gain = gain_ref[...]
    for n in range(nc):
        rows = pl.ds(n * C, C)
        qi = qi_scr[rows, :]
        for h in range(hpb):
            cols = slice(h * LANES, (h + 1) * LANES)
            o = o_scr[rows, cols] + _dot_nt(qi, sin_scr[h, n])
            y_ref[rows, cols] = (_rms(o, gain) * _silu(og_ref[rows, cols])).astype(y_ref.dtype)


def _scan_scratch(T, hpb):
    nc = T // SCAN_CHUNK
    return [
        pltpu.VMEM((T, 2 * LANES), F32),
        pltpu.VMEM((T, hpb * LANES), F32),
        pltpu.VMEM((hpb, nc, LANES, 2 * LANES), F32),
        pltpu.VMEM((nc, 8, 2 * LANES), F32),
        pltpu.VMEM((hpb, nc, LANES, 2 * LANES), F32),
    ]


def _seqs_per_step(B, T):
    nseq = max(1, min(B, SCAN_ROWS // T))
    assert B % nseq == 0
    return nseq


def _gla_scan(proj, gdec, gain, s0, row0, B, T, want_final):
    nseq = _seqs_per_step(B, T)
    R = nseq * T
    rb0 = row0 // R
    assert row0 % R == 0
    has_s0 = s0 is not None

    def col(base, width=LANES):
        return lambda b, j: (rb0 + b, base // width + j)

    args = [proj, proj, proj, proj, gdec, gdec, gain.reshape(1, GLA_DV)]
    specs = [pl.BlockSpec((R, LANES), col(_GQ)), pl.BlockSpec((R, LANES), col(_GK)),
             pl.BlockSpec((R, 2 * LANES), col(_GV, 2 * LANES)), pl.BlockSpec((R, 2 * LANES), col(_GG, 2 * LANES)),
             pl.BlockSpec((R, LANES), col(0)), pl.BlockSpec((R, LANES), col(GLA_HEADS * GLA_DK)),
             pl.BlockSpec((1, GLA_DV), lambda b, j: (0, 0))]
    st_spec = pl.BlockSpec((nseq, 2, 1, LANES, LANES), lambda b, j: (b, 0, j, 0, 0))
    if has_s0:
        args.append(s0.reshape(B, 2, GLA_HEADS // 2, 2 * GLA_DK, GLA_DV))
        specs.append(st_spec)
    out_shapes = [jax.ShapeDtypeStruct((B * T, GLA_HEADS * GLA_DV), BF16)]
    out_specs = [pl.BlockSpec((R, 2 * LANES), lambda b, j: (b, j))]
    if want_final:
        out_shapes.append(jax.ShapeDtypeStruct((B, 2, GLA_HEADS // 2, 2 * GLA_DK, GLA_DV), F32))
        out_specs.append(st_spec)
    kern = functools.partial(_gated_scan_kernel, variant="gla", T=T, nseq=nseq, has_s0=has_s0,
                             want_final=want_final, hpb=2)
    return pl.pallas_call(
        kern, grid=(B // nseq, GLA_HEADS // 2), in_specs=specs, out_specs=out_specs, out_shape=out_shapes,
        scratch_shapes=_scan_scratch(R, 2),
        compiler_params=_cparams(("arbitrary", "arbitrary")), name=f"gla_scan_T{T}",
    )(*args)


def _hgrn_scan(proj, lb_logits, gain, s0, row0, B, T, want_final):
    nseq = _seqs_per_step(B, T)
    R = nseq * T
    rb0 = row0 // R
    assert row0 % R == 0
    has_s0 = s0 is not None

    def col(base):
        return lambda b, j: (rb0 + b, base // LANES + j)

    args = [proj, proj, proj, proj, proj, lb_logits, gain.reshape(1, HG_DV)]
    specs = [pl.BlockSpec((R, LANES), col(_HQ)), pl.BlockSpec((R, LANES), col(_HFF)),
             pl.BlockSpec((R, LANES), col(_HFB)), pl.BlockSpec((R, LANES), col(_HI)),
             pl.BlockSpec((R, LANES), col(_HG)),
             pl.BlockSpec((DEPTH, 2, LANES), lambda b, j: (0, 0, j)),
             pl.BlockSpec((1, HG_DV), lambda b, j: (0, 0))]
    st_spec = pl.BlockSpec((nseq, 2, 1, LANES, LANES), lambda b, j: (b, 0, j, 0, 0))
    if has_s0:
        args.append(s0.reshape(B, 2, HG_HEADS, HG_DK, HG_DV))
        specs.append(st_spec)
    out_shapes = [jax.ShapeDtypeStruct((B * T, HG_HEADS * HG_DV), BF16)]
    out_specs = [pl.BlockSpec((R, LANES), lambda b, j: (b, j))]
    if want_final:
        out_shapes.append(jax.ShapeDtypeStruct((B, 2, HG_HEADS, HG_DK, HG_DV), F32))
        out_specs.append(st_spec)
    kern = functools.partial(_gated_scan_kernel, variant="hgrn", T=T, nseq=nseq, has_s0=has_s0,
                             want_final=want_final, hpb=1)
    return pl.pallas_call(
        kern, grid=(B // nseq, HG_HEADS), in_specs=specs, out_specs=out_specs, out_shape=out_shapes,
        scratch_shapes=_scan_scratch(R, 1),
        compiler_params=_cparams(("arbitrary", "arbitrary")), name=f"hgrn_scan_T{T}",
    )(*args)


def _ret_scan_kernel(*refs, T, nseq, has_s0, want_final, rope):
    C = SCAN_CHUNK
    ncs = T // C
    nc = nseq * ncs
    it = iter(refs)
    q_ref, k_ref, v_ref, og_ref, dexp_ref = (next(it) for _ in range(5))
    if rope:
        cos_ref, sin_ref = next(it), next(it)
    s0_ref = next(it) if has_s0 else None
    y_ref = next(it)
    sfin_ref = next(it) if want_final else None
    qi_scr, o_scr, u_scr, sin_scr = (next(it) for _ in range(4))

    lg = jnp.log1p(-jnp.exp2(-dexp_ref[0]))
    lg_f, lg_b = lg[0:1], lg[1:2]
    r = lax.broadcasted_iota(jnp.int32, (C, C), 0)
    s = lax.broadcasted_iota(jnp.int32, (C, C), 1)
    dist = (r - s).astype(F32)
    dmask = (jnp.where(r >= s, jnp.exp(jnp.maximum(dist, 0.0) * lg_f[:, 0:1]), 0.0)
             + jnp.where(r <= s, jnp.exp(jnp.maximum(-dist, 0.0) * lg_b[:, 0:1]), 0.0))
    pos = lax.broadcasted_iota(jnp.int32, (C, LANES), 0).astype(F32)
    xi = jnp.concatenate([jnp.exp((pos + 1.0) * lg_f), jnp.exp((C - pos) * lg_b)], axis=1)
    zeta = jnp.concatenate([jnp.exp((C - 1.0 - pos) * lg_f), jnp.exp(pos * lg_b)], axis=1)
    d_f, d_b = jnp.exp(C * lg_f), jnp.exp(C * lg_b)

    def rot(x, seq_rows):
        if not rope:
            return x
        return x * cos_ref[seq_rows, :] + pltpu.roll(x, RET_DK // 2, axis=1) * sin_ref[seq_rows, :]

    for n in range(nc):
        rows = pl.ds(n * C, C)
        seq_rows = pl.ds((n % ncs) * C, C)
        q = rot(q_ref[rows, :], seq_rows)
        k = rot(k_ref[rows, :] * (RET_DK ** -0.5), seq_rows)
        v = v_ref[rows, :]
        o_scr[rows, :] = _dot(_dot_nt(q, k) * dmask, v)
        qi_scr[rows, :] = jnp.concatenate([q, q], axis=1) * xi
        u_scr[n] = _dot(v.T, jnp.concatenate([k, k], axis=1) * zeta)

    for sq in range(nseq):
        if has_s0:
            s_f, s_b = s0_ref[sq, 0, 0].T, s0_ref[sq, 1, 0].T
        else:
            s_f = s_b = jnp.zeros((LANES, LANES), F32)
        for n in range(sq * ncs, (sq + 1) * ncs):
            sin_scr[n, :, 0:LANES] = s_f
            s_f = s_f * d_f + u_scr[n, :, 0:LANES]
        for n in reversed(range(sq * ncs, (sq + 1) * ncs)):
            sin_scr[n, :, LANES:2 * LANES] = s_b
            s_b = s_b * d_b + u_scr[n, :, LANES:2 * LANES]
        if want_final:
            sfin_ref[sq, 0, 0] = s_f.T
            sfin_ref[sq, 1, 0] = s_b.T

    for n in range(nc):
        rows = pl.ds(n * C, C)
        o = o_scr[rows, :] + _dot_nt(qi_scr[rows, :], sin_scr[n])
        y_ref[rows, :] = (_rms(o) * _silu(og_ref[rows, :])).astype(y_ref.dtype)


def _ret_scan(proj, dexp, rope_tabs, s0, row0, B, T, want_final):
    nseq = _seqs_per_step(B, T)
    R = nseq * T
    rb0 = row0 // R
    assert row0 % R == 0
    nc = R // SCAN_CHUNK
    has_s0 = s0 is not None
    rope = rope_tabs is not None

    def col(base):
        return lambda b, j: (rb0 + b, base // LANES + j)

    args = [proj, proj, proj, proj, dexp]
    specs = [pl.BlockSpec((R, LANES), col(_RQ)), pl.BlockSpec((R, LANES), col(_RK)),
             pl.BlockSpec((R, LANES), col(_RV)), pl.BlockSpec((R, LANES), col(_RG)),
             pl.BlockSpec((1, 2, LANES), lambda b, j: (j, 0, 0))]
    if rope:
        args += list(rope_tabs)
        specs += [pl.BlockSpec((T, LANES), lambda b, j: (0, 0))] * 2
    st_spec = pl.BlockSpec((nseq, 2, 1, LANES, LANES), lambda b, j: (b, 0, j, 0, 0))
    if has_s0:
        args.append(s0.reshape(B, 2, RET_HEADS, RET_DK, RET_DV))
        specs.append(st_spec)
    out_shapes = [jax.ShapeDtypeStruct((B * T, RET_HEADS * RET_DV), BF16)]
    out_specs = [pl.BlockSpec((R, LANES), lambda b, j: (b, j))]
    if want_final:
        out_shapes.append(jax.ShapeDtypeStruct((B, 2, RET_HEADS, RET_DK, RET_DV), F32))
        out_specs.append(st_spec)
    kern = functools.partial(_ret_scan_kernel, T=T, nseq=nseq, has_s0=has_s0, want_final=want_final, rope=rope)
    scratch = [pltpu.VMEM((R, 2 * LANES), F32), pltpu.VMEM((R, LANES), F32),
               pltpu.VMEM((nc, LANES, 2 * LANES), F32), pltpu.VMEM((nc, LANES, 2 * LANES), F32)]
    return pl.pallas_call(
        kern, grid=(B // nseq, RET_HEADS), in_specs=specs, out_specs=out_specs, out_shape=out_shapes,
        scratch_shapes=scratch,
        compiler_params=_cparams(("arbitrary", "arbitrary")), name=f"ret_scan_T{T}",
    )(*args)


def _out_proj_kernel(*refs, n_mix, nb_ctx):
    it = iter(refs)
    y_refs = [(next(it), next(it)) for _ in range(n_mix)]
    wo_ref, x_ref, mod_ref, gain_ref, rw_ref, rb_ref = (next(it) for _ in range(6))
    xnew_ref, hn_ref, idx_ref, wt_ref, rank_ref, cnt_ref = (next(it) for _ in range(6))
    cnt_scr = next(it)
    D = D_MODEL
    is_ctx = pl.program_id(0) < nb_ctx
    mix = None
    r0 = 0
    for ya_ref, yb_ref in y_refs:
        w = ya_ref.shape[1]
        part = _dot(jnp.where(is_ctx, ya_ref[...], yb_ref[...]), wo_ref[r0:r0 + w, :])
        mix = part if mix is None else mix + part
        r0 += w
    m = mod_ref[0]
    xn = x_ref[...] + m[:, 2 * D:3 * D] * mix
    xnew_ref[...] = xn
    hn = _rms(xn, gain_ref[...]) * (1.0 + m[:, 4 * D:5 * D]) + m[:, 3 * D:4 * D]
    hn_hi = hn.astype(BF16)
    _rows_to_units(hn_ref, hn)
    hn_lo = (hn - hn_hi.astype(F32)).astype(BF16)
    rw = rw_ref[...]
    rw_hi = rw.astype(BF16)
    rw_lo = (rw - rw_hi.astype(F32)).astype(BF16)
    logits = (_dot(hn_hi, rw_hi) + (_dot(hn_lo, rw_hi) + _dot(hn_hi, rw_lo))) + rb_ref[...]
    lane = lax.broadcasted_iota(jnp.int32, logits.shape, 1).astype(F32)
    vals, idxs = [], []
    cur = logits
    for _ in range(TOP_K):
        mx = jnp.max(cur, axis=-1, keepdims=True)
        ik = jnp.min(jnp.where(cur == mx, lane, float(LANES)), axis=-1, keepdims=True)
        vals.append(mx)
        idxs.append(ik)
        cur = jnp.where(lane == ik, -jnp.inf, cur)
    es = [jnp.exp(v - vals[0]) for v in vals]
    tot = (es[0] + es[1]) + (es[2] + es[3])
    idx_out = jnp.zeros(logits.shape, F32)
    wt_out = jnp.zeros(logits.shape, F32)
    for k in range(TOP_K):
        idx_out = jnp.where(lane == float(k), idxs[k], idx_out)
        wt_out = jnp.where(lane == float(k), es[k] / tot, wt_out)
    idx_ref[...] = idx_out.astype(jnp.int32)
    wt_ref[...] = wt_out

    @pl.when(pl.program_id(0) == 0)
    def _():
        cnt_scr[...] = jnp.zeros(cnt_scr.shape, F32)

    tm = logits.shape[0]
    hits = [lane == idxs[k] for k in range(TOP_K)]
    sel = jnp.zeros(logits.shape, F32)
    for k in range(TOP_K):
        sel = sel + jnp.where(hits[k], 1.0, 0.0)
    rr = lax.broadcasted_iota(jnp.int32, (tm, tm), 0)
    cc = lax.broadcasted_iota(jnp.int32, (tm, tm), 1)
    before = jnp.where(rr > cc, 1.0, 0.0).astype(BF16)
    rank_all = cnt_scr[0:1, :] + _dot(before, sel.astype(BF16))
    rank_out = jnp.zeros(logits.shape, F32)
    for k in range(TOP_K):
        rk = jnp.sum(jnp.where(hits[k], rank_all, 0.0), axis=-1, keepdims=True)
        rank_out = jnp.where(lane == float(k), rk, rank_out)
    rank_ref[...] = rank_out.astype(jnp.int32)
    total = cnt_scr[...] + jnp.sum(sel, axis=0, keepdims=True)
    cnt_scr[...] = total
    cnt_ref[...] = total.astype(jnp.int32)


def _out_proj(layer, n_ctx, n_tok, lat_seq, ys, w_out, x_cur, mods, gain, rw_pad, rb_pad):
    tm = BIG_TOKEN_TILE
    nb, nb_ctx, tpl = n_tok // tm, n_ctx // tm, lat_seq // tm
    D = D_MODEL
    row = lambda i: (i, 0)
    specs = []
    for ya, _ in ys:
        specs += [pl.BlockSpec((tm, ya.shape[1]), lambda i: (jnp.minimum(i, nb_ctx - 1), 0)),
                  pl.BlockSpec((tm, ya.shape[1]), lambda i: (jnp.maximum(i - nb_ctx, 0), 0))]
    specs += [_resident((D, D)), pl.BlockSpec((tm, D), row),
              pl.BlockSpec((1, 1, 6 * D), lambda i: (layer * 8 + _cond_row(i, nb_ctx, tpl), 0, 0)),
              _resident((1, D)), _resident((D, LANES)), _resident((1, LANES))]
    return pl.pallas_call(
        functools.partial(_out_proj_kernel, n_mix=len(ys), nb_ctx=nb_ctx),
        grid=(nb,),
        in_specs=specs,
        out_specs=[pl.BlockSpec((tm, D), row), pl.BlockSpec((tm * ROW_UNITS, LANES), row),
                   pl.BlockSpec((tm, LANES), row), pl.BlockSpec((tm, LANES), row),
                   pl.BlockSpec((tm, LANES), row), pl.BlockSpec((8, LANES), lambda i: (0, 0))],
        out_shape=[jax.ShapeDtypeStruct((n_tok, D), F32), jax.ShapeDtypeStruct((n_tok * ROW_UNITS, LANES), F32),
                   jax.ShapeDtypeStruct((n_tok, LANES), jnp.int32), jax.ShapeDtypeStruct((n_tok, LANES), F32),
                   jax.ShapeDtypeStruct((n_tok, LANES), jnp.int32), jax.ShapeDtypeStruct((8, LANES), jnp.int32)],
        scratch_shapes=[pltpu.VMEM((8, LANES), F32)],
        compiler_params=_cparams(("arbitrary",)),
        name=f"out_proj_l{layer}",
    )(*[y for pair in ys for y in pair], w_out, x_cur, mods, gain.reshape(1, D), rw_pad, rb_pad)


W1_SPLIT = 4
W2_SPLIT = 2
SPARE_TILES = 2
ROUTER_PAD_BIAS = -1e30


def _moe_kernel(seg_ref, nt_ref, code_hbm, x_hbm, *refs, n_tok):
    w1_refs = refs[:W1_SPLIT]
    b1_ref = refs[W1_SPLIT]
    w2_refs = refs[W1_SPLIT + 1:W1_SPLIT + 1 + W2_SPLIT]
    b2_ref, y_hbm, w1_scr, w2_scr, act_scr, xbuf, obuf, code_smem, csem, gsem, ssem = refs[W1_SPLIT + 1 + W2_SPLIT:]
    tm = MOE_TILE
    U = ROW_UNITS
    n_code_tiles = code_hbm.shape[0] // tm
    e = pl.program_id(0)
    nt = nt_ref[e]
    g_first = seg_ref[e] // tm

    GATHER, SCATTER = 0, 1

    def code_copy(kind, t, p):
        first = pl.multiple_of(jnp.clip(t, 0, n_code_tiles - 1) * tm, tm)
        return pltpu.make_async_copy(code_hbm.at[pl.ds(first, tm)], code_smem.at[kind, p], csem.at[kind, p])

    def gather_start(b, p):
        for r in range(tm):
            tok = jnp.bitwise_and(code_smem[GATHER, p, r], n_tok - 1)
            pltpu.make_async_copy(x_hbm.at[pl.ds(pl.multiple_of(tok * U, U), U)],
                                  xbuf.at[b, pl.ds(r * U, U)], gsem.at[b]).start()

    def gather_wait(b):
        pltpu.make_async_copy(x_hbm.at[pl.ds(0, tm * U)], xbuf.at[b], gsem.at[b]).wait()

    def scatter_start(b, p):
        for r in range(tm):
            dst = code_smem[SCATTER, p, r]
            pltpu.make_async_copy(obuf.at[b, pl.ds(r * U, U)],
                                  y_hbm.at[pl.ds(pl.multiple_of(dst * U, U), U)], ssem.at[b]).start()

    def scatter_wait(b):
        pltpu.make_async_copy(obuf.at[b], y_hbm.at[pl.ds(0, tm * U)], ssem.at[b]).wait()

    @pl.when(e == 0)
    def _():
        obuf[...] = jnp.zeros(obuf.shape, obuf.dtype)
        for h in range(SPARE_TILES):
            cp = pltpu.make_async_copy(obuf.at[0], y_hbm.at[pl.ds((TOP_K * n_tok + h * tm) * U, tm * U)], ssem.at[0])
            cp.start()
            cp.wait()
        code_copy(GATHER, 0, 0).start()
        code_copy(GATHER, 1, 1).start()
        code_copy(SCATTER, n_code_tiles - 1, 1).start()
        code_copy(SCATTER, 0, 0).start()
        code_copy(GATHER, 0, 0).wait()
        gather_start(0, 0)

    @pl.when(nt > 0)
    def _():
        grp = 2 * LANES
        rr = lax.broadcasted_iota(jnp.int32, (grp, grp), 0)
        cc = lax.broadcasted_iota(jnp.int32, (grp, grp), 1)
        src = jnp.where(cc < LANES, 2 * cc, 2 * (cc - LANES) + 1)
        perm = jnp.where(rr == src, 1.0, 0.0).astype(BF16)
        cw = 2 * D_FF // W1_SPLIT
        for c, w_ref in enumerate(w1_refs):
            for j in range(cw // grp):
                cols = slice(j * grp, (j + 1) * grp)
                w1_scr[:, c * cw + j * grp:c * cw + (j + 1) * grp] = _dot(w_ref[0, 0, :, cols].astype(BF16),
                                                                         perm).astype(BF16)
        rh = D_FF // W2_SPLIT
        for c, w_ref in enumerate(w2_refs):
            w2_scr[c * rh:(c + 1) * rh, :] = w_ref[0, 0].astype(BF16)
        blk = 4 * LANES

        def tile(g, b):
            o = 1 - b

            @pl.when(g >= 0)
            def _():
                code_copy(GATHER, g + 1, o).wait()
                code_copy(SCATTER, g - 1, o).wait()
                gather_start(o, o)
                scatter_start(o, o)
                code_copy(GATHER, g + 2, b).start()
                code_copy(SCATTER, g + 1, o).start()

            gather_wait(b)
            x = _rows_from_units(xbuf.at[b], tm).astype(BF16)
            for c in range(2 * D_FF // blk):
                cols = slice(c * blk, (c + 1) * blk)
                hid = _dot(x, w1_scr[:, cols]) + b1_ref[0, 0][:, cols]
                acts = []
                for j in range(blk // grp):
                    glu = jnp.minimum(hid[:, j * grp:j * grp + LANES], SWIGLU_LIMIT)
                    lin = jnp.clip(hid[:, j * grp + LANES:(j + 1) * grp], -SWIGLU_LIMIT, SWIGLU_LIMIT)
                    acts.append(glu * _sigmoid(SWIGLU_ALPHA * glu) * (lin + 1.0))
                act_scr[:, c * (blk // 2):(c + 1) * (blk // 2)] = jnp.concatenate(acts, axis=1).astype(BF16)
            y = _dot(act_scr[...], w2_scr[...]) + b2_ref[0, 0]

            @pl.when(g >= 1)
            def _():
                scatter_wait(b)

            _rows_to_units(obuf.at[b], y)

        g_end = g_first + nt

        def tile_pair(m, carry):
            for b in range(2):
                g = 2 * m + b

                @pl.when(jnp.logical_and(g >= g_first, g < g_end))
                def _():
                    tile(g, b)

            return carry

        lax.fori_loop(g_first // 2, (g_end + 1) // 2, tile_pair, 0)

    @pl.when(e == N_EXPERTS - 1)
    def _():
        g_end = g_first + nt
        last = lax.rem(g_end + 1, 2)
        code_copy(SCATTER, g_end - 1, last).wait()
        scatter_start(last, last)
        scatter_wait(last)
        scatter_wait(1 - last)
        gather_wait(1 - last)
        code_copy(GATHER, g_end + 1, last).wait()
        code_copy(SCATTER, g_end, 1 - last).wait()


def _moe_experts(layer, n_tok, seg_start, tiles_per_e, code, x_units, w1, b1, w2, b2):
    tm = MOE_TILE
    D, F2, U = D_MODEL, 2 * D_FF, ROW_UNITS
    cw, rh = F2 // W1_SPLIT, D_FF // W2_SPLIT
    any_space = pl.BlockSpec(memory_space=pl.ANY)
    w1_specs = [pl.BlockSpec((1, 1, D, cw), (lambda c: (lambda e, sg, nt: (layer, e, 0, c)))(c))
                for c in range(W1_SPLIT)]
    w2_specs = [pl.BlockSpec((1, 1, rh, D), (lambda c: (lambda e, sg, nt: (layer, e, c, 0)))(c))
                for c in range(W2_SPLIT)]
    grid_spec = pltpu.PrefetchScalarGridSpec(
        num_scalar_prefetch=2,
        grid=(N_EXPERTS,),
        in_specs=[any_space, any_space]
        + w1_specs + [pl.BlockSpec((1, 1, 1, F2), lambda e, sg, nt: (layer, e, 0, 0))]
        + w2_specs + [pl.BlockSpec((1, 1, 1, D), lambda e, sg, nt: (layer, e, 0, 0))],
        out_specs=any_space,
        scratch_shapes=[pltpu.VMEM((D, F2), BF16), pltpu.VMEM((D_FF, D), BF16), pltpu.VMEM((tm, D_FF), BF16),
                        pltpu.VMEM((2, tm * U, LANES), F32), pltpu.VMEM((2, tm * U, LANES), F32),
                        pltpu.SMEM((2, 2, tm), jnp.int32),
                        pltpu.SemaphoreType.DMA((2, 2)), pltpu.SemaphoreType.DMA((2,)), pltpu.SemaphoreType.DMA((2,))],
    )
    n_out_rows = TOP_K * n_tok + SPARE_TILES * tm
    b1 = b1.reshape(DEPTH, N_EXPERTS, F2 // (2 * LANES), LANES, 2).swapaxes(-1, -2)
    return pl.pallas_call(
        functools.partial(_moe_kernel, n_tok=n_tok),
        grid_spec=grid_spec,
        out_shape=jax.ShapeDtypeStruct((n_out_rows * U, LANES), F32),
        compiler_params=_cparams(("arbitrary",)),
        name="moe_experts",
    )(seg_start, tiles_per_e, code, x_units, *([w1] * W1_SPLIT), b1.reshape(DEPTH, N_EXPERTS, 1, F2),
      *([w2] * W2_SPLIT), b2.reshape(DEPTH, N_EXPERTS, 1, D))


def _slot_code_kernel(seg_ref, idx_ref, rank_ref, unused_hbm, code_ref, sem, *, n_tok):
    i = pl.program_id(0)
    tm = idx_ref.shape[0] // TOP_K

    @pl.when(i == 0)
    def _():
        cp = pltpu.make_async_copy(unused_hbm, code_ref, sem)
        cp.start()
        cp.wait()

    base = i * tm
    for t in range(tm):
        for k in range(TOP_K):
            p = t * TOP_K + k
            code_ref[seg_ref[idx_ref[p]] + rank_ref[p]] = base + (k * n_tok + t)


def _slot_codes(idx, rank, seg_start, unused):
    n_tok = idx.shape[0]
    tm = TOKEN_TILE
    pairs = lambda a: a[:, :TOP_K].reshape(-1)
    flat = pl.BlockSpec((tm * TOP_K,), lambda i, sg: (i,), memory_space=pltpu.SMEM)
    grid_spec = pltpu.PrefetchScalarGridSpec(
        num_scalar_prefetch=1,
        grid=(n_tok // tm,),
        in_specs=[flat, flat, pl.BlockSpec(memory_space=pl.ANY)],
        out_specs=pl.BlockSpec(memory_space=pltpu.SMEM),
        scratch_shapes=[pltpu.SemaphoreType.DMA(())],
    )
    return pl.pallas_call(
        functools.partial(_slot_code_kernel, n_tok=n_tok),
        grid_spec=grid_spec,
        out_shape=jax.ShapeDtypeStruct(unused.shape, jnp.int32),
        compiler_params=_cparams(("arbitrary",)),
        name="slot_codes",
    )(seg_start, pairs(idx), pairs(rank), unused)


def _moe_layer(layer, hn_units, idx, rank, counts, w1, b1, w2, b2):
    n_tok = hn_units.shape[0] // ROW_UNITS
    assert n_tok & (n_tok - 1) == 0
    tm = MOE_TILE
    n_pairs = n_tok * TOP_K
    n_slots = ((n_pairs + N_EXPERTS * (tm - 1)) // tm + 1) * tm
    tiles_per_e = (counts + tm - 1) // tm
    seg_start = (jnp.cumsum(tiles_per_e) - tiles_per_e) * tm
    unused = n_pairs + jnp.arange(n_slots, dtype=jnp.int32) % (SPARE_TILES * tm)
    code = _slot_codes(idx, rank, seg_start, unused)
    return _moe_experts(layer, n_tok, seg_start, tiles_per_e, code, hn_units, w1, b1, w2, b2)


def _final_kernel(*refs, nb_ctx):
    x_ref = refs[0]
    y_refs = refs[1:1 + TOP_K]
    wt_ref, mod_ref, gain_ref, oa_ref, ob_ref = refs[1 + TOP_K:]
    D = D_MODEL
    i = pl.program_id(0)
    x = x_ref[...] + mod_ref[0][:, 5 * D:6 * D] * _moe_combine(y_refs, wt_ref)
    out = _rms(x, gain_ref[...])

    @pl.when(i < nb_ctx)
    def _():
        oa_ref[...] = out

    @pl.when(i >= nb_ctx)
    def _():
        ob_ref[...] = out


def _final(n_ctx, n_tok, lat_seq, x_new, y4, wts, mods, gain):
    tm = TOKEN_TILE
    nb, nb_ctx, tpl = n_tok // tm, n_ctx // tm, lat_seq // tm
    D = D_MODEL
    specs = [pl.BlockSpec((tm, D), lambda i: (i, 0))]
    specs += [pl.BlockSpec((tm * ROW_UNITS, LANES), (lambda k: (lambda i: (k * nb + i, 0)))(k)) for k in range(TOP_K)]
    specs += [pl.BlockSpec((tm, LANES), lambda i: (i, 0)),
              pl.BlockSpec((1, 1, 6 * D), lambda i: ((DEPTH - 1) * 8 + _cond_row(i, nb_ctx, tpl), 0, 0)),
              _resident((1, D))]
    return pl.pallas_call(
        functools.partial(_final_kernel, nb_ctx=nb_ctx),
        grid=(nb,),
        in_specs=specs,
        out_specs=[pl.BlockSpec((tm, D), lambda i: (jnp.minimum(i, nb_ctx - 1), 0)),
                   pl.BlockSpec((tm, D), lambda i: (jnp.maximum(i - nb_ctx, 0), 0))],
        out_shape=[jax.ShapeDtypeStruct((n_ctx, D), F32), jax.ShapeDtypeStruct((n_tok - n_ctx, D), F32)],
        compiler_params=_cparams(("arbitrary",)),
        name="final_norm",
    )(x_new, *([y4] * TOP_K), wts, mods, gain.reshape(1, D))


def _rope_tables(n_tokens):
    rows = n_tokens // GRID_W
    r = jnp.repeat(jnp.arange(rows, dtype=F32), GRID_W)
    col = jnp.tile(jnp.arange(GRID_W, dtype=F32), rows)
    nf = RET_DK // 4
    inv = ROPE_BASE ** (-jnp.arange(nf, dtype=F32) / nf)
    ang = jnp.concatenate([r[:, None] * inv, col[:, None] * inv], axis=-1)
    cos, sin = jnp.cos(ang), jnp.sin(ang)
    return jnp.concatenate([cos, cos], axis=-1), jnp.concatenate([-sin, sin], axis=-1)


def kernel(x_prompt, x_sample, c, state_gla, state_ret, state_hgrn, c_ctx, norm_mix, norm_ffn, ada_w, ada_b,
           w_in_even, w_out_even, gla_gk_w, gla_gk_b, gla_gain, ret_decay_exp, w_in_odd, w_out_odd,
           hgrn_lb_logits, hgrn_gain, router_w, router_b, moe_w1, moe_b1, moe_w2, moe_b2, final_norm):
    D = D_MODEL
    B_ctx, T_ctx, _ = x_prompt.shape
    B_lat, T_lat, _ = x_sample.shape
    n_ctx, n_lat = B_ctx * T_ctx, B_lat * T_lat
    n_tok = n_ctx + n_lat
    assert n_ctx % BIG_TOKEN_TILE == 0 and T_lat % BIG_TOKEN_TILE == 0 and B_lat + 1 <= 8
    assert T_ctx % SCAN_CHUNK == 0 and T_lat % SCAN_CHUNK == 0 and n_ctx % T_lat == 0

    cond8 = jnp.concatenate([c_ctx[None, :], c, jnp.zeros((8 - 1 - B_lat, D), F32)], axis=0)
    mods = _ada_mods(cond8, ada_w, ada_b).reshape(DEPTH * 8, 1, 6 * D)

    rw_pad = jnp.pad(router_w, ((0, 0), (0, 0), (0, LANES - N_EXPERTS)))
    rb_pad = jnp.pad(router_b, ((0, 0), (0, LANES - N_EXPERTS)), constant_values=ROUTER_PAD_BIAS)

    w_even = w_in_even[0]
    lr0 = GLA_HEADS * (2 * GLA_DK + GLA_DV)
    lr1 = lr0 + 2 * GLA_LOW_RANK
    w_main = jnp.concatenate([w_even[:, :lr0], w_even[:, lr1:]], axis=1)
    w_lr = jnp.pad(w_even[:, lr0:lr1], ((0, 0), (0, LANES - 2 * GLA_LOW_RANK)))
    nqk = GLA_HEADS * GLA_DK
    gkw = jnp.zeros((LANES, 2 * nqk), F32)
    gkw = gkw.at[0:GLA_LOW_RANK, 0:nqk].set(gla_gk_w[0, 0])
    gkw = gkw.at[GLA_LOW_RANK:2 * GLA_LOW_RANK, nqk:].set(gla_gk_w[0, 1])
    gkb = gla_gk_b[0].reshape(1, 2 * nqk)
    proj, gdec, x_cur = _in_proj(0, n_ctx, n_tok, T_lat, mods, norm_mix[0], w_main,
                                 x_parts=(x_prompt.reshape(n_ctx, D), x_sample.reshape(n_lat, D)),
                                 even_extra=(w_lr, gkw, gkb))

    y_gla_c, fin_gla = _gla_scan(proj, gdec, gla_gain[0], None, 0, B_ctx, T_ctx, True)
    (y_gla_l,) = _gla_scan(proj, gdec, gla_gain[0], state_gla[:, 0], n_ctx, B_lat, T_lat, False)
    dexp = jnp.broadcast_to(ret_decay_exp[0].T[:, :, None], (RET_HEADS, 2, LANES))
    y_ret_c, fin_ret = _ret_scan(proj, dexp, None, None, 0, B_ctx, T_ctx, True)
    (y_ret_l,) = _ret_scan(proj, dexp, _rope_tables(T_lat), state_ret[:, 0], n_ctx, B_lat, T_lat, False)

    x_new, hn, idx, wts, rank, cnt = _out_proj(0, n_ctx, n_tok, T_lat, [(y_gla_c, y_gla_l), (y_ret_c, y_ret_l)],
                                               w_out_even[0], x_cur, mods, norm_ffn[0], rw_pad[0], rb_pad[0:1])
    y4 = _moe_layer(0, hn, idx, rank, cnt[0, :N_EXPERTS], moe_w1, moe_b1, moe_w2, moe_b2)

    (proj, x_cur) = _in_proj(1, n_ctx, n_tok, T_lat, mods, norm_mix[1], w_in_odd[0], x_prev=x_new, y4=(y4, wts))
    y_h_c, fin_h = _hgrn_scan(proj, hgrn_lb_logits, hgrn_gain[0], None, 0, B_ctx, T_ctx, True)
    (y_h_l,) = _hgrn_scan(proj, hgrn_lb_logits, hgrn_gain[0], state_hgrn[:, 0], n_ctx, B_lat, T_lat, False)
    x_new, hn, idx, wts, rank, cnt = _out_proj(1, n_ctx, n_tok, T_lat, [(y_h_c, y_h_l)], w_out_odd[0], x_cur, mods,
                                               norm_ffn[1], rw_pad[1], rb_pad[1:2])
    y4 = _moe_layer(1, hn, idx, rank, cnt[0, :N_EXPERTS], moe_w1, moe_b1, moe_w2, moe_b2)

    y_ctx, y_lat = _final(n_ctx, n_tok, T_lat, x_new, y4, wts, mods, final_norm)

    new_state_gla = fin_gla.reshape(B_ctx, 1, 2, GLA_HEADS, GLA_DK, GLA_DV)
    new_state_ret = fin_ret.reshape(B_ctx, 1, 2, RET_HEADS, RET_DK, RET_DV)
    new_state_hgrn = fin_h.reshape(B_ctx, 1, 2, HG_HEADS, HG_DK, HG_DV)
    return (y_ctx.reshape(B_ctx, T_ctx, D), y_lat.reshape(B_lat, T_lat, D), new_state_gla, new_state_ret,
            new_state_hgrn)
```

```python
import functools

import jax
import jax.numpy as jnp
from jax import lax
from jax.experimental import pallas as pl
from jax.experimental.pallas import tpu as pltpu

F32 = jnp.float32
BF16 = jnp.bfloat16

D_MODEL = 1024
DEPTH = 2
GRID_W = 64
GLA_HEADS, GLA_DK, GLA_DV, GLA_LOW_RANK = 4, 64, 128, 16
GLA_NORMALIZER = 16.0
RET_HEADS, RET_DK, RET_DV = 4, 128, 128
ROPE_BASE = 10000.0
HG_HEADS, HG_DK, HG_DV = 8, 128, 128
N_EXPERTS, TOP_K, D_FF = 32, 4, 1024
SWIGLU_ALPHA, SWIGLU_LIMIT = 1.702, 7.0
EPS = 1e-6

LANES = 128
SCAN_CHUNK = 128
SCAN_ROWS = 2048
TOKEN_TILE = 256
BIG_TOKEN_TILE = 512
MOE_TILE = 384
VMEM_LIMIT = 56 * 1024 * 1024

_GQ, _GK, _GV, _GG, _RQ, _RK, _RV, _RG, _EVEN_MAIN = 0, 256, 512, 1024, 1536, 2048, 2560, 3072, 3584
_HQ, _HFF, _HFB, _HI, _HG, _ODD_MAIN = 0, 1024, 2048, 3072, 4096, 5120


def _dot(a, b):
    return jnp.dot(a, b, preferred_element_type=F32)


def _dot_nt(a, b):
    return lax.dot_general(a, b, (((1,), (1,)), ((), ())), preferred_element_type=F32)


def _rms(x, gain=None):
    y = x * lax.rsqrt(jnp.mean(x * x, axis=-1, keepdims=True) + EPS)
    if gain is not None:
        y = y * gain
    return y


def _sigmoid(x):
    return 0.5 * jnp.tanh(0.5 * x) + 0.5


def _silu(x):
    return x * _sigmoid(x)


def _cparams(sem, vmem=VMEM_LIMIT):
    return pltpu.CompilerParams(dimension_semantics=sem, vmem_limit_bytes=vmem)


def _resident(shape):
    nd = len(shape)
    return pl.BlockSpec(shape, lambda *_: (0,) * nd, pipeline_mode=pl.Buffered(1))


def _ada_kernel(c_ref, w_ref, b_ref, o_ref):
    o_ref[0] = _dot(_silu(c_ref[...]), w_ref[0]) + b_ref[0]


def _ada_mods(cond8, ada_w, ada_b):
    tn = 1536
    return pl.pallas_call(
        _ada_kernel,
        grid=(DEPTH, 6 * D_MODEL // tn),
        in_specs=[
            pl.BlockSpec((8, D_MODEL), lambda l, j: (0, 0)),
            pl.BlockSpec((1, D_MODEL, tn), lambda l, j: (l, 0, j)),
            pl.BlockSpec((1, 1, tn), lambda l, j: (l, 0, j)),
        ],
        out_specs=pl.BlockSpec((1, 8, tn), lambda l, j: (l, 0, j)),
        out_shape=jax.ShapeDtypeStruct((DEPTH, 8, 6 * D_MODEL), F32),
        compiler_params=_cparams(("arbitrary", "arbitrary")),
        name="ada_mods",
    )(cond8, ada_w, ada_b.reshape(DEPTH, 1, 6 * D_MODEL))


ROW_UNITS = D_MODEL // LANES


def _rows_from_units(ref, n_rows):
    return jnp.concatenate([ref[pl.ds(c, n_rows, stride=ROW_UNITS), :] for c in range(ROW_UNITS)], axis=1)


def _rows_to_units(ref, val):
    n_rows = val.shape[0]
    for c in range(ROW_UNITS):
        ref[pl.ds(c, n_rows, stride=ROW_UNITS), :] = val[:, c * LANES:(c + 1) * LANES]


def _moe_combine(y_refs, wt_ref):
    w = wt_ref[...]
    terms = [w[:, k:k + 1] * _rows_from_units(y_refs[k], w.shape[0]) for k in range(TOP_K)]
    return (terms[0] + terms[1]) + (terms[2] + terms[3])


def _in_proj_kernel(*refs, first, even, nb_ctx):
    it = iter(refs)
    if first:
        xa_ref, xb_ref = next(it), next(it)
    else:
        xp_ref = next(it)
        y_refs = [next(it) for _ in range(TOP_K)]
        wt_ref, modp_ref = next(it), next(it)
    gain_ref, mod_ref, w_ref = next(it), next(it), next(it)
    if even:
        wlr_ref, gkw_ref, gkb_ref = next(it), next(it), next(it)
    proj_ref = next(it)
    if even:
        g_ref = next(it)
    xcur_ref = next(it)

    D = D_MODEL
    i = pl.program_id(0)
    if first:
        x = jnp.where(i < nb_ctx, xa_ref[...], xb_ref[...])
    else:
        x = xp_ref[...] + modp_ref[0][:, 5 * D:6 * D] * _moe_combine(y_refs, wt_ref)
    xcur_ref[...] = x
    m = mod_ref[0]
    hn = _rms(x, gain_ref[...]) * (1.0 + m[:, D:2 * D]) + m[:, 0:D]
    proj_ref[...] = _dot(hn, w_ref[...])
    if even:
        z = _dot(_dot(hn, wlr_ref[...]), gkw_ref[...]) + gkb_ref[...]
        g_ref[...] = (jnp.minimum(z, 0.0) - jnp.log(1.0 + jnp.exp(-jnp.abs(z)))) * (1.0 / GLA_NORMALIZER)


def _cond_row(i, nb_ctx, tiles_per_lat_seq):
    return jnp.where(i < nb_ctx, 0, 1 + (i - nb_ctx) // tiles_per_lat_seq)


def _in_proj(layer, n_ctx, n_tok, lat_seq, mods, gain, w_main, *, x_parts=None, x_prev=None, y4=None, even_extra=None):
    first = x_parts is not None
    even = even_extra is not None
    tm = BIG_TOKEN_TILE if first else TOKEN_TILE
    nb = n_tok // tm
    nb_ctx = n_ctx // tm
    tpl = lat_seq // tm
    np_cols = w_main.shape[1]
    D = D_MODEL

    def mod_map(l):
        return lambda i: (l * 8 + _cond_row(i, nb_ctx, tpl), 0, 0)

    row = lambda i: (i, 0)
    args, specs = [], []
    if first:
        xa, xb = x_parts
        args += [xa, xb]
        specs += [pl.BlockSpec((tm, D), lambda i: (jnp.minimum(i, nb_ctx - 1), 0)),
                  pl.BlockSpec((tm, D), lambda i: (jnp.maximum(i - nb_ctx, 0), 0))]
    else:
        y4, wts = y4
        args += [x_prev] + [y4] * TOP_K + [wts, mods]
        specs += [pl.BlockSpec((tm, D), row)]
        specs += [pl.BlockSpec((tm * ROW_UNITS, LANES), (lambda k: (lambda i: (k * nb + i, 0)))(k))
                  for k in range(TOP_K)]
        specs += [pl.BlockSpec((tm, LANES), row), pl.BlockSpec((1, 1, 6 * D), mod_map(layer - 1))]
    args += [gain.reshape(1, D), mods, w_main]
    specs += [_resident((1, D)), pl.BlockSpec((1, 1, 6 * D), mod_map(layer)), _resident((D, np_cols))]
    out_shapes = [jax.ShapeDtypeStruct((n_tok, np_cols), F32)]
    out_specs = [pl.BlockSpec((tm, np_cols), row)]
    if even:
        w_lr, gkw, gkb = even_extra
        args += [w_lr, gkw, gkb]
        specs += [_resident(w_lr.shape), _resident(gkw.shape), _resident(gkb.shape)]
        out_shapes.append(jax.ShapeDtypeStruct((n_tok, 2 * GLA_HEADS * GLA_DK), F32))
        out_specs.append(pl.BlockSpec((tm, 2 * GLA_HEADS * GLA_DK), row))
    out_shapes.append(jax.ShapeDtypeStruct((n_tok, D), F32))
    out_specs.append(pl.BlockSpec((tm, D), row))
    return pl.pallas_call(
        functools.partial(_in_proj_kernel, first=first, even=even, nb_ctx=nb_ctx),
        grid=(nb,),
        in_specs=specs,
        out_specs=out_specs,
        out_shape=out_shapes,
        compiler_params=_cparams(("arbitrary",)),
        name=f"in_proj_l{layer}",
    )(*args)


def _tri(c, lower):
    r = lax.broadcasted_iota(jnp.int32, (c, c), 0)
    s = lax.broadcasted_iota(jnp.int32, (c, c), 1)
    return (r >= s) if lower else (r <= s)


def _cumsum_mm(tri_bf16, g):
    g_hi = g.astype(BF16)
    g_lo = (g - g_hi.astype(F32)).astype(BF16)
    r = _dot(tri_bf16, jnp.concatenate([g_hi, g_lo], axis=1))
    w = g.shape[1]
    return r[:, :w] + r[:, w:]


def _gated_scan_kernel(*refs, variant, T, nseq, has_s0, want_final, hpb):
    C = SCAN_CHUNK
    ncs = T // C
    nc = nseq * ncs
    it = iter(refs)
    if variant == "gla":
        q_ref, k_ref, v_ref, og_ref, gf_ref, gb_ref, gain_ref = (next(it) for _ in range(7))
    else:
        q_ref, ff_ref, fb_ref, v_ref, og_ref, lbl_ref, gain_ref = (next(it) for _ in range(7))
    s0_ref = next(it) if has_s0 else None
    y_ref = next(it)
    sfin_ref = next(it) if want_final else None
    qi_scr, o_scr, u_scr, dec_scr, sin_scr = (next(it) for _ in range(5))

    lane = lax.broadcasted_iota(jnp.int32, (1, LANES), 1)
    if hpb == 2:
        masks = [lane < GLA_DK, lane >= GLA_DK]
    else:
        masks = [None]
    tri_l, tri_u = _tri(C, True), _tri(C, False)
    tri_l16, tri_u16 = tri_l.astype(F32).astype(BF16), tri_u.astype(F32).astype(BF16)
    H = C // 2
    upper_rows = lax.broadcasted_iota(jnp.int32, (C, 1), 0) < H

    if variant == "hgrn":
        lgs = [lbl_ref[l] for l in range(DEPTH)]
        mx = functools.reduce(jnp.maximum, lgs)
        es = [jnp.exp(l - mx) for l in lgs]
        tot = functools.reduce(lambda a, b: a + b, es)
        ps = [e / tot for e in es]
        layer = DEPTH - 1
        lb = functools.reduce(lambda a, b: a + b, ps[:layer + 1]) - ps[0]
        lb_f, lb_b = lb[0:1], lb[1:2]

    for n in range(nc):
        rows = pl.ds(n * C, C)
        if variant == "gla":
            q = q_ref[rows, :] * (GLA_DK ** -0.5)
            kf = kb = k_ref[rows, :]
            gf, gb = gf_ref[rows, :], gb_ref[rows, :]
        else:
            q = _silu(q_ref[rows, :])
            f_f = lb_f + (1.0 - lb_f) * _sigmoid(ff_ref[rows, :])
            f_b = lb_b + (1.0 - lb_b) * _sigmoid(fb_ref[rows, :])
            kf, kb = 1.0 - f_f, 1.0 - f_b
            gf, gb = jnp.log(f_f), jnp.log(f_b)
        bf = _cumsum_mm(tri_l16, gf)
        bb = _cumsum_mm(tri_u16, gb)
        bf_end, bb_end = bf[C - 1:C], bb[0:1]
        ref_f = jnp.where(upper_rows, bf[H // 2 - 1:H // 2], bf[H + H // 2 - 1:H + H // 2])
        ref_b = jnp.where(upper_rows, bb[H // 2:H // 2 + 1], bb[H + H // 2:H + H // 2 + 1])
        qd_f, qd_b = q * jnp.exp(bf - ref_f), q * jnp.exp(bb - ref_b)
        kd_f = [jnp.where(upper_rows, kf * jnp.exp(bf[H // 2 - 1:H // 2] - bf), 0.0),
                kf * jnp.exp(bf[H + H // 2 - 1:H + H // 2] - bf)]
        kd_b = [kb * jnp.exp(bb[H // 2:H // 2 + 1] - bb),
                jnp.where(upper_rows, 0.0, kb * jnp.exp(bb[H + H // 2:H + H // 2 + 1] - bb))]
        kend = jnp.concatenate([kf * jnp.exp(bf_end - bf), kb * jnp.exp(bb_end - bb)], axis=1)
        qi_scr[rows, :] = jnp.concatenate([q * jnp.exp(bf), q * jnp.exp(bb)], axis=1)
        dec_scr[n] = jnp.broadcast_to(jnp.concatenate([jnp.exp(bf_end), jnp.exp(bb_end)], axis=1), (8, 2 * LANES))
        for h in range(hpb):
            v = v_ref[rows, h * LANES:(h + 1) * LANES]
            if masks[h] is None:
                qf_h, qb_h, kend_h = qd_f, qd_b, kend
            else:
                qf_h, qb_h = jnp.where(masks[h], qd_f, 0.0), jnp.where(masks[h], qd_b, 0.0)
                kend_h = jnp.where(jnp.concatenate([masks[h], masks[h]], axis=1), kend, 0.0)
            s_f = jnp.concatenate([_dot_nt(qf_h[:H], kd_f[0]), _dot_nt(qf_h[H:], kd_f[1])], axis=0)
            s_b = jnp.concatenate([_dot_nt(qb_h[:H], kd_b[0]), _dot_nt(qb_h[H:], kd_b[1])], axis=0)
            s = jnp.where(tri_l, s_f, 0.0) + jnp.where(tri_u, s_b, 0.0)
            o_scr[rows, h * LANES:(h + 1) * LANES] = _dot(s, v)
            u_scr[h, n] = _dot(v.T, kend_h)

    for sq in range(nseq):
        for h in range(hpb):
            if has_s0:
                s_f, s_b = s0_ref[sq, 0, 0].T, s0_ref[sq, 1, 0].T
                if masks[h] is not None:
                    s_f, s_b = jnp.where(masks[h], s_f, 0.0), jnp.where(masks[h], s_b, 0.0)
            else:
                s_f = s_b = jnp.zeros((LANES, LANES), F32)
            for n in range(sq * ncs, (sq + 1) * ncs):
                sin_scr[h, n, :, 0:LANES] = s_f
                s_f = s_f * dec_scr[n, 0:1, 0:LANES] + u_scr[h, n, :, 0:LANES]
            for n in reversed(range(sq * ncs, (sq + 1) * ncs)):
                sin_scr[h, n, :, LANES:2 * LANES] = s_b
                s_b = s_b * dec_scr[n, 0:1, LANES:2 * LANES] + u_scr[h, n, :, LANES:2 * LANES]
            if want_final:
                if h == 0:
                    fin_f, fin_b = s_f, s_b
                else:
                    fin_f, fin_b = fin_f + s_f, fin_b + s_b
        if want_final:
            sfin_ref[sq, 0, 0] = fin_f.T
            sfin_ref[sq, 1, 0] = fin_b.T

    gain = gain_ref[...]
    for n in range(nc):
        rows = pl.ds(n * C, C)
        qi = qi_scr[rows, :]
        for h in range(hpb):
            cols = slice(h * LANES, (h + 1) * LANES)
            o = o_scr[rows, cols] + _dot_nt(qi, sin_scr[h, n])
            y_ref[rows, cols] = (_rms(o, gain) * _silu(og_ref[rows, cols])).astype(y_ref.dtype)


def _scan_scratch(T, hpb):
    nc = T // SCAN_CHUNK
    return [
        pltpu.VMEM((T, 2 * LANES), F32),
        pltpu.VMEM((T, hpb * LANES), F32),
        pltpu.VMEM((hpb, nc, LANES, 2 * LANES), F32),
        pltpu.VMEM((nc, 8, 2 * LANES), F32),
        pltpu.VMEM((hpb, nc, LANES, 2 * LANES), F32),
    ]


def _seqs_per_step(B, T):
    nseq = max(1, min(B, SCAN_ROWS // T))
    assert B % nseq == 0
    return nseq


def _gla_scan(proj, gdec, gain, s0, row0, B, T, want_final):
    nseq = _seqs_per_step(B, T)
    R = nseq * T
    rb0 = row0 // R
    assert row0 % R == 0
    has_s0 = s0 is not None

    def col(base, width=LANES):
        return lambda b, j: (rb0 + b, base // width + j)

    args = [proj, proj, proj, proj, gdec, gdec, gain.reshape(1, GLA_DV)]
    specs = [pl.BlockSpec((R, LANES), col(_GQ)), pl.BlockSpec((R, LANES), col(_GK)),
             pl.BlockSpec((R, 2 * LANES), col(_GV, 2 * LANES)), pl.BlockSpec((R, 2 * LANES), col(_GG, 2 * LANES)),
             pl.BlockSpec((R, LANES), col(0)), pl.BlockSpec((R, LANES), col(GLA_HEADS * GLA_DK)),
             pl.BlockSpec((1, GLA_DV), lambda b, j: (0, 0))]
    st_spec = pl.BlockSpec((nseq, 2, 1, LANES, LANES), lambda b, j: (b, 0, j, 0, 0))
    if has_s0:
        args.append(s0.reshape(B, 2, GLA_HEADS // 2, 2 * GLA_DK, GLA_DV))
        specs.append(st_spec)
    out_shapes = [jax.ShapeDtypeStruct((B * T, GLA_HEADS * GLA_DV), BF16)]
    out_specs = [pl.BlockSpec((R, 2 * LANES), lambda b, j: (b, j))]
    if want_final:
        out_shapes.append(jax.ShapeDtypeStruct((B, 2, GLA_HEADS // 2, 2 * GLA_DK, GLA_DV), F32))
        out_specs.append(st_spec)
    kern = functools.partial(_gated_scan_kernel, variant="gla", T=T, nseq=nseq, has_s0=has_s0,
                             want_final=want_final, hpb=2)
    return pl.pallas_call(
        kern, grid=(B // nseq, GLA_HEADS // 2), in_specs=specs, out_specs=out_specs, out_shape=out_shapes,
        scratch_shapes=_scan_scratch(R, 2),
        compiler_params=_cparams(("arbitrary", "arbitrary")), name=f"gla_scan_T{T}",
    )(*args)


def _hgrn_scan(proj, lb_logits, gain, s0, row0, B, T, want_final):
    nseq = _seqs_per_step(B, T)
    R = nseq * T
    rb0 = row0 // R
    assert row0 % R == 0
    has_s0 = s0 is not None

    def col(base):
        return lambda b, j: (rb0 + b, base // LANES + j)

    args = [proj, proj, proj, proj, proj, lb_logits, gain.reshape(1, HG_DV)]
    specs = [pl.BlockSpec((R, LANES), col(_HQ)), pl.BlockSpec((R, LANES), col(_HFF)),
             pl.BlockSpec((R, LANES), col(_HFB)), pl.BlockSpec((R, LANES), col(_HI)),
             pl.BlockSpec((R, LANES), col(_HG)),
             pl.BlockSpec((DEPTH, 2, LANES), lambda b, j: (0, 0, j)),
             pl.BlockSpec((1, HG_DV), lambda b, j: (0, 0))]
    st_spec = pl.BlockSpec((nseq, 2, 1, LANES, LANES), lambda b, j: (b, 0, j, 0, 0))
    if has_s0:
        args.append(s0.reshape(B, 2, HG_HEADS, HG_DK, HG_DV))
        specs.append(st_spec)
    out_shapes = [jax.ShapeDtypeStruct((B * T, HG_HEADS * HG_DV), BF16)]
    out_specs = [pl.BlockSpec((R, LANES), lambda b, j: (b, j))]
    if want_final:
        out_shapes.append(jax.ShapeDtypeStruct((B, 2, HG_HEADS, HG_DK, HG_DV), F32))
        out_specs.append(st_spec)
    kern = functools.partial(_gated_scan_kernel, variant="hgrn", T=T, nseq=nseq, has_s0=has_s0,
                             want_final=want_final, hpb=1)
    return pl.pallas_call(
        kern, grid=(B // nseq, HG_HEADS), in_specs=specs, out_specs=out_specs, out_shape=out_shapes,
        scratch_shapes=_scan_scratch(R, 1),
        compiler_params=_cparams(("arbitrary", "arbitrary")), name=f"hgrn_scan_T{T}",
    )(*args)


def _ret_scan_kernel(*refs, T, nseq, has_s0, want_final, rope):
    C = SCAN_CHUNK
    ncs = T // C
    nc = nseq * ncs
    it = iter(refs)
    q_ref, k_ref, v_ref, og_ref, dexp_ref = (next(it) for _ in range(5))
    if rope:
        cos_ref, sin_ref = next(it), next(it)
    s0_ref = next(it) if has_s0 else None
    y_ref = next(it)
    sfin_ref = next(it) if want_final else None
    qi_scr, o_scr, u_scr, sin_scr = (next(it) for _ in range(4))

    lg = jnp.log1p(-jnp.exp2(-dexp_ref[0]))
    lg_f, lg_b = lg[0:1], lg[1:2]
    r = lax.broadcasted_iota(jnp.int32, (C, C), 0)
    s = lax.broadcasted_iota(jnp.int32, (C, C), 1)
    dist = (r - s).astype(F32)
    dmask = (jnp.where(r >= s, jnp.exp(jnp.maximum(dist, 0.0) * lg_f[:, 0:1]), 0.0)
             + jnp.where(r <= s, jnp.exp(jnp.maximum(-dist, 0.0) * lg_b[:, 0:1]), 0.0))
    pos = lax.broadcasted_iota(jnp.int32, (C, LANES), 0).astype(F32)
    xi = jnp.concatenate([jnp.exp((pos + 1.0) * lg_f), jnp.exp((C - pos) * lg_b)], axis=1)
    zeta = jnp.concatenate([jnp.exp((C - 1.0 - pos) * lg_f), jnp.exp(pos * lg_b)], axis=1)
    d_f, d_b = jnp.exp(C * lg_f), jnp.exp(C * lg_b)

    def rot(x, seq_rows):
        if not rope:
            return x
        return x * cos_ref[seq_rows, :] + pltpu.roll(x, RET_DK // 2, axis=1) * sin_ref[seq_rows, :]

    for n in range(nc):
        rows = pl.ds(n * C, C)
        seq_rows = pl.ds((n % ncs) * C, C)
        q = rot(q_ref[rows, :], seq_rows)
        k = rot(k_ref[rows, :] * (RET_DK ** -0.5), seq_rows)
        v = v_ref[rows, :]
        o_scr[rows, :] = _dot(_dot_nt(q, k) * dmask, v)
        qi_scr[rows, :] = jnp.concatenate([q, q], axis=1) * xi
        u_scr[n] = _dot(v.T, jnp.concatenate([k, k], axis=1) * zeta)

    for sq in range(nseq):
        if has_s0:
            s_f, s_b = s0_ref[sq, 0, 0].T, s0_ref[sq, 1, 0].T
        else:
            s_f = s_b = jnp.zeros((LANES, LANES), F32)
        for n in range(sq * ncs, (sq + 1) * ncs):
            sin_scr[n, :, 0:LANES] = s_f
            s_f = s_f * d_f + u_scr[n, :, 0:LANES]
        for n in reversed(range(sq * ncs, (sq + 1) * ncs)):
            sin_scr[n, :, LANES:2 * LANES] = s_b
            s_b = s_b * d_b + u_scr[n, :, LANES:2 * LANES]
        if want_final:
            sfin_ref[sq, 0, 0] = s_f.T
            sfin_ref[sq, 1, 0] = s_b.T

    for n in range(nc):
        rows = pl.ds(n * C, C)
        o = o_scr[rows, :] + _dot_nt(qi_scr[rows, :], sin_scr[n])
        y_ref[rows, :] = (_rms(o) * _silu(og_ref[rows, :])).astype(y_ref.dtype)


def _ret_scan(proj, dexp, rope_tabs, s0, row0, B, T, want_final):
    nseq = _seqs_per_step(B, T)
    R = nseq * T
    rb0 = row0 // R
    assert row0 % R == 0
    nc = R // SCAN_CHUNK
    has_s0 = s0 is not None
    rope = rope_tabs is not None

    def col(base):
        return lambda b, j: (rb0 + b, base // LANES + j)

    args = [proj, proj, proj, proj, dexp]
    specs = [pl.BlockSpec((R, LANES), col(_RQ)), pl.BlockSpec((R, LANES), col(_RK)),
             pl.BlockSpec((R, LANES), col(_RV)), pl.BlockSpec((R, LANES), col(_RG)),
             pl.BlockSpec((1, 2, LANES), lambda b, j: (j, 0, 0))]
    if rope:
        args += list(rope_tabs)
        specs += [pl.BlockSpec((T, LANES), lambda b, j: (0, 0))] * 2
    st_spec = pl.BlockSpec((nseq, 2, 1, LANES, LANES), lambda b, j: (b, 0, j, 0, 0))
    if has_s0:
        args.append(s0.reshape(B, 2, RET_HEADS, RET_DK, RET_DV))
        specs.append(st_spec)
    out_shapes = [jax.ShapeDtypeStruct((B * T, RET_HEADS * RET_DV), BF16)]
    out_specs = [pl.BlockSpec((R, LANES), lambda b, j: (b, j))]
    if want_final:
        out_shapes.append(jax.ShapeDtypeStruct((B, 2, RET_HEADS, RET_DK, RET_DV), F32))
        out_specs.append(st_spec)
    kern = functools.partial(_ret_scan_kernel, T=T, nseq=nseq, has_s0=has_s0, want_final=want_final, rope=rope)
    scratch = [pltpu.VMEM((R, 2 * LANES), F32), pltpu.VMEM((R, LANES), F32),
               pltpu.VMEM((nc, LANES, 2 * LANES), F32), pltpu.VMEM((nc, LANES, 2 * LANES), F32)]
    return pl.pallas_call(
        kern, grid=(B // nseq, RET_HEADS), in_specs=specs, out_specs=out_specs, out_shape=out_shapes,
        scratch_shapes=scratch,
        compiler_params=_cparams(("arbitrary", "arbitrary")), name=f"ret_scan_T{T}",
    )(*args)


def _out_proj_kernel(*refs, n_mix, nb_ctx):
    it = iter(refs)
    y_refs = [(next(it), next(it)) for _ in range(n_mix)]
    wo_ref, x_ref, mod_ref, gain_ref, rw_ref, rb_ref = (next(it) for _ in range(6))
    xnew_ref, hn_ref, idx_ref, wt_ref, rank_ref, cnt_ref = (next(it) for _ in range(6))
    cnt_scr = next(it)
    D = D_MODEL
    is_ctx = pl.program_id(0) < nb_ctx
    mix = None
    r0 = 0
    for ya_ref, yb_ref in y_refs:
        w = ya_ref.shape[1]
        part = _dot(jnp.where(is_ctx, ya_ref[...], yb_ref[...]), wo_ref[r0:r0 + w, :])
        mix = part if mix is None else mix + part
        r0 += w
    m = mod_ref[0]
    xn = x_ref[...] + m[:, 2 * D:3 * D] * mix
    xnew_ref[...] = xn
    hn = _rms(xn, gain_ref[...]) * (1.0 + m[:, 4 * D:5 * D]) + m[:, 3 * D:4 * D]
    hn_hi = hn.astype(BF16)
    _rows_to_units(hn_ref, hn)
    hn_lo = (hn - hn_hi.astype(F32)).astype(BF16)
    rw = rw_ref[...]
    rw_hi = rw.astype(BF16)
    rw_lo = (rw - rw_hi.astype(F32)).astype(BF16)
    logits = (_dot(hn_hi, rw_hi) + (_dot(hn_lo, rw_hi) + _dot(hn_hi, rw_lo))) + rb_ref[...]
    lane = lax.broadcasted_iota(jnp.int32, logits.shape, 1).astype(F32)
    vals, idxs = [], []
    cur = logits
    for _ in range(TOP_K):
        mx = jnp.max(cur, axis=-1, keepdims=True)
        ik = jnp.min(jnp.where(cur == mx, lane, float(LANES)), axis=-1, keepdims=True)
        vals.append(mx)
        idxs.append(ik)
        cur = jnp.where(lane == ik, -jnp.inf, cur)
    es = [jnp.exp(v - vals[0]) for v in vals]
    tot = (es[0] + es[1]) + (es[2] + es[3])
    idx_out = jnp.zeros(logits.shape, F32)
    wt_out = jnp.zeros(logits.shape, F32)
    for k in range(TOP_K):
        idx_out = jnp.where(lane == float(k), idxs[k], idx_out)
        wt_out = jnp.where(lane == float(k), es[k] / tot, wt_out)
    idx_ref[...] = idx_out.astype(jnp.int32)
    wt_ref[...] = wt_out

    @pl.when(pl.program_id(0) == 0)
    def _():
        cnt_scr[...] = jnp.zeros(cnt_scr.shape, F32)

    tm = logits.shape[0]
    hits = [lane == idxs[k] for k in range(TOP_K)]
    sel = jnp.zeros(logits.shape, F32)
    for k in range(TOP_K):
        sel = sel + jnp.where(hits[k], 1.0, 0.0)
    rr = lax.broadcasted_iota(jnp.int32, (tm, tm), 0)
    cc = lax.broadcasted_iota(jnp.int32, (tm, tm), 1)
    before = jnp.where(rr > cc, 1.0, 0.0).astype(BF16)
    rank_all = cnt_scr[0:1, :] + _dot(before, sel.astype(BF16))
    rank_out = jnp.zeros(logits.shape, F32)
    for k in range(TOP_K):
        rk = jnp.sum(jnp.where(hits[k], rank_all, 0.0), axis=-1, keepdims=True)
        rank_out = jnp.where(lane == float(k), rk, rank_out)
    rank_ref[...] = rank_out.astype(jnp.int32)
    total = cnt_scr[...] + jnp.sum(sel, axis=0, keepdims=True)
    cnt_scr[...] = total
    cnt_ref[...] = total.astype(jnp.int32)


def _out_proj(layer, n_ctx, n_tok, lat_seq, ys, w_out, x_cur, mods, gain, rw_pad, rb_pad):
    tm = BIG_TOKEN_TILE
    nb, nb_ctx, tpl = n_tok // tm, n_ctx // tm, lat_seq // tm
    D = D_MODEL
    row = lambda i: (i, 0)
    specs = []
    for ya, _ in ys:
        specs += [pl.BlockSpec((tm, ya.shape[1]), lambda i: (jnp.minimum(i, nb_ctx - 1), 0)),
                  pl.BlockSpec((tm, ya.shape[1]), lambda i: (jnp.maximum(i - nb_ctx, 0), 0))]
    specs += [_resident((D, D)), pl.BlockSpec((tm, D), row),
              pl.BlockSpec((1, 1, 6 * D), lambda i: (layer * 8 + _cond_row(i, nb_ctx, tpl), 0, 0)),
              _resident((1, D)), _resident((D, LANES)), _resident((1, LANES))]
    return pl.pallas_call(
        functools.partial(_out_proj_kernel, n_mix=len(ys), nb_ctx=nb_ctx),
        grid=(nb,),
        in_specs=specs,
        out_specs=[pl.BlockSpec((tm, D), row), pl.BlockSpec((tm * ROW_UNITS, LANES), row),
                   pl.BlockSpec((tm, LANES), row), pl.BlockSpec((tm, LANES), row),
                   pl.BlockSpec((tm, LANES), row), pl.BlockSpec((8, LANES), lambda i: (0, 0))],
        out_shape=[jax.ShapeDtypeStruct((n_tok, D), F32), jax.ShapeDtypeStruct((n_tok * ROW_UNITS, LANES), F32),
                   jax.ShapeDtypeStruct((n_tok, LANES), jnp.int32), jax.ShapeDtypeStruct((n_tok, LANES), F32),
                   jax.ShapeDtypeStruct((n_tok, LANES), jnp.int32), jax.ShapeDtypeStruct((8, LANES), jnp.int32)],
        scratch_shapes=[pltpu.VMEM((8, LANES), F32)],
        compiler_params=_cparams(("arbitrary",)),
        name=f"out_proj_l{layer}",
    )(*[y for pair in ys for y in pair], w_out, x_cur, mods, gain.reshape(1, D), rw_pad, rb_pad)


W1_SPLIT = 4
W2_SPLIT = 2
SPARE_TILES = 2
ROUTER_PAD_BIAS = -1e30


def _moe_kernel(seg_ref, nt_ref, code_hbm, x_hbm, *refs, n_tok):
    w1_refs = refs[:W1_SPLIT]
    b1_ref = refs[W1_SPLIT]
    w2_refs = refs[W1_SPLIT + 1:W1_SPLIT + 1 + W2_SPLIT]
    b2_ref, y_hbm, w1_scr, w2_scr, act_scr, xbuf, obuf, code_smem, csem, gsem, ssem = refs[W1_SPLIT + 1 + W2_SPLIT:]
    tm = MOE_TILE
    U = ROW_UNITS
    n_code_tiles = code_hbm.shape[0] // tm
    e = pl.program_id(0)
    nt = nt_ref[e]
    g_first = seg_ref[e] // tm

    GATHER, SCATTER = 0, 1

    def code_copy(kind, t, p):
        first = pl.multiple_of(jnp.clip(t, 0, n_code_tiles - 1) * tm, tm)
        return pltpu.make_async_copy(code_hbm.at[pl.ds(first, tm)], code_smem.at[kind, p], csem.at[kind, p])

    def gather_start(b, p):
        for r in range(tm):
            tok = jnp.bitwise_and(code_smem[GATHER, p, r], n_tok - 1)
            pltpu.make_async_copy(x_hbm.at[pl.ds(pl.multiple_of(tok * U, U), U)],
                                  xbuf.at[b, pl.ds(r * U, U)], gsem.at[b]).start()

    def gather_wait(b):
        pltpu.make_async_copy(x_hbm.at[pl.ds(0, tm * U)], xbuf.at[b], gsem.at[b]).wait()

    def scatter_start(b, p):
        for r in range(tm):
            dst = code_smem[SCATTER, p, r]
            pltpu.make_async_copy(obuf.at[b, pl.ds(r * U, U)],
                                  y_hbm.at[pl.ds(pl.multiple_of(dst * U, U), U)], ssem.at[b]).start(priority=r % 2)

    def scatter_wait(b):
        pltpu.make_async_copy(obuf.at[b], y_hbm.at[pl.ds(0, tm * U)], ssem.at[b]).wait()

    @pl.when(e == 0)
    def _():
        obuf[...] = jnp.zeros(obuf.shape, obuf.dtype)
        for h in range(SPARE_TILES):
            cp = pltpu.make_async_copy(obuf.at[0], y_hbm.at[pl.ds((TOP_K * n_tok + h * tm) * U, tm * U)], ssem.at[0])
            cp.start()
            cp.wait()
        code_copy(GATHER, 0, 0).start()
        code_copy(GATHER, 1, 1).start()
        code_copy(SCATTER, n_code_tiles - 1, 1).start()
        code_copy(SCATTER, 0, 0).start()
        code_copy(GATHER, 0, 0).wait()
        gather_start(0, 0)

    @pl.when(nt > 0)
    def _():
        grp = 2 * LANES
        rr = lax.broadcasted_iota(jnp.int32, (grp, grp), 0)
        cc = lax.broadcasted_iota(jnp.int32, (grp, grp), 1)
        src = jnp.where(cc < LANES, 2 * cc, 2 * (cc - LANES) + 1)
        perm = jnp.where(rr == src, 1.0, 0.0).astype(BF16)
        cw = 2 * D_FF // W1_SPLIT
        for c, w_ref in enumerate(w1_refs):
            for j in range(cw // grp):
                cols = slice(j * grp, (j + 1) * grp)
                w1_scr[:, c * cw + j * grp:c * cw + (j + 1) * grp] = _dot(w_ref[0, 0, :, cols].astype(BF16),
                                                                         perm).astype(BF16)
        rh = D_FF // W2_SPLIT
        for c, w_ref in enumerate(w2_refs):
            w2_scr[c * rh:(c + 1) * rh, :] = w_ref[0, 0].astype(BF16)
        blk = 4 * LANES

        def tile(g, b):
            o = 1 - b

            @pl.when(g >= 0)
            def _():
                code_copy(GATHER, g + 1, o).wait()
                code_copy(SCATTER, g - 1, o).wait()
                gather_start(o, o)
                scatter_start(o, o)
                code_copy(GATHER, g + 2, b).start()
                code_copy(SCATTER, g + 1, o).start()

            gather_wait(b)
            x = _rows_from_units(xbuf.at[b], tm).astype(BF16)
            for c in range(2 * D_FF // blk):
                cols = slice(c * blk, (c + 1) * blk)
                hid = _dot(x, w1_scr[:, cols]) + b1_ref[0, 0][:, cols]
                acts = []
                for j in range(blk // grp):
                    glu = jnp.minimum(hid[:, j * grp:j * grp + LANES], SWIGLU_LIMIT)
                    lin = jnp.clip(hid[:, j * grp + LANES:(j + 1) * grp], -SWIGLU_LIMIT, SWIGLU_LIMIT)
                    acts.append(glu * _sigmoid(SWIGLU_ALPHA * glu) * (lin + 1.0))
                act_scr[:, c * (blk // 2):(c + 1) * (blk // 2)] = jnp.concatenate(acts, axis=1).astype(BF16)
            y = _dot(act_scr[...], w2_scr[...]) + b2_ref[0, 0]

            @pl.when(g >= 1)
            def _():
                scatter_wait(b)

            _rows_to_units(obuf.at[b], y)

        g_end = g_first + nt

        def tile_pair(m, carry):
            for b in range(2):
                g = 2 * m + b

                @pl.when(jnp.logical_and(g >= g_first, g < g_end))
                def _():
                    tile(g, b)

            return carry

        lax.fori_loop(g_first // 2, (g_end + 1) // 2, tile_pair, 0)

    @pl.when(e == N_EXPERTS - 1)
    def _():
        g_end = g_first + nt
        last = lax.rem(g_end + 1, 2)
        code_copy(SCATTER, g_end - 1, last).wait()
        scatter_start(last, last)
        scatter_wait(last)
        scatter_wait(1 - last)
        gather_wait(1 - last)
        code_copy(GATHER, g_end + 1, last).wait()
        code_copy(SCATTER, g_end, 1 - last).wait()


def _moe_experts(layer, n_tok, seg_start, tiles_per_e, code, x_units, w1, b1, w2, b2):
    tm = MOE_TILE
    D, F2, U = D_MODEL, 2 * D_FF, ROW_UNITS
    cw, rh = F2 // W1_SPLIT, D_FF // W2_SPLIT
    any_space = pl.BlockSpec(memory_space=pl.ANY)
    w1_specs = [pl.BlockSpec((1, 1, D, cw), (lambda c: (lambda e, sg, nt: (layer, e, 0, c)))(c))
                for c in range(W1_SPLIT)]
    w2_specs = [pl.BlockSpec((1, 1, rh, D), (lambda c: (lambda e, sg, nt: (layer, e, c, 0)))(c))
                for c in range(W2_SPLIT)]
    grid_spec = pltpu.PrefetchScalarGridSpec(
        num_scalar_prefetch=2,
        grid=(N_EXPERTS,),
        in_specs=[any_space, any_space]
        + w1_specs + [pl.BlockSpec((1, 1, 1, F2), lambda e, sg, nt: (layer, e, 0, 0))]
        + w2_specs + [pl.BlockSpec((1, 1, 1, D), lambda e, sg, nt: (layer, e, 0, 0))],
        out_specs=any_space,
        scratch_shapes=[pltpu.VMEM((D, F2), BF16), pltpu.VMEM((D_FF, D), BF16), pltpu.VMEM((tm, D_FF), BF16),
                        pltpu.VMEM((2, tm * U, LANES), F32), pltpu.VMEM((2, tm * U, LANES), F32),
                        pltpu.SMEM((2, 2, tm), jnp.int32),
                        pltpu.SemaphoreType.DMA((2, 2)), pltpu.SemaphoreType.DMA((2,)), pltpu.SemaphoreType.DMA((2,))],
    )
    n_out_rows = TOP_K * n_tok + SPARE_TILES * tm
    b1 = b1.reshape(DEPTH, N_EXPERTS, F2 // (2 * LANES), LANES, 2).swapaxes(-1, -2)
    return pl.pallas_call(
        functools.partial(_moe_kernel, n_tok=n_tok),
        grid_spec=grid_spec,
        out_shape=jax.ShapeDtypeStruct((n_out_rows * U, LANES), F32),
        compiler_params=_cparams(("arbitrary",)),
        name="moe_experts",
    )(seg_start, tiles_per_e, code, x_units, *([w1] * W1_SPLIT), b1.reshape(DEPTH, N_EXPERTS, 1, F2),
      *([w2] * W2_SPLIT), b2.reshape(DEPTH, N_EXPERTS, 1, D))


def _slot_code_kernel(seg_ref, idx_ref, rank_ref, unused_hbm, code_ref, sem, *, n_tok):
    i = pl.program_id(0)
    tm = idx_ref.shape[0] // TOP_K

    @pl.when(i == 0)
    def _():
        cp = pltpu.make_async_copy(unused_hbm, code_ref, sem)
        cp.start()
        cp.wait()

    base = i * tm
    for t in range(tm):
        for k in range(TOP_K):
            p = t * TOP_K + k
            code_ref[seg_ref[idx_ref[p]] + rank_ref[p]] = base + (k * n_tok + t)


def _slot_codes(idx, rank, seg_start, unused):
    n_tok = idx.shape[0]
    tm = TOKEN_TILE
    pairs = lambda a: a[:, :TOP_K].reshape(-1)
    flat = pl.BlockSpec((tm * TOP_K,), lambda i, sg: (i,), memory_space=pltpu.SMEM)
    grid_spec = pltpu.PrefetchScalarGridSpec(
        num_scalar_prefetch=1,
        grid=(n_tok // tm,),
        in_specs=[flat, flat, pl.BlockSpec(memory_space=pl.ANY)],
        out_specs=pl.BlockSpec(memory_space=pltpu.SMEM),
        scratch_shapes=[pltpu.SemaphoreType.DMA(())],
    )
    return pl.pallas_call(
        functools.partial(_slot_code_kernel, n_tok=n_tok),
        grid_spec=grid_spec,
        out_shape=jax.ShapeDtypeStruct(unused.shape, jnp.int32),
        compiler_params=_cparams(("arbitrary",)),
        name="slot_codes",
    )(seg_start, pairs(idx), pairs(rank), unused)


def _moe_layer(layer, hn_units, idx, rank, counts, w1, b1, w2, b2):
    n_tok = hn_units.shape[0] // ROW_UNITS
    assert n_tok & (n_tok - 1) == 0
    tm = MOE_TILE
    n_pairs = n_tok * TOP_K
    n_slots = ((n_pairs + N_EXPERTS * (tm - 1)) // tm + 1) * tm
    tiles_per_e = (counts + tm - 1) // tm
    seg_start = (jnp.cumsum(tiles_per_e) - tiles_per_e) * tm
    unused = n_pairs + jnp.arange(n_slots, dtype=jnp.int32) % (SPARE_TILES * tm)
    code = _slot_codes(idx, rank, seg_start, unused)
    return _moe_experts(layer, n_tok, seg_start, tiles_per_e, code, hn_units, w1, b1, w2, b2)


def _final_kernel(*refs, nb_ctx):
    x_ref = refs[0]
    y_refs = refs[1:1 + TOP_K]
    wt_ref, mod_ref, gain_ref, oa_ref, ob_ref = refs[1 + TOP_K:]
    D = D_MODEL
    i = pl.program_id(0)
    x = x_ref[...] + mod_ref[0][:, 5 * D:6 * D] * _moe_combine(y_refs, wt_ref)
    out = _rms(x, gain_ref[...])

    @pl.when(i < nb_ctx)
    def _():
        oa_ref[...] = out

    @pl.when(i >= nb_ctx)
    def _():
        ob_ref[...] = out


def _final(n_ctx, n_tok, lat_seq, x_new, y4, wts, mods, gain):
    tm = TOKEN_TILE
    nb, nb_ctx, tpl = n_tok // tm, n_ctx // tm, lat_seq // tm
    D = D_MODEL
    specs = [pl.BlockSpec((tm, D), lambda i: (i, 0))]
    specs += [pl.BlockSpec((tm * ROW_UNITS, LANES), (lambda k: (lambda i: (k * nb + i, 0)))(k)) for k in range(TOP_K)]
    specs += [pl.BlockSpec((tm, LANES), lambda i: (i, 0)),
              pl.BlockSpec((1, 1, 6 * D), lambda i: ((DEPTH - 1) * 8 + _cond_row(i, nb_ctx, tpl), 0, 0)),
              _resident((1, D))]
    return pl.pallas_call(
        functools.partial(_final_kernel, nb_ctx=nb_ctx),
        grid=(nb,),
        in_specs=specs,
        out_specs=[pl.BlockSpec((tm, D), lambda i: (jnp.minimum(i, nb_ctx - 1), 0)),
                   pl.BlockSpec((tm, D), lambda i: (jnp.maximum(i - nb_ctx, 0), 0))],
        out_shape=[jax.ShapeDtypeStruct((n_ctx, D), F32), jax.ShapeDtypeStruct((n_tok - n_ctx, D), F32)],
        compiler_params=_cparams(("arbitrary",)),
        name="final_norm",
    )(x_new, *([y4] * TOP_K), wts, mods, gain.reshape(1, D))


def _rope_tables(n_tokens):
    rows = n_tokens // GRID_W
    r = jnp.repeat(jnp.arange(rows, dtype=F32), GRID_W)
    col = jnp.tile(jnp.arange(GRID_W, dtype=F32), rows)
    nf = RET_DK // 4
    inv = ROPE_BASE ** (-jnp.arange(nf, dtype=F32) / nf)
    ang = jnp.concatenate([r[:, None] * inv, col[:, None] * inv], axis=-1)
    cos, sin = jnp.cos(ang), jnp.sin(ang)
    return jnp.concatenate([cos, cos], axis=-1), jnp.concatenate([-sin, sin], axis=-1)


def kernel(x_prompt, x_sample, c, state_gla, state_ret, state_hgrn, c_ctx, norm_mix, norm_ffn, ada_w, ada_b,
           w_in_even, w_out_even, gla_gk_w, gla_gk_b, gla_gain, ret_decay_exp, w_in_odd, w_out_odd,
           hgrn_lb_logits, hgrn_gain, router_w, router_b, moe_w1, moe_b1, moe_w2, moe_b2, final_norm):
    D = D_MODEL
    B_ctx, T_ctx, _ = x_prompt.shape
    B_lat, T_lat, _ = x_sample.shape
    n_ctx, n_lat = B_ctx * T_ctx, B_lat * T_lat
    n_tok = n_ctx + n_lat
    assert n_ctx % BIG_TOKEN_TILE == 0 and T_lat % BIG_TOKEN_TILE == 0 and B_lat + 1 <= 8
    assert T_ctx % SCAN_CHUNK == 0 and T_lat % SCAN_CHUNK == 0 and n_ctx % T_lat == 0

    cond8 = jnp.concatenate([c_ctx[None, :], c, jnp.zeros((8 - 1 - B_lat, D), F32)], axis=0)
    mods = _ada_mods(cond8, ada_w, ada_b).reshape(DEPTH * 8, 1, 6 * D)

    rw_pad = jnp.pad(router_w, ((0, 0), (0, 0), (0, LANES - N_EXPERTS)))
    rb_pad = jnp.pad(router_b, ((0, 0), (0, LANES - N_EXPERTS)), constant_values=ROUTER_PAD_BIAS)

    w_even = w_in_even[0]
    lr0 = GLA_HEADS * (2 * GLA_DK + GLA_DV)
    lr1 = lr0 + 2 * GLA_LOW_RANK
    w_main = jnp.concatenate([w_even[:, :lr0], w_even[:, lr1:]], axis=1)
    w_lr = jnp.pad(w_even[:, lr0:lr1], ((0, 0), (0, LANES - 2 * GLA_LOW_RANK)))
    nqk = GLA_HEADS * GLA_DK
    gkw = jnp.zeros((LANES, 2 * nqk), F32)
    gkw = gkw.at[0:GLA_LOW_RANK, 0:nqk].set(gla_gk_w[0, 0])
    gkw = gkw.at[GLA_LOW_RANK:2 * GLA_LOW_RANK, nqk:].set(gla_gk_w[0, 1])
    gkb = gla_gk_b[0].reshape(1, 2 * nqk)
    proj, gdec, x_cur = _in_proj(0, n_ctx, n_tok, T_lat, mods, norm_mix[0], w_main,
                                 x_parts=(x_prompt.reshape(n_ctx, D), x_sample.reshape(n_lat, D)),
                                 even_extra=(w_lr, gkw, gkb))

    y_gla_c, fin_gla = _gla_scan(proj, gdec, gla_gain[0], None, 0, B_ctx, T_ctx, True)
    (y_gla_l,) = _gla_scan(proj, gdec, gla_gain[0], state_gla[:, 0], n_ctx, B_lat, T_lat, False)
    dexp = jnp.broadcast_to(ret_decay_exp[0].T[:, :, None], (RET_HEADS, 2, LANES))
    y_ret_c, fin_ret = _ret_scan(proj, dexp, None, None, 0, B_ctx, T_ctx, True)
    (y_ret_l,) = _ret_scan(proj, dexp, _rope_tables(T_lat), state_ret[:, 0], n_ctx, B_lat, T_lat, False)

    x_new, hn, idx, wts, rank, cnt = _out_proj(0, n_ctx, n_tok, T_lat, [(y_gla_c, y_gla_l), (y_ret_c, y_ret_l)],
                                               w_out_even[0], x_cur, mods, norm_ffn[0], rw_pad[0], rb_pad[0:1])
    y4 = _moe_layer(0, hn, idx, rank, cnt[0, :N_EXPERTS], moe_w1, moe_b1, moe_w2, moe_b2)

    (proj, x_cur) = _in_proj(1, n_ctx, n_tok, T_lat, mods, norm_mix[1], w_in_odd[0], x_prev=x_new, y4=(y4, wts))
    y_h_c, fin_h = _hgrn_scan(proj, hgrn_lb_logits, hgrn_gain[0], None, 0, B_ctx, T_ctx, True)
    (y_h_l,) = _hgrn_scan(proj, hgrn_lb_logits, hgrn_gain[0], state_hgrn[:, 0], n_ctx, B_lat, T_lat, False)
    x_new, hn, idx, wts, rank, cnt = _out_proj(1, n_ctx, n_tok, T_lat, [(y_h_c, y_h_l)], w_out_odd[0], x_cur, mods,
                                               norm_ffn[1], rw_pad[1], rb_pad[1:2])
    y4 = _moe_layer(1, hn, idx, rank, cnt[0, :N_EXPERTS], moe_w1, moe_b1, moe_w2, moe_b2)

    y_ctx, y_lat = _final(n_ctx, n_tok, T_lat, x_new, y4, wts, mods, final_norm)

    new_state_gla = fin_gla.reshape(B_ctx, 1, 2, GLA_HEADS, GLA_DK, GLA_DV)
    new_state_ret = fin_ret.reshape(B_ctx, 1, 2, RET_HEADS, RET_DK, RET_DV)
    new_state_hgrn = fin_h.reshape(B_ctx, 1, 2, HG_HEADS, HG_DK, HG_DV)
    return (y_ctx.reshape(B_ctx, T_ctx, D), y_lat.reshape(B_lat, T_lat, D), new_state_gla, new_state_ret,
            new_state_hgrn)
```
